```python
import math
import jax, jax.numpy as jnp
from jax import lax
import numpy as np

D_MODEL = 2048
BATCH = 4
SEQ = 2048
DEPTH = 1
DEC_BATCH = 128
DEC_SEQ = 8
PAST_LEN = 16384
PAGE_SIZE = 128

D_MIX = D_MODEL
D_CONV = D_MIX // 2
CONV_W = 3
ML_HEADS = 4
ML_DV = (D_MIX // 2) // ML_HEADS
ML_DK = ML_DV // 2
ML_CHUNK = 64
D_FF = ((8 * D_MODEL + 3 * 256 - 1) // (3 * 256)) * 256
EPS = 1e-6

_SPLITS = (D_CONV, D_CONV, D_CONV,
           ML_HEADS * ML_DK, ML_HEADS * ML_DK, ML_HEADS * ML_DV, ML_HEADS * ML_DV,
           ML_HEADS, ML_HEADS)
N_IN = sum(_SPLITS)
_SPLIT_IDX = [sum(_SPLITS[: i + 1]) for i in range(len(_SPLITS) - 1)]

kernel_name = "hybrid_shortconv_mlstm_step"


def _rmsnorm(x, g):
    xf = x.astype(jnp.float32)
    xf = xf * lax.rsqrt(jnp.mean(xf * xf, axis=-1, keepdims=True) + EPS)
    return xf.astype(x.dtype) * g


def _mlstm_chunked(q, k, v, li, lf, C0, n0, m0):
    Bsz, S, H, DK = q.shape
    DV = v.shape[-1]
    L = math.gcd(S, ML_CHUNK)
    nc = S // L

    def to_chunks(a):
        a = a.reshape((Bsz, nc, L) + a.shape[2:])
        return jnp.moveaxis(a, (1, 2), (0, 3))

    causal = jnp.tril(jnp.ones((L, L), dtype=bool))

    def step(carry, inp):
        C, n, m = carry
        qc, kc, vc, lic, lfc = inp
        b = jnp.cumsum(lfc, axis=-1)
        dmat = b[..., :, None] - b[..., None, :] + lic[..., None, :]
        dmat = jnp.where(causal, dmat, -jnp.inf)
        inter = b + m[..., None]
        m_t = jnp.maximum(inter, jnp.max(dmat, axis=-1))
        w = jnp.exp(dmat - m_t[..., None])
        a_inter = jnp.exp(inter - m_t)
        s = jnp.einsum('bhtd,bhsd->bhts', qc, kc) * w
        num = a_inter[..., None] * jnp.einsum('bhtd,bhde->bhte', qc, C) + jnp.einsum('bhts,bhse->bhte', s, vc)
        den = a_inter * jnp.einsum('bhtd,bhd->bht', qc, n) + jnp.sum(s, axis=-1)
        h = num / jnp.maximum(jnp.abs(den), jnp.exp(-m_t))[..., None]
        bL = b[..., -1]
        g = bL[..., None] - b + lic
        m_new = jnp.maximum(bL + m, jnp.max(g, axis=-1))
        wk = jnp.exp(g - m_new[..., None])
        sc = jnp.exp(bL + m - m_new)
        C_new = sc[..., None, None] * C + jnp.einsum('bhsd,bhse->bhde', kc * wk[..., None], vc)
        n_new = sc[..., None] * n + jnp.einsum('bhs,bhsd->bhd', wk, kc)
        return (C_new, n_new, m_new), h

    xs = (to_chunks(q), to_chunks(k), to_chunks(v), to_chunks(li), to_chunks(lf))
    (C1, n1, m1), hs = lax.scan(step, (C0, n0, m0), xs)
    h = jnp.moveaxis(hs, (0, 3), (1, 2)).reshape(Bsz, S, H, DV)
    return h, C1, n1, m1


def _layer(x, conv_buf, C0, n0, m0, n1g, w_in, b_i, b_f, conv_w, conv_g, ml_g, w_out,
           n2g, wg, wu, wd):
    Bsz, S, _ = x.shape
    xn = _rmsnorm(x, n1g)
    z = xn @ w_in
    bg, cg, xt, q, k, v, o, ig, fg = jnp.split(z, _SPLIT_IDX, axis=-1)
    u = cg * xt
    pad = jnp.concatenate([conv_buf.astype(u.dtype), u], axis=1)
    conv = sum(pad[:, j:j + S] * conv_w[j] for j in range(CONV_W))
    new_buf = pad[:, -(CONV_W - 1):].astype(jnp.float32)
    y_conv = _rmsnorm(bg * conv, conv_g)
    f32 = jnp.float32
    qh = q.reshape(Bsz, S, ML_HEADS, ML_DK).astype(f32) * (ML_DK ** -0.5)
    kh = k.reshape(Bsz, S, ML_HEADS, ML_DK).astype(f32)
    vh = v.reshape(Bsz, S, ML_HEADS, ML_DV).astype(f32)
    li = (ig + b_i).astype(f32)
    lf = jax.nn.log_sigmoid((fg + b_f).astype(f32))
    h, C1, n1, m1 = _mlstm_chunked(qh, kh, vh, li, lf,
                                   C0.astype(f32), n0.astype(f32), m0.astype(f32))
    h = _rmsnorm(h.astype(x.dtype), ml_g.reshape(ML_HEADS, ML_DV)).reshape(Bsz, S, ML_HEADS * ML_DV)
    y_ml = jax.nn.sigmoid(o) * h
    x = x + jnp.concatenate([y_conv, y_ml], axis=-1) @ w_out
    hn = _rmsnorm(x, n2g)
    x = x + (jax.nn.silu(hn @ wg) * (hn @ wu)) @ wd
    return x, new_buf, C1, n1, m1


def setup_inputs(seed: int = 0) -> dict:
    key = jax.random.key(seed)
    ks = jax.random.split(key, 24)
    nrm = jax.random.normal
    f32 = jnp.float32
    return {
        "x_prompt": nrm(ks[0], (BATCH, SEQ, D_MODEL), f32),
        "x_sample": nrm(ks[1], (DEC_BATCH, DEC_SEQ, D_MODEL), f32),
        "state_conv": nrm(ks[2], (DEPTH, DEC_BATCH, CONV_W - 1, D_CONV), f32),
        "state_mlstm_C": 0.3 * nrm(ks[3], (DEPTH, DEC_BATCH, ML_HEADS, ML_DK, ML_DV), f32),
        "state_mlstm_n": 0.5 * nrm(ks[4], (DEPTH, DEC_BATCH, ML_HEADS, ML_DK), f32),
        "state_mlstm_m": 0.5 * nrm(ks[5], (DEPTH, DEC_BATCH, ML_HEADS), f32),
        "norm1_g": 1.0 + 0.02 * nrm(ks[6], (DEPTH, D_MODEL), f32),
        "w_in": nrm(ks[7], (DEPTH, D_MODEL, N_IN), f32) * D_MODEL ** -0.5,
        "b_igate": -2.0 + 0.1 * nrm(ks[8], (DEPTH, ML_HEADS), f32),
        "b_fgate": jnp.linspace(3.0, 6.0, ML_HEADS, dtype=f32)[None] + 0.1 * nrm(ks[9], (DEPTH, ML_HEADS), f32),
        "conv_w": nrm(ks[10], (DEPTH, CONV_W, D_CONV), f32) * CONV_W ** -0.5,
        "conv_out_g": 1.0 + 0.02 * nrm(ks[11], (DEPTH, D_CONV), f32),
        "mlstm_out_g": 1.0 + 0.02 * nrm(ks[12], (DEPTH, ML_HEADS * ML_DV), f32),
        "w_out": nrm(ks[13], (DEPTH, D_MIX, D_MODEL), f32) * D_MIX ** -0.5,
        "norm2_g": 1.0 + 0.02 * nrm(ks[14], (DEPTH, D_MODEL), f32),
        "w_ffn_gate": nrm(ks[15], (DEPTH, D_MODEL, D_FF), f32) * D_MODEL ** -0.5,
        "w_ffn_up": nrm(ks[16], (DEPTH, D_MODEL, D_FF), f32) * D_MODEL ** -0.5,
        "w_ffn_down": nrm(ks[17], (DEPTH, D_FF, D_MODEL), f32) * D_FF ** -0.5,
        "final_norm_g": 1.0 + 0.02 * nrm(ks[18], (D_MODEL,), f32),
    }


def reference(x_prompt, x_sample, state_conv, state_mlstm_C, state_mlstm_n, state_mlstm_m,
              norm1_g, w_in, b_igate, b_fgate, conv_w, conv_out_g, mlstm_out_g, w_out,
              norm2_g, w_ffn_gate, w_ffn_up, w_ffn_down, final_norm_g):
    hp, hs = x_prompt, x_sample
    bp = x_prompt.shape[0]
    p_conv, p_C, p_n, p_m = [], [], [], []
    s_conv, s_C, s_n, s_m = [], [], [], []
    for l in range(DEPTH):
        params = (norm1_g[l], w_in[l], b_igate[l], b_fgate[l], conv_w[l], conv_out_g[l],
                  mlstm_out_g[l], w_out[l], norm2_g[l], w_ffn_gate[l], w_ffn_up[l], w_ffn_down[l])
        zb = jnp.zeros((bp, CONV_W - 1, D_CONV), jnp.float32)
        zC = jnp.zeros((bp, ML_HEADS, ML_DK, ML_DV), jnp.float32)
        zn = jnp.zeros((bp, ML_HEADS, ML_DK), jnp.float32)
        zm = jnp.zeros((bp, ML_HEADS), jnp.float32)
        hp, cb, C1, n1, m1 = _layer(hp, zb, zC, zn, zm, *params)
        p_conv.append(cb); p_C.append(C1); p_n.append(n1); p_m.append(m1)
        hs, cb, C1, n1, m1 = _layer(hs, state_conv[l], state_mlstm_C[l], state_mlstm_n[l],
                                    state_mlstm_m[l], *params)
        s_conv.append(cb); s_C.append(C1); s_n.append(n1); s_m.append(m1)
    y_prompt = _rmsnorm(hp, final_norm_g)
    y_sample = _rmsnorm(hs, final_norm_g)
    return (y_prompt, y_sample,
            jnp.stack(p_conv), jnp.stack(p_C), jnp.stack(p_n), jnp.stack(p_m),
            jnp.stack(s_conv), jnp.stack(s_C), jnp.stack(s_n), jnp.stack(s_m))
```

```python
import functools

import jax
import jax.numpy as jnp
from jax import lax
from jax.experimental import pallas as pl
from jax.experimental.pallas import tpu as pltpu

F32 = jnp.float32
BF16 = jnp.bfloat16

D_MODEL = 2048
D_CONV = 1024
CONV_W = 3
N_HEADS = 4
D_K = 128
D_V = 256
D_FF = 5632
EPS = 1e-6

OFF_BG, OFF_CG, OFF_XT = 0, D_CONV, 2 * D_CONV
OFF_Q = 3 * D_CONV
OFF_K = OFF_Q + N_HEADS * D_K
OFF_V = OFF_K + N_HEADS * D_K
OFF_O = OFF_V + N_HEADS * D_V
N_MAIN = OFF_O + N_HEADS * D_V

LANES = 128
SUBLANES = 8
GATE_ROWS = 16
MIB = 1024 * 1024

PROMPT_CHUNK = 256
SAMPLE_GROUP = 16
NORM_ROWS = 64


def _compiler_params(semantics, vmem_mib):
    return pltpu.CompilerParams(dimension_semantics=semantics,
                                vmem_limit_bytes=vmem_mib * MIB)


def _rmsnorm_rows(src_ref, gain, dst_ref):
    rows = src_ref.shape[0]
    chunk = min(NORM_ROWS, rows)

    def body(i, carry):
        r = pl.multiple_of(i * chunk, chunk)
        x = src_ref[pl.ds(r, chunk), :].astype(F32)
        ms = jnp.mean(x * x, axis=-1, keepdims=True)
        dst_ref[pl.ds(r, chunk), :] = ((x * lax.rsqrt(ms + EPS)) * gain).astype(dst_ref.dtype)
        return carry

    lax.fori_loop(0, rows // chunk, body, 0)


def _inproj_kernel(x_ref, g_ref, w_ref, wgate_ref, wgate_t_ref,
                   z_ref, gcol_ref, grow_ref, xn_ref):
    @pl.when(pl.program_id(1) == 0)
    def _():
        _rmsnorm_rows(x_ref, g_ref[...], xn_ref)
        xn = xn_ref[...]
        gcol_ref[...] = jnp.dot(xn, wgate_ref[...].astype(BF16), preferred_element_type=F32)
        grow_ref[...] = lax.dot_general(wgate_t_ref[...].astype(BF16), xn,
                                        (((1,), (1,)), ((), ())), preferred_element_type=F32)

    z_ref[...] = jnp.dot(xn_ref[...], w_ref[...].astype(BF16), preferred_element_type=F32)


def _inproj(x2d, g1, w_in, wgate, wgate_t):
    tokens = x2d.shape[0]
    tm = min(1024, tokens)
    tn = 512
    return pl.pallas_call(
        _inproj_kernel,
        grid=(tokens // tm, N_MAIN // tn),
        in_specs=[
            pl.BlockSpec((tm, D_MODEL), lambda i, j: (i, 0)),
            pl.BlockSpec((1, D_MODEL), lambda i, j: (0, 0)),
            pl.BlockSpec((D_MODEL, tn), lambda i, j: (0, j)),
            pl.BlockSpec((D_MODEL, LANES), lambda i, j: (0, 0)),
            pl.BlockSpec((GATE_ROWS, D_MODEL), lambda i, j: (0, 0)),
        ],
        out_specs=[
            pl.BlockSpec((tm, tn), lambda i, j: (i, j)),
            pl.BlockSpec((tm, LANES), lambda i, j: (i, 0)),
            pl.BlockSpec((GATE_ROWS, tm), lambda i, j: (0, i)),
        ],
        out_shape=[
            jax.ShapeDtypeStruct((tokens, N_MAIN), F32),
            jax.ShapeDtypeStruct((tokens, LANES), F32),
            jax.ShapeDtypeStruct((GATE_ROWS, tokens), F32),
        ],
        scratch_shapes=[pltpu.VMEM((tm, D_MODEL), BF16)],
        compiler_params=_compiler_params(("parallel", "arbitrary"), 48),
        name="inproj",
    )(x2d, g1, w_in, wgate, wgate_t)


def _log_sigmoid(x):
    return jnp.minimum(x, 0.0) - jnp.log1p(jnp.exp(-jnp.abs(x)))


def _dot_exact(a, b):
    return jnp.dot(a, b, precision=lax.Precision.HIGHEST, preferred_element_type=F32)


def _intra(q_bf, k_bf, b_col, b_row, li_row, m0_col, mask):
    dmat = jnp.where(mask, b_col - b_row + li_row, -jnp.inf)
    inter = b_col + m0_col
    m_t = jnp.maximum(inter, jnp.max(dmat, axis=1, keepdims=True))
    w = jnp.exp(dmat - m_t)
    a_inter = jnp.exp(inter - m_t)
    qk = lax.dot_general(q_bf, k_bf, (((1,), (1,)), ((), ())), preferred_element_type=F32)
    return qk * w, a_inter, m_t


def _head_output(num, den, m_t, o_gate, gain):
    denom = jnp.maximum(jnp.abs(den), jnp.exp(-m_t))
    h = num * (1.0 / denom)
    hn = (h * lax.rsqrt(jnp.mean(h * h, axis=1, keepdims=True) + EPS)) * gain
    return jax.nn.sigmoid(o_gate) * hn


def _conv_norm(bg, conv, gain):
    yc = bg * conv
    return (yc * lax.rsqrt(jnp.mean(yc * yc, axis=1, keepdims=True) + EPS)) * gain


def _mix_prompt_kernel(z_ref, gcol_ref, grow_ref, convw_ref, convg_ref, mlg_ref,
                       bias_row_ref, bias_col_ref,
                       y_ref, convs_ref, c_out_ref, n_out_ref, m_out_ref,
                       ubuf, c_scr, n_scr, m_scr):
    chunk = pl.program_id(1)
    last = pl.num_programs(1) - 1
    L = z_ref.shape[0]

    @pl.when(chunk == 0)
    def _():
        ubuf[0:SUBLANES, :] = jnp.zeros((SUBLANES, D_CONV), F32)
        c_scr[...] = jnp.zeros(c_scr.shape, F32)
        n_scr[...] = jnp.zeros(n_scr.shape, F32)
        m_scr[...] = jnp.zeros(m_scr.shape, F32)

    u = z_ref[:, OFF_CG:OFF_CG + D_CONV] * z_ref[:, OFF_XT:OFF_XT + D_CONV]
    ubuf[SUBLANES:SUBLANES + L, :] = u
    conv = (ubuf[SUBLANES - 2:SUBLANES - 2 + L, :] * convw_ref[0:1, :]
            + ubuf[SUBLANES - 1:SUBLANES - 1 + L, :] * convw_ref[1:2, :]
            + u * convw_ref[2:3, :])
    y_ref[:, 0:D_CONV] = _conv_norm(z_ref[:, OFF_BG:OFF_BG + D_CONV], conv,
                                    convg_ref[...]).astype(y_ref.dtype)
    ubuf[0:SUBLANES, :] = ubuf[L:L + SUBLANES, :]

    @pl.when(chunk == last)
    def _():
        convs_ref[0] = ubuf[SUBLANES - 2:SUBLANES, :]

    pre_c = gcol_ref[...] + bias_row_ref[...]
    pre_r = grow_ref[0:SUBLANES, :] + bias_col_ref[:, 0:1]
    row_id = lax.broadcasted_iota(jnp.int32, (L, L), 0)
    col_id = lax.broadcasted_iota(jnp.int32, (L, L), 1)
    causal = col_id <= row_id
    b_c = _dot_exact(causal.astype(F32), _log_sigmoid(pre_c))
    b_r = _dot_exact(_log_sigmoid(pre_r), (row_id <= col_id).astype(F32))

    for h in range(N_HEADS):
        li_r = pre_r[h:h + 1, :]
        li_c = pre_c[:, h:h + 1]
        bc = b_c[:, N_HEADS + h:N_HEADS + h + 1]
        br = b_r[N_HEADS + h:N_HEADS + h + 1, :]
        m0 = m_scr[h, 0:1, 0:1]
        qf = z_ref[:, OFF_Q + h * D_K:OFF_Q + (h + 1) * D_K] * (D_K ** -0.5)
        q_bf = qf.astype(BF16)
        kf = z_ref[:, OFF_K + h * D_K:OFF_K + (h + 1) * D_K]
        v_bf = z_ref[:, OFF_V + h * D_V:OFF_V + (h + 1) * D_V].astype(BF16)
        s, a_inter, m_t = _intra(q_bf, kf.astype(BF16), bc, br, li_r, m0, causal)
        c_old = c_scr[h]
        n_old = n_scr[h]
        num = (a_inter * jnp.dot(q_bf, c_old.astype(BF16), preferred_element_type=F32)
               + jnp.dot(s.astype(BF16), v_bf, preferred_element_type=F32))
        den = (a_inter * jnp.sum(qf * n_old, axis=1, keepdims=True)
               + jnp.sum(s, axis=1, keepdims=True))
        y_ref[:, D_CONV + h * D_V:D_CONV + (h + 1) * D_V] = _head_output(
            num, den, m_t, z_ref[:, OFF_O + h * D_V:OFF_O + (h + 1) * D_V],
            mlg_ref[:, h * D_V:(h + 1) * D_V]).astype(y_ref.dtype)

        b_last = br[:, L - 1:L]
        m_new = jnp.maximum(b_last + m0,
                            jnp.max(b_last - br + li_r, axis=1, keepdims=True))
        decay = jnp.exp(b_last + m0 - m_new)
        kw = kf * jnp.exp(b_last - bc + li_c - m_new)
        c_scr[h] = decay * c_old + lax.dot_general(
            kw.astype(BF16), v_bf, (((0,), (0,)), ((), ())), preferred_element_type=F32)
        n_scr[h] = decay * n_old + jnp.sum(kw, axis=0, keepdims=True)
        m_scr[h] = jnp.broadcast_to(m_new, (SUBLANES, LANES))

    @pl.when(chunk == last)
    def _():
        c_out_ref[0] = c_scr[...]
        for h in range(N_HEADS):
            n_out_ref[0, h:h + 1, :] = n_scr[h]
            m_out_ref[0, h:h + 1, :] = m_scr[h, 0:1, :]


def _mix_prompt(z, gcol, grow, batch, seq, conv_w, conv_g, ml_g, bias_row, bias_col):
    L = PROMPT_CHUNK
    nc = seq // L
    const = lambda b, c: (0, 0)
    return pl.pallas_call(
        _mix_prompt_kernel,
        grid=(batch, nc),
        in_specs=[
            pl.BlockSpec((L, N_MAIN), lambda b, c: (b * nc + c, 0)),
            pl.BlockSpec((L, LANES), lambda b, c: (b * nc + c, 0)),
            pl.BlockSpec((GATE_ROWS, L), lambda b, c: (0, b * nc + c)),
            pl.BlockSpec((CONV_W, D_CONV), const),
            pl.BlockSpec((1, D_CONV), const),
            pl.BlockSpec((1, N_HEADS * D_V), const),
            pl.BlockSpec((1, LANES), const),
            pl.BlockSpec((SUBLANES, LANES), const),
        ],
        out_specs=[
            pl.BlockSpec((L, D_MODEL), lambda b, c: (b * nc + c, 0)),
            pl.BlockSpec((1, CONV_W - 1, D_CONV), lambda b, c: (b, 0, 0)),
            pl.BlockSpec((1, N_HEADS, D_K, D_V), lambda b, c: (b, 0, 0, 0)),
            pl.BlockSpec((1, N_HEADS, D_K), lambda b, c: (b, 0, 0)),
            pl.BlockSpec((1, N_HEADS, LANES), lambda b, c: (b, 0, 0)),
        ],
        out_shape=[
            jax.ShapeDtypeStruct((batch * seq, D_MODEL), BF16),
            jax.ShapeDtypeStruct((batch, CONV_W - 1, D_CONV), F32),
            jax.ShapeDtypeStruct((batch, N_HEADS, D_K, D_V), F32),
            jax.ShapeDtypeStruct((batch, N_HEADS, D_K), F32),
            jax.ShapeDtypeStruct((batch, N_HEADS, LANES), F32),
        ],
        scratch_shapes=[
            pltpu.VMEM((L + 2 * SUBLANES, D_CONV), F32),
            pltpu.VMEM((N_HEADS, D_K, D_V), F32),
            pltpu.VMEM((N_HEADS, 1, D_K), F32),
            pltpu.VMEM((N_HEADS, SUBLANES, LANES), F32),
        ],
        compiler_params=_compiler_params(("parallel", "arbitrary"), 48),
        name="mix_prompt",
    )(z, gcol, grow, conv_w, conv_g, ml_g, bias_row, bias_col)


def _mix_sample_kernel(z_ref, gcol_ref, grow_ref, mcol_ref, mrow_ref,
                       convs_in_ref, c_in_ref, n_in_ref,
                       convw_ref, convg_ref, mlg_ref, bias_row_ref, bias_col_ref,
                       y_ref, convs_ref, c_out_ref, n_out_ref, m_out_ref,
                       pad_scr, conv_scr, *, seq):
    L = z_ref.shape[0]
    group = L // seq
    stride = seq + SUBLANES

    for g in range(group):
        u = (z_ref[g * seq:(g + 1) * seq, OFF_CG:OFF_CG + D_CONV]
             * z_ref[g * seq:(g + 1) * seq, OFF_XT:OFF_XT + D_CONV])
        base = g * stride
        pad_scr[base + SUBLANES - 2:base + SUBLANES, :] = convs_in_ref[g]
        pad_scr[base + SUBLANES:base + SUBLANES + seq, :] = u
        conv_scr[g * seq:(g + 1) * seq, :] = (
            pad_scr[base + SUBLANES - 2:base + SUBLANES - 2 + seq, :] * convw_ref[0:1, :]
            + pad_scr[base + SUBLANES - 1:base + SUBLANES - 1 + seq, :] * convw_ref[1:2, :]
            + u * convw_ref[2:3, :])
        convs_ref[g] = pad_scr[base + seq + SUBLANES - 2:base + seq + SUBLANES, :]
    y_ref[:, 0:D_CONV] = _conv_norm(z_ref[:, OFF_BG:OFF_BG + D_CONV], conv_scr[...],
                                    convg_ref[...]).astype(y_ref.dtype)

    pre_c = gcol_ref[...] + bias_row_ref[...]
    pre_r = grow_ref[0:SUBLANES, :] + bias_col_ref[:, 0:1]
    row_id = lax.broadcasted_iota(jnp.int32, (L, L), 0)
    col_id = lax.broadcasted_iota(jnp.int32, (L, L), 1)
    same = (row_id // seq) == (col_id // seq)
    causal = same & (col_id <= row_id)
    lf_c = _log_sigmoid(pre_c)
    lf_r = _log_sigmoid(pre_r)
    b_c = _dot_exact(causal.astype(F32), lf_c)
    b_r = _dot_exact(lf_r, (same & (row_id <= col_id)).astype(F32))
    same_f = same.astype(F32)
    tot_c = _dot_exact(same_f, lf_c)
    tot_r = _dot_exact(lf_r, same_f)
    lane_seq = lax.broadcasted_iota(jnp.int32, (1, L), 1) // seq

    for h in range(N_HEADS):
        li_r = pre_r[h:h + 1, :]
        li_c = pre_c[:, h:h + 1]
        bc = b_c[:, N_HEADS + h:N_HEADS + h + 1]
        br = b_r[N_HEADS + h:N_HEADS + h + 1, :]
        bl_c = tot_c[:, N_HEADS + h:N_HEADS + h + 1]
        bl_r = tot_r[N_HEADS + h:N_HEADS + h + 1, :]
        m0_c = mcol_ref[:, h:h + 1]
        m0_r = mrow_ref[h:h + 1, :]
        qf = z_ref[:, OFF_Q + h * D_K:OFF_Q + (h + 1) * D_K] * (D_K ** -0.5)
        q_bf = qf.astype(BF16)
        kf = z_ref[:, OFF_K + h * D_K:OFF_K + (h + 1) * D_K]
        v_bf = z_ref[:, OFF_V + h * D_V:OFF_V + (h + 1) * D_V].astype(BF16)
        s, a_inter, m_t = _intra(q_bf, kf.astype(BF16), bc, br, li_r, m0_c, causal)

        qc_rows, qn_rows = [], []
        for g in range(group):
            rows = slice(g * seq, (g + 1) * seq)
            qc_rows.append(jnp.dot(qf[rows], c_in_ref[g, h], preferred_element_type=F32))
            qn_rows.append(jnp.sum(qf[rows] * n_in_ref[g, h:h + 1, :], axis=1, keepdims=True))
        num = (a_inter * jnp.concatenate(qc_rows, axis=0)
               + jnp.dot(s.astype(BF16), v_bf, preferred_element_type=F32))
        den = (a_inter * jnp.concatenate(qn_rows, axis=0)
               + jnp.sum(s, axis=1, keepdims=True))
        y_ref[:, D_CONV + h * D_V:D_CONV + (h + 1) * D_V] = _head_output(
            num, den, m_t, z_ref[:, OFF_O + h * D_V:OFF_O + (h + 1) * D_V],
            mlg_ref[:, h * D_V:(h + 1) * D_V]).astype(y_ref.dtype)

        g_r = bl_r - br + li_r
        gmax_c = jnp.max(jnp.where(same, g_r, -jnp.inf), axis=1, keepdims=True)
        gmax_r = jnp.max(jnp.where(same, gmax_c, -jnp.inf), axis=0, keepdims=True)
        m_new_c = jnp.maximum(bl_c + m0_c, gmax_c)
        m_new_r = jnp.maximum(bl_r + m0_r, gmax_r)
        decay_r = jnp.exp(bl_r + m0_r - m_new_r)
        kw = kf * jnp.exp(bl_c - bc + li_c - m_new_c)
        kw_t = kw.T
        for g in range(group):
            decay = decay_r[:, g * seq:g * seq + 1]
            kw_g = jnp.where(lane_seq == g, kw_t, 0.0).astype(BF16)
            c_out_ref[g, h] = decay * c_in_ref[g, h] + jnp.dot(
                kw_g, v_bf, preferred_element_type=F32)
            n_out_ref[g, h:h + 1, :] = (decay * n_in_ref[g, h:h + 1, :]
                                        + jnp.sum(kw[g * seq:(g + 1) * seq], axis=0, keepdims=True))
        m_out_ref[h:h + 1, :] = m_new_r
    m_out_ref[N_HEADS:, :] = jnp.zeros((SUBLANES - N_HEADS, L), F32)


def _mix_sample(z, gcol, grow, mcol, mrow, convs, c_state, n_state, batch, seq,
                conv_w, conv_g, ml_g, bias_row, bias_col):
    group = SAMPLE_GROUP
    L = group * seq
    const = lambda i: (0, 0)
    return pl.pallas_call(
        functools.partial(_mix_sample_kernel, seq=seq),
        grid=(batch // group,),
        in_specs=[
            pl.BlockSpec((L, N_MAIN), lambda i: (i, 0)),
            pl.BlockSpec((L, LANES), lambda i: (i, 0)),
            pl.BlockSpec((GATE_ROWS, L), lambda i: (0, i)),
            pl.BlockSpec((L, LANES), lambda i: (i, 0)),
            pl.BlockSpec((SUBLANES, L), lambda i: (0, i)),
            pl.BlockSpec((group, CONV_W - 1, D_CONV), lambda i: (i, 0, 0)),
            pl.BlockSpec((group, N_HEADS, D_K, D_V), lambda i: (i, 0, 0, 0)),
            pl.BlockSpec((group, N_HEADS, D_K), lambda i: (i, 0, 0)),
            pl.BlockSpec((CONV_W, D_CONV), const),
            pl.BlockSpec((1, D_CONV), const),
            pl.BlockSpec((1, N_HEADS * D_V), const),
            pl.BlockSpec((1, LANES), const),
            pl.BlockSpec((SUBLANES, LANES), const),
        ],
        out_specs=[
            pl.BlockSpec((L, D_MODEL), lambda i: (i, 0)),
            pl.BlockSpec((group, CONV_W - 1, D_CONV), lambda i: (i, 0, 0)),
            pl.BlockSpec((group, N_HEADS, D_K, D_V), lambda i: (i, 0, 0, 0)),
            pl.BlockSpec((group, N_HEADS, D_K), lambda i: (i, 0, 0)),
            pl.BlockSpec((SUBLANES, L), lambda i: (0, i)),
        ],
        out_shape=[
            jax.ShapeDtypeStruct((batch * seq, D_MODEL), BF16),
            jax.ShapeDtypeStruct((batch, CONV_W - 1, D_CONV), F32),
            jax.ShapeDtypeStruct((batch, N_HEADS, D_K, D_V), F32),
            jax.ShapeDtypeStruct((batch, N_HEADS, D_K), F32),
            jax.ShapeDtypeStruct((SUBLANES, batch * seq), F32),
        ],
        scratch_shapes=[
            pltpu.VMEM((group * (seq + SUBLANES), D_CONV), F32),
            pltpu.VMEM((L, D_CONV), F32),
        ],
        compiler_params=_compiler_params(("parallel",), 56),
        name="mix_sample",
    )(z, gcol, grow, mcol, mrow, convs, c_state, n_state,
      conv_w, conv_g, ml_g, bias_row, bias_col)


def _outproj_kernel(y_ref, w_ref, x_ref, o_ref):
    o_ref[...] = x_ref[...] + jnp.dot(y_ref[...], w_ref[...].astype(BF16),
                                      preferred_element_type=F32)


def _outproj(y, w_out, x2d):
    tokens = x2d.shape[0]
    tm = min(1024, tokens)
    tn = 512
    return pl.pallas_call(
        _outproj_kernel,
        grid=(tokens // tm, D_MODEL // tn),
        in_specs=[
            pl.BlockSpec((tm, D_MODEL), lambda i, j: (i, 0)),
            pl.BlockSpec((D_MODEL, tn), lambda i, j: (0, j)),
            pl.BlockSpec((tm, tn), lambda i, j: (i, j)),
        ],
        out_specs=pl.BlockSpec((tm, tn), lambda i, j: (i, j)),
        out_shape=jax.ShapeDtypeStruct((tokens, D_MODEL), F32),
        compiler_params=_compiler_params(("parallel", "arbitrary"), 48),
        name="outproj",
    )(y, w_out, x2d)


def _ffn_kernel(x_ref, g2_ref, wg_ref, wu_ref, wd_ref, gf_ref, o_ref, hn_ref, *, final_norm):
    j = pl.program_id(1)

    @pl.when(j == 0)
    def _():
        _rmsnorm_rows(x_ref, g2_ref[...], hn_ref)
        o_ref[...] = x_ref[...]

    hn = hn_ref[...]
    gate = jnp.dot(hn, wg_ref[...].astype(BF16), preferred_element_type=F32)
    up = jnp.dot(hn, wu_ref[...].astype(BF16), preferred_element_type=F32)
    act = (gate * jax.nn.sigmoid(gate)) * up
    o_ref[...] += jnp.dot(act.astype(BF16), wd_ref[...].astype(BF16),
                          preferred_element_type=F32)

    if final_norm:
        @pl.when(j == pl.num_programs(1) - 1)
        def _():
            _rmsnorm_rows(o_ref, gf_ref[...], o_ref)


def _ffn(x1, g2, wg, wu, wd, gf, final_norm):
    tokens = x1.shape[0]
    tm = min(1024, tokens)
    tf = 256
    return pl.pallas_call(
        functools.partial(_ffn_kernel, final_norm=final_norm),
        grid=(tokens // tm, D_FF // tf),
        in_specs=[
            pl.BlockSpec((tm, D_MODEL), lambda i, j: (i, 0)),
            pl.BlockSpec((1, D_MODEL), lambda i, j: (0, 0)),
            pl.BlockSpec((D_MODEL, tf), lambda i, j: (0, j)),
            pl.BlockSpec((D_MODEL, tf), lambda i, j: (0, j)),
            pl.BlockSpec((tf, D_MODEL), lambda i, j: (j, 0)),
            pl.BlockSpec((1, D_MODEL), lambda i, j: (0, 0)),
        ],
        out_specs=pl.BlockSpec((tm, D_MODEL), lambda i, j: (i, 0)),
        out_shape=jax.ShapeDtypeStruct((tokens, D_MODEL), F32),
        scratch_shapes=[pltpu.VMEM((tm, D_MODEL), BF16)],
        compiler_params=_compiler_params(("parallel", "arbitrary"), 58),
        name="ffn",
    )(x1, g2, wg, wu, wd, gf)


def _gate_params(w_in, b_i, b_f):
    wgate = jnp.pad(w_in[:, N_MAIN:], ((0, 0), (0, LANES - 2 * N_HEADS)))
    wgate_t = jnp.pad(w_in[:, N_MAIN:].T, ((0, GATE_ROWS - 2 * N_HEADS), (0, 0)))
    bias = jnp.concatenate([b_i, b_f]).astype(F32)
    bias_row = jnp.pad(bias, (0, LANES - 2 * N_HEADS))[None, :]
    bias_col = jnp.broadcast_to(bias[:, None], (SUBLANES, LANES))
    return wgate, wgate_t, bias_row, bias_col


def _tail(x2d, y, w_out, g2, wg, wu, wd, gf, final_norm):
    x1 = _outproj(y, w_out, x2d)
    return _ffn(x1, g2[None, :], wg, wu, wd, gf[None, :], final_norm)


def kernel(x_prompt, x_sample, state_conv, state_mlstm_C, state_mlstm_n, state_mlstm_m,
           norm1_g, w_in, b_igate, b_fgate, conv_w, conv_out_g, mlstm_out_g, w_out,
           norm2_g, w_ffn_gate, w_ffn_up, w_ffn_down, final_norm_g):
    depth = w_in.shape[0]
    bp, sp, _ = x_prompt.shape
    bs, ss, _ = x_sample.shape
    hp = x_prompt.reshape(bp * sp, D_MODEL)
    hs = x_sample.reshape(bs * ss, D_MODEL)
    p_states, s_states = [], []
    for l in range(depth):
        final_norm = l == depth - 1
        wgate, wgate_t, bias_row, bias_col = _gate_params(w_in[l], b_igate[l], b_fgate[l])
        g1 = norm1_g[l][None, :]
        conv_g = conv_out_g[l][None, :]
        ml_g = mlstm_out_g[l][None, :]
        tail = (w_out[l], norm2_g[l], w_ffn_gate[l], w_ffn_up[l], w_ffn_down[l],
                final_norm_g, final_norm)

        z, gcol, grow = _inproj(hp, g1, w_in[l], wgate, wgate_t)
        y, cb, c1, n1, m1 = _mix_prompt(z, gcol, grow, bp, sp, conv_w[l], conv_g, ml_g,
                                        bias_row, bias_col)
        hp = _tail(hp, y, *tail)
        p_states.append((cb, c1, n1, m1[:, :, 0]))

        z, gcol, grow = _inproj(hs, g1, w_in[l], wgate, wgate_t)
        m_tok = jnp.repeat(state_mlstm_m[l], ss, axis=0)
        mcol = jnp.pad(m_tok, ((0, 0), (0, LANES - N_HEADS)))
        mrow = jnp.pad(m_tok.T, ((0, SUBLANES - N_HEADS), (0, 0)))
        y, cb, c1, n1, m_row = _mix_sample(z, gcol, grow, mcol, mrow, state_conv[l],
                                           state_mlstm_C[l], state_mlstm_n[l], bs, ss,
                                           conv_w[l], conv_g, ml_g, bias_row, bias_col)
        hs = _tail(hs, y, *tail)
        s_states.append((cb, c1, n1, m_row[:N_HEADS, ::ss].T))

    stack = lambda states, k: jnp.stack([st[k] for st in states])
    return (hp.reshape(bp, sp, D_MODEL), hs.reshape(bs, ss, D_MODEL),
            stack(p_states, 0), stack(p_states, 1), stack(p_states, 2), stack(p_states, 3),
            stack(s_states, 0), stack(s_states, 1), stack(s_states, 2), stack(s_states, 3))
```

```python
import functools

import jax
import jax.numpy as jnp
from jax import lax
from jax.experimental import pallas as pl
from jax.experimental.pallas import tpu as pltpu

F32 = jnp.float32
BF16 = jnp.bfloat16

D_MODEL = 2048
D_CONV = 1024
CONV_W = 3
N_HEADS = 4
D_K = 128
D_V = 256
D_FF = 5632
EPS = 1e-6

OFF_BG, OFF_CG, OFF_XT = 0, D_CONV, 2 * D_CONV
OFF_Q = 3 * D_CONV
OFF_K = OFF_Q + N_HEADS * D_K
OFF_V = OFF_K + N_HEADS * D_K
OFF_O = OFF_V + N_HEADS * D_V
N_MAIN = OFF_O + N_HEADS * D_V

LANES = 128
SUBLANES = 8
GATE_ROWS = 16
MIB = 1024 * 1024

PROMPT_CHUNK = 256
SAMPLE_GROUP = 16
NORM_ROWS = 64


def _compiler_params(semantics, vmem_mib):
    return pltpu.CompilerParams(dimension_semantics=semantics,
                                vmem_limit_bytes=vmem_mib * MIB)


def _rmsnorm_rows(src_ref, gain, dst_ref):
    rows = src_ref.shape[0]
    chunk = min(NORM_ROWS, rows)

    def body(i, carry):
        r = pl.multiple_of(i * chunk, chunk)
        x = src_ref[pl.ds(r, chunk), :].astype(F32)
        ms = jnp.mean(x * x, axis=-1, keepdims=True)
        dst_ref[pl.ds(r, chunk), :] = ((x * lax.rsqrt(ms + EPS)) * gain).astype(dst_ref.dtype)
        return carry

    lax.fori_loop(0, rows // chunk, body, 0)


def _resident_weight_spec(rows, cols, n_tiles, axis):
    def index_map(i, j):
        t = jnp.where(i == 0, j, n_tiles - 1)
        return (0, t) if axis == 1 else (t, 0)
    return pl.BlockSpec((rows, cols), index_map, pipeline_mode=pl.Buffered(1))


def _inproj_kernel(x_ref, g_ref, w_ref, wgate_ref, wgate_t_ref,
                   z_ref, gcol_ref, grow_ref, xn_ref, w_res):
    i = pl.program_id(0)
    j = pl.program_id(1)

    @pl.when(i == 0)
    def _():
        w_res[j] = w_ref[...].astype(BF16)

    @pl.when(j == 0)
    def _():
        _rmsnorm_rows(x_ref, g_ref[...], xn_ref)
        xn = xn_ref[...]
        gcol_ref[...] = jnp.dot(xn, wgate_ref[...].astype(BF16), preferred_element_type=F32)
        grow_ref[...] = lax.dot_general(wgate_t_ref[...].astype(BF16), xn,
                                        (((1,), (1,)), ((), ())), preferred_element_type=F32)

    z_ref[...] = jnp.dot(xn_ref[...], w_res[j], preferred_element_type=F32)


def _inproj(x2d, g1, w_in, wgate, wgate_t):
    tokens = x2d.shape[0]
    tm = min(1024, tokens)
    tn = 512
    n_tiles = N_MAIN // tn
    return pl.pallas_call(
        _inproj_kernel,
        grid=(tokens // tm, n_tiles),
        in_specs=[
            pl.BlockSpec((tm, D_MODEL), lambda i, j: (i, 0)),
            pl.BlockSpec((1, D_MODEL), lambda i, j: (0, 0)),
            _resident_weight_spec(D_MODEL, tn, n_tiles, axis=1),
            pl.BlockSpec((D_MODEL, LANES), lambda i, j: (0, 0)),
            pl.BlockSpec((GATE_ROWS, D_MODEL), lambda i, j: (0, 0)),
        ],
        out_specs=[
            pl.BlockSpec((tm, tn), lambda i, j: (i, j)),
            pl.BlockSpec((tm, LANES), lambda i, j: (i, 0)),
            pl.BlockSpec((GATE_ROWS, tm), lambda i, j: (0, i)),
        ],
        out_shape=[
            jax.ShapeDtypeStruct((tokens, N_MAIN), F32),
            jax.ShapeDtypeStruct((tokens, LANES), F32),
            jax.ShapeDtypeStruct((GATE_ROWS, tokens), F32),
        ],
        scratch_shapes=[pltpu.VMEM((tm, D_MODEL), BF16),
                        pltpu.VMEM((n_tiles, D_MODEL, tn), BF16)],
        compiler_params=_compiler_params(("arbitrary", "arbitrary"), 58),
        name="inproj",
    )(x2d, g1, w_in, wgate, wgate_t)


def _log_sigmoid(x):
    return jnp.minimum(x, 0.0) - jnp.log1p(jnp.exp(-jnp.abs(x)))


def _dot_exact(a, b):
    return jnp.dot(a, b, precision=lax.Precision.HIGHEST, preferred_element_type=F32)


def _intra(q_bf, k_bf, b_col, b_row, li_row, m0_col, mask):
    dmat = jnp.where(mask, b_col - b_row + li_row, -jnp.inf)
    inter = b_col + m0_col
    m_t = jnp.maximum(inter, jnp.max(dmat, axis=1, keepdims=True))
    w = jnp.exp(dmat - m_t)
    a_inter = jnp.exp(inter - m_t)
    qk = lax.dot_general(q_bf, k_bf, (((1,), (1,)), ((), ())), preferred_element_type=F32)
    return qk * w, a_inter, m_t


def _head_output(num, den, m_t, o_gate, gain):
    denom = jnp.maximum(jnp.abs(den), jnp.exp(-m_t))
    h = num * (1.0 / denom)
    hn = (h * lax.rsqrt(jnp.mean(h * h, axis=1, keepdims=True) + EPS)) * gain
    return jax.nn.sigmoid(o_gate) * hn


def _conv_norm(bg, conv, gain):
    yc = bg * conv
    return (yc * lax.rsqrt(jnp.mean(yc * yc, axis=1, keepdims=True) + EPS)) * gain


def _mix_prompt_kernel(z_ref, gcol_ref, grow_ref, convw_ref, convg_ref, mlg_ref,
                       bias_row_ref, bias_col_ref,
                       y_ref, convs_ref, c_out_ref, n_out_ref, m_out_ref,
                       ubuf, c_scr, n_scr, m_scr):
    chunk = pl.program_id(1)
    last = pl.num_programs(1) - 1
    L = z_ref.shape[0]

    @pl.when(chunk == 0)
    def _():
        ubuf[0:SUBLANES, :] = jnp.zeros((SUBLANES, D_CONV), F32)
        c_scr[...] = jnp.zeros(c_scr.shape, F32)
        n_scr[...] = jnp.zeros(n_scr.shape, F32)
        m_scr[...] = jnp.zeros(m_scr.shape, F32)

    u = z_ref[:, OFF_CG:OFF_CG + D_CONV] * z_ref[:, OFF_XT:OFF_XT + D_CONV]
    ubuf[SUBLANES:SUBLANES + L, :] = u
    conv = (ubuf[SUBLANES - 2:SUBLANES - 2 + L, :] * convw_ref[0:1, :]
            + ubuf[SUBLANES - 1:SUBLANES - 1 + L, :] * convw_ref[1:2, :]
            + u * convw_ref[2:3, :])
    y_ref[:, 0:D_CONV] = _conv_norm(z_ref[:, OFF_BG:OFF_BG + D_CONV], conv,
                                    convg_ref[...]).astype(y_ref.dtype)
    ubuf[0:SUBLANES, :] = ubuf[L:L + SUBLANES, :]

    @pl.when(chunk == last)
    def _():
        convs_ref[0] = ubuf[SUBLANES - 2:SUBLANES, :]

    pre_c = gcol_ref[...] + bias_row_ref[...]
    pre_r = grow_ref[0:SUBLANES, :] + bias_col_ref[:, 0:1]
    row_id = lax.broadcasted_iota(jnp.int32, (L, L), 0)
    col_id = lax.broadcasted_iota(jnp.int32, (L, L), 1)
    causal = col_id <= row_id
    b_c = _dot_exact(causal.astype(F32), _log_sigmoid(pre_c))
    b_r = _dot_exact(_log_sigmoid(pre_r), (row_id <= col_id).astype(F32))

    for h in range(N_HEADS):
        li_r = pre_r[h:h + 1, :]
        li_c = pre_c[:, h:h + 1]
        bc = b_c[:, N_HEADS + h:N_HEADS + h + 1]
        br = b_r[N_HEADS + h:N_HEADS + h + 1, :]
        m0 = m_scr[h, 0:1, 0:1]
        qf = z_ref[:, OFF_Q + h * D_K:OFF_Q + (h + 1) * D_K] * (D_K ** -0.5)
        q_bf = qf.astype(BF16)
        kf = z_ref[:, OFF_K + h * D_K:OFF_K + (h + 1) * D_K]
        v_bf = z_ref[:, OFF_V + h * D_V:OFF_V + (h + 1) * D_V].astype(BF16)
        s, a_inter, m_t = _intra(q_bf, kf.astype(BF16), bc, br, li_r, m0, causal)
        c_old = c_scr[h]
        n_old = n_scr[h]
        num = (a_inter * jnp.dot(q_bf, c_old.astype(BF16), preferred_element_type=F32)
               + jnp.dot(s.astype(BF16), v_bf, preferred_element_type=F32))
        den = (a_inter * jnp.sum(qf * n_old, axis=1, keepdims=True)
               + jnp.sum(s, axis=1, keepdims=True))
        y_ref[:, D_CONV + h * D_V:D_CONV + (h + 1) * D_V] = _head_output(
            num, den, m_t, z_ref[:, OFF_O + h * D_V:OFF_O + (h + 1) * D_V],
            mlg_ref[:, h * D_V:(h + 1) * D_V]).astype(y_ref.dtype)

        b_last = br[:, L - 1:L]
        m_new = jnp.maximum(b_last + m0,
                            jnp.max(b_last - br + li_r, axis=1, keepdims=True))
        decay = jnp.exp(b_last + m0 - m_new)
        kw = kf * jnp.exp(b_last - bc + li_c - m_new)
        c_scr[h] = decay * c_old + lax.dot_general(
            kw.astype(BF16), v_bf, (((0,), (0,)), ((), ())), preferred_element_type=F32)
        n_scr[h] = decay * n_old + jnp.sum(kw, axis=0, keepdims=True)
        m_scr[h] = jnp.broadcast_to(m_new, (SUBLANES, LANES))

    @pl.when(chunk == last)
    def _():
        c_out_ref[0] = c_scr[...]
        for h in range(N_HEADS):
            n_out_ref[0, h:h + 1, :] = n_scr[h]
            m_out_ref[0, h:h + 1, :] = m_scr[h, 0:1, :]


def _mix_prompt(z, gcol, grow, batch, seq, conv_w, conv_g, ml_g, bias_row, bias_col):
    L = PROMPT_CHUNK
    nc = seq // L
    const = lambda b, c: (0, 0)
    return pl.pallas_call(
        _mix_prompt_kernel,
        grid=(batch, nc),
        in_specs=[
            pl.BlockSpec((L, N_MAIN), lambda b, c: (b * nc + c, 0)),
            pl.BlockSpec((L, LANES), lambda b, c: (b * nc + c, 0)),
            pl.BlockSpec((GATE_ROWS, L), lambda b, c: (0, b * nc + c)),
            pl.BlockSpec((CONV_W, D_CONV), const),
            pl.BlockSpec((1, D_CONV), const),
            pl.BlockSpec((1, N_HEADS * D_V), const),
            pl.BlockSpec((1, LANES), const),
            pl.BlockSpec((SUBLANES, LANES), const),
        ],
        out_specs=[
            pl.BlockSpec((L, D_MODEL), lambda b, c: (b * nc + c, 0)),
            pl.BlockSpec((1, CONV_W - 1, D_CONV), lambda b, c: (b, 0, 0)),
            pl.BlockSpec((1, N_HEADS, D_K, D_V), lambda b, c: (b, 0, 0, 0)),
            pl.BlockSpec((1, N_HEADS, D_K), lambda b, c: (b, 0, 0)),
            pl.BlockSpec((1, N_HEADS, LANES), lambda b, c: (b, 0, 0)),
        ],
        out_shape=[
            jax.ShapeDtypeStruct((batch * seq, D_MODEL), BF16),
            jax.ShapeDtypeStruct((batch, CONV_W - 1, D_CONV), F32),
            jax.ShapeDtypeStruct((batch, N_HEADS, D_K, D_V), F32),
            jax.ShapeDtypeStruct((batch, N_HEADS, D_K), F32),
            jax.ShapeDtypeStruct((batch, N_HEADS, LANES), F32),
        ],
        scratch_shapes=[
            pltpu.VMEM((L + 2 * SUBLANES, D_CONV), F32),
            pltpu.VMEM((N_HEADS, D_K, D_V), F32),
            pltpu.VMEM((N_HEADS, 1, D_K), F32),
            pltpu.VMEM((N_HEADS, SUBLANES, LANES), F32),
        ],
        compiler_params=_compiler_params(("parallel", "arbitrary"), 48),
        name="mix_prompt",
    )(z, gcol, grow, conv_w, conv_g, ml_g, bias_row, bias_col)


def _mix_sample_kernel(z_ref, gcol_ref, grow_ref, mcol_ref, mrow_ref,
                       convs_in_ref, c_in_ref, n_in_ref,
                       convw_ref, convg_ref, mlg_ref, bias_row_ref, bias_col_ref,
                       y_ref, convs_ref, c_out_ref, n_out_ref, m_out_ref,
                       pad_scr, conv_scr, *, seq):
    L = z_ref.shape[0]
    group = L // seq
    stride = seq + SUBLANES

    for g in range(group):
        u = (z_ref[g * seq:(g + 1) * seq, OFF_CG:OFF_CG + D_CONV]
             * z_ref[g * seq:(g + 1) * seq, OFF_XT:OFF_XT + D_CONV])
        base = g * stride
        pad_scr[base + SUBLANES - 2:base + SUBLANES, :] = convs_in_ref[g]
        pad_scr[base + SUBLANES:base + SUBLANES + seq, :] = u
        conv_scr[g * seq:(g + 1) * seq, :] = (
            pad_scr[base + SUBLANES - 2:base + SUBLANES - 2 + seq, :] * convw_ref[0:1, :]
            + pad_scr[base + SUBLANES - 1:base + SUBLANES - 1 + seq, :] * convw_ref[1:2, :]
            + u * convw_ref[2:3, :])
        convs_ref[g] = pad_scr[base + seq + SUBLANES - 2:base + seq + SUBLANES, :]
    y_ref[:, 0:D_CONV] = _conv_norm(z_ref[:, OFF_BG:OFF_BG + D_CONV], conv_scr[...],
                                    convg_ref[...]).astype(y_ref.dtype)

    pre_c = gcol_ref[...] + bias_row_ref[...]
    pre_r = grow_ref[0:SUBLANES, :] + bias_col_ref[:, 0:1]
    row_id = lax.broadcasted_iota(jnp.int32, (L, L), 0)
    col_id = lax.broadcasted_iota(jnp.int32, (L, L), 1)
    same = (row_id // seq) == (col_id // seq)
    causal = same & (col_id <= row_id)
    lf_c = _log_sigmoid(pre_c)
    lf_r = _log_sigmoid(pre_r)
    b_c = _dot_exact(causal.astype(F32), lf_c)
    b_r = _dot_exact(lf_r, (same & (row_id <= col_id)).astype(F32))
    same_f = same.astype(F32)
    tot_c = _dot_exact(same_f, lf_c)
    tot_r = _dot_exact(lf_r, same_f)
    lane_seq = lax.broadcasted_iota(jnp.int32, (1, L), 1) // seq

    for h in range(N_HEADS):
        li_r = pre_r[h:h + 1, :]
        li_c = pre_c[:, h:h + 1]
        bc = b_c[:, N_HEADS + h:N_HEADS + h + 1]
        br = b_r[N_HEADS + h:N_HEADS + h + 1, :]
        bl_c = tot_c[:, N_HEADS + h:N_HEADS + h + 1]
        bl_r = tot_r[N_HEADS + h:N_HEADS + h + 1, :]
        m0_c = mcol_ref[:, h:h + 1]
        m0_r = mrow_ref[h:h + 1, :]
        qf = z_ref[:, OFF_Q + h * D_K:OFF_Q + (h + 1) * D_K] * (D_K ** -0.5)
        q_bf = qf.astype(BF16)
        kf = z_ref[:, OFF_K + h * D_K:OFF_K + (h + 1) * D_K]
        v_bf = z_ref[:, OFF_V + h * D_V:OFF_V + (h + 1) * D_V].astype(BF16)
        s, a_inter, m_t = _intra(q_bf, kf.astype(BF16), bc, br, li_r, m0_c, causal)

        qc_rows, qn_rows = [], []
        for g in range(group):
            rows = slice(g * seq, (g + 1) * seq)
            qc_rows.append(jnp.dot(qf[rows], c_in_ref[g, h], preferred_element_type=F32))
            qn_rows.append(jnp.sum(qf[rows] * n_in_ref[g, h:h + 1, :], axis=1, keepdims=True))
        num = (a_inter * jnp.concatenate(qc_rows, axis=0)
               + jnp.dot(s.astype(BF16), v_bf, preferred_element_type=F32))
        den = (a_inter * jnp.concatenate(qn_rows, axis=0)
               + jnp.sum(s, axis=1, keepdims=True))
        y_ref[:, D_CONV + h * D_V:D_CONV + (h + 1) * D_V] = _head_output(
            num, den, m_t, z_ref[:, OFF_O + h * D_V:OFF_O + (h + 1) * D_V],
            mlg_ref[:, h * D_V:(h + 1) * D_V]).astype(y_ref.dtype)

        g_r = bl_r - br + li_r
        gmax_c = jnp.max(jnp.where(same, g_r, -jnp.inf), axis=1, keepdims=True)
        gmax_r = jnp.max(jnp.where(same, gmax_c, -jnp.inf), axis=0, keepdims=True)
        m_new_c = jnp.maximum(bl_c + m0_c, gmax_c)
        m_new_r = jnp.maximum(bl_r + m0_r, gmax_r)
        decay_r = jnp.exp(bl_r + m0_r - m_new_r)
        kw = kf * jnp.exp(bl_c - bc + li_c - m_new_c)
        kw_t = kw.T
        for g in range(group):
            decay = decay_r[:, g * seq:g * seq + 1]
            kw_g = jnp.where(lane_seq == g, kw_t, 0.0).astype(BF16)
            c_out_ref[g, h] = decay * c_in_ref[g, h] + jnp.dot(
                kw_g, v_bf, preferred_element_type=F32)
            n_out_ref[g, h:h + 1, :] = (decay * n_in_ref[g, h:h + 1, :]
                                        + jnp.sum(kw[g * seq:(g + 1) * seq], axis=0, keepdims=True))
        m_out_ref[h:h + 1, :] = m_new_r
    m_out_ref[N_HEADS:, :] = jnp.zeros((SUBLANES - N_HEADS, L), F32)


def _mix_sample(z, gcol, grow, mcol, mrow, convs, c_state, n_state, batch, seq,
                conv_w, conv_g, ml_g, bias_row, bias_col):
    group = SAMPLE_GROUP
    L = group * seq
    const = lambda i: (0, 0)
    return pl.pallas_call(
        functools.partial(_mix_sample_kernel, seq=seq),
        grid=(batch // group,),
        in_specs=[
            pl.BlockSpec((L, N_MAIN), lambda i: (i, 0)),
            pl.BlockSpec((L, LANES), lambda i: (i, 0)),
            pl.BlockSpec((GATE_ROWS, L), lambda i: (0, i)),
            pl.BlockSpec((L, LANES), lambda i: (i, 0)),
            pl.BlockSpec((SUBLANES, L), lambda i: (0, i)),
            pl.BlockSpec((group, CONV_W - 1, D_CONV), lambda i: (i, 0, 0)),
            pl.BlockSpec((group, N_HEADS, D_K, D_V), lambda i: (i, 0, 0, 0)),
            pl.BlockSpec((group, N_HEADS, D_K), lambda i: (i, 0, 0)),
            pl.BlockSpec((CONV_W, D_CONV), const),
            pl.BlockSpec((1, D_CONV), const),
            pl.BlockSpec((1, N_HEADS * D_V), const),
            pl.BlockSpec((1, LANES), const),
            pl.BlockSpec((SUBLANES, LANES), const),
        ],
        out_specs=[
            pl.BlockSpec((L, D_MODEL), lambda i: (i, 0)),
            pl.BlockSpec((group, CONV_W - 1, D_CONV), lambda i: (i, 0, 0)),
            pl.BlockSpec((group, N_HEADS, D_K, D_V), lambda i: (i, 0, 0, 0)),
            pl.BlockSpec((group, N_HEADS, D_K), lambda i: (i, 0, 0)),
            pl.BlockSpec((SUBLANES, L), lambda i: (0, i)),
        ],
        out_shape=[
            jax.ShapeDtypeStruct((batch * seq, D_MODEL), BF16),
            jax.ShapeDtypeStruct((batch, CONV_W - 1, D_CONV), F32),
            jax.ShapeDtypeStruct((batch, N_HEADS, D_K, D_V), F32),
            jax.ShapeDtypeStruct((batch, N_HEADS, D_K), F32),
            jax.ShapeDtypeStruct((SUBLANES, batch * seq), F32),
        ],
        scratch_shapes=[
            pltpu.VMEM((group * (seq + SUBLANES), D_CONV), F32),
            pltpu.VMEM((L, D_CONV), F32),
        ],
        compiler_params=_compiler_params(("parallel",), 56),
        name="mix_sample",
    )(z, gcol, grow, mcol, mrow, convs, c_state, n_state,
      conv_w, conv_g, ml_g, bias_row, bias_col)


def _outproj_kernel(y_ref, w_ref, x_ref, o_ref, w_res):
    j = pl.program_id(1)

    @pl.when(pl.program_id(0) == 0)
    def _():
        w_res[j] = w_ref[...].astype(BF16)

    o_ref[...] = x_ref[...] + jnp.dot(y_ref[...], w_res[j], preferred_element_type=F32)


def _outproj(y, w_out, x2d):
    tokens = x2d.shape[0]
    tm = min(1024, tokens)
    tn = 512
    n_tiles = D_MODEL // tn
    return pl.pallas_call(
        _outproj_kernel,
        grid=(tokens // tm, n_tiles),
        in_specs=[
            pl.BlockSpec((tm, D_MODEL), lambda i, j: (i, 0)),
            _resident_weight_spec(D_MODEL, tn, n_tiles, axis=1),
            pl.BlockSpec((tm, tn), lambda i, j: (i, j)),
        ],
        out_specs=pl.BlockSpec((tm, tn), lambda i, j: (i, j)),
        out_shape=jax.ShapeDtypeStruct((tokens, D_MODEL), F32),
        scratch_shapes=[pltpu.VMEM((n_tiles, D_MODEL, tn), BF16)],
        compiler_params=_compiler_params(("arbitrary", "arbitrary"), 48),
        name="outproj",
    )(y, w_out, x2d)


def _ffn_kernel(x_ref, g2_ref, wg_ref, wu_ref, wd_ref, gf_ref, o_ref, hn_ref, *, final_norm):
    j = pl.program_id(1)

    @pl.when(j == 0)
    def _():
        _rmsnorm_rows(x_ref, g2_ref[...], hn_ref)
        o_ref[...] = x_ref[...]

    hn = hn_ref[...]
    gate = jnp.dot(hn, wg_ref[...].astype(BF16), preferred_element_type=F32)
    up = jnp.dot(hn, wu_ref[...].astype(BF16), preferred_element_type=F32)
    act = (gate * jax.nn.sigmoid(gate)) * up
    o_ref[...] += jnp.dot(act.astype(BF16), wd_ref[...].astype(BF16),
                          preferred_element_type=F32)

    if final_norm:
        @pl.when(j == pl.num_programs(1) - 1)
        def _():
            _rmsnorm_rows(o_ref, gf_ref[...], o_ref)


def _ffn(x1, g2, wg, wu, wd, gf, final_norm):
    tokens = x1.shape[0]
    tm = min(1024, tokens)
    tf = 256
    return pl.pallas_call(
        functools.partial(_ffn_kernel, final_norm=final_norm),
        grid=(tokens // tm, D_FF // tf),
        in_specs=[
            pl.BlockSpec((tm, D_MODEL), lambda i, j: (i, 0)),
            pl.BlockSpec((1, D_MODEL), lambda i, j: (0, 0)),
            pl.BlockSpec((D_MODEL, tf), lambda i, j: (0, j)),
            pl.BlockSpec((D_MODEL, tf), lambda i, j: (0, j)),
            pl.BlockSpec((tf, D_MODEL), lambda i, j: (j, 0)),
            pl.BlockSpec((1, D_MODEL), lambda i, j: (0, 0)),
        ],
        out_specs=pl.BlockSpec((tm, D_MODEL), lambda i, j: (i, 0)),
        out_shape=jax.ShapeDtypeStruct((tokens, D_MODEL), F32),
        scratch_shapes=[pltpu.VMEM((tm, D_MODEL), BF16)],
        compiler_params=_compiler_params(("parallel", "arbitrary"), 58),
        name="ffn",
    )(x1, g2, wg, wu, wd, gf)


def _gate_params(w_in, b_i, b_f):
    wgate = jnp.pad(w_in[:, N_MAIN:], ((0, 0), (0, LANES - 2 * N_HEADS)))
    wgate_t = jnp.pad(w_in[:, N_MAIN:].T, ((0, GATE_ROWS - 2 * N_HEADS), (0, 0)))
    bias = jnp.concatenate([b_i, b_f]).astype(F32)
    bias_row = jnp.pad(bias, (0, LANES - 2 * N_HEADS))[None, :]
    bias_col = jnp.broadcast_to(bias[:, None], (SUBLANES, LANES))
    return wgate, wgate_t, bias_row, bias_col


def _tail(x2d, y, w_out, g2, wg, wu, wd, gf, final_norm):
    x1 = _outproj(y, w_out, x2d)
    return _ffn(x1, g2[None, :], wg, wu, wd, gf[None, :], final_norm)


def kernel(x_prompt, x_sample, state_conv, state_mlstm_C, state_mlstm_n, state_mlstm_m,
           norm1_g, w_in, b_igate, b_fgate, conv_w, conv_out_g, mlstm_out_g, w_out,
           norm2_g, w_ffn_gate, w_ffn_up, w_ffn_down, final_norm_g):
    depth = w_in.shape[0]
    bp, sp, _ = x_prompt.shape
    bs, ss, _ = x_sample.shape
    hp = x_prompt.reshape(bp * sp, D_MODEL)
    hs = x_sample.reshape(bs * ss, D_MODEL)
    p_states, s_states = [], []
    for l in range(depth):
        final_norm = l == depth - 1
        wgate, wgate_t, bias_row, bias_col = _gate_params(w_in[l], b_igate[l], b_fgate[l])
        g1 = norm1_g[l][None, :]
        conv_g = conv_out_g[l][None, :]
        ml_g = mlstm_out_g[l][None, :]
        tail = (w_out[l], norm2_g[l], w_ffn_gate[l], w_ffn_up[l], w_ffn_down[l],
                final_norm_g, final_norm)

        z, gcol, grow = _inproj(hp, g1, w_in[l], wgate, wgate_t)
        y, cb, c1, n1, m1 = _mix_prompt(z, gcol, grow, bp, sp, conv_w[l], conv_g, ml_g,
                                        bias_row, bias_col)
        hp = _tail(hp, y, *tail)
        p_states.append((cb, c1, n1, m1[:, :, 0]))

        z, gcol, grow = _inproj(hs, g1, w_in[l], wgate, wgate_t)
        m_tok = jnp.repeat(state_mlstm_m[l], ss, axis=0)
        mcol = jnp.pad(m_tok, ((0, 0), (0, LANES - N_HEADS)))
        mrow = jnp.pad(m_tok.T, ((0, SUBLANES - N_HEADS), (0, 0)))
        y, cb, c1, n1, m_row = _mix_sample(z, gcol, grow, mcol, mrow, state_conv[l],
                                           state_mlstm_C[l], state_mlstm_n[l], bs, ss,
                                           conv_w[l], conv_g, ml_g, bias_row, bias_col)
        hs = _tail(hs, y, *tail)
        s_states.append((cb, c1, n1, m_row[:N_HEADS, ::ss].T))

    def stack(states, k):
        if len(states) == 1:
            return states[0][k][None]
        return jnp.stack([st[k] for st in states])

    return (hp.reshape(bp, sp, D_MODEL), hs.reshape(bs, ss, D_MODEL),
            stack(p_states, 0), stack(p_states, 1), stack(p_states, 2), stack(p_states, 3),
            stack(s_states, 0), stack(s_states, 1), stack(s_states, 2), stack(s_states, 3))
```

```python
import functools

import jax
import jax.numpy as jnp
from jax import lax
from jax.experimental import pallas as pl
from jax.experimental.pallas import tpu as pltpu

F32 = jnp.float32
BF16 = jnp.bfloat16

D_MODEL = 2048
D_CONV = 1024
CONV_W = 3
N_HEADS = 4
D_K = 128
D_V = 256
D_FF = 5632
EPS = 1e-6

SRC_Q = 3 * D_CONV
SRC_K = SRC_Q + N_HEADS * D_K
SRC_V = SRC_K + N_HEADS * D_K
SRC_O = SRC_V + N_HEADS * D_V
N_MAIN = SRC_O + N_HEADS * D_V

HEAD_COLS = 2 * D_K + 2 * D_V
N_MLSTM = N_HEADS * HEAD_COLS
COLS_BG = slice(N_MLSTM, N_MLSTM + D_CONV)
COLS_CG = slice(N_MLSTM + D_CONV, N_MLSTM + 2 * D_CONV)
COLS_XT = slice(N_MLSTM + 2 * D_CONV, N_MLSTM + 3 * D_CONV)


def _cols_q(h):
    return slice(h * HEAD_COLS, h * HEAD_COLS + D_K)


def _cols_k(h):
    return slice(h * HEAD_COLS + D_K, h * HEAD_COLS + 2 * D_K)


def _cols_v(h):
    return slice(h * HEAD_COLS + 2 * D_K, h * HEAD_COLS + 2 * D_K + D_V)


def _cols_o(h):
    return slice(h * HEAD_COLS + 2 * D_K + D_V, (h + 1) * HEAD_COLS)


def _source_lane_block(j):
    per_head = HEAD_COLS // LANES
    h, r = j // per_head, j % per_head
    v_blocks = D_V // LANES
    mlstm_src = jnp.where(
        r == 0, SRC_Q // LANES + h,
        jnp.where(r == 1, SRC_K // LANES + h,
                  jnp.where(r < 2 + v_blocks,
                            SRC_V // LANES + v_blocks * h + (r - 2),
                            SRC_O // LANES + v_blocks * h + (r - 2 - v_blocks))))
    return jnp.where(j < N_MLSTM // LANES, mlstm_src, j - N_MLSTM // LANES)

LANES = 128
SUBLANES = 8
GATE_ROWS = 16
MIB = 1024 * 1024

PROMPT_CHUNK = 256
SAMPLE_GROUP = 16
NORM_ROWS = 64
W_SLAB = 512


def _compiler_params(semantics, vmem_mib):
    return pltpu.CompilerParams(dimension_semantics=semantics,
                                vmem_limit_bytes=vmem_mib * MIB)


def _rmsnorm_piece(x, gain, dtype):
    ms = jnp.mean(x * x, axis=-1, keepdims=True)
    return ((x * lax.rsqrt(ms + EPS)) * gain).astype(dtype)


def _rmsnorm_rows(src_ref, gain, dst_ref):
    rows = src_ref.shape[0]
    chunk = min(NORM_ROWS, rows)

    def body(i, carry):
        r = pl.multiple_of(i * chunk, chunk)
        dst_ref[pl.ds(r, chunk), :] = _rmsnorm_piece(
            src_ref[pl.ds(r, chunk), :].astype(F32), gain, dst_ref.dtype)
        return carry

    lax.fori_loop(0, rows // chunk, body, 0)


def _cast_kernel(w_ref, o_ref):
    o_ref[...] = w_ref[...].astype(o_ref.dtype)


def _cast_main_columns(w_in):
    return pl.pallas_call(
        _cast_kernel,
        grid=(N_MAIN // LANES,),
        in_specs=[pl.BlockSpec((D_MODEL, LANES), lambda j: (0, _source_lane_block(j)))],
        out_specs=pl.BlockSpec((D_MODEL, LANES), lambda j: (0, j)),
        out_shape=jax.ShapeDtypeStruct((D_MODEL, N_MAIN), BF16),
        compiler_params=_compiler_params(("parallel",), 32),
        name="cast_w_in",
    )(w_in)


def _inproj_kernel(x_ref, g_ref, w_ref, wgate_ref, z_ref, gcol_ref, grow_ref, xn_ref):
    @pl.when(pl.program_id(1) == 0)
    def _():
        _rmsnorm_rows(x_ref, g_ref[...], xn_ref)
        gc = jnp.dot(xn_ref[...], wgate_ref[...].astype(BF16), preferred_element_type=F32)
        gcol_ref[...] = gc
        grow_ref[...] = gc.T[0:GATE_ROWS, :]

    z_ref[...] = jnp.dot(xn_ref[...], w_ref[...], preferred_element_type=F32)


def _inproj(x2d, g1, w_bf, wgate):
    tokens = x2d.shape[0]
    tm = min(1024, tokens)
    tn = W_SLAB
    return pl.pallas_call(
        _inproj_kernel,
        grid=(tokens // tm, N_MAIN // tn),
        in_specs=[
            pl.BlockSpec((tm, D_MODEL), lambda i, j: (i, 0)),
            pl.BlockSpec((1, D_MODEL), lambda i, j: (0, 0)),
            pl.BlockSpec((D_MODEL, tn), lambda i, j: (0, j)),
            pl.BlockSpec((D_MODEL, LANES), lambda i, j: (0, 0)),
        ],
        out_specs=[
            pl.BlockSpec((tm, tn), lambda i, j: (i, j)),
            pl.BlockSpec((tm, LANES), lambda i, j: (i, 0)),
            pl.BlockSpec((GATE_ROWS, tm), lambda i, j: (0, i)),
        ],
        out_shape=[
            jax.ShapeDtypeStruct((tokens, N_MAIN), F32),
            jax.ShapeDtypeStruct((tokens, LANES), F32),
            jax.ShapeDtypeStruct((GATE_ROWS, tokens), F32),
        ],
        scratch_shapes=[pltpu.VMEM((tm, D_MODEL), BF16)],
        compiler_params=_compiler_params(("parallel", "arbitrary"), 48),
        name="inproj",
    )(x2d, g1, w_bf, wgate)


def _log_sigmoid(x):
    return jnp.minimum(x, 0.0) - jnp.log1p(jnp.exp(-jnp.abs(x)))


def _dot_exact(a, b):
    return jnp.dot(a, b, precision=lax.Precision.HIGHEST, preferred_element_type=F32)


def _intra(q_bf, k_bf, b_col, b_row, li_row, m0_col, mask):
    dmat = jnp.where(mask, b_col - b_row + li_row, -jnp.inf)
    inter = b_col + m0_col
    m_t = jnp.maximum(inter, jnp.max(dmat, axis=1, keepdims=True))
    w = jnp.exp(dmat - m_t)
    a_inter = jnp.exp(inter - m_t)
    qk = lax.dot_general(q_bf, k_bf, (((1,), (1,)), ((), ())), preferred_element_type=F32)
    return qk * w, a_inter, m_t


def _head_output(num, den, m_t, o_gate, gain):
    denom = jnp.maximum(jnp.abs(den), jnp.exp(-m_t))
    h = num * (1.0 / denom)
    hn = (h * lax.rsqrt(jnp.mean(h * h, axis=1, keepdims=True) + EPS)) * gain
    return jax.nn.sigmoid(o_gate) * hn


def _conv_norm(bg, conv, gain):
    yc = bg * conv
    return (yc * lax.rsqrt(jnp.mean(yc * yc, axis=1, keepdims=True) + EPS)) * gain


def _prompt_conv(z, convw_ref, convg_ref, y_ref, ubuf):
    L = z.shape[0]
    u = z[:, COLS_CG] * z[:, COLS_XT]
    ubuf[SUBLANES:SUBLANES + L, :] = u
    conv = (ubuf[SUBLANES - 2:SUBLANES - 2 + L, :] * convw_ref[0:1, :]
            + ubuf[SUBLANES - 1:SUBLANES - 1 + L, :] * convw_ref[1:2, :]
            + u * convw_ref[2:3, :])
    y_ref[:, 0:D_CONV] = _conv_norm(z[:, COLS_BG], conv,
                                    convg_ref[...]).astype(y_ref.dtype)
    ubuf[0:SUBLANES, :] = ubuf[L:L + SUBLANES, :]


def _prompt_mlstm(z, gc, gr, mlg_ref, bias_row_ref, bias_col_ref, y_ref, c_scr, n_scr, m_scr,
                  fillers):
    L = z.shape[0]
    fillers = list(fillers)

    def fill():
        if fillers:
            fillers.pop(0)()

    pre_c = gc + bias_row_ref[...]
    pre_r = gr + bias_col_ref[:, 0:1]
    row_id = lax.broadcasted_iota(jnp.int32, (L, L), 0)
    col_id = lax.broadcasted_iota(jnp.int32, (L, L), 1)
    causal = col_id <= row_id
    b_c = _dot_exact(causal.astype(F32), _log_sigmoid(pre_c))
    b_r = _dot_exact(_log_sigmoid(pre_r), (row_id <= col_id).astype(F32))

    for h in range(N_HEADS):
        li_r = pre_r[h:h + 1, :]
        li_c = pre_c[:, h:h + 1]
        bc = b_c[:, N_HEADS + h:N_HEADS + h + 1]
        br = b_r[N_HEADS + h:N_HEADS + h + 1, :]
        m0 = m_scr[h, 0:1, 0:1]
        qf = z[:, _cols_q(h)] * (D_K ** -0.5)
        q_bf = qf.astype(BF16)
        kf = z[:, _cols_k(h)]
        v_bf = z[:, _cols_v(h)].astype(BF16)
        s, a_inter, m_t = _intra(q_bf, kf.astype(BF16), bc, br, li_r, m0, causal)
        fill()
        c_old = c_scr[h]
        n_old = n_scr[h]
        num = (a_inter * jnp.dot(q_bf, c_old.astype(BF16), preferred_element_type=F32)
               + jnp.dot(s.astype(BF16), v_bf, preferred_element_type=F32))
        den = (a_inter * jnp.sum(qf * n_old, axis=1, keepdims=True)
               + jnp.sum(s, axis=1, keepdims=True))
        fill()
        y_ref[:, D_CONV + h * D_V:D_CONV + (h + 1) * D_V] = _head_output(
            num, den, m_t, z[:, _cols_o(h)],
            mlg_ref[:, h * D_V:(h + 1) * D_V]).astype(y_ref.dtype)

        b_last = br[:, L - 1:L]
        m_new = jnp.maximum(b_last + m0,
                            jnp.max(b_last - br + li_r, axis=1, keepdims=True))
        decay = jnp.exp(b_last + m0 - m_new)
        kw = kf * jnp.exp(b_last - bc + li_c - m_new)
        c_scr[h] = decay * c_old + lax.dot_general(
            kw.astype(BF16), v_bf, (((0,), (0,)), ((), ())), preferred_element_type=F32)
        n_scr[h] = decay * n_old + jnp.sum(kw, axis=0, keepdims=True)
        m_scr[h] = jnp.broadcast_to(m_new, (SUBLANES, LANES))
        fill()
    while fillers:
        fill()


def _fused_prompt_kernel(x_ref, g1_ref, w_ref, wgate_ref, convw_ref, convg_ref, mlg_ref,
                         bias_row_ref, bias_col_ref,
                         y_ref, convs_ref, c_out_ref, n_out_ref, m_out_ref,
                         xn_scr, z_scr, ubuf, c_scr, n_scr, m_scr):
    chunk = pl.program_id(1)
    L = x_ref.shape[0]

    @pl.when(chunk == 0)
    def _():
        ubuf[0:SUBLANES, :] = jnp.zeros((SUBLANES, D_CONV), F32)
        c_scr[...] = jnp.zeros(c_scr.shape, F32)
        n_scr[...] = jnp.zeros(n_scr.shape, F32)
        m_scr[...] = jnp.zeros(m_scr.shape, F32)

    gain = g1_ref[...]
    for r in range(0, L, NORM_ROWS):
        xn_scr[r:r + NORM_ROWS, :] = _rmsnorm_piece(x_ref[r:r + NORM_ROWS, :], gain, BF16)
    xn = xn_scr[...]
    gc = jnp.dot(xn, wgate_ref[...].astype(BF16), preferred_element_type=F32)
    gr = gc.T[0:SUBLANES, :]

    def project(j):
        cols = slice(j * W_SLAB, (j + 1) * W_SLAB)
        z_scr[:, cols] = jnp.dot(xn_scr[...], w_ref[:, cols], preferred_element_type=F32)

    piece = lambda j: functools.partial(project, j)
    skip = lambda: None
    project(0)
    project(1)
    fillers = [piece(2), piece(6), piece(7),
               piece(3), piece(4), piece(8),
               piece(5), piece(9), skip,
               piece(10), piece(11), skip]
    _prompt_mlstm(z_scr, gc, gr, mlg_ref, bias_row_ref, bias_col_ref, y_ref,
                  c_scr, n_scr, m_scr, fillers)
    _prompt_conv(z_scr, convw_ref, convg_ref, y_ref, ubuf)

    @pl.when(chunk == pl.num_programs(1) - 1)
    def _():
        convs_ref[0] = ubuf[SUBLANES - 2:SUBLANES, :]
        c_out_ref[0] = c_scr[...]
        for h in range(N_HEADS):
            n_out_ref[0, h:h + 1, :] = n_scr[h]
            m_out_ref[0, h:h + 1, :] = m_scr[h, 0:1, :]


def _fused_prompt(x2d, g1, w_bf, wgate, batch, seq, conv_w, conv_g, ml_g, bias_row, bias_col):
    L = PROMPT_CHUNK
    nc = seq // L
    const = lambda b, c: (0, 0)
    return pl.pallas_call(
        _fused_prompt_kernel,
        grid=(batch, nc),
        in_specs=[
            pl.BlockSpec((L, D_MODEL), lambda b, c: (b * nc + c, 0)),
            pl.BlockSpec((1, D_MODEL), const),
            pl.BlockSpec((D_MODEL, N_MAIN), const, pipeline_mode=pl.Buffered(1)),
            pl.BlockSpec((D_MODEL, LANES), const),
            pl.BlockSpec((CONV_W, D_CONV), const),
            pl.BlockSpec((1, D_CONV), const),
            pl.BlockSpec((1, N_HEADS * D_V), const),
            pl.BlockSpec((1, LANES), const),
            pl.BlockSpec((SUBLANES, LANES), const),
        ],
        out_specs=[
            pl.BlockSpec((L, D_MODEL), lambda b, c: (b * nc + c, 0)),
            pl.BlockSpec((1, CONV_W - 1, D_CONV), lambda b, c: (b, 0, 0)),
            pl.BlockSpec((1, N_HEADS, D_K, D_V), lambda b, c: (b, 0, 0, 0)),
            pl.BlockSpec((1, N_HEADS, D_K), lambda b, c: (b, 0, 0)),
            pl.BlockSpec((1, N_HEADS, LANES), lambda b, c: (b, 0, 0)),
        ],
        out_shape=[
            jax.ShapeDtypeStruct((batch * seq, D_MODEL), BF16),
            jax.ShapeDtypeStruct((batch, CONV_W - 1, D_CONV), F32),
            jax.ShapeDtypeStruct((batch, N_HEADS, D_K, D_V), F32),
            jax.ShapeDtypeStruct((batch, N_HEADS, D_K), F32),
            jax.ShapeDtypeStruct((batch, N_HEADS, LANES), F32),
        ],
        scratch_shapes=[
            pltpu.VMEM((L, D_MODEL), BF16),
            pltpu.VMEM((L, N_MAIN), F32),
            pltpu.VMEM((L + 2 * SUBLANES, D_CONV), F32),
            pltpu.VMEM((N_HEADS, D_K, D_V), F32),
            pltpu.VMEM((N_HEADS, 1, D_K), F32),
            pltpu.VMEM((N_HEADS, SUBLANES, LANES), F32),
        ],
        compiler_params=_compiler_params(("arbitrary", "arbitrary"), 56),
        name="fused_prompt",
    )(x2d, g1, w_bf, wgate, conv_w, conv_g, ml_g, bias_row, bias_col)


def _mix_sample_kernel(z_ref, gcol_ref, grow_ref, mcol_ref, mrow_ref,
                       convs_in_ref, c_in_ref, n_in_ref,
                       convw_ref, convg_ref, mlg_ref, bias_row_ref, bias_col_ref,
                       y_ref, convs_ref, c_out_ref, n_out_ref, m_out_ref,
                       pad_scr, conv_scr, *, seq):
    L = z_ref.shape[0]
    group = L // seq
    stride = seq + SUBLANES

    for g in range(group):
        u = z_ref[g * seq:(g + 1) * seq, COLS_CG] * z_ref[g * seq:(g + 1) * seq, COLS_XT]
        base = g * stride
        pad_scr[base + SUBLANES - 2:base + SUBLANES, :] = convs_in_ref[g]
        pad_scr[base + SUBLANES:base + SUBLANES + seq, :] = u
        conv_scr[g * seq:(g + 1) * seq, :] = (
            pad_scr[base + SUBLANES - 2:base + SUBLANES - 2 + seq, :] * convw_ref[0:1, :]
            + pad_scr[base + SUBLANES - 1:base + SUBLANES - 1 + seq, :] * convw_ref[1:2, :]
            + u * convw_ref[2:3, :])
        convs_ref[g] = pad_scr[base + seq + SUBLANES - 2:base + seq + SUBLANES, :]
    y_ref[:, 0:D_CONV] = _conv_norm(z_ref[:, COLS_BG], conv_scr[...],
                                    convg_ref[...]).astype(y_ref.dtype)

    pre_c = gcol_ref[...] + bias_row_ref[...]
    pre_r = grow_ref[0:SUBLANES, :] + bias_col_ref[:, 0:1]
    row_id = lax.broadcasted_iota(jnp.int32, (L, L), 0)
    col_id = lax.broadcasted_iota(jnp.int32, (L, L), 1)
    same = (row_id // seq) == (col_id // seq)
    causal = same & (col_id <= row_id)
    lf_c = _log_sigmoid(pre_c)
    lf_r = _log_sigmoid(pre_r)
    b_c = _dot_exact(causal.astype(F32), lf_c)
    b_r = _dot_exact(lf_r, (same & (row_id <= col_id)).astype(F32))
    same_f = same.astype(F32)
    tot_c = _dot_exact(same_f, lf_c)
    tot_r = _dot_exact(lf_r, same_f)
    lane_seq = lax.broadcasted_iota(jnp.int32, (1, L), 1) // seq

    for h in range(N_HEADS):
        li_r = pre_r[h:h + 1, :]
        li_c = pre_c[:, h:h + 1]
        bc = b_c[:, N_HEADS + h:N_HEADS + h + 1]
        br = b_r[N_HEADS + h:N_HEADS + h + 1, :]
        bl_c = tot_c[:, N_HEADS + h:N_HEADS + h + 1]
        bl_r = tot_r[N_HEADS + h:N_HEADS + h + 1, :]
        m0_c = mcol_ref[:, h:h + 1]
        m0_r = mrow_ref[h:h + 1, :]
        qf = z_ref[:, _cols_q(h)] * (D_K ** -0.5)
        q_bf = qf.astype(BF16)
        kf = z_ref[:, _cols_k(h)]
        v_bf = z_ref[:, _cols_v(h)].astype(BF16)
        s, a_inter, m_t = _intra(q_bf, kf.astype(BF16), bc, br, li_r, m0_c, causal)

        qc_rows, qn_rows = [], []
        for g in range(group):
            rows = slice(g * seq, (g + 1) * seq)
            qc_rows.append(jnp.dot(qf[rows], c_in_ref[g, h], preferred_element_type=F32))
            qn_rows.append(jnp.sum(qf[rows] * n_in_ref[g, h:h + 1, :], axis=1, keepdims=True))
        num = (a_inter * jnp.concatenate(qc_rows, axis=0)
               + jnp.dot(s.astype(BF16), v_bf, preferred_element_type=F32))
        den = (a_inter * jnp.concatenate(qn_rows, axis=0)
               + jnp.sum(s, axis=1, keepdims=True))
        y_ref[:, D_CONV + h * D_V:D_CONV + (h + 1) * D_V] = _head_output(
            num, den, m_t, z_ref[:, _cols_o(h)],
            mlg_ref[:, h * D_V:(h + 1) * D_V]).astype(y_ref.dtype)

        g_r = bl_r - br + li_r
        gmax_c = jnp.max(jnp.where(same, g_r, -jnp.inf), axis=1, keepdims=True)
        gmax_r = jnp.max(jnp.where(same, gmax_c, -jnp.inf), axis=0, keepdims=True)
        m_new_c = jnp.maximum(bl_c + m0_c, gmax_c)
        m_new_r = jnp.maximum(bl_r + m0_r, gmax_r)
        decay_r = jnp.exp(bl_r + m0_r - m_new_r)
        kw = kf * jnp.exp(bl_c - bc + li_c - m_new_c)
        kw_t = kw.T
        for g in range(group):
            decay = decay_r[:, g * seq:g * seq + 1]
            kw_g = jnp.where(lane_seq == g, kw_t, 0.0).astype(BF16)
            c_out_ref[g, h] = decay * c_in_ref[g, h] + jnp.dot(
                kw_g, v_bf, preferred_element_type=F32)
            n_out_ref[g, h:h + 1, :] = (decay * n_in_ref[g, h:h + 1, :]
                                        + jnp.sum(kw[g * seq:(g + 1) * seq], axis=0, keepdims=True))
        m_out_ref[h:h + 1, :] = m_new_r
    m_out_ref[N_HEADS:, :] = jnp.zeros((SUBLANES - N_HEADS, L), F32)


def _mix_sample(z, gcol, grow, mcol, mrow, convs, c_state, n_state, batch, seq,
                conv_w, conv_g, ml_g, bias_row, bias_col):
    group = SAMPLE_GROUP
    L = group * seq
    const = lambda i: (0, 0)
    return pl.pallas_call(
        functools.partial(_mix_sample_kernel, seq=seq),
        grid=(batch // group,),
        in_specs=[
            pl.BlockSpec((L, N_MAIN), lambda i: (i, 0)),
            pl.BlockSpec((L, LANES), lambda i: (i, 0)),
            pl.BlockSpec((GATE_ROWS, L), lambda i: (0, i)),
            pl.BlockSpec((L, LANES), lambda i: (i, 0)),
            pl.BlockSpec((SUBLANES, L), lambda i: (0, i)),
            pl.BlockSpec((group, CONV_W - 1, D_CONV), lambda i: (i, 0, 0)),
            pl.BlockSpec((group, N_HEADS, D_K, D_V), lambda i: (i, 0, 0, 0)),
            pl.BlockSpec((group, N_HEADS, D_K), lambda i: (i, 0, 0)),
            pl.BlockSpec((CONV_W, D_CONV), const),
            pl.BlockSpec((1, D_CONV), const),
            pl.BlockSpec((1, N_HEADS * D_V), const),
            pl.BlockSpec((1, LANES), const),
            pl.BlockSpec((SUBLANES, LANES), const),
        ],
        out_specs=[
            pl.BlockSpec((L, D_MODEL), lambda i: (i, 0)),
            pl.BlockSpec((group, CONV_W - 1, D_CONV), lambda i: (i, 0, 0)),
            pl.BlockSpec((group, N_HEADS, D_K, D_V), lambda i: (i, 0, 0, 0)),
            pl.BlockSpec((group, N_HEADS, D_K), lambda i: (i, 0, 0)),
            pl.BlockSpec((SUBLANES, L), lambda i: (0, i)),
        ],
        out_shape=[
            jax.ShapeDtypeStruct((batch * seq, D_MODEL), BF16),
            jax.ShapeDtypeStruct((batch, CONV_W - 1, D_CONV), F32),
            jax.ShapeDtypeStruct((batch, N_HEADS, D_K, D_V), F32),
            jax.ShapeDtypeStruct((batch, N_HEADS, D_K), F32),
            jax.ShapeDtypeStruct((SUBLANES, batch * seq), F32),
        ],
        scratch_shapes=[
            pltpu.VMEM((group * (seq + SUBLANES), D_CONV), F32),
            pltpu.VMEM((L, D_CONV), F32),
        ],
        compiler_params=_compiler_params(("parallel",), 56),
        name="mix_sample",
    )(z, gcol, grow, mcol, mrow, convs, c_state, n_state,
      conv_w, conv_g, ml_g, bias_row, bias_col)


def _outproj_kernel(y_ref, w_ref, x_ref, o_ref):
    o_ref[...] = x_ref[...] + jnp.dot(y_ref[...], w_ref[...].astype(BF16),
                                      preferred_element_type=F32)


def _outproj(y, w_out, x2d):
    tokens = x2d.shape[0]
    tm = min(1024, tokens)
    tn = 512
    return pl.pallas_call(
        _outproj_kernel,
        grid=(tokens // tm, D_MODEL // tn),
        in_specs=[
            pl.BlockSpec((tm, D_MODEL), lambda i, j: (i, 0)),
            pl.BlockSpec((D_MODEL, tn), lambda i, j: (0, j)),
            pl.BlockSpec((tm, tn), lambda i, j: (i, j)),
        ],
        out_specs=pl.BlockSpec((tm, tn), lambda i, j: (i, j)),
        out_shape=jax.ShapeDtypeStruct((tokens, D_MODEL), F32),
        compiler_params=_compiler_params(("parallel", "arbitrary"), 48),
        name="outproj",
    )(y, w_out, x2d)


def _ffn_kernel(x_ref, g2_ref, wg_ref, wu_ref, wd_ref, gf_ref, o_ref, hn_ref, *, final_norm):
    j = pl.program_id(1)

    @pl.when(j == 0)
    def _():
        _rmsnorm_rows(x_ref, g2_ref[...], hn_ref)
        o_ref[...] = x_ref[...]

    hn = hn_ref[...]
    gate = jnp.dot(hn, wg_ref[...].astype(BF16), preferred_element_type=F32)
    up = jnp.dot(hn, wu_ref[...].astype(BF16), preferred_element_type=F32)
    act = (gate * jax.nn.sigmoid(gate)) * up
    o_ref[...] += jnp.dot(act.astype(BF16), wd_ref[...].astype(BF16),
                          preferred_element_type=F32)

    if final_norm:
        @pl.when(j == pl.num_programs(1) - 1)
        def _():
            _rmsnorm_rows(o_ref, gf_ref[...], o_ref)


def _ffn(x1, g2, wg, wu, wd, gf, final_norm):
    tokens = x1.shape[0]
    tm = min(1024, tokens)
    tf = 256
    return pl.pallas_call(
        functools.partial(_ffn_kernel, final_norm=final_norm),
        grid=(tokens // tm, D_FF // tf),
        in_specs=[
            pl.BlockSpec((tm, D_MODEL), lambda i, j: (i, 0)),
            pl.BlockSpec((1, D_MODEL), lambda i, j: (0, 0)),
            pl.BlockSpec((D_MODEL, tf), lambda i, j: (0, j)),
            pl.BlockSpec((D_MODEL, tf), lambda i, j: (0, j)),
            pl.BlockSpec((tf, D_MODEL), lambda i, j: (j, 0)),
            pl.BlockSpec((1, D_MODEL), lambda i, j: (0, 0)),
        ],
        out_specs=pl.BlockSpec((tm, D_MODEL), lambda i, j: (i, 0)),
        out_shape=jax.ShapeDtypeStruct((tokens, D_MODEL), F32),
        scratch_shapes=[pltpu.VMEM((tm, D_MODEL), BF16)],
        compiler_params=_compiler_params(("parallel", "arbitrary"), 58),
        name="ffn",
    )(x1, g2, wg, wu, wd, gf)


def _gate_params(w_in, b_i, b_f):
    wgate = jnp.pad(w_in[:, N_MAIN:], ((0, 0), (0, LANES - 2 * N_HEADS)))
    bias = jnp.concatenate([b_i, b_f]).astype(F32)
    bias_row = jnp.pad(bias, (0, LANES - 2 * N_HEADS))[None, :]
    bias_col = jnp.broadcast_to(bias[:, None], (SUBLANES, LANES))
    return wgate, bias_row, bias_col


def _tail(x2d, y, w_out, g2, wg, wu, wd, gf, final_norm):
    x1 = _outproj(y, w_out, x2d)
    return _ffn(x1, g2[None, :], wg, wu, wd, gf[None, :], final_norm)


def _stack(states, k):
    if len(states) == 1:
        return states[0][k][None]
    return jnp.stack([st[k] for st in states])


def kernel(x_prompt, x_sample, state_conv, state_mlstm_C, state_mlstm_n, state_mlstm_m,
           norm1_g, w_in, b_igate, b_fgate, conv_w, conv_out_g, mlstm_out_g, w_out,
           norm2_g, w_ffn_gate, w_ffn_up, w_ffn_down, final_norm_g):
    depth = w_in.shape[0]
    bp, sp, _ = x_prompt.shape
    bs, ss, _ = x_sample.shape
    hp = x_prompt.reshape(bp * sp, D_MODEL)
    hs = x_sample.reshape(bs * ss, D_MODEL)
    p_states, s_states = [], []
    for l in range(depth):
        final_norm = l == depth - 1
        wgate, bias_row, bias_col = _gate_params(w_in[l], b_igate[l], b_fgate[l])
        w_bf = _cast_main_columns(w_in[l])
        g1 = norm1_g[l][None, :]
        conv_g = conv_out_g[l][None, :]
        ml_g = mlstm_out_g[l][None, :]
        tail = (w_out[l], norm2_g[l], w_ffn_gate[l], w_ffn_up[l], w_ffn_down[l],
                final_norm_g, final_norm)

        y, cb, c1, n1, m1 = _fused_prompt(hp, g1, w_bf, wgate, bp, sp, conv_w[l], conv_g, ml_g,
                                          bias_row, bias_col)
        hp = _tail(hp, y, *tail)
        p_states.append((cb, c1, n1, m1[:, :, 0]))

        z, gcol, grow = _inproj(hs, g1, w_bf, wgate)
        m_tok = jnp.repeat(state_mlstm_m[l], ss, axis=0)
        mcol = jnp.pad(m_tok, ((0, 0), (0, LANES - N_HEADS)))
        mrow = jnp.pad(m_tok.T, ((0, SUBLANES - N_HEADS), (0, 0)))
        y, cb, c1, n1, m_row = _mix_sample(z, gcol, grow, mcol, mrow, state_conv[l],
                                           state_mlstm_C[l], state_mlstm_n[l], bs, ss,
                                           conv_w[l], conv_g, ml_g, bias_row, bias_col)
        hs = _tail(hs, y, *tail)
        s_states.append((cb, c1, n1, m_row[:N_HEADS, ::ss].T))

    return (hp.reshape(bp, sp, D_MODEL), hs.reshape(bs, ss, D_MODEL),
            _stack(p_states, 0), _stack(p_states, 1), _stack(p_states, 2), _stack(p_states, 3),
            _stack(s_states, 0), _stack(s_states, 1), _stack(s_states, 2), _stack(s_states, 3))
```

```python
import functools

import jax
import jax.numpy as jnp
from jax import lax
from jax.experimental import pallas as pl
from jax.experimental.pallas import tpu as pltpu

F32 = jnp.float32
BF16 = jnp.bfloat16

D_MODEL = 2048
D_CONV = 1024
CONV_W = 3
N_HEADS = 4
D_K = 128
D_V = 256
D_FF = 5632
EPS = 1e-6

SRC_Q = 3 * D_CONV
SRC_K = SRC_Q + N_HEADS * D_K
SRC_V = SRC_K + N_HEADS * D_K
SRC_O = SRC_V + N_HEADS * D_V
N_MAIN = SRC_O + N_HEADS * D_V

HEAD_COLS = 2 * D_K + 2 * D_V
N_MLSTM = N_HEADS * HEAD_COLS
COLS_BG = slice(N_MLSTM, N_MLSTM + D_CONV)
COLS_CG = slice(N_MLSTM + D_CONV, N_MLSTM + 2 * D_CONV)
COLS_XT = slice(N_MLSTM + 2 * D_CONV, N_MLSTM + 3 * D_CONV)


def _cols_q(h):
    return slice(h * HEAD_COLS, h * HEAD_COLS + D_K)


def _cols_k(h):
    return slice(h * HEAD_COLS + D_K, h * HEAD_COLS + 2 * D_K)


def _cols_v(h):
    return slice(h * HEAD_COLS + 2 * D_K, h * HEAD_COLS + 2 * D_K + D_V)


def _cols_o(h):
    return slice(h * HEAD_COLS + 2 * D_K + D_V, (h + 1) * HEAD_COLS)


def _source_lane_block(j):
    per_head = HEAD_COLS // LANES
    h, r = j // per_head, j % per_head
    v_blocks = D_V // LANES
    mlstm_src = jnp.where(
        r == 0, SRC_Q // LANES + h,
        jnp.where(r == 1, SRC_K // LANES + h,
                  jnp.where(r < 2 + v_blocks,
                            SRC_V // LANES + v_blocks * h + (r - 2),
                            SRC_O // LANES + v_blocks * h + (r - 2 - v_blocks))))
    return jnp.where(j < N_MLSTM // LANES, mlstm_src, j - N_MLSTM // LANES)

LANES = 128
SUBLANES = 8
GATE_ROWS = 16
MIB = 1024 * 1024

PROMPT_CHUNK = 256
SAMPLE_GROUP = 16
NORM_ROWS = 64
W_SLAB = 512


def _compiler_params(semantics, vmem_mib):
    return pltpu.CompilerParams(dimension_semantics=semantics,
                                vmem_limit_bytes=vmem_mib * MIB)


def _rmsnorm_piece(x, gain, dtype):
    ms = jnp.mean(x * x, axis=-1, keepdims=True)
    return ((x * lax.rsqrt(ms + EPS)) * gain).astype(dtype)


def _rmsnorm_rows(src_ref, gain, dst_ref):
    rows = src_ref.shape[0]
    chunk = min(NORM_ROWS, rows)

    def body(i, carry):
        r = pl.multiple_of(i * chunk, chunk)
        dst_ref[pl.ds(r, chunk), :] = _rmsnorm_piece(
            src_ref[pl.ds(r, chunk), :].astype(F32), gain, dst_ref.dtype)
        return carry

    lax.fori_loop(0, rows // chunk, body, 0)


def _cast_kernel(*refs):
    *w_refs, o_ref = refs
    for r, w_ref in enumerate(w_refs):
        o_ref[:, r * LANES:(r + 1) * LANES] = w_ref[...].T.astype(o_ref.dtype)


def _cast_main_columns(w_t):
    per_slab = W_SLAB // LANES
    in_specs = [
        pl.BlockSpec((LANES, D_MODEL),
                     lambda j, r=r: (_source_lane_block(per_slab * j + r), 0))
        for r in range(per_slab)
    ]
    return pl.pallas_call(
        _cast_kernel,
        grid=(N_MAIN // W_SLAB,),
        in_specs=in_specs,
        out_specs=pl.BlockSpec((D_MODEL, W_SLAB), lambda j: (0, j)),
        out_shape=jax.ShapeDtypeStruct((D_MODEL, N_MAIN), BF16),
        compiler_params=_compiler_params(("parallel",), 32),
        name="cast_w_in",
    )(*([w_t] * per_slab))


def _gate_dot(xn, wgate_t_ref):
    return lax.dot_general(xn, wgate_t_ref[...].astype(BF16), (((1,), (1,)), ((), ())),
                           preferred_element_type=F32)


def _inproj_kernel(x_ref, g_ref, w_ref, wgate_ref, z_ref, gcol_ref, grow_ref, xn_ref):
    @pl.when(pl.program_id(1) == 0)
    def _():
        _rmsnorm_rows(x_ref, g_ref[...], xn_ref)
        gc = _gate_dot(xn_ref[...], wgate_ref)
        gcol_ref[...] = gc
        grow_ref[...] = gc.T[0:GATE_ROWS, :]

    z_ref[...] = jnp.dot(xn_ref[...], w_ref[...], preferred_element_type=F32)


def _inproj(x2d, g1, w_bf, wgate):
    tokens = x2d.shape[0]
    tm = min(1024, tokens)
    tn = W_SLAB
    return pl.pallas_call(
        _inproj_kernel,
        grid=(tokens // tm, N_MAIN // tn),
        in_specs=[
            pl.BlockSpec((tm, D_MODEL), lambda i, j: (i, 0)),
            pl.BlockSpec((1, D_MODEL), lambda i, j: (0, 0)),
            pl.BlockSpec((D_MODEL, tn), lambda i, j: (0, j)),
            pl.BlockSpec((LANES, D_MODEL), lambda i, j: (0, 0)),
        ],
        out_specs=[
            pl.BlockSpec((tm, tn), lambda i, j: (i, j)),
            pl.BlockSpec((tm, LANES), lambda i, j: (i, 0)),
            pl.BlockSpec((GATE_ROWS, tm), lambda i, j: (0, i)),
        ],
        out_shape=[
            jax.ShapeDtypeStruct((tokens, N_MAIN), F32),
            jax.ShapeDtypeStruct((tokens, LANES), F32),
            jax.ShapeDtypeStruct((GATE_ROWS, tokens), F32),
        ],
        scratch_shapes=[pltpu.VMEM((tm, D_MODEL), BF16)],
        compiler_params=_compiler_params(("parallel", "arbitrary"), 48),
        name="inproj",
    )(x2d, g1, w_bf, wgate)


def _log_sigmoid(x):
    return jnp.minimum(x, 0.0) - jnp.log1p(jnp.exp(-jnp.abs(x)))


def _dot_exact(a, b):
    return jnp.dot(a, b, precision=lax.Precision.HIGHEST, preferred_element_type=F32)


def _intra(q_bf, k_bf, b_col, b_row, li_row, m0_col, mask):
    dmat = jnp.where(mask, b_col - b_row + li_row, -jnp.inf)
    inter = b_col + m0_col
    m_t = jnp.maximum(inter, jnp.max(dmat, axis=1, keepdims=True))
    w = jnp.exp(dmat - m_t)
    a_inter = jnp.exp(inter - m_t)
    qk = lax.dot_general(q_bf, k_bf, (((1,), (1,)), ((), ())), preferred_element_type=F32)
    return qk * w, a_inter, m_t


def _head_output(num, den, m_t, o_gate, gain):
    denom = jnp.maximum(jnp.abs(den), jnp.exp(-m_t))
    h = num * (1.0 / denom)
    hn = (h * lax.rsqrt(jnp.mean(h * h, axis=1, keepdims=True) + EPS)) * gain
    return jax.nn.sigmoid(o_gate) * hn


def _conv_norm(bg, conv, gain):
    yc = bg * conv
    return (yc * lax.rsqrt(jnp.mean(yc * yc, axis=1, keepdims=True) + EPS)) * gain


def _prompt_conv(z, convw_ref, convg_ref, y_ref, ubuf):
    L = z.shape[0]
    u = z[:, COLS_CG] * z[:, COLS_XT]
    ubuf[SUBLANES:SUBLANES + L, :] = u
    conv = (ubuf[SUBLANES - 2:SUBLANES - 2 + L, :] * convw_ref[0:1, :]
            + ubuf[SUBLANES - 1:SUBLANES - 1 + L, :] * convw_ref[1:2, :]
            + u * convw_ref[2:3, :])
    y_ref[:, 0:D_CONV] = _conv_norm(z[:, COLS_BG], conv,
                                    convg_ref[...]).astype(y_ref.dtype)
    ubuf[0:SUBLANES, :] = ubuf[L:L + SUBLANES, :]


def _prompt_mlstm(z, gc, gr, mlg_ref, bias_row_ref, bias_col_ref, y_ref, c_scr, n_scr, m_scr,
                  fillers):
    L = z.shape[0]
    fillers = list(fillers)

    def fill():
        if fillers:
            fillers.pop(0)()

    pre_c = gc + bias_row_ref[...]
    pre_r = gr + bias_col_ref[:, 0:1]
    row_id = lax.broadcasted_iota(jnp.int32, (L, L), 0)
    col_id = lax.broadcasted_iota(jnp.int32, (L, L), 1)
    causal = col_id <= row_id
    b_c = _dot_exact(causal.astype(F32), _log_sigmoid(pre_c))
    b_r = _dot_exact(_log_sigmoid(pre_r), (row_id <= col_id).astype(F32))

    for h in range(N_HEADS):
        li_r = pre_r[h:h + 1, :]
        li_c = pre_c[:, h:h + 1]
        bc = b_c[:, N_HEADS + h:N_HEADS + h + 1]
        br = b_r[N_HEADS + h:N_HEADS + h + 1, :]
        m0 = m_scr[h, 0:1, 0:1]
        qf = z[:, _cols_q(h)] * (D_K ** -0.5)
        q_bf = qf.astype(BF16)
        kf = z[:, _cols_k(h)]
        v_bf = z[:, _cols_v(h)].astype(BF16)
        s, a_inter, m_t = _intra(q_bf, kf.astype(BF16), bc, br, li_r, m0, causal)
        fill()
        c_old = c_scr[h]
        n_old = n_scr[h]
        num = (a_inter * jnp.dot(q_bf, c_old.astype(BF16), preferred_element_type=F32)
               + jnp.dot(s.astype(BF16), v_bf, preferred_element_type=F32))
        den = (a_inter * jnp.sum(qf * n_old, axis=1, keepdims=True)
               + jnp.sum(s, axis=1, keepdims=True))
        fill()
        y_ref[:, D_CONV + h * D_V:D_CONV + (h + 1) * D_V] = _head_output(
            num, den, m_t, z[:, _cols_o(h)],
            mlg_ref[:, h * D_V:(h + 1) * D_V]).astype(y_ref.dtype)

        b_last = br[:, L - 1:L]
        m_new = jnp.maximum(b_last + m0,
                            jnp.max(b_last - br + li_r, axis=1, keepdims=True))
        decay = jnp.exp(b_last + m0 - m_new)
        kw = kf * jnp.exp(b_last - bc + li_c - m_new)
        c_scr[h] = decay * c_old + lax.dot_general(
            kw.astype(BF16), v_bf, (((0,), (0,)), ((), ())), preferred_element_type=F32)
        n_scr[h] = decay * n_old + jnp.sum(kw, axis=0, keepdims=True)
        m_scr[h] = jnp.broadcast_to(m_new, (SUBLANES, LANES))
        fill()
    while fillers:
        fill()


def _fused_prompt_kernel(x_ref, g1_ref, w_ref, wgate_ref, convw_ref, convg_ref, mlg_ref,
                         bias_row_ref, bias_col_ref,
                         y_ref, convs_ref, c_out_ref, n_out_ref, m_out_ref,
                         xn_scr, z_scr, ubuf, c_scr, n_scr, m_scr):
    chunk = pl.program_id(1)
    L = x_ref.shape[0]

    @pl.when(chunk == 0)
    def _():
        ubuf[0:SUBLANES, :] = jnp.zeros((SUBLANES, D_CONV), F32)
        c_scr[...] = jnp.zeros(c_scr.shape, F32)
        n_scr[...] = jnp.zeros(n_scr.shape, F32)
        m_scr[...] = jnp.zeros(m_scr.shape, F32)

    gain = g1_ref[...]
    for r in range(0, L, NORM_ROWS):
        xn_scr[r:r + NORM_ROWS, :] = _rmsnorm_piece(x_ref[r:r + NORM_ROWS, :], gain, BF16)
    xn = xn_scr[...]
    gc = _gate_dot(xn, wgate_ref)
    gr = gc.T[0:SUBLANES, :]

    def project(j):
        cols = slice(j * W_SLAB, (j + 1) * W_SLAB)
        z_scr[:, cols] = jnp.dot(xn_scr[...], w_ref[:, cols], preferred_element_type=F32)

    piece = lambda j: functools.partial(project, j)
    skip = lambda: None
    project(0)
    project(1)
    fillers = [piece(2), piece(6), piece(7),
               piece(3), piece(4), piece(8),
               piece(5), piece(9), skip,
               piece(10), piece(11), skip]
    _prompt_mlstm(z_scr, gc, gr, mlg_ref, bias_row_ref, bias_col_ref, y_ref,
                  c_scr, n_scr, m_scr, fillers)
    _prompt_conv(z_scr, convw_ref, convg_ref, y_ref, ubuf)

    @pl.when(chunk == pl.num_programs(1) - 1)
    def _():
        convs_ref[0] = ubuf[SUBLANES - 2:SUBLANES, :]
        c_out_ref[0] = c_scr[...]
        for h in range(N_HEADS):
            n_out_ref[0, h:h + 1, :] = n_scr[h]
            m_out_ref[0, h:h + 1, :] = m_scr[h, 0:1, :]


def _fused_prompt(x2d, g1, w_bf, wgate, batch, seq, conv_w, conv_g, ml_g, bias_row, bias_col):
    L = PROMPT_CHUNK
    nc = seq // L
    const = lambda b, c: (0, 0)
    return pl.pallas_call(
        _fused_prompt_kernel,
        grid=(batch, nc),
        in_specs=[
            pl.BlockSpec((L, D_MODEL), lambda b, c: (b * nc + c, 0)),
            pl.BlockSpec((1, D_MODEL), const),
            pl.BlockSpec((D_MODEL, N_MAIN), const, pipeline_mode=pl.Buffered(1)),
            pl.BlockSpec((LANES, D_MODEL), const),
            pl.BlockSpec((CONV_W, D_CONV), const),
            pl.BlockSpec((1, D_CONV), const),
            pl.BlockSpec((1, N_HEADS * D_V), const),
            pl.BlockSpec((1, LANES), const),
            pl.BlockSpec((SUBLANES, LANES), const),
        ],
        out_specs=[
            pl.BlockSpec((L, D_MODEL), lambda b, c: (b * nc + c, 0)),
            pl.BlockSpec((1, CONV_W - 1, D_CONV), lambda b, c: (b, 0, 0)),
            pl.BlockSpec((1, N_HEADS, D_K, D_V), lambda b, c: (b, 0, 0, 0)),
            pl.BlockSpec((1, N_HEADS, D_K), lambda b, c: (b, 0, 0)),
            pl.BlockSpec((1, N_HEADS, LANES), lambda b, c: (b, 0, 0)),
        ],
        out_shape=[
            jax.ShapeDtypeStruct((batch * seq, D_MODEL), BF16),
            jax.ShapeDtypeStruct((batch, CONV_W - 1, D_CONV), F32),
            jax.ShapeDtypeStruct((batch, N_HEADS, D_K, D_V), F32),
            jax.ShapeDtypeStruct((batch, N_HEADS, D_K), F32),
            jax.ShapeDtypeStruct((batch, N_HEADS, LANES), F32),
        ],
        scratch_shapes=[
            pltpu.VMEM((L, D_MODEL), BF16),
            pltpu.VMEM((L, N_MAIN), F32),
            pltpu.VMEM((L + 2 * SUBLANES, D_CONV), F32),
            pltpu.VMEM((N_HEADS, D_K, D_V), F32),
            pltpu.VMEM((N_HEADS, 1, D_K), F32),
            pltpu.VMEM((N_HEADS, SUBLANES, LANES), F32),
        ],
        compiler_params=_compiler_params(("arbitrary", "arbitrary"), 56),
        name="fused_prompt",
    )(x2d, g1, w_bf, wgate, conv_w, conv_g, ml_g, bias_row, bias_col)


def _mix_sample_kernel(z_ref, gcol_ref, grow_ref, mcol_ref, mrow_ref,
                       convs_in_ref, c_in_ref, n_in_ref,
                       convw_ref, convg_ref, mlg_ref, bias_row_ref, bias_col_ref,
                       y_ref, convs_ref, c_out_ref, n_out_ref, m_out_ref,
                       pad_scr, conv_scr, *, seq):
    L = z_ref.shape[0]
    group = L // seq
    stride = seq + SUBLANES

    for g in range(group):
        u = z_ref[g * seq:(g + 1) * seq, COLS_CG] * z_ref[g * seq:(g + 1) * seq, COLS_XT]
        base = g * stride
        pad_scr[base + SUBLANES - 2:base + SUBLANES, :] = convs_in_ref[g]
        pad_scr[base + SUBLANES:base + SUBLANES + seq, :] = u
        conv_scr[g * seq:(g + 1) * seq, :] = (
            pad_scr[base + SUBLANES - 2:base + SUBLANES - 2 + seq, :] * convw_ref[0:1, :]
            + pad_scr[base + SUBLANES - 1:base + SUBLANES - 1 + seq, :] * convw_ref[1:2, :]
            + u * convw_ref[2:3, :])
        convs_ref[g] = pad_scr[base + seq + SUBLANES - 2:base + seq + SUBLANES, :]
    y_ref[:, 0:D_CONV] = _conv_norm(z_ref[:, COLS_BG], conv_scr[...],
                                    convg_ref[...]).astype(y_ref.dtype)

    pre_c = gcol_ref[...] + bias_row_ref[...]
    pre_r = grow_ref[0:SUBLANES, :] + bias_col_ref[:, 0:1]
    row_id = lax.broadcasted_iota(jnp.int32, (L, L), 0)
    col_id = lax.broadcasted_iota(jnp.int32, (L, L), 1)
    same = (row_id // seq) == (col_id // seq)
    causal = same & (col_id <= row_id)
    lf_c = _log_sigmoid(pre_c)
    lf_r = _log_sigmoid(pre_r)
    b_c = _dot_exact(causal.astype(F32), lf_c)
    b_r = _dot_exact(lf_r, (same & (row_id <= col_id)).astype(F32))
    same_f = same.astype(F32)
    tot_c = _dot_exact(same_f, lf_c)
    tot_r = _dot_exact(lf_r, same_f)
    lane_seq = lax.broadcasted_iota(jnp.int32, (1, L), 1) // seq

    for h in range(N_HEADS):
        li_r = pre_r[h:h + 1, :]
        li_c = pre_c[:, h:h + 1]
        bc = b_c[:, N_HEADS + h:N_HEADS + h + 1]
        br = b_r[N_HEADS + h:N_HEADS + h + 1, :]
        bl_c = tot_c[:, N_HEADS + h:N_HEADS + h + 1]
        bl_r = tot_r[N_HEADS + h:N_HEADS + h + 1, :]
        m0_c = mcol_ref[:, h:h + 1]
        m0_r = mrow_ref[h:h + 1, :]
        qf = z_ref[:, _cols_q(h)] * (D_K ** -0.5)
        q_bf = qf.astype(BF16)
        kf = z_ref[:, _cols_k(h)]
        v_bf = z_ref[:, _cols_v(h)].astype(BF16)
        s, a_inter, m_t = _intra(q_bf, kf.astype(BF16), bc, br, li_r, m0_c, causal)

        qc_rows, qn_rows = [], []
        for g in range(group):
            rows = slice(g * seq, (g + 1) * seq)
            qc_rows.append(jnp.dot(qf[rows], c_in_ref[g, h], preferred_element_type=F32))
            qn_rows.append(jnp.sum(qf[rows] * n_in_ref[g, h:h + 1, :], axis=1, keepdims=True))
        num = (a_inter * jnp.concatenate(qc_rows, axis=0)
               + jnp.dot(s.astype(BF16), v_bf, preferred_element_type=F32))
        den = (a_inter * jnp.concatenate(qn_rows, axis=0)
               + jnp.sum(s, axis=1, keepdims=True))
        y_ref[:, D_CONV + h * D_V:D_CONV + (h + 1) * D_V] = _head_output(
            num, den, m_t, z_ref[:, _cols_o(h)],
            mlg_ref[:, h * D_V:(h + 1) * D_V]).astype(y_ref.dtype)

        g_r = bl_r - br + li_r
        gmax_c = jnp.max(jnp.where(same, g_r, -jnp.inf), axis=1, keepdims=True)
        gmax_r = jnp.max(jnp.where(same, gmax_c, -jnp.inf), axis=0, keepdims=True)
        m_new_c = jnp.maximum(bl_c + m0_c, gmax_c)
        m_new_r = jnp.maximum(bl_r + m0_r, gmax_r)
        decay_r = jnp.exp(bl_r + m0_r - m_new_r)
        kw = kf * jnp.exp(bl_c - bc + li_c - m_new_c)
        kw_t = kw.T
        for g in range(group):
            decay = decay_r[:, g * seq:g * seq + 1]
            kw_g = jnp.where(lane_seq == g, kw_t, 0.0).astype(BF16)
            c_out_ref[g, h] = decay * c_in_ref[g, h] + jnp.dot(
                kw_g, v_bf, preferred_element_type=F32)
            n_out_ref[g, h:h + 1, :] = (decay * n_in_ref[g, h:h + 1, :]
                                        + jnp.sum(kw[g * seq:(g + 1) * seq], axis=0, keepdims=True))
        m_out_ref[h:h + 1, :] = m_new_r
    m_out_ref[N_HEADS:, :] = jnp.zeros((SUBLANES - N_HEADS, L), F32)


def _mix_sample(z, gcol, grow, mcol, mrow, convs, c_state, n_state, batch, seq,
                conv_w, conv_g, ml_g, bias_row, bias_col):
    group = SAMPLE_GROUP
    L = group * seq
    const = lambda i: (0, 0)
    return pl.pallas_call(
        functools.partial(_mix_sample_kernel, seq=seq),
        grid=(batch // group,),
        in_specs=[
            pl.BlockSpec((L, N_MAIN), lambda i: (i, 0)),
            pl.BlockSpec((L, LANES), lambda i: (i, 0)),
            pl.BlockSpec((GATE_ROWS, L), lambda i: (0, i)),
            pl.BlockSpec((L, LANES), lambda i: (i, 0)),
            pl.BlockSpec((SUBLANES, L), lambda i: (0, i)),
            pl.BlockSpec((group, CONV_W - 1, D_CONV), lambda i: (i, 0, 0)),
            pl.BlockSpec((group, N_HEADS, D_K, D_V), lambda i: (i, 0, 0, 0)),
            pl.BlockSpec((group, N_HEADS, D_K), lambda i: (i, 0, 0)),
            pl.BlockSpec((CONV_W, D_CONV), const),
            pl.BlockSpec((1, D_CONV), const),
            pl.BlockSpec((1, N_HEADS * D_V), const),
            pl.BlockSpec((1, LANES), const),
            pl.BlockSpec((SUBLANES, LANES), const),
        ],
        out_specs=[
            pl.BlockSpec((L, D_MODEL), lambda i: (i, 0)),
            pl.BlockSpec((group, CONV_W - 1, D_CONV), lambda i: (i, 0, 0)),
            pl.BlockSpec((group, N_HEADS, D_K, D_V), lambda i: (i, 0, 0, 0)),
            pl.BlockSpec((group, N_HEADS, D_K), lambda i: (i, 0, 0)),
            pl.BlockSpec((SUBLANES, L), lambda i: (0, i)),
        ],
        out_shape=[
            jax.ShapeDtypeStruct((batch * seq, D_MODEL), BF16),
            jax.ShapeDtypeStruct((batch, CONV_W - 1, D_CONV), F32),
            jax.ShapeDtypeStruct((batch, N_HEADS, D_K, D_V), F32),
            jax.ShapeDtypeStruct((batch, N_HEADS, D_K), F32),
            jax.ShapeDtypeStruct((SUBLANES, batch * seq), F32),
        ],
        scratch_shapes=[
            pltpu.VMEM((group * (seq + SUBLANES), D_CONV), F32),
            pltpu.VMEM((L, D_CONV), F32),
        ],
        compiler_params=_compiler_params(("parallel",), 56),
        name="mix_sample",
    )(z, gcol, grow, mcol, mrow, convs, c_state, n_state,
      conv_w, conv_g, ml_g, bias_row, bias_col)


def _outproj_kernel(y_ref, w_ref, x_ref, o_ref, w_bf):
    @pl.when(pl.program_id(0) == 0)
    def _():
        for r in range(0, D_MODEL, W_SLAB):
            w_bf[r:r + W_SLAB, :] = w_ref[r:r + W_SLAB, :].astype(BF16)

    o_ref[...] = x_ref[...] + jnp.dot(y_ref[...], w_bf[...], preferred_element_type=F32)


def _outproj(y, w_out, x2d):
    tokens = x2d.shape[0]
    tm = min(512, tokens)
    return pl.pallas_call(
        _outproj_kernel,
        grid=(tokens // tm,),
        in_specs=[
            pl.BlockSpec((tm, D_MODEL), lambda i: (i, 0)),
            pl.BlockSpec((D_MODEL, D_MODEL), lambda i: (0, 0), pipeline_mode=pl.Buffered(1)),
            pl.BlockSpec((tm, D_MODEL), lambda i: (i, 0)),
        ],
        out_specs=pl.BlockSpec((tm, D_MODEL), lambda i: (i, 0)),
        out_shape=jax.ShapeDtypeStruct((tokens, D_MODEL), F32),
        scratch_shapes=[pltpu.VMEM((D_MODEL, D_MODEL), BF16)],
        compiler_params=_compiler_params(("arbitrary",), 52),
        name="outproj",
    )(y, w_out, x2d)


def _ffn_kernel(x_ref, g2_ref, wg_ref, wu_ref, wd_ref, gf_ref, o_ref, hn_ref, *, final_norm):
    j = pl.program_id(1)

    @pl.when(j == 0)
    def _():
        _rmsnorm_rows(x_ref, g2_ref[...], hn_ref)
        o_ref[...] = x_ref[...]

    hn = hn_ref[...]
    gate = jnp.dot(hn, wg_ref[...].astype(BF16), preferred_element_type=F32)
    up = jnp.dot(hn, wu_ref[...].astype(BF16), preferred_element_type=F32)
    act = (gate * jax.nn.sigmoid(gate)) * up
    o_ref[...] += jnp.dot(act.astype(BF16), wd_ref[...].astype(BF16),
                          preferred_element_type=F32)

    if final_norm:
        @pl.when(j == pl.num_programs(1) - 1)
        def _():
            _rmsnorm_rows(o_ref, gf_ref[...], o_ref)


def _ffn(x1, g2, wg, wu, wd, gf, final_norm):
    tokens = x1.shape[0]
    tm = min(1024, tokens)
    tf = 256
    return pl.pallas_call(
        functools.partial(_ffn_kernel, final_norm=final_norm),
        grid=(tokens // tm, D_FF // tf),
        in_specs=[
            pl.BlockSpec((tm, D_MODEL), lambda i, j: (i, 0)),
            pl.BlockSpec((1, D_MODEL), lambda i, j: (0, 0)),
            pl.BlockSpec((D_MODEL, tf), lambda i, j: (0, j)),
            pl.BlockSpec((D_MODEL, tf), lambda i, j: (0, j)),
            pl.BlockSpec((tf, D_MODEL), lambda i, j: (j, 0)),
            pl.BlockSpec((1, D_MODEL), lambda i, j: (0, 0)),
        ],
        out_specs=pl.BlockSpec((tm, D_MODEL), lambda i, j: (i, 0)),
        out_shape=jax.ShapeDtypeStruct((tokens, D_MODEL), F32),
        scratch_shapes=[pltpu.VMEM((tm, D_MODEL), BF16)],
        compiler_params=_compiler_params(("parallel", "arbitrary"), 58),
        name="ffn",
    )(x1, g2, wg, wu, wd, gf)


def _gate_params(w_t, b_i, b_f):
    wgate = jnp.pad(w_t[N_MAIN:, :], ((0, LANES - 2 * N_HEADS), (0, 0)))
    bias = jnp.concatenate([b_i, b_f]).astype(F32)
    bias_row = jnp.pad(bias, (0, LANES - 2 * N_HEADS))[None, :]
    bias_col = jnp.broadcast_to(bias[:, None], (SUBLANES, LANES))
    return wgate, bias_row, bias_col


def _tail(x2d, y, w_out, g2, wg, wu, wd, gf, final_norm):
    x1 = _outproj(y, w_out, x2d)
    return _ffn(x1, g2[None, :], wg, wu, wd, gf[None, :], final_norm)


def _stack(states, k):
    if len(states) == 1:
        return states[0][k][None]
    return jnp.stack([st[k] for st in states])


def kernel(x_prompt, x_sample, state_conv, state_mlstm_C, state_mlstm_n, state_mlstm_m,
           norm1_g, w_in, b_igate, b_fgate, conv_w, conv_out_g, mlstm_out_g, w_out,
           norm2_g, w_ffn_gate, w_ffn_up, w_ffn_down, final_norm_g):
    depth = w_in.shape[0]
    bp, sp, _ = x_prompt.shape
    bs, ss, _ = x_sample.shape
    hp = x_prompt.reshape(bp * sp, D_MODEL)
    hs = x_sample.reshape(bs * ss, D_MODEL)
    p_states, s_states = [], []
    for l in range(depth):
        final_norm = l == depth - 1
        w_t = w_in[l].T
        wgate, bias_row, bias_col = _gate_params(w_t, b_igate[l], b_fgate[l])
        w_bf = _cast_main_columns(w_t)
        g1 = norm1_g[l][None, :]
        conv_g = conv_out_g[l][None, :]
        ml_g = mlstm_out_g[l][None, :]
        tail = (w_out[l], norm2_g[l], w_ffn_gate[l], w_ffn_up[l], w_ffn_down[l],
                final_norm_g, final_norm)

        y, cb, c1, n1, m1 = _fused_prompt(hp, g1, w_bf, wgate, bp, sp, conv_w[l], conv_g, ml_g,
                                          bias_row, bias_col)
        hp = _tail(hp, y, *tail)
        p_states.append((cb, c1, n1, m1[:, :, 0]))

        z, gcol, grow = _inproj(hs, g1, w_bf, wgate)
        m_tok = jnp.repeat(state_mlstm_m[l], ss, axis=0)
        mcol = jnp.pad(m_tok, ((0, 0), (0, LANES - N_HEADS)))
        mrow = jnp.pad(m_tok.T, ((0, SUBLANES - N_HEADS), (0, 0)))
        y, cb, c1, n1, m_row = _mix_sample(z, gcol, grow, mcol, mrow, state_conv[l],
                                           state_mlstm_C[l], state_mlstm_n[l], bs, ss,
                                           conv_w[l], conv_g, ml_g, bias_row, bias_col)
        hs = _tail(hs, y, *tail)
        s_states.append((cb, c1, n1, m_row[:N_HEADS, ::ss].T))

    return (hp.reshape(bp, sp, D_MODEL), hs.reshape(bs, ss, D_MODEL),
            _stack(p_states, 0), _stack(p_states, 1), _stack(p_states, 2), _stack(p_states, 3),
            _stack(s_states, 0), _stack(s_states, 1), _stack(s_states, 2), _stack(s_states, 3))
```

```python
import functools

import jax
import jax.numpy as jnp
from jax import lax
from jax.experimental import pallas as pl
from jax.experimental.pallas import tpu as pltpu

F32 = jnp.float32
BF16 = jnp.bfloat16

D_MODEL = 2048
D_CONV = 1024
CONV_W = 3
N_HEADS = 4
D_K = 128
D_V = 256
D_FF = 5632
EPS = 1e-6

SRC_Q = 3 * D_CONV
SRC_K = SRC_Q + N_HEADS * D_K
SRC_V = SRC_K + N_HEADS * D_K
SRC_O = SRC_V + N_HEADS * D_V
N_MAIN = SRC_O + N_HEADS * D_V

HEAD_COLS = 2 * D_K + 2 * D_V
N_MLSTM = N_HEADS * HEAD_COLS
COLS_BG = slice(N_MLSTM, N_MLSTM + D_CONV)
COLS_CG = slice(N_MLSTM + D_CONV, N_MLSTM + 2 * D_CONV)
COLS_XT = slice(N_MLSTM + 2 * D_CONV, N_MLSTM + 3 * D_CONV)


def _cols_q(h):
    return slice(h * HEAD_COLS, h * HEAD_COLS + D_K)


def _cols_k(h):
    return slice(h * HEAD_COLS + D_K, h * HEAD_COLS + 2 * D_K)


def _cols_v(h):
    return slice(h * HEAD_COLS + 2 * D_K, h * HEAD_COLS + 2 * D_K + D_V)


def _cols_o(h):
    return slice(h * HEAD_COLS + 2 * D_K + D_V, (h + 1) * HEAD_COLS)


def _source_lane_block(j):
    per_head = HEAD_COLS // LANES
    h, r = j // per_head, j % per_head
    v_blocks = D_V // LANES
    mlstm_src = jnp.where(
        r == 0, SRC_Q // LANES + h,
        jnp.where(r == 1, SRC_K // LANES + h,
                  jnp.where(r < 2 + v_blocks,
                            SRC_V // LANES + v_blocks * h + (r - 2),
                            SRC_O // LANES + v_blocks * h + (r - 2 - v_blocks))))
    return jnp.where(j < N_MLSTM // LANES, mlstm_src, j - N_MLSTM // LANES)

LANES = 128
SUBLANES = 8
GATE_ROWS = 16
MIB = 1024 * 1024

PROMPT_CHUNK = 256
SAMPLE_GROUP = 16
NORM_ROWS = 64
W_SLAB = 512


def _compiler_params(semantics, vmem_mib):
    return pltpu.CompilerParams(dimension_semantics=semantics,
                                vmem_limit_bytes=vmem_mib * MIB)


def _rmsnorm_piece(x, gain, dtype):
    ms = jnp.mean(x * x, axis=-1, keepdims=True)
    return ((x * lax.rsqrt(ms + EPS)) * gain).astype(dtype)


def _rmsnorm_rows(src_ref, gain, dst_ref, copy_ref=None):
    rows = src_ref.shape[0]
    chunk = min(NORM_ROWS, rows)

    def body(i, carry):
        r = pl.multiple_of(i * chunk, chunk)
        x = src_ref[pl.ds(r, chunk), :].astype(F32)
        if copy_ref is not None:
            copy_ref[pl.ds(r, chunk), :] = x
        dst_ref[pl.ds(r, chunk), :] = _rmsnorm_piece(x, gain, dst_ref.dtype)
        return carry

    lax.fori_loop(0, rows // chunk, body, 0, unroll=2)


def _cast_kernel(*refs):
    *w_refs, o_ref = refs
    for r, w_ref in enumerate(w_refs):
        o_ref[:, r * LANES:(r + 1) * LANES] = w_ref[...].T.astype(o_ref.dtype)


def _cast_main_columns(w_t):
    per_slab = W_SLAB // LANES
    in_specs = [
        pl.BlockSpec((LANES, D_MODEL),
                     lambda j, r=r: (_source_lane_block(per_slab * j + r), 0))
        for r in range(per_slab)
    ]
    return pl.pallas_call(
        _cast_kernel,
        grid=(N_MAIN // W_SLAB,),
        in_specs=in_specs,
        out_specs=pl.BlockSpec((D_MODEL, W_SLAB), lambda j: (0, j)),
        out_shape=jax.ShapeDtypeStruct((D_MODEL, N_MAIN), BF16),
        compiler_params=_compiler_params(("parallel",), 32),
        name="cast_w_in",
    )(*([w_t] * per_slab))


def _gate_dot(xn, wgate_t_ref):
    return lax.dot_general(xn, wgate_t_ref[...].astype(BF16), (((1,), (1,)), ((), ())),
                           preferred_element_type=F32)


def _inproj_kernel(x_ref, g_ref, w_ref, wgate_ref, z_ref, gcol_ref, grow_ref, xn_ref):
    @pl.when(pl.program_id(1) == 0)
    def _():
        _rmsnorm_rows(x_ref, g_ref[...], xn_ref)
        gc = _gate_dot(xn_ref[...], wgate_ref)
        gcol_ref[...] = gc
        grow_ref[...] = gc.T[0:GATE_ROWS, :]

    z_ref[...] = jnp.dot(xn_ref[...], w_ref[...], preferred_element_type=F32)


def _inproj(x2d, g1, w_bf, wgate):
    tokens = x2d.shape[0]
    tm = min(1024, tokens)
    tn = W_SLAB
    return pl.pallas_call(
        _inproj_kernel,
        grid=(tokens // tm, N_MAIN // tn),
        in_specs=[
            pl.BlockSpec((tm, D_MODEL), lambda i, j: (i, 0)),
            pl.BlockSpec((1, D_MODEL), lambda i, j: (0, 0)),
            pl.BlockSpec((D_MODEL, tn), lambda i, j: (0, j)),
            pl.BlockSpec((LANES, D_MODEL), lambda i, j: (0, 0)),
        ],
        out_specs=[
            pl.BlockSpec((tm, tn), lambda i, j: (i, j)),
            pl.BlockSpec((tm, LANES), lambda i, j: (i, 0)),
            pl.BlockSpec((GATE_ROWS, tm), lambda i, j: (0, i)),
        ],
        out_shape=[
            jax.ShapeDtypeStruct((tokens, N_MAIN), F32),
            jax.ShapeDtypeStruct((tokens, LANES), F32),
            jax.ShapeDtypeStruct((GATE_ROWS, tokens), F32),
        ],
        scratch_shapes=[pltpu.VMEM((tm, D_MODEL), BF16)],
        compiler_params=_compiler_params(("parallel", "arbitrary"), 48),
        name="inproj",
    )(x2d, g1, w_bf, wgate)


def _log_sigmoid(x):
    return jnp.minimum(x, 0.0) - jnp.log1p(jnp.exp(-jnp.abs(x)))


def _dot_exact(a, b):
    return jnp.dot(a, b, precision=lax.Precision.HIGHEST, preferred_element_type=F32)


def _intra(q_bf, k_bf, b_col, b_row, li_row, m0_col, mask):
    dmat = jnp.where(mask, b_col - b_row + li_row, -jnp.inf)
    inter = b_col + m0_col
    m_t = jnp.maximum(inter, jnp.max(dmat, axis=1, keepdims=True))
    w = jnp.exp(dmat - m_t)
    a_inter = jnp.exp(inter - m_t)
    qk = lax.dot_general(q_bf, k_bf, (((1,), (1,)), ((), ())), preferred_element_type=F32)
    return qk * w, a_inter, m_t


def _head_output(num, den, m_t, o_gate, gain):
    denom = jnp.maximum(jnp.abs(den), jnp.exp(-m_t))
    h = num * (1.0 / denom)
    hn = (h * lax.rsqrt(jnp.mean(h * h, axis=1, keepdims=True) + EPS)) * gain
    return jax.nn.sigmoid(o_gate) * hn


def _conv_norm(bg, conv, gain):
    yc = bg * conv
    return (yc * lax.rsqrt(jnp.mean(yc * yc, axis=1, keepdims=True) + EPS)) * gain


def _prompt_conv(z, convw_ref, convg_ref, y_ref, ubuf):
    L = z.shape[0]
    u = z[:, COLS_CG] * z[:, COLS_XT]
    ubuf[SUBLANES:SUBLANES + L, :] = u
    conv = (ubuf[SUBLANES - 2:SUBLANES - 2 + L, :] * convw_ref[0:1, :]
            + ubuf[SUBLANES - 1:SUBLANES - 1 + L, :] * convw_ref[1:2, :]
            + u * convw_ref[2:3, :])
    y_ref[:, 0:D_CONV] = _conv_norm(z[:, COLS_BG], conv,
                                    convg_ref[...]).astype(y_ref.dtype)
    ubuf[0:SUBLANES, :] = ubuf[L:L + SUBLANES, :]


def _prompt_mlstm(z, gc, gr, mlg_ref, bias_row_ref, bias_col_ref, y_ref, c_scr, n_scr, m_scr,
                  fillers):
    L = z.shape[0]
    fillers = list(fillers)

    def fill():
        if fillers:
            fillers.pop(0)()

    pre_c = gc + bias_row_ref[...]
    pre_r = gr + bias_col_ref[:, 0:1]
    row_id = lax.broadcasted_iota(jnp.int32, (L, L), 0)
    col_id = lax.broadcasted_iota(jnp.int32, (L, L), 1)
    causal = col_id <= row_id
    b_c = _dot_exact(causal.astype(F32), _log_sigmoid(pre_c))
    b_r = _dot_exact(_log_sigmoid(pre_r), (row_id <= col_id).astype(F32))

    for h in range(N_HEADS):
        li_r = pre_r[h:h + 1, :]
        li_c = pre_c[:, h:h + 1]
        bc = b_c[:, N_HEADS + h:N_HEADS + h + 1]
        br = b_r[N_HEADS + h:N_HEADS + h + 1, :]
        m0 = m_scr[h, 0:1, 0:1]
        qf = z[:, _cols_q(h)] * (D_K ** -0.5)
        q_bf = qf.astype(BF16)
        kf = z[:, _cols_k(h)]
        v_bf = z[:, _cols_v(h)].astype(BF16)
        s, a_inter, m_t = _intra(q_bf, kf.astype(BF16), bc, br, li_r, m0, causal)
        fill()
        c_old = c_scr[h]
        n_old = n_scr[h]
        num = (a_inter * jnp.dot(q_bf, c_old.astype(BF16), preferred_element_type=F32)
               + jnp.dot(s.astype(BF16), v_bf, preferred_element_type=F32))
        den = (a_inter * jnp.sum(qf * n_old, axis=1, keepdims=True)
               + jnp.sum(s, axis=1, keepdims=True))
        fill()
        y_ref[:, D_CONV + h * D_V:D_CONV + (h + 1) * D_V] = _head_output(
            num, den, m_t, z[:, _cols_o(h)],
            mlg_ref[:, h * D_V:(h + 1) * D_V]).astype(y_ref.dtype)

        b_last = br[:, L - 1:L]
        m_new = jnp.maximum(b_last + m0,
                            jnp.max(b_last - br + li_r, axis=1, keepdims=True))
        decay = jnp.exp(b_last + m0 - m_new)
        kw = kf * jnp.exp(b_last - bc + li_c - m_new)
        c_scr[h] = decay * c_old + lax.dot_general(
            kw.astype(BF16), v_bf, (((0,), (0,)), ((), ())), preferred_element_type=F32)
        n_scr[h] = decay * n_old + jnp.sum(kw, axis=0, keepdims=True)
        m_scr[h] = jnp.broadcast_to(m_new, (SUBLANES, LANES))
        fill()
    while fillers:
        fill()


def _fused_prompt_kernel(x_ref, g1_ref, w_ref, wgate_ref, convw_ref, convg_ref, mlg_ref,
                         bias_row_ref, bias_col_ref,
                         y_ref, convs_ref, c_out_ref, n_out_ref, m_out_ref,
                         xn_scr, z_scr, ubuf, c_scr, n_scr, m_scr):
    chunk = pl.program_id(1)
    L = x_ref.shape[0]

    @pl.when(chunk == 0)
    def _():
        ubuf[0:SUBLANES, :] = jnp.zeros((SUBLANES, D_CONV), F32)
        c_scr[...] = jnp.zeros(c_scr.shape, F32)
        n_scr[...] = jnp.zeros(n_scr.shape, F32)
        m_scr[...] = jnp.zeros(m_scr.shape, F32)

    gain = g1_ref[...]
    for r in range(0, L, NORM_ROWS):
        xn_scr[r:r + NORM_ROWS, :] = _rmsnorm_piece(x_ref[r:r + NORM_ROWS, :], gain, BF16)
    xn = xn_scr[...]
    gc = _gate_dot(xn, wgate_ref)
    gr = gc.T[0:SUBLANES, :]

    def project(j):
        cols = slice(j * W_SLAB, (j + 1) * W_SLAB)
        z_scr[:, cols] = jnp.dot(xn_scr[...], w_ref[:, cols], preferred_element_type=F32)

    piece = lambda j: functools.partial(project, j)
    skip = lambda: None
    project(0)
    project(1)
    fillers = [piece(2), piece(6), piece(7),
               piece(3), piece(4), piece(8),
               piece(5), piece(9), skip,
               piece(10), piece(11), skip]
    _prompt_mlstm(z_scr, gc, gr, mlg_ref, bias_row_ref, bias_col_ref, y_ref,
                  c_scr, n_scr, m_scr, fillers)
    _prompt_conv(z_scr, convw_ref, convg_ref, y_ref, ubuf)

    @pl.when(chunk == pl.num_programs(1) - 1)
    def _():
        convs_ref[0] = ubuf[SUBLANES - 2:SUBLANES, :]
        c_out_ref[0] = c_scr[...]
        for h in range(N_HEADS):
            n_out_ref[0, h:h + 1, :] = n_scr[h]
            m_out_ref[0, h:h + 1, :] = m_scr[h, 0:1, :]


def _fused_prompt(x2d, g1, w_bf, wgate, batch, seq, conv_w, conv_g, ml_g, bias_row, bias_col):
    L = PROMPT_CHUNK
    nc = seq // L
    const = lambda b, c: (0, 0)
    return pl.pallas_call(
        _fused_prompt_kernel,
        grid=(batch, nc),
        in_specs=[
            pl.BlockSpec((L, D_MODEL), lambda b, c: (b * nc + c, 0)),
            pl.BlockSpec((1, D_MODEL), const),
            pl.BlockSpec((D_MODEL, N_MAIN), const, pipeline_mode=pl.Buffered(1)),
            pl.BlockSpec((LANES, D_MODEL), const),
            pl.BlockSpec((CONV_W, D_CONV), const),
            pl.BlockSpec((1, D_CONV), const),
            pl.BlockSpec((1, N_HEADS * D_V), const),
            pl.BlockSpec((1, LANES), const),
            pl.BlockSpec((SUBLANES, LANES), const),
        ],
        out_specs=[
            pl.BlockSpec((L, D_MODEL), lambda b, c: (b * nc + c, 0)),
            pl.BlockSpec((1, CONV_W - 1, D_CONV), lambda b, c: (b, 0, 0)),
            pl.BlockSpec((1, N_HEADS, D_K, D_V), lambda b, c: (b, 0, 0, 0)),
            pl.BlockSpec((1, N_HEADS, D_K), lambda b, c: (b, 0, 0)),
            pl.BlockSpec((1, N_HEADS, LANES), lambda b, c: (b, 0, 0)),
        ],
        out_shape=[
            jax.ShapeDtypeStruct((batch * seq, D_MODEL), BF16),
            jax.ShapeDtypeStruct((batch, CONV_W - 1, D_CONV), F32),
            jax.ShapeDtypeStruct((batch, N_HEADS, D_K, D_V), F32),
            jax.ShapeDtypeStruct((batch, N_HEADS, D_K), F32),
            jax.ShapeDtypeStruct((batch, N_HEADS, LANES), F32),
        ],
        scratch_shapes=[
            pltpu.VMEM((L, D_MODEL), BF16),
            pltpu.VMEM((L, N_MAIN), F32),
            pltpu.VMEM((L + 2 * SUBLANES, D_CONV), F32),
            pltpu.VMEM((N_HEADS, D_K, D_V), F32),
            pltpu.VMEM((N_HEADS, 1, D_K), F32),
            pltpu.VMEM((N_HEADS, SUBLANES, LANES), F32),
        ],
        compiler_params=_compiler_params(("arbitrary", "arbitrary"), 56),
        name="fused_prompt",
    )(x2d, g1, w_bf, wgate, conv_w, conv_g, ml_g, bias_row, bias_col)


def _mix_sample_kernel(z_ref, gcol_ref, grow_ref, mcol_ref, mrow_ref,
                       convs_in_ref, c_in_ref, n_in_ref,
                       convw_ref, convg_ref, mlg_ref, bias_row_ref, bias_col_ref,
                       y_ref, convs_ref, c_out_ref, n_out_ref, m_out_ref,
                       pad_scr, conv_scr, *, seq):
    L = z_ref.shape[0]
    group = L // seq
    stride = seq + SUBLANES

    for g in range(group):
        u = z_ref[g * seq:(g + 1) * seq, COLS_CG] * z_ref[g * seq:(g + 1) * seq, COLS_XT]
        base = g * stride
        pad_scr[base + SUBLANES - 2:base + SUBLANES, :] = convs_in_ref[g]
        pad_scr[base + SUBLANES:base + SUBLANES + seq, :] = u
        conv_scr[g * seq:(g + 1) * seq, :] = (
            pad_scr[base + SUBLANES - 2:base + SUBLANES - 2 + seq, :] * convw_ref[0:1, :]
            + pad_scr[base + SUBLANES - 1:base + SUBLANES - 1 + seq, :] * convw_ref[1:2, :]
            + u * convw_ref[2:3, :])
        convs_ref[g] = pad_scr[base + seq + SUBLANES - 2:base + seq + SUBLANES, :]
    y_ref[:, 0:D_CONV] = _conv_norm(z_ref[:, COLS_BG], conv_scr[...],
                                    convg_ref[...]).astype(y_ref.dtype)

    pre_c = gcol_ref[...] + bias_row_ref[...]
    pre_r = grow_ref[0:SUBLANES, :] + bias_col_ref[:, 0:1]
    row_id = lax.broadcasted_iota(jnp.int32, (L, L), 0)
    col_id = lax.broadcasted_iota(jnp.int32, (L, L), 1)
    same = (row_id // seq) == (col_id // seq)
    causal = same & (col_id <= row_id)
    lf_c = _log_sigmoid(pre_c)
    lf_r = _log_sigmoid(pre_r)
    b_c = _dot_exact(causal.astype(F32), lf_c)
    b_r = _dot_exact(lf_r, (same & (row_id <= col_id)).astype(F32))
    same_f = same.astype(F32)
    tot_c = _dot_exact(same_f, lf_c)
    tot_r = _dot_exact(lf_r, same_f)
    lane_seq = lax.broadcasted_iota(jnp.int32, (1, L), 1) // seq

    for h in range(N_HEADS):
        li_r = pre_r[h:h + 1, :]
        li_c = pre_c[:, h:h + 1]
        bc = b_c[:, N_HEADS + h:N_HEADS + h + 1]
        br = b_r[N_HEADS + h:N_HEADS + h + 1, :]
        bl_c = tot_c[:, N_HEADS + h:N_HEADS + h + 1]
        bl_r = tot_r[N_HEADS + h:N_HEADS + h + 1, :]
        m0_c = mcol_ref[:, h:h + 1]
        m0_r = mrow_ref[h:h + 1, :]
        qf = z_ref[:, _cols_q(h)] * (D_K ** -0.5)
        q_bf = qf.astype(BF16)
        kf = z_ref[:, _cols_k(h)]
        v_bf = z_ref[:, _cols_v(h)].astype(BF16)
        s, a_inter, m_t = _intra(q_bf, kf.astype(BF16), bc, br, li_r, m0_c, causal)

        qc_rows, qn_rows = [], []
        for g in range(group):
            rows = slice(g * seq, (g + 1) * seq)
            qc_rows.append(jnp.dot(qf[rows], c_in_ref[g, h], preferred_element_type=F32))
            qn_rows.append(jnp.sum(qf[rows] * n_in_ref[g, h:h + 1, :], axis=1, keepdims=True))
        num = (a_inter * jnp.concatenate(qc_rows, axis=0)
               + jnp.dot(s.astype(BF16), v_bf, preferred_element_type=F32))
        den = (a_inter * jnp.concatenate(qn_rows, axis=0)
               + jnp.sum(s, axis=1, keepdims=True))
        y_ref[:, D_CONV + h * D_V:D_CONV + (h + 1) * D_V] = _head_output(
            num, den, m_t, z_ref[:, _cols_o(h)],
            mlg_ref[:, h * D_V:(h + 1) * D_V]).astype(y_ref.dtype)

        g_r = bl_r - br + li_r
        gmax_c = jnp.max(jnp.where(same, g_r, -jnp.inf), axis=1, keepdims=True)
        gmax_r = jnp.max(jnp.where(same, gmax_c, -jnp.inf), axis=0, keepdims=True)
        m_new_c = jnp.maximum(bl_c + m0_c, gmax_c)
        m_new_r = jnp.maximum(bl_r + m0_r, gmax_r)
        decay_r = jnp.exp(bl_r + m0_r - m_new_r)
        kw = kf * jnp.exp(bl_c - bc + li_c - m_new_c)
        kw_t = kw.T
        for g in range(group):
            decay = decay_r[:, g * seq:g * seq + 1]
            kw_g = jnp.where(lane_seq == g, kw_t, 0.0).astype(BF16)
            c_out_ref[g, h] = decay * c_in_ref[g, h] + jnp.dot(
                kw_g, v_bf, preferred_element_type=F32)
            n_out_ref[g, h:h + 1, :] = (decay * n_in_ref[g, h:h + 1, :]
                                        + jnp.sum(kw[g * seq:(g + 1) * seq], axis=0, keepdims=True))
        m_out_ref[h:h + 1, :] = m_new_r
    m_out_ref[N_HEADS:, :] = jnp.zeros((SUBLANES - N_HEADS, L), F32)


def _mix_sample(z, gcol, grow, mcol, mrow, convs, c_state, n_state, batch, seq,
                conv_w, conv_g, ml_g, bias_row, bias_col):
    group = SAMPLE_GROUP
    L = group * seq
    const = lambda i: (0, 0)
    return pl.pallas_call(
        functools.partial(_mix_sample_kernel, seq=seq),
        grid=(batch // group,),
        in_specs=[
            pl.BlockSpec((L, N_MAIN), lambda i: (i, 0)),
            pl.BlockSpec((L, LANES), lambda i: (i, 0)),
            pl.BlockSpec((GATE_ROWS, L), lambda i: (0, i)),
            pl.BlockSpec((L, LANES), lambda i: (i, 0)),
            pl.BlockSpec((SUBLANES, L), lambda i: (0, i)),
            pl.BlockSpec((group, CONV_W - 1, D_CONV), lambda i: (i, 0, 0)),
            pl.BlockSpec((group, N_HEADS, D_K, D_V), lambda i: (i, 0, 0, 0)),
            pl.BlockSpec((group, N_HEADS, D_K), lambda i: (i, 0, 0)),
            pl.BlockSpec((CONV_W, D_CONV), const),
            pl.BlockSpec((1, D_CONV), const),
            pl.BlockSpec((1, N_HEADS * D_V), const),
            pl.BlockSpec((1, LANES), const),
            pl.BlockSpec((SUBLANES, LANES), const),
        ],
        out_specs=[
            pl.BlockSpec((L, D_MODEL), lambda i: (i, 0)),
            pl.BlockSpec((group, CONV_W - 1, D_CONV), lambda i: (i, 0, 0)),
            pl.BlockSpec((group, N_HEADS, D_K, D_V), lambda i: (i, 0, 0, 0)),
            pl.BlockSpec((group, N_HEADS, D_K), lambda i: (i, 0, 0)),
            pl.BlockSpec((SUBLANES, L), lambda i: (0, i)),
        ],
        out_shape=[
            jax.ShapeDtypeStruct((batch * seq, D_MODEL), BF16),
            jax.ShapeDtypeStruct((batch, CONV_W - 1, D_CONV), F32),
            jax.ShapeDtypeStruct((batch, N_HEADS, D_K, D_V), F32),
            jax.ShapeDtypeStruct((batch, N_HEADS, D_K), F32),
            jax.ShapeDtypeStruct((SUBLANES, batch * seq), F32),
        ],
        scratch_shapes=[
            pltpu.VMEM((group * (seq + SUBLANES), D_CONV), F32),
            pltpu.VMEM((L, D_CONV), F32),
        ],
        compiler_params=_compiler_params(("parallel",), 56),
        name="mix_sample",
    )(z, gcol, grow, mcol, mrow, convs, c_state, n_state,
      conv_w, conv_g, ml_g, bias_row, bias_col)


def _outproj_kernel(y_ref, w_ref, x_ref, o_ref, w_bf):
    @pl.when(pl.program_id(0) == 0)
    def _():
        for r in range(0, D_MODEL, W_SLAB):
            w_bf[r:r + W_SLAB, :] = w_ref[r:r + W_SLAB, :].astype(BF16)

    o_ref[...] = x_ref[...] + jnp.dot(y_ref[...], w_bf[...], preferred_element_type=F32)


def _outproj(y, w_out, x2d):
    tokens = x2d.shape[0]
    tm = min(512, tokens)
    return pl.pallas_call(
        _outproj_kernel,
        grid=(tokens // tm,),
        in_specs=[
            pl.BlockSpec((tm, D_MODEL), lambda i: (i, 0)),
            pl.BlockSpec((D_MODEL, D_MODEL), lambda i: (0, 0), pipeline_mode=pl.Buffered(1)),
            pl.BlockSpec((tm, D_MODEL), lambda i: (i, 0)),
        ],
        out_specs=pl.BlockSpec((tm, D_MODEL), lambda i: (i, 0)),
        out_shape=jax.ShapeDtypeStruct((tokens, D_MODEL), F32),
        scratch_shapes=[pltpu.VMEM((D_MODEL, D_MODEL), BF16)],
        compiler_params=_compiler_params(("arbitrary",), 52),
        name="outproj",
    )(y, w_out, x2d)


def _ffn_kernel(x_ref, g2_ref, wg_ref, wu_ref, wd_ref, gf_ref, o_ref, hn_ref, *, final_norm):
    j = pl.program_id(1)

    @pl.when(j == 0)
    def _():
        _rmsnorm_rows(x_ref, g2_ref[...], hn_ref, copy_ref=o_ref)

    hn = hn_ref[...]
    gate = jnp.dot(hn, wg_ref[...].astype(BF16), preferred_element_type=F32)
    up = jnp.dot(hn, wu_ref[...].astype(BF16), preferred_element_type=F32)
    act = (gate * jax.nn.sigmoid(gate)) * up
    o_ref[...] += jnp.dot(act.astype(BF16), wd_ref[...].astype(BF16),
                          preferred_element_type=F32)

    if final_norm:
        @pl.when(j == pl.num_programs(1) - 1)
        def _():
            _rmsnorm_rows(o_ref, gf_ref[...], o_ref)


def _ffn(x1, g2, wg, wu, wd, gf, final_norm):
    tokens = x1.shape[0]
    tm = min(1024, tokens)
    tf = 512
    return pl.pallas_call(
        functools.partial(_ffn_kernel, final_norm=final_norm),
        grid=(tokens // tm, D_FF // tf),
        in_specs=[
            pl.BlockSpec((tm, D_MODEL), lambda i, j: (i, 0), pipeline_mode=pl.Buffered(1)),
            pl.BlockSpec((1, D_MODEL), lambda i, j: (0, 0)),
            pl.BlockSpec((D_MODEL, tf), lambda i, j: (0, j)),
            pl.BlockSpec((D_MODEL, tf), lambda i, j: (0, j)),
            pl.BlockSpec((tf, D_MODEL), lambda i, j: (j, 0)),
            pl.BlockSpec((1, D_MODEL), lambda i, j: (0, 0)),
        ],
        out_specs=pl.BlockSpec((tm, D_MODEL), lambda i, j: (i, 0)),
        out_shape=jax.ShapeDtypeStruct((tokens, D_MODEL), F32),
        scratch_shapes=[pltpu.VMEM((tm, D_MODEL), BF16)],
        compiler_params=_compiler_params(("parallel", "arbitrary"), 62),
        name="ffn",
    )(x1, g2, wg, wu, wd, gf)


def _gate_params(w_t, b_i, b_f):
    wgate = jnp.pad(w_t[N_MAIN:, :], ((0, LANES - 2 * N_HEADS), (0, 0)))
    bias = jnp.concatenate([b_i, b_f]).astype(F32)
    bias_row = jnp.pad(bias, (0, LANES - 2 * N_HEADS))[None, :]
    bias_col = jnp.broadcast_to(bias[:, None], (SUBLANES, LANES))
    return wgate, bias_row, bias_col


def _tail(x2d, y, w_out, g2, wg, wu, wd, gf, final_norm):
    x1 = _outproj(y, w_out, x2d)
    return _ffn(x1, g2[None, :], wg, wu, wd, gf[None, :], final_norm)


def _stack(states, k):
    if len(states) == 1:
        return states[0][k][None]
    return jnp.stack([st[k] for st in states])


def kernel(x_prompt, x_sample, state_conv, state_mlstm_C, state_mlstm_n, state_mlstm_m,
           norm1_g, w_in, b_igate, b_fgate, conv_w, conv_out_g, mlstm_out_g, w_out,
           norm2_g, w_ffn_gate, w_ffn_up, w_ffn_down, final_norm_g):
    depth = w_in.shape[0]
    bp, sp, _ = x_prompt.shape
    bs, ss, _ = x_sample.shape
    hp = x_prompt.reshape(bp * sp, D_MODEL)
    hs = x_sample.reshape(bs * ss, D_MODEL)
    p_states, s_states = [], []
    for l in range(depth):
        final_norm = l == depth - 1
        w_t = w_in[l].T
        wgate, bias_row, bias_col = _gate_params(w_t, b_igate[l], b_fgate[l])
        w_bf = _cast_main_columns(w_t)
        g1 = norm1_g[l][None, :]
        conv_g = conv_out_g[l][None, :]
        ml_g = mlstm_out_g[l][None, :]
        tail = (w_out[l], norm2_g[l], w_ffn_gate[l], w_ffn_up[l], w_ffn_down[l],
                final_norm_g, final_norm)

        y, cb, c1, n1, m1 = _fused_prompt(hp, g1, w_bf, wgate, bp, sp, conv_w[l], conv_g, ml_g,
                                          bias_row, bias_col)
        hp = _tail(hp, y, *tail)
        p_states.append((cb, c1, n1, m1[:, :, 0]))

        z, gcol, grow = _inproj(hs, g1, w_bf, wgate)
        m_tok = jnp.repeat(state_mlstm_m[l], ss, axis=0)
        mcol = jnp.pad(m_tok, ((0, 0), (0, LANES - N_HEADS)))
        mrow = jnp.pad(m_tok.T, ((0, SUBLANES - N_HEADS), (0, 0)))
        y, cb, c1, n1, m_row = _mix_sample(z, gcol, grow, mcol, mrow, state_conv[l],
                                           state_mlstm_C[l], state_mlstm_n[l], bs, ss,
                                           conv_w[l], conv_g, ml_g, bias_row, bias_col)
        hs = _tail(hs, y, *tail)
        s_states.append((cb, c1, n1, m_row[:N_HEADS, ::ss].T))

    return (hp.reshape(bp, sp, D_MODEL), hs.reshape(bs, ss, D_MODEL),
            _stack(p_states, 0), _stack(p_states, 1), _stack(p_states, 2), _stack(p_states, 3),
            _stack(s_states, 0), _stack(s_states, 1), _stack(s_states, 2), _stack(s_states, 3))
```

```python
import functools

import jax
import jax.numpy as jnp
from jax import lax
from jax.experimental import pallas as pl
from jax.experimental.pallas import tpu as pltpu

F32 = jnp.float32
BF16 = jnp.bfloat16

D_MODEL = 2048
D_CONV = 1024
CONV_W = 3
N_HEADS = 4
D_K = 128
D_V = 256
D_FF = 5632
EPS = 1e-6

SRC_Q = 3 * D_CONV
SRC_K = SRC_Q + N_HEADS * D_K
SRC_V = SRC_K + N_HEADS * D_K
SRC_O = SRC_V + N_HEADS * D_V
N_MAIN = SRC_O + N_HEADS * D_V

HEAD_COLS = 2 * D_K + 2 * D_V
N_MLSTM = N_HEADS * HEAD_COLS
COLS_BG = slice(N_MLSTM, N_MLSTM + D_CONV)
COLS_CG = slice(N_MLSTM + D_CONV, N_MLSTM + 2 * D_CONV)
COLS_XT = slice(N_MLSTM + 2 * D_CONV, N_MLSTM + 3 * D_CONV)


def _cols_q(h):
    return slice(h * HEAD_COLS, h * HEAD_COLS + D_K)


def _cols_k(h):
    return slice(h * HEAD_COLS + D_K, h * HEAD_COLS + 2 * D_K)


def _cols_v(h):
    return slice(h * HEAD_COLS + 2 * D_K, h * HEAD_COLS + 2 * D_K + D_V)


def _cols_o(h):
    return slice(h * HEAD_COLS + 2 * D_K + D_V, (h + 1) * HEAD_COLS)


def _source_lane_block(j):
    per_head = HEAD_COLS // LANES
    h, r = j // per_head, j % per_head
    v_blocks = D_V // LANES
    mlstm_src = jnp.where(
        r == 0, SRC_Q // LANES + h,
        jnp.where(r == 1, SRC_K // LANES + h,
                  jnp.where(r < 2 + v_blocks,
                            SRC_V // LANES + v_blocks * h + (r - 2),
                            SRC_O // LANES + v_blocks * h + (r - 2 - v_blocks))))
    return jnp.where(j < N_MLSTM // LANES, mlstm_src, j - N_MLSTM // LANES)

LANES = 128
SUBLANES = 8
GATE_ROWS = 16
MIB = 1024 * 1024

PROMPT_CHUNK = 256
SAMPLE_GROUP = 16
NORM_ROWS = 64
W_SLAB = 512
FFN_TILE = 512


def _compiler_params(semantics, vmem_mib):
    return pltpu.CompilerParams(dimension_semantics=semantics,
                                vmem_limit_bytes=vmem_mib * MIB)


def _rmsnorm_piece(x, gain, dtype):
    ms = jnp.mean(x * x, axis=-1, keepdims=True)
    return ((x * lax.rsqrt(ms + EPS)) * gain).astype(dtype)


def _rmsnorm_rows(src_ref, gain, dst_ref, copy_ref=None):
    rows = src_ref.shape[0]
    chunk = min(NORM_ROWS, rows)

    def body(i, carry):
        r = pl.multiple_of(i * chunk, chunk)
        x = src_ref[pl.ds(r, chunk), :].astype(F32)
        if copy_ref is not None:
            copy_ref[pl.ds(r, chunk), :] = x
        dst_ref[pl.ds(r, chunk), :] = _rmsnorm_piece(x, gain, dst_ref.dtype)
        return carry

    lax.fori_loop(0, rows // chunk, body, 0, unroll=2)


def _cast_kernel(*refs):
    *w_refs, o_ref = refs
    for r, w_ref in enumerate(w_refs):
        o_ref[:, r * LANES:(r + 1) * LANES] = w_ref[...].T.astype(o_ref.dtype)


def _cast_main_columns(w_t):
    per_slab = W_SLAB // LANES
    in_specs = [
        pl.BlockSpec((LANES, D_MODEL),
                     lambda j, r=r: (_source_lane_block(per_slab * j + r), 0))
        for r in range(per_slab)
    ]
    return pl.pallas_call(
        _cast_kernel,
        grid=(N_MAIN // W_SLAB,),
        in_specs=in_specs,
        out_specs=pl.BlockSpec((D_MODEL, W_SLAB), lambda j: (0, j)),
        out_shape=jax.ShapeDtypeStruct((D_MODEL, N_MAIN), BF16),
        compiler_params=_compiler_params(("parallel",), 32),
        name="cast_w_in",
    )(*([w_t] * per_slab))


def _gate_dot(xn, wgate_t_ref):
    return lax.dot_general(xn, wgate_t_ref[...].astype(BF16), (((1,), (1,)), ((), ())),
                           preferred_element_type=F32)


def _inproj_kernel(x_ref, g_ref, w_ref, wgate_ref, z_ref, gcol_ref, grow_ref, xn_ref):
    @pl.when(pl.program_id(1) == 0)
    def _():
        _rmsnorm_rows(x_ref, g_ref[...], xn_ref)
        gc = _gate_dot(xn_ref[...], wgate_ref)
        gcol_ref[...] = gc
        grow_ref[...] = gc.T[0:GATE_ROWS, :]

    z_ref[...] = jnp.dot(xn_ref[...], w_ref[...], preferred_element_type=F32)


def _inproj(x2d, g1, w_bf, wgate):
    tokens = x2d.shape[0]
    tm = min(1024, tokens)
    tn = W_SLAB
    return pl.pallas_call(
        _inproj_kernel,
        grid=(tokens // tm, N_MAIN // tn),
        in_specs=[
            pl.BlockSpec((tm, D_MODEL), lambda i, j: (i, 0)),
            pl.BlockSpec((1, D_MODEL), lambda i, j: (0, 0)),
            pl.BlockSpec((D_MODEL, tn), lambda i, j: (0, j)),
            pl.BlockSpec((LANES, D_MODEL), lambda i, j: (0, 0)),
        ],
        out_specs=[
            pl.BlockSpec((tm, tn), lambda i, j: (i, j)),
            pl.BlockSpec((tm, LANES), lambda i, j: (i, 0)),
            pl.BlockSpec((GATE_ROWS, tm), lambda i, j: (0, i)),
        ],
        out_shape=[
            jax.ShapeDtypeStruct((tokens, N_MAIN), F32),
            jax.ShapeDtypeStruct((tokens, LANES), F32),
            jax.ShapeDtypeStruct((GATE_ROWS, tokens), F32),
        ],
        scratch_shapes=[pltpu.VMEM((tm, D_MODEL), BF16)],
        compiler_params=_compiler_params(("parallel", "arbitrary"), 48),
        name="inproj",
    )(x2d, g1, w_bf, wgate)


def _log_sigmoid(x):
    return jnp.minimum(x, 0.0) - jnp.log1p(jnp.exp(-jnp.abs(x)))


def _dot_exact(a, b):
    return jnp.dot(a, b, precision=lax.Precision.HIGHEST, preferred_element_type=F32)


def _intra(q_bf, k_bf, b_col, b_row, li_row, m0_col, mask):
    dmat = jnp.where(mask, b_col - b_row + li_row, -jnp.inf)
    inter = b_col + m0_col
    m_t = jnp.maximum(inter, jnp.max(dmat, axis=1, keepdims=True))
    w = jnp.exp(dmat - m_t)
    a_inter = jnp.exp(inter - m_t)
    qk = lax.dot_general(q_bf, k_bf, (((1,), (1,)), ((), ())), preferred_element_type=F32)
    return qk * w, a_inter, m_t


def _head_output(num, den, m_t, o_gate, gain):
    denom = jnp.maximum(jnp.abs(den), jnp.exp(-m_t))
    h = num * (1.0 / denom)
    hn = (h * lax.rsqrt(jnp.mean(h * h, axis=1, keepdims=True) + EPS)) * gain
    return jax.nn.sigmoid(o_gate) * hn


def _conv_norm(bg, conv, gain):
    yc = bg * conv
    return (yc * lax.rsqrt(jnp.mean(yc * yc, axis=1, keepdims=True) + EPS)) * gain


def _prompt_conv(z, convw_ref, convg_ref, y_ref, ubuf):
    L = z.shape[0]
    u = z[:, COLS_CG] * z[:, COLS_XT]
    ubuf[SUBLANES:SUBLANES + L, :] = u
    conv = (ubuf[SUBLANES - 2:SUBLANES - 2 + L, :] * convw_ref[0:1, :]
            + ubuf[SUBLANES - 1:SUBLANES - 1 + L, :] * convw_ref[1:2, :]
            + u * convw_ref[2:3, :])
    y_ref[:, 0:D_CONV] = _conv_norm(z[:, COLS_BG], conv,
                                    convg_ref[...]).astype(y_ref.dtype)
    ubuf[0:SUBLANES, :] = ubuf[L:L + SUBLANES, :]


def _prompt_mlstm(z, gc, gr, mlg_ref, bias_row_ref, bias_col_ref, y_ref, c_scr, n_scr, m_scr,
                  fillers):
    L = z.shape[0]
    fillers = list(fillers)

    def fill():
        if fillers:
            fillers.pop(0)()

    pre_c = gc + bias_row_ref[...]
    pre_r = gr + bias_col_ref[:, 0:1]
    row_id = lax.broadcasted_iota(jnp.int32, (L, L), 0)
    col_id = lax.broadcasted_iota(jnp.int32, (L, L), 1)
    causal = col_id <= row_id
    b_c = _dot_exact(causal.astype(F32), _log_sigmoid(pre_c))
    b_r = _dot_exact(_log_sigmoid(pre_r), (row_id <= col_id).astype(F32))

    for h in range(N_HEADS):
        li_r = pre_r[h:h + 1, :]
        li_c = pre_c[:, h:h + 1]
        bc = b_c[:, N_HEADS + h:N_HEADS + h + 1]
        br = b_r[N_HEADS + h:N_HEADS + h + 1, :]
        m0 = m_scr[h, 0:1, 0:1]
        qf = z[:, _cols_q(h)] * (D_K ** -0.5)
        q_bf = qf.astype(BF16)
        kf = z[:, _cols_k(h)]
        v_bf = z[:, _cols_v(h)].astype(BF16)
        s, a_inter, m_t = _intra(q_bf, kf.astype(BF16), bc, br, li_r, m0, causal)
        fill()
        c_old = c_scr[h]
        n_old = n_scr[h]
        num = (a_inter * jnp.dot(q_bf, c_old.astype(BF16), preferred_element_type=F32)
               + jnp.dot(s.astype(BF16), v_bf, preferred_element_type=F32))
        den = (a_inter * jnp.sum(qf * n_old, axis=1, keepdims=True)
               + jnp.sum(s, axis=1, keepdims=True))
        fill()
        y_ref[:, D_CONV + h * D_V:D_CONV + (h + 1) * D_V] = _head_output(
            num, den, m_t, z[:, _cols_o(h)],
            mlg_ref[:, h * D_V:(h + 1) * D_V]).astype(y_ref.dtype)

        b_last = br[:, L - 1:L]
        m_new = jnp.maximum(b_last + m0,
                            jnp.max(b_last - br + li_r, axis=1, keepdims=True))
        decay = jnp.exp(b_last + m0 - m_new)
        kw = kf * jnp.exp(b_last - bc + li_c - m_new)
        c_scr[h] = decay * c_old + lax.dot_general(
            kw.astype(BF16), v_bf, (((0,), (0,)), ((), ())), preferred_element_type=F32)
        n_scr[h] = decay * n_old + jnp.sum(kw, axis=0, keepdims=True)
        m_scr[h] = jnp.broadcast_to(m_new, (SUBLANES, LANES))
        fill()
    while fillers:
        fill()


def _fused_prompt_kernel(x_ref, g1_ref, w_ref, wgate_ref, convw_ref, convg_ref, mlg_ref,
                         bias_row_ref, bias_col_ref,
                         y_ref, convs_ref, c_out_ref, n_out_ref, m_out_ref,
                         xn_scr, z_scr, ubuf, c_scr, n_scr, m_scr):
    chunk = pl.program_id(1)
    L = x_ref.shape[0]

    @pl.when(chunk == 0)
    def _():
        ubuf[0:SUBLANES, :] = jnp.zeros((SUBLANES, D_CONV), F32)
        c_scr[...] = jnp.zeros(c_scr.shape, F32)
        n_scr[...] = jnp.zeros(n_scr.shape, F32)
        m_scr[...] = jnp.zeros(m_scr.shape, F32)

    gain = g1_ref[...]
    for r in range(0, L, NORM_ROWS):
        xn_scr[r:r + NORM_ROWS, :] = _rmsnorm_piece(x_ref[r:r + NORM_ROWS, :], gain, BF16)
    xn = xn_scr[...]
    gc = _gate_dot(xn, wgate_ref)
    gr = gc.T[0:SUBLANES, :]

    def project(j):
        cols = slice(j * W_SLAB, (j + 1) * W_SLAB)
        z_scr[:, cols] = jnp.dot(xn_scr[...], w_ref[:, cols], preferred_element_type=F32)

    piece = lambda j: functools.partial(project, j)
    skip = lambda: None
    project(0)
    project(1)
    fillers = [piece(2), piece(6), piece(7),
               piece(3), piece(4), piece(8),
               piece(5), piece(9), skip,
               piece(10), piece(11), skip]
    _prompt_mlstm(z_scr, gc, gr, mlg_ref, bias_row_ref, bias_col_ref, y_ref,
                  c_scr, n_scr, m_scr, fillers)
    _prompt_conv(z_scr, convw_ref, convg_ref, y_ref, ubuf)

    @pl.when(chunk == pl.num_programs(1) - 1)
    def _():
        convs_ref[0] = ubuf[SUBLANES - 2:SUBLANES, :]
        c_out_ref[0] = c_scr[...]
        for h in range(N_HEADS):
            n_out_ref[0, h:h + 1, :] = n_scr[h]
            m_out_ref[0, h:h + 1, :] = m_scr[h, 0:1, :]


def _fused_prompt(x2d, g1, w_bf, wgate, batch, seq, conv_w, conv_g, ml_g, bias_row, bias_col):
    L = PROMPT_CHUNK
    nc = seq // L
    const = lambda b, c: (0, 0)
    return pl.pallas_call(
        _fused_prompt_kernel,
        grid=(batch, nc),
        in_specs=[
            pl.BlockSpec((L, D_MODEL), lambda b, c: (b * nc + c, 0)),
            pl.BlockSpec((1, D_MODEL), const),
            pl.BlockSpec((D_MODEL, N_MAIN), const, pipeline_mode=pl.Buffered(1)),
            pl.BlockSpec((LANES, D_MODEL), const),
            pl.BlockSpec((CONV_W, D_CONV), const),
            pl.BlockSpec((1, D_CONV), const),
            pl.BlockSpec((1, N_HEADS * D_V), const),
            pl.BlockSpec((1, LANES), const),
            pl.BlockSpec((SUBLANES, LANES), const),
        ],
        out_specs=[
            pl.BlockSpec((L, D_MODEL), lambda b, c: (b * nc + c, 0)),
            pl.BlockSpec((1, CONV_W - 1, D_CONV), lambda b, c: (b, 0, 0)),
            pl.BlockSpec((1, N_HEADS, D_K, D_V), lambda b, c: (b, 0, 0, 0)),
            pl.BlockSpec((1, N_HEADS, D_K), lambda b, c: (b, 0, 0)),
            pl.BlockSpec((1, N_HEADS, LANES), lambda b, c: (b, 0, 0)),
        ],
        out_shape=[
            jax.ShapeDtypeStruct((batch * seq, D_MODEL), BF16),
            jax.ShapeDtypeStruct((batch, CONV_W - 1, D_CONV), F32),
            jax.ShapeDtypeStruct((batch, N_HEADS, D_K, D_V), F32),
            jax.ShapeDtypeStruct((batch, N_HEADS, D_K), F32),
            jax.ShapeDtypeStruct((batch, N_HEADS, LANES), F32),
        ],
        scratch_shapes=[
            pltpu.VMEM((L, D_MODEL), BF16),
            pltpu.VMEM((L, N_MAIN), F32),
            pltpu.VMEM((L + 2 * SUBLANES, D_CONV), F32),
            pltpu.VMEM((N_HEADS, D_K, D_V), F32),
            pltpu.VMEM((N_HEADS, 1, D_K), F32),
            pltpu.VMEM((N_HEADS, SUBLANES, LANES), F32),
        ],
        compiler_params=_compiler_params(("arbitrary", "arbitrary"), 56),
        name="fused_prompt",
    )(x2d, g1, w_bf, wgate, conv_w, conv_g, ml_g, bias_row, bias_col)


def _mix_sample_kernel(z_ref, gcol_ref, grow_ref, mcol_ref, mrow_ref,
                       convs_in_ref, c_in_ref, n_in_ref,
                       convw_ref, convg_ref, mlg_ref, bias_row_ref, bias_col_ref,
                       y_ref, convs_ref, c_out_ref, n_out_ref, m_out_ref,
                       pad_scr, conv_scr, *, seq):
    L = z_ref.shape[0]
    group = L // seq
    stride = seq + SUBLANES

    for g in range(group):
        u = z_ref[g * seq:(g + 1) * seq, COLS_CG] * z_ref[g * seq:(g + 1) * seq, COLS_XT]
        base = g * stride
        pad_scr[base + SUBLANES - 2:base + SUBLANES, :] = convs_in_ref[g]
        pad_scr[base + SUBLANES:base + SUBLANES + seq, :] = u
        conv_scr[g * seq:(g + 1) * seq, :] = (
            pad_scr[base + SUBLANES - 2:base + SUBLANES - 2 + seq, :] * convw_ref[0:1, :]
            + pad_scr[base + SUBLANES - 1:base + SUBLANES - 1 + seq, :] * convw_ref[1:2, :]
            + u * convw_ref[2:3, :])
        convs_ref[g] = pad_scr[base + seq + SUBLANES - 2:base + seq + SUBLANES, :]
    y_ref[:, 0:D_CONV] = _conv_norm(z_ref[:, COLS_BG], conv_scr[...],
                                    convg_ref[...]).astype(y_ref.dtype)

    pre_c = gcol_ref[...] + bias_row_ref[...]
    pre_r = grow_ref[0:SUBLANES, :] + bias_col_ref[:, 0:1]
    row_id = lax.broadcasted_iota(jnp.int32, (L, L), 0)
    col_id = lax.broadcasted_iota(jnp.int32, (L, L), 1)
    same = (row_id // seq) == (col_id // seq)
    causal = same & (col_id <= row_id)
    lf_c = _log_sigmoid(pre_c)
    lf_r = _log_sigmoid(pre_r)
    b_c = _dot_exact(causal.astype(F32), lf_c)
    b_r = _dot_exact(lf_r, (same & (row_id <= col_id)).astype(F32))
    same_f = same.astype(F32)
    tot_c = _dot_exact(same_f, lf_c)
    tot_r = _dot_exact(lf_r, same_f)
    lane_seq = lax.broadcasted_iota(jnp.int32, (1, L), 1) // seq

    for h in range(N_HEADS):
        li_r = pre_r[h:h + 1, :]
        li_c = pre_c[:, h:h + 1]
        bc = b_c[:, N_HEADS + h:N_HEADS + h + 1]
        br = b_r[N_HEADS + h:N_HEADS + h + 1, :]
        bl_c = tot_c[:, N_HEADS + h:N_HEADS + h + 1]
        bl_r = tot_r[N_HEADS + h:N_HEADS + h + 1, :]
        m0_c = mcol_ref[:, h:h + 1]
        m0_r = mrow_ref[h:h + 1, :]
        qf = z_ref[:, _cols_q(h)] * (D_K ** -0.5)
        q_bf = qf.astype(BF16)
        kf = z_ref[:, _cols_k(h)]
        v_bf = z_ref[:, _cols_v(h)].astype(BF16)
        s, a_inter, m_t = _intra(q_bf, kf.astype(BF16), bc, br, li_r, m0_c, causal)

        qc_rows, qn_rows = [], []
        for g in range(group):
            rows = slice(g * seq, (g + 1) * seq)
            qc_rows.append(jnp.dot(qf[rows], c_in_ref[g, h], preferred_element_type=F32))
            qn_rows.append(jnp.sum(qf[rows] * n_in_ref[g, h:h + 1, :], axis=1, keepdims=True))
        num = (a_inter * jnp.concatenate(qc_rows, axis=0)
               + jnp.dot(s.astype(BF16), v_bf, preferred_element_type=F32))
        den = (a_inter * jnp.concatenate(qn_rows, axis=0)
               + jnp.sum(s, axis=1, keepdims=True))
        y_ref[:, D_CONV + h * D_V:D_CONV + (h + 1) * D_V] = _head_output(
            num, den, m_t, z_ref[:, _cols_o(h)],
            mlg_ref[:, h * D_V:(h + 1) * D_V]).astype(y_ref.dtype)

        g_r = bl_r - br + li_r
        gmax_c = jnp.max(jnp.where(same, g_r, -jnp.inf), axis=1, keepdims=True)
        gmax_r = jnp.max(jnp.where(same, gmax_c, -jnp.inf), axis=0, keepdims=True)
        m_new_c = jnp.maximum(bl_c + m0_c, gmax_c)
        m_new_r = jnp.maximum(bl_r + m0_r, gmax_r)
        decay_r = jnp.exp(bl_r + m0_r - m_new_r)
        kw = kf * jnp.exp(bl_c - bc + li_c - m_new_c)
        kw_t = kw.T
        for g in range(group):
            decay = decay_r[:, g * seq:g * seq + 1]
            kw_g = jnp.where(lane_seq == g, kw_t, 0.0).astype(BF16)
            c_out_ref[g, h] = decay * c_in_ref[g, h] + jnp.dot(
                kw_g, v_bf, preferred_element_type=F32)
            n_out_ref[g, h:h + 1, :] = (decay * n_in_ref[g, h:h + 1, :]
                                        + jnp.sum(kw[g * seq:(g + 1) * seq], axis=0, keepdims=True))
        m_out_ref[h:h + 1, :] = m_new_r
    m_out_ref[N_HEADS:, :] = jnp.zeros((SUBLANES - N_HEADS, L), F32)


def _mix_sample(z, gcol, grow, mcol, mrow, convs, c_state, n_state, batch, seq,
                conv_w, conv_g, ml_g, bias_row, bias_col):
    group = SAMPLE_GROUP
    L = group * seq
    const = lambda i: (0, 0)
    return pl.pallas_call(
        functools.partial(_mix_sample_kernel, seq=seq),
        grid=(batch // group,),
        in_specs=[
            pl.BlockSpec((L, N_MAIN), lambda i: (i, 0)),
            pl.BlockSpec((L, LANES), lambda i: (i, 0)),
            pl.BlockSpec((GATE_ROWS, L), lambda i: (0, i)),
            pl.BlockSpec((L, LANES), lambda i: (i, 0)),
            pl.BlockSpec((SUBLANES, L), lambda i: (0, i)),
            pl.BlockSpec((group, CONV_W - 1, D_CONV), lambda i: (i, 0, 0)),
            pl.BlockSpec((group, N_HEADS, D_K, D_V), lambda i: (i, 0, 0, 0)),
            pl.BlockSpec((group, N_HEADS, D_K), lambda i: (i, 0, 0)),
            pl.BlockSpec((CONV_W, D_CONV), const),
            pl.BlockSpec((1, D_CONV), const),
            pl.BlockSpec((1, N_HEADS * D_V), const),
            pl.BlockSpec((1, LANES), const),
            pl.BlockSpec((SUBLANES, LANES), const),
        ],
        out_specs=[
            pl.BlockSpec((L, D_MODEL), lambda i: (i, 0)),
            pl.BlockSpec((group, CONV_W - 1, D_CONV), lambda i: (i, 0, 0)),
            pl.BlockSpec((group, N_HEADS, D_K, D_V), lambda i: (i, 0, 0, 0)),
            pl.BlockSpec((group, N_HEADS, D_K), lambda i: (i, 0, 0)),
            pl.BlockSpec((SUBLANES, L), lambda i: (0, i)),
        ],
        out_shape=[
            jax.ShapeDtypeStruct((batch * seq, D_MODEL), BF16),
            jax.ShapeDtypeStruct((batch, CONV_W - 1, D_CONV), F32),
            jax.ShapeDtypeStruct((batch, N_HEADS, D_K, D_V), F32),
            jax.ShapeDtypeStruct((batch, N_HEADS, D_K), F32),
            jax.ShapeDtypeStruct((SUBLANES, batch * seq), F32),
        ],
        scratch_shapes=[
            pltpu.VMEM((group * (seq + SUBLANES), D_CONV), F32),
            pltpu.VMEM((L, D_CONV), F32),
        ],
        compiler_params=_compiler_params(("parallel",), 56),
        name="mix_sample",
    )(z, gcol, grow, mcol, mrow, convs, c_state, n_state,
      conv_w, conv_g, ml_g, bias_row, bias_col)


def _outproj_kernel(y_ref, w_ref, x_ref, o_ref, w_bf):
    @pl.when(pl.program_id(0) == 0)
    def _():
        for r in range(0, D_MODEL, W_SLAB):
            w_bf[r:r + W_SLAB, :] = w_ref[r:r + W_SLAB, :].astype(BF16)

    o_ref[...] = x_ref[...] + jnp.dot(y_ref[...], w_bf[...], preferred_element_type=F32)


def _outproj(y, w_out, x2d):
    tokens = x2d.shape[0]
    tm = min(512, tokens)
    return pl.pallas_call(
        _outproj_kernel,
        grid=(tokens // tm,),
        in_specs=[
            pl.BlockSpec((tm, D_MODEL), lambda i: (i, 0)),
            pl.BlockSpec((D_MODEL, D_MODEL), lambda i: (0, 0), pipeline_mode=pl.Buffered(1)),
            pl.BlockSpec((tm, D_MODEL), lambda i: (i, 0)),
        ],
        out_specs=pl.BlockSpec((tm, D_MODEL), lambda i: (i, 0)),
        out_shape=jax.ShapeDtypeStruct((tokens, D_MODEL), F32),
        scratch_shapes=[pltpu.VMEM((D_MODEL, D_MODEL), BF16)],
        compiler_params=_compiler_params(("arbitrary",), 52),
        name="outproj",
    )(y, w_out, x2d)


def _cast_gate_up_kernel(wg_ref, wu_ref, o_ref):
    tf = wg_ref.shape[1]
    o_ref[0, :, 0:tf] = wg_ref[...].astype(BF16)
    o_ref[0, :, tf:2 * tf] = wu_ref[...].astype(BF16)


def _cast_rows_kernel(w_ref, o_ref):
    o_ref[...] = w_ref[...].astype(BF16)


def _cast_ffn_weights(wg, wu, wd):
    tf = FFN_TILE
    n_f = D_FF // tf
    wgu = pl.pallas_call(
        _cast_gate_up_kernel,
        grid=(n_f,),
        in_specs=[pl.BlockSpec((D_MODEL, tf), lambda j: (0, j)),
                  pl.BlockSpec((D_MODEL, tf), lambda j: (0, j))],
        out_specs=pl.BlockSpec((1, D_MODEL, 2 * tf), lambda j: (j, 0, 0)),
        out_shape=jax.ShapeDtypeStruct((n_f, D_MODEL, 2 * tf), BF16),
        compiler_params=_compiler_params(("parallel",), 48),
        name="cast_gate_up",
    )(wg, wu)
    wd_bf = pl.pallas_call(
        _cast_rows_kernel,
        grid=(n_f,),
        in_specs=[pl.BlockSpec((tf, D_MODEL), lambda j: (j, 0))],
        out_specs=pl.BlockSpec((tf, D_MODEL), lambda j: (j, 0)),
        out_shape=jax.ShapeDtypeStruct((D_FF, D_MODEL), BF16),
        compiler_params=_compiler_params(("parallel",), 32),
        name="cast_down",
    )(wd)
    return wgu, wd_bf


def _ffn_kernel(x_ref, g2_ref, wgu_ref, wd_ref, gf_ref, o_ref, hn_ref, *, final_norm):
    j = pl.program_id(1)
    tf = wd_ref.shape[0]

    @pl.when(j == 0)
    def _():
        _rmsnorm_rows(x_ref, g2_ref[...], hn_ref, copy_ref=o_ref)

    gate_up = jnp.dot(hn_ref[...], wgu_ref[0], preferred_element_type=F32)
    gate = gate_up[:, 0:tf]
    act = (gate * jax.nn.sigmoid(gate)) * gate_up[:, tf:2 * tf]
    o_ref[...] += jnp.dot(act.astype(BF16), wd_ref[...], preferred_element_type=F32)

    if final_norm:
        @pl.when(j == pl.num_programs(1) - 1)
        def _():
            _rmsnorm_rows(o_ref, gf_ref[...], o_ref)


def _ffn(x1, g2, wgu, wd_bf, gf, final_norm):
    tokens = x1.shape[0]
    tm = min(1024, tokens)
    tf = FFN_TILE
    return pl.pallas_call(
        functools.partial(_ffn_kernel, final_norm=final_norm),
        grid=(tokens // tm, D_FF // tf),
        in_specs=[
            pl.BlockSpec((tm, D_MODEL), lambda i, j: (i, 0)),
            pl.BlockSpec((1, D_MODEL), lambda i, j: (0, 0)),
            pl.BlockSpec((1, D_MODEL, 2 * tf), lambda i, j: (j, 0, 0)),
            pl.BlockSpec((tf, D_MODEL), lambda i, j: (j, 0)),
            pl.BlockSpec((1, D_MODEL), lambda i, j: (0, 0)),
        ],
        out_specs=pl.BlockSpec((tm, D_MODEL), lambda i, j: (i, 0)),
        out_shape=jax.ShapeDtypeStruct((tokens, D_MODEL), F32),
        scratch_shapes=[pltpu.VMEM((tm, D_MODEL), BF16)],
        compiler_params=_compiler_params(("parallel", "arbitrary"), 60),
        name="ffn",
    )(x1, g2, wgu, wd_bf, gf)


def _gate_params(w_t, b_i, b_f):
    wgate = jnp.pad(w_t[N_MAIN:, :], ((0, LANES - 2 * N_HEADS), (0, 0)))
    bias = jnp.concatenate([b_i, b_f]).astype(F32)
    bias_row = jnp.pad(bias, (0, LANES - 2 * N_HEADS))[None, :]
    bias_col = jnp.broadcast_to(bias[:, None], (SUBLANES, LANES))
    return wgate, bias_row, bias_col


def _tail(x2d, y, w_out, g2, wgu, wd_bf, gf, final_norm):
    x1 = _outproj(y, w_out, x2d)
    return _ffn(x1, g2[None, :], wgu, wd_bf, gf[None, :], final_norm)


def _stack(states, k):
    if len(states) == 1:
        return states[0][k][None]
    return jnp.stack([st[k] for st in states])


def kernel(x_prompt, x_sample, state_conv, state_mlstm_C, state_mlstm_n, state_mlstm_m,
           norm1_g, w_in, b_igate, b_fgate, conv_w, conv_out_g, mlstm_out_g, w_out,
           norm2_g, w_ffn_gate, w_ffn_up, w_ffn_down, final_norm_g):
    depth = w_in.shape[0]
    bp, sp, _ = x_prompt.shape
    bs, ss, _ = x_sample.shape
    hp = x_prompt.reshape(bp * sp, D_MODEL)
    hs = x_sample.reshape(bs * ss, D_MODEL)
    p_states, s_states = [], []
    for l in range(depth):
        final_norm = l == depth - 1
        w_t = w_in[l].T
        wgate, bias_row, bias_col = _gate_params(w_t, b_igate[l], b_fgate[l])
        w_bf = _cast_main_columns(w_t)
        g1 = norm1_g[l][None, :]
        conv_g = conv_out_g[l][None, :]
        ml_g = mlstm_out_g[l][None, :]
        wgu, wd_bf = _cast_ffn_weights(w_ffn_gate[l], w_ffn_up[l], w_ffn_down[l])
        tail = (w_out[l], norm2_g[l], wgu, wd_bf, final_norm_g, final_norm)

        y, cb, c1, n1, m1 = _fused_prompt(hp, g1, w_bf, wgate, bp, sp, conv_w[l], conv_g, ml_g,
                                          bias_row, bias_col)
        hp = _tail(hp, y, *tail)
        p_states.append((cb, c1, n1, m1[:, :, 0]))

        z, gcol, grow = _inproj(hs, g1, w_bf, wgate)
        m_tok = jnp.repeat(state_mlstm_m[l], ss, axis=0)
        mcol = jnp.pad(m_tok, ((0, 0), (0, LANES - N_HEADS)))
        mrow = jnp.pad(m_tok.T, ((0, SUBLANES - N_HEADS), (0, 0)))
        y, cb, c1, n1, m_row = _mix_sample(z, gcol, grow, mcol, mrow, state_conv[l],
                                           state_mlstm_C[l], state_mlstm_n[l], bs, ss,
                                           conv_w[l], conv_g, ml_g, bias_row, bias_col)
        hs = _tail(hs, y, *tail)
        s_states.append((cb, c1, n1, m_row[:N_HEADS, ::ss].T))

    return (hp.reshape(bp, sp, D_MODEL), hs.reshape(bs, ss, D_MODEL),
            _stack(p_states, 0), _stack(p_states, 1), _stack(p_states, 2), _stack(p_states, 3),
            _stack(s_states, 0), _stack(s_states, 1), _stack(s_states, 2), _stack(s_states, 3))
```

```python
import functools

import jax
import jax.numpy as jnp
from jax import lax
from jax.experimental import pallas as pl
from jax.experimental.pallas import tpu as pltpu

F32 = jnp.float32
BF16 = jnp.bfloat16

D_MODEL = 2048
D_CONV = 1024
CONV_W = 3
N_HEADS = 4
D_K = 128
D_V = 256
D_FF = 5632
EPS = 1e-6

SRC_Q = 3 * D_CONV
SRC_K = SRC_Q + N_HEADS * D_K
SRC_V = SRC_K + N_HEADS * D_K
SRC_O = SRC_V + N_HEADS * D_V
N_MAIN = SRC_O + N_HEADS * D_V

HEAD_COLS = 2 * D_K + 2 * D_V
N_MLSTM = N_HEADS * HEAD_COLS
COLS_BG = slice(N_MLSTM, N_MLSTM + D_CONV)
COLS_CG = slice(N_MLSTM + D_CONV, N_MLSTM + 2 * D_CONV)
COLS_XT = slice(N_MLSTM + 2 * D_CONV, N_MLSTM + 3 * D_CONV)


def _cols_q(h):
    return slice(h * HEAD_COLS, h * HEAD_COLS + D_K)


def _cols_k(h):
    return slice(h * HEAD_COLS + D_K, h * HEAD_COLS + 2 * D_K)


def _cols_v(h):
    return slice(h * HEAD_COLS + 2 * D_K, h * HEAD_COLS + 2 * D_K + D_V)


def _cols_o(h):
    return slice(h * HEAD_COLS + 2 * D_K + D_V, (h + 1) * HEAD_COLS)


def _source_lane_block(j):
    per_head = HEAD_COLS // LANES
    h, r = j // per_head, j % per_head
    v_blocks = D_V // LANES
    mlstm_src = jnp.where(
        r == 0, SRC_Q // LANES + h,
        jnp.where(r == 1, SRC_K // LANES + h,
                  jnp.where(r < 2 + v_blocks,
                            SRC_V // LANES + v_blocks * h + (r - 2),
                            SRC_O // LANES + v_blocks * h + (r - 2 - v_blocks))))
    return jnp.where(j < N_MLSTM // LANES, mlstm_src, j - N_MLSTM // LANES)

LANES = 128
SUBLANES = 8
GATE_ROWS = 16
MIB = 1024 * 1024

PROMPT_CHUNK = 256
SAMPLE_GROUP = 16
NORM_ROWS = 64
W_SLAB = 512
FFN_TILE = 512
CAST_PIECE = 256


def _compiler_params(semantics, vmem_mib):
    return pltpu.CompilerParams(dimension_semantics=semantics,
                                vmem_limit_bytes=vmem_mib * MIB)


def _rmsnorm_piece(x, gain, dtype):
    ms = jnp.mean(x * x, axis=-1, keepdims=True)
    return ((x * lax.rsqrt(ms + EPS)) * gain).astype(dtype)


def _rmsnorm_rows(src_ref, gain, dst_ref, copy_ref=None):
    rows = src_ref.shape[0]
    chunk = min(NORM_ROWS, rows)

    def body(i, carry):
        r = pl.multiple_of(i * chunk, chunk)
        x = src_ref[pl.ds(r, chunk), :].astype(F32)
        if copy_ref is not None:
            copy_ref[pl.ds(r, chunk), :] = x
        dst_ref[pl.ds(r, chunk), :] = _rmsnorm_piece(x, gain, dst_ref.dtype)
        return carry

    lax.fori_loop(0, rows // chunk, body, 0, unroll=2)


def _cast_kernel(*refs):
    *w_refs, o_ref = refs
    for r, w_ref in enumerate(w_refs):
        o_ref[:, r * LANES:(r + 1) * LANES] = w_ref[...].T.astype(o_ref.dtype)


def _cast_main_columns(w_t):
    per_slab = W_SLAB // LANES
    in_specs = [
        pl.BlockSpec((LANES, D_MODEL),
                     lambda j, r=r: (_source_lane_block(per_slab * j + r), 0))
        for r in range(per_slab)
    ]
    return pl.pallas_call(
        _cast_kernel,
        grid=(N_MAIN // W_SLAB,),
        in_specs=in_specs,
        out_specs=pl.BlockSpec((D_MODEL, W_SLAB), lambda j: (0, j)),
        out_shape=jax.ShapeDtypeStruct((D_MODEL, N_MAIN), BF16),
        compiler_params=_compiler_params(("parallel",), 32),
        name="cast_w_in",
    )(*([w_t] * per_slab))


def _gate_dot(xn, wgate_t_ref):
    return lax.dot_general(xn, wgate_t_ref[...].astype(BF16), (((1,), (1,)), ((), ())),
                           preferred_element_type=F32)


def _inproj_kernel(x_ref, g_ref, w_ref, wgate_ref, z_ref, gcol_ref, grow_ref, xn_ref):
    @pl.when(pl.program_id(1) == 0)
    def _():
        _rmsnorm_rows(x_ref, g_ref[...], xn_ref)
        gc = _gate_dot(xn_ref[...], wgate_ref)
        gcol_ref[...] = gc
        grow_ref[...] = gc.T[0:GATE_ROWS, :]

    z_ref[...] = jnp.dot(xn_ref[...], w_ref[...], preferred_element_type=F32)


def _inproj(x2d, g1, w_bf, wgate):
    tokens = x2d.shape[0]
    tm = min(1024, tokens)
    tn = W_SLAB
    return pl.pallas_call(
        _inproj_kernel,
        grid=(tokens // tm, N_MAIN // tn),
        in_specs=[
            pl.BlockSpec((tm, D_MODEL), lambda i, j: (i, 0)),
            pl.BlockSpec((1, D_MODEL), lambda i, j: (0, 0)),
            pl.BlockSpec((D_MODEL, tn), lambda i, j: (0, j)),
            pl.BlockSpec((LANES, D_MODEL), lambda i, j: (0, 0)),
        ],
        out_specs=[
            pl.BlockSpec((tm, tn), lambda i, j: (i, j)),
            pl.BlockSpec((tm, LANES), lambda i, j: (i, 0)),
            pl.BlockSpec((GATE_ROWS, tm), lambda i, j: (0, i)),
        ],
        out_shape=[
            jax.ShapeDtypeStruct((tokens, N_MAIN), F32),
            jax.ShapeDtypeStruct((tokens, LANES), F32),
            jax.ShapeDtypeStruct((GATE_ROWS, tokens), F32),
        ],
        scratch_shapes=[pltpu.VMEM((tm, D_MODEL), BF16)],
        compiler_params=_compiler_params(("parallel", "arbitrary"), 48),
        name="inproj",
    )(x2d, g1, w_bf, wgate)


def _log_sigmoid(x):
    return jnp.minimum(x, 0.0) - jnp.log1p(jnp.exp(-jnp.abs(x)))


def _dot_exact(a, b):
    return jnp.dot(a, b, precision=lax.Precision.HIGHEST, preferred_element_type=F32)


def _intra(q_bf, k_bf, b_col, b_row, li_row, m0_col, mask):
    dmat = jnp.where(mask, b_col - b_row + li_row, -jnp.inf)
    inter = b_col + m0_col
    m_t = jnp.maximum(inter, jnp.max(dmat, axis=1, keepdims=True))
    w = jnp.exp(dmat - m_t)
    a_inter = jnp.exp(inter - m_t)
    qk = lax.dot_general(q_bf, k_bf, (((1,), (1,)), ((), ())), preferred_element_type=F32)
    return qk * w, a_inter, m_t


def _head_output(num, den, m_t, o_gate, gain):
    denom = jnp.maximum(jnp.abs(den), jnp.exp(-m_t))
    h = num * (1.0 / denom)
    hn = (h * lax.rsqrt(jnp.mean(h * h, axis=1, keepdims=True) + EPS)) * gain
    return jax.nn.sigmoid(o_gate) * hn


def _conv_norm(bg, conv, gain):
    yc = bg * conv
    return (yc * lax.rsqrt(jnp.mean(yc * yc, axis=1, keepdims=True) + EPS)) * gain


def _prompt_conv(z, convw_ref, convg_ref, y_ref, ubuf):
    L = z.shape[0]
    u = z[:, COLS_CG] * z[:, COLS_XT]
    ubuf[SUBLANES:SUBLANES + L, :] = u
    conv = (ubuf[SUBLANES - 2:SUBLANES - 2 + L, :] * convw_ref[0:1, :]
            + ubuf[SUBLANES - 1:SUBLANES - 1 + L, :] * convw_ref[1:2, :]
            + u * convw_ref[2:3, :])
    y_ref[:, 0:D_CONV] = _conv_norm(z[:, COLS_BG], conv,
                                    convg_ref[...]).astype(y_ref.dtype)
    ubuf[0:SUBLANES, :] = ubuf[L:L + SUBLANES, :]


def _prompt_mlstm(z, gc, gr, mlg_ref, bias_row_ref, bias_col_ref, y_ref, c_scr, n_scr, m_scr,
                  fillers):
    L = z.shape[0]
    fillers = list(fillers)

    def fill():
        if fillers:
            fillers.pop(0)()

    pre_c = gc + bias_row_ref[...]
    pre_r = gr + bias_col_ref[:, 0:1]
    row_id = lax.broadcasted_iota(jnp.int32, (L, L), 0)
    col_id = lax.broadcasted_iota(jnp.int32, (L, L), 1)
    causal = col_id <= row_id
    b_c = _dot_exact(causal.astype(F32), _log_sigmoid(pre_c))
    b_r = _dot_exact(_log_sigmoid(pre_r), (row_id <= col_id).astype(F32))

    for h in range(N_HEADS):
        li_r = pre_r[h:h + 1, :]
        li_c = pre_c[:, h:h + 1]
        bc = b_c[:, N_HEADS + h:N_HEADS + h + 1]
        br = b_r[N_HEADS + h:N_HEADS + h + 1, :]
        m0 = m_scr[h, 0:1, 0:1]
        qf = z[:, _cols_q(h)] * (D_K ** -0.5)
        q_bf = qf.astype(BF16)
        kf = z[:, _cols_k(h)]
        v_bf = z[:, _cols_v(h)].astype(BF16)
        s, a_inter, m_t = _intra(q_bf, kf.astype(BF16), bc, br, li_r, m0, causal)
        fill()
        c_old = c_scr[h]
        n_old = n_scr[h]
        num = (a_inter * jnp.dot(q_bf, c_old.astype(BF16), preferred_element_type=F32)
               + jnp.dot(s.astype(BF16), v_bf, preferred_element_type=F32))
        den = (a_inter * jnp.sum(qf * n_old, axis=1, keepdims=True)
               + jnp.sum(s, axis=1, keepdims=True))
        fill()
        y_ref[:, D_CONV + h * D_V:D_CONV + (h + 1) * D_V] = _head_output(
            num, den, m_t, z[:, _cols_o(h)],
            mlg_ref[:, h * D_V:(h + 1) * D_V]).astype(y_ref.dtype)

        b_last = br[:, L - 1:L]
        m_new = jnp.maximum(b_last + m0,
                            jnp.max(b_last - br + li_r, axis=1, keepdims=True))
        decay = jnp.exp(b_last + m0 - m_new)
        kw = kf * jnp.exp(b_last - bc + li_c - m_new)
        c_scr[h] = decay * c_old + lax.dot_general(
            kw.astype(BF16), v_bf, (((0,), (0,)), ((), ())), preferred_element_type=F32)
        n_scr[h] = decay * n_old + jnp.sum(kw, axis=0, keepdims=True)
        m_scr[h] = jnp.broadcast_to(m_new, (SUBLANES, LANES))
        fill()
    while fillers:
        fill()


def _fused_prompt_kernel(x_ref, g1_ref, w_ref, wgate_ref, convw_ref, convg_ref, mlg_ref,
                         bias_row_ref, bias_col_ref, wg_ref, wu_ref,
                         y_ref, convs_ref, c_out_ref, n_out_ref, m_out_ref,
                         wg_bf_ref, wu_bf_ref,
                         xn_scr, z_scr, ubuf, c_scr, n_scr, m_scr):
    chunk = pl.program_id(1)
    L = x_ref.shape[0]

    @pl.when(chunk == 0)
    def _():
        ubuf[0:SUBLANES, :] = jnp.zeros((SUBLANES, D_CONV), F32)
        c_scr[...] = jnp.zeros(c_scr.shape, F32)
        n_scr[...] = jnp.zeros(n_scr.shape, F32)
        m_scr[...] = jnp.zeros(m_scr.shape, F32)

    gain = g1_ref[...]
    for r in range(0, L, NORM_ROWS):
        xn_scr[r:r + NORM_ROWS, :] = _rmsnorm_piece(x_ref[r:r + NORM_ROWS, :], gain, BF16)
    xn = xn_scr[...]
    gc = _gate_dot(xn, wgate_ref)
    gr = gc.T[0:SUBLANES, :]

    def project(j):
        cols = slice(j * W_SLAB, (j + 1) * W_SLAB)
        z_scr[:, cols] = jnp.dot(xn_scr[...], w_ref[:, cols], preferred_element_type=F32)

    piece = lambda j: functools.partial(project, j)
    skip = lambda: None
    project(0)
    wg_bf_ref[0] = wg_ref[...].astype(BF16)
    project(1)
    wu_bf_ref[0] = wu_ref[...].astype(BF16)
    fillers = [piece(2), piece(6), piece(7),
               piece(3), piece(4), piece(8),
               piece(5), piece(9), skip,
               piece(10), piece(11), skip]
    _prompt_mlstm(z_scr, gc, gr, mlg_ref, bias_row_ref, bias_col_ref, y_ref,
                  c_scr, n_scr, m_scr, fillers)
    _prompt_conv(z_scr, convw_ref, convg_ref, y_ref, ubuf)

    @pl.when(chunk == pl.num_programs(1) - 1)
    def _():
        convs_ref[0] = ubuf[SUBLANES - 2:SUBLANES, :]
        c_out_ref[0] = c_scr[...]
        for h in range(N_HEADS):
            n_out_ref[0, h:h + 1, :] = n_scr[h]
            m_out_ref[0, h:h + 1, :] = m_scr[h, 0:1, :]


def _fused_prompt(x2d, g1, w_bf, wgate, batch, seq, conv_w, conv_g, ml_g, bias_row, bias_col,
                  wg, wu):
    L = PROMPT_CHUNK
    nc = seq // L
    const = lambda b, c: (0, 0)
    n_pieces = D_FF // CAST_PIECE
    assert batch * nc >= n_pieces, "not enough grid steps to cast the FFN weights"
    per_tile = FFN_TILE // CAST_PIECE
    piece_of = lambda b, c: jnp.minimum(b * nc + c, n_pieces - 1)
    tiled = lambda b, c: (piece_of(b, c) // per_tile, 0, piece_of(b, c) % per_tile)
    return pl.pallas_call(
        _fused_prompt_kernel,
        grid=(batch, nc),
        in_specs=[
            pl.BlockSpec((L, D_MODEL), lambda b, c: (b * nc + c, 0)),
            pl.BlockSpec((1, D_MODEL), const),
            pl.BlockSpec((D_MODEL, N_MAIN), const, pipeline_mode=pl.Buffered(1)),
            pl.BlockSpec((LANES, D_MODEL), const),
            pl.BlockSpec((CONV_W, D_CONV), const),
            pl.BlockSpec((1, D_CONV), const),
            pl.BlockSpec((1, N_HEADS * D_V), const),
            pl.BlockSpec((1, LANES), const),
            pl.BlockSpec((SUBLANES, LANES), const),
            pl.BlockSpec((D_MODEL, CAST_PIECE), lambda b, c: (0, piece_of(b, c))),
            pl.BlockSpec((D_MODEL, CAST_PIECE), lambda b, c: (0, piece_of(b, c))),
        ],
        out_specs=[
            pl.BlockSpec((L, D_MODEL), lambda b, c: (b * nc + c, 0)),
            pl.BlockSpec((1, CONV_W - 1, D_CONV), lambda b, c: (b, 0, 0)),
            pl.BlockSpec((1, N_HEADS, D_K, D_V), lambda b, c: (b, 0, 0, 0)),
            pl.BlockSpec((1, N_HEADS, D_K), lambda b, c: (b, 0, 0)),
            pl.BlockSpec((1, N_HEADS, LANES), lambda b, c: (b, 0, 0)),
            pl.BlockSpec((1, D_MODEL, CAST_PIECE), tiled),
            pl.BlockSpec((1, D_MODEL, CAST_PIECE), tiled),
        ],
        out_shape=[
            jax.ShapeDtypeStruct((batch * seq, D_MODEL), BF16),
            jax.ShapeDtypeStruct((batch, CONV_W - 1, D_CONV), F32),
            jax.ShapeDtypeStruct((batch, N_HEADS, D_K, D_V), F32),
            jax.ShapeDtypeStruct((batch, N_HEADS, D_K), F32),
            jax.ShapeDtypeStruct((batch, N_HEADS, LANES), F32),
            jax.ShapeDtypeStruct((D_FF // FFN_TILE, D_MODEL, FFN_TILE), BF16),
            jax.ShapeDtypeStruct((D_FF // FFN_TILE, D_MODEL, FFN_TILE), BF16),
        ],
        scratch_shapes=[
            pltpu.VMEM((L, D_MODEL), BF16),
            pltpu.VMEM((L, N_MAIN), F32),
            pltpu.VMEM((L + 2 * SUBLANES, D_CONV), F32),
            pltpu.VMEM((N_HEADS, D_K, D_V), F32),
            pltpu.VMEM((N_HEADS, 1, D_K), F32),
            pltpu.VMEM((N_HEADS, SUBLANES, LANES), F32),
        ],
        compiler_params=_compiler_params(("arbitrary", "arbitrary"), 62),
        name="fused_prompt",
    )(x2d, g1, w_bf, wgate, conv_w, conv_g, ml_g, bias_row, bias_col, wg, wu)


def _mix_sample_kernel(z_ref, gcol_ref, grow_ref, mcol_ref, mrow_ref,
                       convs_in_ref, c_in_ref, n_in_ref,
                       convw_ref, convg_ref, mlg_ref, bias_row_ref, bias_col_ref,
                       y_ref, convs_ref, c_out_ref, n_out_ref, m_out_ref,
                       pad_scr, conv_scr, *, seq):
    L = z_ref.shape[0]
    group = L // seq
    stride = seq + SUBLANES

    for g in range(group):
        u = z_ref[g * seq:(g + 1) * seq, COLS_CG] * z_ref[g * seq:(g + 1) * seq, COLS_XT]
        base = g * stride
        pad_scr[base + SUBLANES - 2:base + SUBLANES, :] = convs_in_ref[g]
        pad_scr[base + SUBLANES:base + SUBLANES + seq, :] = u
        conv_scr[g * seq:(g + 1) * seq, :] = (
            pad_scr[base + SUBLANES - 2:base + SUBLANES - 2 + seq, :] * convw_ref[0:1, :]
            + pad_scr[base + SUBLANES - 1:base + SUBLANES - 1 + seq, :] * convw_ref[1:2, :]
            + u * convw_ref[2:3, :])
        convs_ref[g] = pad_scr[base + seq + SUBLANES - 2:base + seq + SUBLANES, :]
    y_ref[:, 0:D_CONV] = _conv_norm(z_ref[:, COLS_BG], conv_scr[...],
                                    convg_ref[...]).astype(y_ref.dtype)

    pre_c = gcol_ref[...] + bias_row_ref[...]
    pre_r = grow_ref[0:SUBLANES, :] + bias_col_ref[:, 0:1]
    row_id = lax.broadcasted_iota(jnp.int32, (L, L), 0)
    col_id = lax.broadcasted_iota(jnp.int32, (L, L), 1)
    same = (row_id // seq) == (col_id // seq)
    causal = same & (col_id <= row_id)
    lf_c = _log_sigmoid(pre_c)
    lf_r = _log_sigmoid(pre_r)
    b_c = _dot_exact(causal.astype(F32), lf_c)
    b_r = _dot_exact(lf_r, (same & (row_id <= col_id)).astype(F32))
    same_f = same.astype(F32)
    tot_c = _dot_exact(same_f, lf_c)
    tot_r = _dot_exact(lf_r, same_f)
    lane_seq = lax.broadcasted_iota(jnp.int32, (1, L), 1) // seq

    for h in range(N_HEADS):
        li_r = pre_r[h:h + 1, :]
        li_c = pre_c[:, h:h + 1]
        bc = b_c[:, N_HEADS + h:N_HEADS + h + 1]
        br = b_r[N_HEADS + h:N_HEADS + h + 1, :]
        bl_c = tot_c[:, N_HEADS + h:N_HEADS + h + 1]
        bl_r = tot_r[N_HEADS + h:N_HEADS + h + 1, :]
        m0_c = mcol_ref[:, h:h + 1]
        m0_r = mrow_ref[h:h + 1, :]
        qf = z_ref[:, _cols_q(h)] * (D_K ** -0.5)
        q_bf = qf.astype(BF16)
        kf = z_ref[:, _cols_k(h)]
        v_bf = z_ref[:, _cols_v(h)].astype(BF16)
        s, a_inter, m_t = _intra(q_bf, kf.astype(BF16), bc, br, li_r, m0_c, causal)

        qc_rows, qn_rows = [], []
        for g in range(group):
            rows = slice(g * seq, (g + 1) * seq)
            qc_rows.append(jnp.dot(qf[rows], c_in_ref[g, h], preferred_element_type=F32))
            qn_rows.append(jnp.sum(qf[rows] * n_in_ref[g, h:h + 1, :], axis=1, keepdims=True))
        num = (a_inter * jnp.concatenate(qc_rows, axis=0)
               + jnp.dot(s.astype(BF16), v_bf, preferred_element_type=F32))
        den = (a_inter * jnp.concatenate(qn_rows, axis=0)
               + jnp.sum(s, axis=1, keepdims=True))
        y_ref[:, D_CONV + h * D_V:D_CONV + (h + 1) * D_V] = _head_output(
            num, den, m_t, z_ref[:, _cols_o(h)],
            mlg_ref[:, h * D_V:(h + 1) * D_V]).astype(y_ref.dtype)

        g_r = bl_r - br + li_r
        gmax_c = jnp.max(jnp.where(same, g_r, -jnp.inf), axis=1, keepdims=True)
        gmax_r = jnp.max(jnp.where(same, gmax_c, -jnp.inf), axis=0, keepdims=True)
        m_new_c = jnp.maximum(bl_c + m0_c, gmax_c)
        m_new_r = jnp.maximum(bl_r + m0_r, gmax_r)
        decay_r = jnp.exp(bl_r + m0_r - m_new_r)
        kw = kf * jnp.exp(bl_c - bc + li_c - m_new_c)
        kw_t = kw.T
        for g in range(group):
            decay = decay_r[:, g * seq:g * seq + 1]
            kw_g = jnp.where(lane_seq == g, kw_t, 0.0).astype(BF16)
            c_out_ref[g, h] = decay * c_in_ref[g, h] + jnp.dot(
                kw_g, v_bf, preferred_element_type=F32)
            n_out_ref[g, h:h + 1, :] = (decay * n_in_ref[g, h:h + 1, :]
                                        + jnp.sum(kw[g * seq:(g + 1) * seq], axis=0, keepdims=True))
        m_out_ref[h:h + 1, :] = m_new_r
    m_out_ref[N_HEADS:, :] = jnp.zeros((SUBLANES - N_HEADS, L), F32)


def _mix_sample(z, gcol, grow, mcol, mrow, convs, c_state, n_state, batch, seq,
                conv_w, conv_g, ml_g, bias_row, bias_col):
    group = SAMPLE_GROUP
    L = group * seq
    const = lambda i: (0, 0)
    return pl.pallas_call(
        functools.partial(_mix_sample_kernel, seq=seq),
        grid=(batch // group,),
        in_specs=[
            pl.BlockSpec((L, N_MAIN), lambda i: (i, 0)),
            pl.BlockSpec((L, LANES), lambda i: (i, 0)),
            pl.BlockSpec((GATE_ROWS, L), lambda i: (0, i)),
            pl.BlockSpec((L, LANES), lambda i: (i, 0)),
            pl.BlockSpec((SUBLANES, L), lambda i: (0, i)),
            pl.BlockSpec((group, CONV_W - 1, D_CONV), lambda i: (i, 0, 0)),
            pl.BlockSpec((group, N_HEADS, D_K, D_V), lambda i: (i, 0, 0, 0)),
            pl.BlockSpec((group, N_HEADS, D_K), lambda i: (i, 0, 0)),
            pl.BlockSpec((CONV_W, D_CONV), const),
            pl.BlockSpec((1, D_CONV), const),
            pl.BlockSpec((1, N_HEADS * D_V), const),
            pl.BlockSpec((1, LANES), const),
            pl.BlockSpec((SUBLANES, LANES), const),
        ],
        out_specs=[
            pl.BlockSpec((L, D_MODEL), lambda i: (i, 0)),
            pl.BlockSpec((group, CONV_W - 1, D_CONV), lambda i: (i, 0, 0)),
            pl.BlockSpec((group, N_HEADS, D_K, D_V), lambda i: (i, 0, 0, 0)),
            pl.BlockSpec((group, N_HEADS, D_K), lambda i: (i, 0, 0)),
            pl.BlockSpec((SUBLANES, L), lambda i: (0, i)),
        ],
        out_shape=[
            jax.ShapeDtypeStruct((batch * seq, D_MODEL), BF16),
            jax.ShapeDtypeStruct((batch, CONV_W - 1, D_CONV), F32),
            jax.ShapeDtypeStruct((batch, N_HEADS, D_K, D_V), F32),
            jax.ShapeDtypeStruct((batch, N_HEADS, D_K), F32),
            jax.ShapeDtypeStruct((SUBLANES, batch * seq), F32),
        ],
        scratch_shapes=[
            pltpu.VMEM((group * (seq + SUBLANES), D_CONV), F32),
            pltpu.VMEM((L, D_CONV), F32),
        ],
        compiler_params=_compiler_params(("parallel",), 56),
        name="mix_sample",
    )(z, gcol, grow, mcol, mrow, convs, c_state, n_state,
      conv_w, conv_g, ml_g, bias_row, bias_col)


def _outproj_kernel(y_ref, w_ref, x_ref, o_ref, w_bf):
    @pl.when(pl.program_id(0) == 0)
    def _():
        for r in range(0, D_MODEL, W_SLAB):
            w_bf[r:r + W_SLAB, :] = w_ref[r:r + W_SLAB, :].astype(BF16)

    o_ref[...] = x_ref[...] + jnp.dot(y_ref[...], w_bf[...], preferred_element_type=F32)


def _outproj(y, w_out, x2d):
    tokens = x2d.shape[0]
    tm = min(512, tokens)
    return pl.pallas_call(
        _outproj_kernel,
        grid=(tokens // tm,),
        in_specs=[
            pl.BlockSpec((tm, D_MODEL), lambda i: (i, 0)),
            pl.BlockSpec((D_MODEL, D_MODEL), lambda i: (0, 0), pipeline_mode=pl.Buffered(1)),
            pl.BlockSpec((tm, D_MODEL), lambda i: (i, 0)),
        ],
        out_specs=pl.BlockSpec((tm, D_MODEL), lambda i: (i, 0)),
        out_shape=jax.ShapeDtypeStruct((tokens, D_MODEL), F32),
        scratch_shapes=[pltpu.VMEM((D_MODEL, D_MODEL), BF16)],
        compiler_params=_compiler_params(("arbitrary",), 52),
        name="outproj",
    )(y, w_out, x2d)


def _ffn_kernel(x_ref, g2_ref, wg_ref, wu_ref, wd_ref, gf_ref, o_ref, hn_ref, *, final_norm):
    j = pl.program_id(1)

    @pl.when(j == 0)
    def _():
        _rmsnorm_rows(x_ref, g2_ref[...], hn_ref, copy_ref=o_ref)

    hn = hn_ref[...]
    gate = jnp.dot(hn, wg_ref[0], preferred_element_type=F32)
    up = jnp.dot(hn, wu_ref[0], preferred_element_type=F32)
    act = (gate * jax.nn.sigmoid(gate)) * up
    o_ref[...] += jnp.dot(act.astype(BF16), wd_ref[...].astype(BF16),
                          preferred_element_type=F32)

    if final_norm:
        @pl.when(j == pl.num_programs(1) - 1)
        def _():
            _rmsnorm_rows(o_ref, gf_ref[...], o_ref)


def _ffn(x1, g2, wg_bf, wu_bf, wd, gf, final_norm):
    tokens = x1.shape[0]
    tm = min(1024, tokens)
    tf = FFN_TILE
    return pl.pallas_call(
        functools.partial(_ffn_kernel, final_norm=final_norm),
        grid=(tokens // tm, D_FF // tf),
        in_specs=[
            pl.BlockSpec((tm, D_MODEL), lambda i, j: (i, 0)),
            pl.BlockSpec((1, D_MODEL), lambda i, j: (0, 0)),
            pl.BlockSpec((1, D_MODEL, tf), lambda i, j: (j, 0, 0)),
            pl.BlockSpec((1, D_MODEL, tf), lambda i, j: (j, 0, 0)),
            pl.BlockSpec((tf, D_MODEL), lambda i, j: (j, 0)),
            pl.BlockSpec((1, D_MODEL), lambda i, j: (0, 0)),
        ],
        out_specs=pl.BlockSpec((tm, D_MODEL), lambda i, j: (i, 0)),
        out_shape=jax.ShapeDtypeStruct((tokens, D_MODEL), F32),
        scratch_shapes=[pltpu.VMEM((tm, D_MODEL), BF16)],
        compiler_params=_compiler_params(("parallel", "arbitrary"), 60),
        name="ffn",
    )(x1, g2, wg_bf, wu_bf, wd, gf)


def _gate_params(w_t, b_i, b_f):
    wgate = jnp.pad(w_t[N_MAIN:, :], ((0, LANES - 2 * N_HEADS), (0, 0)))
    bias = jnp.concatenate([b_i, b_f]).astype(F32)
    bias_row = jnp.pad(bias, (0, LANES - 2 * N_HEADS))[None, :]
    bias_col = jnp.broadcast_to(bias[:, None], (SUBLANES, LANES))
    return wgate, bias_row, bias_col


def _tail(x2d, y, w_out, g2, ffn_weights, gf, final_norm):
    x1 = _outproj(y, w_out, x2d)
    return _ffn(x1, g2[None, :], *ffn_weights, gf[None, :], final_norm)


def _stack(states, k):
    if len(states) == 1:
        return states[0][k][None]
    return jnp.stack([st[k] for st in states])


def kernel(x_prompt, x_sample, state_conv, state_mlstm_C, state_mlstm_n, state_mlstm_m,
           norm1_g, w_in, b_igate, b_fgate, conv_w, conv_out_g, mlstm_out_g, w_out,
           norm2_g, w_ffn_gate, w_ffn_up, w_ffn_down, final_norm_g):
    depth = w_in.shape[0]
    bp, sp, _ = x_prompt.shape
    bs, ss, _ = x_sample.shape
    hp = x_prompt.reshape(bp * sp, D_MODEL)
    hs = x_sample.reshape(bs * ss, D_MODEL)
    p_states, s_states = [], []
    for l in range(depth):
        final_norm = l == depth - 1
        w_t = w_in[l].T
        wgate, bias_row, bias_col = _gate_params(w_t, b_igate[l], b_fgate[l])
        w_bf = _cast_main_columns(w_t)
        g1 = norm1_g[l][None, :]
        conv_g = conv_out_g[l][None, :]
        ml_g = mlstm_out_g[l][None, :]

        y, cb, c1, n1, m1, wg_bf, wu_bf = _fused_prompt(
            hp, g1, w_bf, wgate, bp, sp, conv_w[l], conv_g, ml_g, bias_row, bias_col,
            w_ffn_gate[l], w_ffn_up[l])
        tail = (w_out[l], norm2_g[l], (wg_bf, wu_bf, w_ffn_down[l]), final_norm_g, final_norm)
        hp = _tail(hp, y, *tail)
        p_states.append((cb, c1, n1, m1[:, :, 0]))

        z, gcol, grow = _inproj(hs, g1, w_bf, wgate)
        m_tok = jnp.repeat(state_mlstm_m[l], ss, axis=0)
        mcol = jnp.pad(m_tok, ((0, 0), (0, LANES - N_HEADS)))
        mrow = jnp.pad(m_tok.T, ((0, SUBLANES - N_HEADS), (0, 0)))
        y, cb, c1, n1, m_row = _mix_sample(z, gcol, grow, mcol, mrow, state_conv[l],
                                           state_mlstm_C[l], state_mlstm_n[l], bs, ss,
                                           conv_w[l], conv_g, ml_g, bias_row, bias_col)
        hs = _tail(hs, y, *tail)
        s_states.append((cb, c1, n1, m_row[:N_HEADS, ::ss].T))

    return (hp.reshape(bp, sp, D_MODEL), hs.reshape(bs, ss, D_MODEL),
            _stack(p_states, 0), _stack(p_states, 1), _stack(p_states, 2), _stack(p_states, 3),
            _stack(s_states, 0), _stack(s_states, 1), _stack(s_states, 2), _stack(s_states, 3))
```

```python
import functools

import jax
import jax.numpy as jnp
from jax import lax
from jax.experimental import pallas as pl
from jax.experimental.pallas import tpu as pltpu

F32 = jnp.float32
BF16 = jnp.bfloat16

D_MODEL = 2048
D_CONV = 1024
CONV_W = 3
N_HEADS = 4
D_K = 128
D_V = 256
D_FF = 5632
EPS = 1e-6

SRC_Q = 3 * D_CONV
SRC_K = SRC_Q + N_HEADS * D_K
SRC_V = SRC_K + N_HEADS * D_K
SRC_O = SRC_V + N_HEADS * D_V
N_MAIN = SRC_O + N_HEADS * D_V

HEAD_COLS = 2 * D_K + 2 * D_V
N_MLSTM = N_HEADS * HEAD_COLS
CONV_HALF = D_CONV // 2
BG, CG, XT = 0, 1, 2


def _cols_conv(kind, half):
    start = N_MLSTM + (3 * half + kind) * CONV_HALF
    return slice(start, start + CONV_HALF)


def _conv_operand(z, kind, rows=slice(None)):
    return jnp.concatenate([z[rows, _cols_conv(kind, 0)], z[rows, _cols_conv(kind, 1)]], axis=1)


def _cols_q(h):
    return slice(h * HEAD_COLS, h * HEAD_COLS + D_K)


def _cols_k(h):
    return slice(h * HEAD_COLS + D_K, h * HEAD_COLS + 2 * D_K)


def _cols_v(h):
    return slice(h * HEAD_COLS + 2 * D_K, h * HEAD_COLS + 2 * D_K + D_V)


def _cols_o(h):
    return slice(h * HEAD_COLS + 2 * D_K + D_V, (h + 1) * HEAD_COLS)


def _source_lane_block(j):
    per_head = HEAD_COLS // LANES
    h, r = j // per_head, j % per_head
    v_blocks = D_V // LANES
    mlstm_src = jnp.where(
        r == 0, SRC_Q // LANES + h,
        jnp.where(r == 1, SRC_K // LANES + h,
                  jnp.where(r < 2 + v_blocks,
                            SRC_V // LANES + v_blocks * h + (r - 2),
                            SRC_O // LANES + v_blocks * h + (r - 2 - v_blocks))))
    per_kind = CONV_HALF // LANES
    c = j - N_MLSTM // LANES
    half, r = c // (3 * per_kind), c % (3 * per_kind)
    conv_src = (r // per_kind) * (D_CONV // LANES) + half * per_kind + r % per_kind
    return jnp.where(j < N_MLSTM // LANES, mlstm_src, conv_src)

LANES = 128
SUBLANES = 8
GATE_ROWS = 16
MIB = 1024 * 1024

PROMPT_CHUNK = 256
SAMPLE_GROUP = 16
NORM_ROWS = 64
W_SLAB = 512
FFN_TILE = 512
CAST_PIECE = 256


def _compiler_params(semantics, vmem_mib):
    return pltpu.CompilerParams(dimension_semantics=semantics,
                                vmem_limit_bytes=vmem_mib * MIB)


def _rmsnorm_piece(x, gain, dtype):
    ms = jnp.mean(x * x, axis=-1, keepdims=True)
    return ((x * lax.rsqrt(ms + EPS)) * gain).astype(dtype)


def _rmsnorm_rows(src_ref, gain, dst_ref, copy_ref=None):
    rows = src_ref.shape[0]
    chunk = min(NORM_ROWS, rows)

    def body(i, carry):
        r = pl.multiple_of(i * chunk, chunk)
        x = src_ref[pl.ds(r, chunk), :].astype(F32)
        if copy_ref is not None:
            copy_ref[pl.ds(r, chunk), :] = x
        dst_ref[pl.ds(r, chunk), :] = _rmsnorm_piece(x, gain, dst_ref.dtype)
        return carry

    lax.fori_loop(0, rows // chunk, body, 0, unroll=2)


def _cast_kernel(*refs):
    *w_refs, o_ref = refs
    for r, w_ref in enumerate(w_refs):
        o_ref[:, r * LANES:(r + 1) * LANES] = w_ref[...].T.astype(o_ref.dtype)


def _cast_main_columns(w_t):
    per_slab = W_SLAB // LANES
    in_specs = [
        pl.BlockSpec((LANES, D_MODEL),
                     lambda j, r=r: (_source_lane_block(per_slab * j + r), 0))
        for r in range(per_slab)
    ]
    return pl.pallas_call(
        _cast_kernel,
        grid=(N_MAIN // W_SLAB,),
        in_specs=in_specs,
        out_specs=pl.BlockSpec((D_MODEL, W_SLAB), lambda j: (0, j)),
        out_shape=jax.ShapeDtypeStruct((D_MODEL, N_MAIN), BF16),
        compiler_params=_compiler_params(("parallel",), 32),
        name="cast_w_in",
    )(*([w_t] * per_slab))


def _gate_dot(xn, wgate_t_ref):
    return lax.dot_general(xn, wgate_t_ref[...].astype(BF16), (((1,), (1,)), ((), ())),
                           preferred_element_type=F32)


def _inproj_kernel(x_ref, g_ref, w_ref, wgate_ref, z_ref, gcol_ref, grow_ref, xn_ref):
    @pl.when(pl.program_id(1) == 0)
    def _():
        _rmsnorm_rows(x_ref, g_ref[...], xn_ref)
        gc = _gate_dot(xn_ref[...], wgate_ref)
        gcol_ref[...] = gc
        grow_ref[...] = gc.T[0:GATE_ROWS, :]

    z_ref[...] = jnp.dot(xn_ref[...], w_ref[...], preferred_element_type=F32)


def _inproj(x2d, g1, w_bf, wgate):
    tokens = x2d.shape[0]
    tm = min(1024, tokens)
    tn = W_SLAB
    return pl.pallas_call(
        _inproj_kernel,
        grid=(tokens // tm, N_MAIN // tn),
        in_specs=[
            pl.BlockSpec((tm, D_MODEL), lambda i, j: (i, 0)),
            pl.BlockSpec((1, D_MODEL), lambda i, j: (0, 0)),
            pl.BlockSpec((D_MODEL, tn), lambda i, j: (0, j)),
            pl.BlockSpec((LANES, D_MODEL), lambda i, j: (0, 0)),
        ],
        out_specs=[
            pl.BlockSpec((tm, tn), lambda i, j: (i, j)),
            pl.BlockSpec((tm, LANES), lambda i, j: (i, 0)),
            pl.BlockSpec((GATE_ROWS, tm), lambda i, j: (0, i)),
        ],
        out_shape=[
            jax.ShapeDtypeStruct((tokens, N_MAIN), F32),
            jax.ShapeDtypeStruct((tokens, LANES), F32),
            jax.ShapeDtypeStruct((GATE_ROWS, tokens), F32),
        ],
        scratch_shapes=[pltpu.VMEM((tm, D_MODEL), BF16)],
        compiler_params=_compiler_params(("parallel", "arbitrary"), 48),
        name="inproj",
    )(x2d, g1, w_bf, wgate)


def _log_sigmoid(x):
    return jnp.minimum(x, 0.0) - jnp.log1p(jnp.exp(-jnp.abs(x)))


def _dot_exact(a, b):
    return jnp.dot(a, b, precision=lax.Precision.HIGHEST, preferred_element_type=F32)


def _intra(q_bf, k_bf, b_col, b_row, li_row, m0_col, mask):
    dmat = jnp.where(mask, b_col - b_row + li_row, -jnp.inf)
    inter = b_col + m0_col
    m_t = jnp.maximum(inter, jnp.max(dmat, axis=1, keepdims=True))
    w = jnp.exp(dmat - m_t)
    a_inter = jnp.exp(inter - m_t)
    qk = lax.dot_general(q_bf, k_bf, (((1,), (1,)), ((), ())), preferred_element_type=F32)
    return qk * w, a_inter, m_t


def _head_output(num, den, m_t, o_gate, gain):
    denom = jnp.maximum(jnp.abs(den), jnp.exp(-m_t))
    h = num * (1.0 / denom)
    hn = (h * lax.rsqrt(jnp.mean(h * h, axis=1, keepdims=True) + EPS)) * gain
    return jax.nn.sigmoid(o_gate) * hn


def _conv_norm(bg, conv, gain):
    yc = bg * conv
    return (yc * lax.rsqrt(jnp.mean(yc * yc, axis=1, keepdims=True) + EPS)) * gain


def _prompt_conv_half(z, half, convw_ref, ubuf, yc_scr):
    L = z.shape[0]
    ch = slice(half * CONV_HALF, (half + 1) * CONV_HALF)
    u = z[:, _cols_conv(CG, half)] * z[:, _cols_conv(XT, half)]
    ubuf[SUBLANES:SUBLANES + L, ch] = u
    conv = (ubuf[SUBLANES - 2:SUBLANES - 2 + L, ch] * convw_ref[0:1, ch]
            + ubuf[SUBLANES - 1:SUBLANES - 1 + L, ch] * convw_ref[1:2, ch]
            + u * convw_ref[2:3, ch])
    ubuf[0:SUBLANES, ch] = ubuf[L:L + SUBLANES, ch]
    yc = z[:, _cols_conv(BG, half)] * conv
    yc_scr[:, ch] = yc
    return jnp.sum(yc * yc, axis=1, keepdims=True)


def _prompt_conv_finish(sumsq, yc_scr, convg_ref, y_ref):
    scale = lax.rsqrt(sumsq * (1.0 / D_CONV) + EPS)
    y_ref[:, 0:D_CONV] = ((yc_scr[...] * scale) * convg_ref[...]).astype(y_ref.dtype)


def _prompt_mlstm(z, gc, gr, mlg_ref, bias_row_ref, bias_col_ref, y_ref, c_scr, n_scr, m_scr,
                  fillers):
    L = z.shape[0]
    fillers = list(fillers)

    def fill():
        if fillers:
            fillers.pop(0)()

    pre_c = gc + bias_row_ref[...]
    pre_r = gr + bias_col_ref[:, 0:1]
    row_id = lax.broadcasted_iota(jnp.int32, (L, L), 0)
    col_id = lax.broadcasted_iota(jnp.int32, (L, L), 1)
    causal = col_id <= row_id
    b_c = _dot_exact(causal.astype(F32), _log_sigmoid(pre_c))
    b_r = _dot_exact(_log_sigmoid(pre_r), (row_id <= col_id).astype(F32))

    for h in range(N_HEADS):
        li_r = pre_r[h:h + 1, :]
        li_c = pre_c[:, h:h + 1]
        bc = b_c[:, N_HEADS + h:N_HEADS + h + 1]
        br = b_r[N_HEADS + h:N_HEADS + h + 1, :]
        m0 = m_scr[h, 0:1, 0:1]
        qf = z[:, _cols_q(h)] * (D_K ** -0.5)
        q_bf = qf.astype(BF16)
        kf = z[:, _cols_k(h)]
        v_bf = z[:, _cols_v(h)].astype(BF16)
        s, a_inter, m_t = _intra(q_bf, kf.astype(BF16), bc, br, li_r, m0, causal)
        fill()
        c_old = c_scr[h]
        n_old = n_scr[h]
        num = (a_inter * jnp.dot(q_bf, c_old.astype(BF16), preferred_element_type=F32)
               + jnp.dot(s.astype(BF16), v_bf, preferred_element_type=F32))
        den = (a_inter * jnp.sum(qf * n_old, axis=1, keepdims=True)
               + jnp.sum(s, axis=1, keepdims=True))
        fill()
        y_ref[:, D_CONV + h * D_V:D_CONV + (h + 1) * D_V] = _head_output(
            num, den, m_t, z[:, _cols_o(h)],
            mlg_ref[:, h * D_V:(h + 1) * D_V]).astype(y_ref.dtype)

        b_last = br[:, L - 1:L]
        m_new = jnp.maximum(b_last + m0,
                            jnp.max(b_last - br + li_r, axis=1, keepdims=True))
        decay = jnp.exp(b_last + m0 - m_new)
        kw = kf * jnp.exp(b_last - bc + li_c - m_new)
        c_scr[h] = decay * c_old + lax.dot_general(
            kw.astype(BF16), v_bf, (((0,), (0,)), ((), ())), preferred_element_type=F32)
        n_scr[h] = decay * n_old + jnp.sum(kw, axis=0, keepdims=True)
        m_scr[h] = jnp.broadcast_to(m_new, (SUBLANES, LANES))
        fill()
    while fillers:
        fill()


def _fused_prompt_kernel(x_ref, g1_ref, w_ref, wgate_ref, convw_ref, convg_ref, mlg_ref,
                         bias_row_ref, bias_col_ref, wg_ref, wu_ref, wout_ref,
                         y_ref, convs_ref, c_out_ref, n_out_ref, m_out_ref,
                         wg_bf_ref, wu_bf_ref, wout_bf_ref,
                         xn_scr, z_scr, ubuf, yc_scr, c_scr, n_scr, m_scr):
    chunk = pl.program_id(1)
    L = x_ref.shape[0]

    @pl.when(chunk == 0)
    def _():
        ubuf[0:SUBLANES, :] = jnp.zeros((SUBLANES, D_CONV), F32)
        c_scr[...] = jnp.zeros(c_scr.shape, F32)
        n_scr[...] = jnp.zeros(n_scr.shape, F32)
        m_scr[...] = jnp.zeros(m_scr.shape, F32)

    gain = g1_ref[...]
    for r in range(0, L, NORM_ROWS):
        xn_scr[r:r + NORM_ROWS, :] = _rmsnorm_piece(x_ref[r:r + NORM_ROWS, :], gain, BF16)
    xn = xn_scr[...]
    gc = _gate_dot(xn, wgate_ref)
    gr = gc.T[0:SUBLANES, :]

    def project(j):
        cols = slice(j * W_SLAB, (j + 1) * W_SLAB)
        z_scr[:, cols] = jnp.dot(xn_scr[...], w_ref[:, cols], preferred_element_type=F32)

    piece = lambda j: functools.partial(project, j)
    skip = lambda: None
    sumsq = []

    def conv_first_half_beside(j):
        project(j)
        sumsq.append(_prompt_conv_half(z_scr, 0, convw_ref, ubuf, yc_scr))

    project(0)
    wg_bf_ref[0] = wg_ref[...].astype(BF16)
    wout_bf_ref[...] = wout_ref[...].astype(BF16)
    project(1)
    wu_bf_ref[0] = wu_ref[...].astype(BF16)
    fillers = [piece(2), piece(6), piece(7),
               piece(3), piece(4), piece(8),
               piece(5), functools.partial(conv_first_half_beside, 9), skip,
               piece(10), piece(11), skip]
    _prompt_mlstm(z_scr, gc, gr, mlg_ref, bias_row_ref, bias_col_ref, y_ref,
                  c_scr, n_scr, m_scr, fillers)
    sumsq.append(_prompt_conv_half(z_scr, 1, convw_ref, ubuf, yc_scr))
    _prompt_conv_finish(sumsq[0] + sumsq[1], yc_scr, convg_ref, y_ref)

    @pl.when(chunk == pl.num_programs(1) - 1)
    def _():
        convs_ref[0] = ubuf[SUBLANES - 2:SUBLANES, :]
        c_out_ref[0] = c_scr[...]
        for h in range(N_HEADS):
            n_out_ref[0, h:h + 1, :] = n_scr[h]
            m_out_ref[0, h:h + 1, :] = m_scr[h, 0:1, :]


def _fused_prompt(x2d, g1, w_bf, wgate, batch, seq, conv_w, conv_g, ml_g, bias_row, bias_col,
                  wg, wu, w_out):
    L = PROMPT_CHUNK
    nc = seq // L
    const = lambda b, c: (0, 0)
    wout_rows = D_MODEL // (batch * nc)
    assert wout_rows * batch * nc == D_MODEL and wout_rows % (2 * SUBLANES) == 0
    wout_piece = pl.BlockSpec((wout_rows, D_MODEL), lambda b, c: (b * nc + c, 0))
    n_pieces = D_FF // CAST_PIECE
    assert batch * nc >= n_pieces, "not enough grid steps to cast the FFN weights"
    per_tile = FFN_TILE // CAST_PIECE
    piece_of = lambda b, c: jnp.minimum(b * nc + c, n_pieces - 1)
    tiled = lambda b, c: (piece_of(b, c) // per_tile, 0, piece_of(b, c) % per_tile)
    return pl.pallas_call(
        _fused_prompt_kernel,
        grid=(batch, nc),
        in_specs=[
            pl.BlockSpec((L, D_MODEL), lambda b, c: (b * nc + c, 0)),
            pl.BlockSpec((1, D_MODEL), const),
            pl.BlockSpec((D_MODEL, N_MAIN), const, pipeline_mode=pl.Buffered(1)),
            pl.BlockSpec((LANES, D_MODEL), const),
            pl.BlockSpec((CONV_W, D_CONV), const),
            pl.BlockSpec((1, D_CONV), const),
            pl.BlockSpec((1, N_HEADS * D_V), const),
            pl.BlockSpec((1, LANES), const),
            pl.BlockSpec((SUBLANES, LANES), const),
            pl.BlockSpec((D_MODEL, CAST_PIECE), lambda b, c: (0, piece_of(b, c))),
            pl.BlockSpec((D_MODEL, CAST_PIECE), lambda b, c: (0, piece_of(b, c))),
            wout_piece,
        ],
        out_specs=[
            pl.BlockSpec((L, D_MODEL), lambda b, c: (b * nc + c, 0)),
            pl.BlockSpec((1, CONV_W - 1, D_CONV), lambda b, c: (b, 0, 0)),
            pl.BlockSpec((1, N_HEADS, D_K, D_V), lambda b, c: (b, 0, 0, 0)),
            pl.BlockSpec((1, N_HEADS, D_K), lambda b, c: (b, 0, 0)),
            pl.BlockSpec((1, N_HEADS, LANES), lambda b, c: (b, 0, 0)),
            pl.BlockSpec((1, D_MODEL, CAST_PIECE), tiled),
            pl.BlockSpec((1, D_MODEL, CAST_PIECE), tiled),
            wout_piece,
        ],
        out_shape=[
            jax.ShapeDtypeStruct((batch * seq, D_MODEL), BF16),
            jax.ShapeDtypeStruct((batch, CONV_W - 1, D_CONV), F32),
            jax.ShapeDtypeStruct((batch, N_HEADS, D_K, D_V), F32),
            jax.ShapeDtypeStruct((batch, N_HEADS, D_K), F32),
            jax.ShapeDtypeStruct((batch, N_HEADS, LANES), F32),
            jax.ShapeDtypeStruct((D_FF // FFN_TILE, D_MODEL, FFN_TILE), BF16),
            jax.ShapeDtypeStruct((D_FF // FFN_TILE, D_MODEL, FFN_TILE), BF16),
            jax.ShapeDtypeStruct((D_MODEL, D_MODEL), BF16),
        ],
        scratch_shapes=[
            pltpu.VMEM((L, D_MODEL), BF16),
            pltpu.VMEM((L, N_MAIN), F32),
            pltpu.VMEM((L + 2 * SUBLANES, D_CONV), F32),
            pltpu.VMEM((L, D_CONV), F32),
            pltpu.VMEM((N_HEADS, D_K, D_V), F32),
            pltpu.VMEM((N_HEADS, 1, D_K), F32),
            pltpu.VMEM((N_HEADS, SUBLANES, LANES), F32),
        ],
        compiler_params=_compiler_params(("arbitrary", "arbitrary"), 62),
        name="fused_prompt",
    )(x2d, g1, w_bf, wgate, conv_w, conv_g, ml_g, bias_row, bias_col, wg, wu, w_out)


def _mix_sample_kernel(z_ref, gcol_ref, grow_ref, mcol_ref, mrow_ref,
                       convs_in_ref, c_in_ref, n_in_ref,
                       convw_ref, convg_ref, mlg_ref, bias_row_ref, bias_col_ref,
                       y_ref, convs_ref, c_out_ref, n_out_ref, m_out_ref,
                       pad_scr, conv_scr, *, seq):
    L = z_ref.shape[0]
    group = L // seq
    stride = seq + SUBLANES

    for g in range(group):
        rows = slice(g * seq, (g + 1) * seq)
        u = _conv_operand(z_ref, CG, rows) * _conv_operand(z_ref, XT, rows)
        base = g * stride
        pad_scr[base + SUBLANES - 2:base + SUBLANES, :] = convs_in_ref[g]
        pad_scr[base + SUBLANES:base + SUBLANES + seq, :] = u
        conv_scr[g * seq:(g + 1) * seq, :] = (
            pad_scr[base + SUBLANES - 2:base + SUBLANES - 2 + seq, :] * convw_ref[0:1, :]
            + pad_scr[base + SUBLANES - 1:base + SUBLANES - 1 + seq, :] * convw_ref[1:2, :]
            + u * convw_ref[2:3, :])
        convs_ref[g] = pad_scr[base + seq + SUBLANES - 2:base + seq + SUBLANES, :]
    y_ref[:, 0:D_CONV] = _conv_norm(_conv_operand(z_ref, BG), conv_scr[...],
                                    convg_ref[...]).astype(y_ref.dtype)

    pre_c = gcol_ref[...] + bias_row_ref[...]
    pre_r = grow_ref[0:SUBLANES, :] + bias_col_ref[:, 0:1]
    row_id = lax.broadcasted_iota(jnp.int32, (L, L), 0)
    col_id = lax.broadcasted_iota(jnp.int32, (L, L), 1)
    same = (row_id // seq) == (col_id // seq)
    causal = same & (col_id <= row_id)
    lf_c = _log_sigmoid(pre_c)
    lf_r = _log_sigmoid(pre_r)
    b_c = _dot_exact(causal.astype(F32), lf_c)
    b_r = _dot_exact(lf_r, (same & (row_id <= col_id)).astype(F32))
    same_f = same.astype(F32)
    tot_c = _dot_exact(same_f, lf_c)
    tot_r = _dot_exact(lf_r, same_f)
    lane_seq = lax.broadcasted_iota(jnp.int32, (1, L), 1) // seq

    for h in range(N_HEADS):
        li_r = pre_r[h:h + 1, :]
        li_c = pre_c[:, h:h + 1]
        bc = b_c[:, N_HEADS + h:N_HEADS + h + 1]
        br = b_r[N_HEADS + h:N_HEADS + h + 1, :]
        bl_c = tot_c[:, N_HEADS + h:N_HEADS + h + 1]
        bl_r = tot_r[N_HEADS + h:N_HEADS + h + 1, :]
        m0_c = mcol_ref[:, h:h + 1]
        m0_r = mrow_ref[h:h + 1, :]
        qf = z_ref[:, _cols_q(h)] * (D_K ** -0.5)
        q_bf = qf.astype(BF16)
        kf = z_ref[:, _cols_k(h)]
        v_bf = z_ref[:, _cols_v(h)].astype(BF16)
        s, a_inter, m_t = _intra(q_bf, kf.astype(BF16), bc, br, li_r, m0_c, causal)

        qc_rows, qn_rows = [], []
        for g in range(group):
            rows = slice(g * seq, (g + 1) * seq)
            qc_rows.append(jnp.dot(qf[rows], c_in_ref[g, h], preferred_element_type=F32))
            qn_rows.append(jnp.sum(qf[rows] * n_in_ref[g, h:h + 1, :], axis=1, keepdims=True))
        num = (a_inter * jnp.concatenate(qc_rows, axis=0)
               + jnp.dot(s.astype(BF16), v_bf, preferred_element_type=F32))
        den = (a_inter * jnp.concatenate(qn_rows, axis=0)
               + jnp.sum(s, axis=1, keepdims=True))
        y_ref[:, D_CONV + h * D_V:D_CONV + (h + 1) * D_V] = _head_output(
            num, den, m_t, z_ref[:, _cols_o(h)],
            mlg_ref[:, h * D_V:(h + 1) * D_V]).astype(y_ref.dtype)

        g_r = bl_r - br + li_r
        gmax_c = jnp.max(jnp.where(same, g_r, -jnp.inf), axis=1, keepdims=True)
        gmax_r = jnp.max(jnp.where(same, gmax_c, -jnp.inf), axis=0, keepdims=True)
        m_new_c = jnp.maximum(bl_c + m0_c, gmax_c)
        m_new_r = jnp.maximum(bl_r + m0_r, gmax_r)
        decay_r = jnp.exp(bl_r + m0_r - m_new_r)
        kw = kf * jnp.exp(bl_c - bc + li_c - m_new_c)
        kw_t = kw.T
        for g in range(group):
            decay = decay_r[:, g * seq:g * seq + 1]
            kw_g = jnp.where(lane_seq == g, kw_t, 0.0).astype(BF16)
            c_out_ref[g, h] = decay * c_in_ref[g, h] + jnp.dot(
                kw_g, v_bf, preferred_element_type=F32)
            n_out_ref[g, h:h + 1, :] = (decay * n_in_ref[g, h:h + 1, :]
                                        + jnp.sum(kw[g * seq:(g + 1) * seq], axis=0, keepdims=True))
        m_out_ref[h:h + 1, :] = m_new_r
    m_out_ref[N_HEADS:, :] = jnp.zeros((SUBLANES - N_HEADS, L), F32)


def _mix_sample(z, gcol, grow, mcol, mrow, convs, c_state, n_state, batch, seq,
                conv_w, conv_g, ml_g, bias_row, bias_col):
    group = SAMPLE_GROUP
    L = group * seq
    const = lambda i: (0, 0)
    return pl.pallas_call(
        functools.partial(_mix_sample_kernel, seq=seq),
        grid=(batch // group,),
        in_specs=[
            pl.BlockSpec((L, N_MAIN), lambda i: (i, 0)),
            pl.BlockSpec((L, LANES), lambda i: (i, 0)),
            pl.BlockSpec((GATE_ROWS, L), lambda i: (0, i)),
            pl.BlockSpec((L, LANES), lambda i: (i, 0)),
            pl.BlockSpec((SUBLANES, L), lambda i: (0, i)),
            pl.BlockSpec((group, CONV_W - 1, D_CONV), lambda i: (i, 0, 0)),
            pl.BlockSpec((group, N_HEADS, D_K, D_V), lambda i: (i, 0, 0, 0)),
            pl.BlockSpec((group, N_HEADS, D_K), lambda i: (i, 0, 0)),
            pl.BlockSpec((CONV_W, D_CONV), const),
            pl.BlockSpec((1, D_CONV), const),
            pl.BlockSpec((1, N_HEADS * D_V), const),
            pl.BlockSpec((1, LANES), const),
            pl.BlockSpec((SUBLANES, LANES), const),
        ],
        out_specs=[
            pl.BlockSpec((L, D_MODEL), lambda i: (i, 0)),
            pl.BlockSpec((group, CONV_W - 1, D_CONV), lambda i: (i, 0, 0)),
            pl.BlockSpec((group, N_HEADS, D_K, D_V), lambda i: (i, 0, 0, 0)),
            pl.BlockSpec((group, N_HEADS, D_K), lambda i: (i, 0, 0)),
            pl.BlockSpec((SUBLANES, L), lambda i: (0, i)),
        ],
        out_shape=[
            jax.ShapeDtypeStruct((batch * seq, D_MODEL), BF16),
            jax.ShapeDtypeStruct((batch, CONV_W - 1, D_CONV), F32),
            jax.ShapeDtypeStruct((batch, N_HEADS, D_K, D_V), F32),
            jax.ShapeDtypeStruct((batch, N_HEADS, D_K), F32),
            jax.ShapeDtypeStruct((SUBLANES, batch * seq), F32),
        ],
        scratch_shapes=[
            pltpu.VMEM((group * (seq + SUBLANES), D_CONV), F32),
            pltpu.VMEM((L, D_CONV), F32),
        ],
        compiler_params=_compiler_params(("parallel",), 56),
        name="mix_sample",
    )(z, gcol, grow, mcol, mrow, convs, c_state, n_state,
      conv_w, conv_g, ml_g, bias_row, bias_col)


def _outproj_kernel(y_ref, w_ref, x_ref, o_ref):
    o_ref[...] = x_ref[...] + jnp.dot(y_ref[...], w_ref[...], preferred_element_type=F32)


def _outproj(y, w_out_bf, x2d):
    tokens = x2d.shape[0]
    tm = min(512, tokens)
    return pl.pallas_call(
        _outproj_kernel,
        grid=(tokens // tm,),
        in_specs=[
            pl.BlockSpec((tm, D_MODEL), lambda i: (i, 0)),
            pl.BlockSpec((D_MODEL, D_MODEL), lambda i: (0, 0), pipeline_mode=pl.Buffered(1)),
            pl.BlockSpec((tm, D_MODEL), lambda i: (i, 0)),
        ],
        out_specs=pl.BlockSpec((tm, D_MODEL), lambda i: (i, 0)),
        out_shape=jax.ShapeDtypeStruct((tokens, D_MODEL), F32),
        compiler_params=_compiler_params(("parallel",), 40),
        name="outproj",
    )(y, w_out_bf, x2d)


def _ffn_kernel(x_ref, g2_ref, wg_ref, wu_ref, wd_ref, gf_ref, o_ref, hn_ref, *, final_norm):
    j = pl.program_id(1)

    @pl.when(j == 0)
    def _():
        _rmsnorm_rows(x_ref, g2_ref[...], hn_ref, copy_ref=o_ref)

    hn = hn_ref[...]
    half = wd_ref.shape[0] // 2
    halves = (slice(0, half), slice(half, 2 * half))
    pre = [(jnp.dot(hn, wg_ref[0, :, cols], preferred_element_type=F32),
            jnp.dot(hn, wu_ref[0, :, cols], preferred_element_type=F32)) for cols in halves]
    down = None
    for (gate, up), rows in zip(pre, halves):
        act = ((gate * jax.nn.sigmoid(gate)) * up).astype(BF16)
        part = jnp.dot(act, wd_ref[rows, :].astype(BF16), preferred_element_type=F32)
        down = part if down is None else down + part
    o_ref[...] += down

    if final_norm:
        @pl.when(j == pl.num_programs(1) - 1)
        def _():
            _rmsnorm_rows(o_ref, gf_ref[...], o_ref)


def _ffn(x1, g2, wg_bf, wu_bf, wd, gf, final_norm):
    tokens = x1.shape[0]
    tm = min(1024, tokens)
    tf = FFN_TILE
    return pl.pallas_call(
        functools.partial(_ffn_kernel, final_norm=final_norm),
        grid=(tokens // tm, D_FF // tf),
        in_specs=[
            pl.BlockSpec((tm, D_MODEL), lambda i, j: (i, 0)),
            pl.BlockSpec((1, D_MODEL), lambda i, j: (0, 0)),
            pl.BlockSpec((1, D_MODEL, tf), lambda i, j: (j, 0, 0)),
            pl.BlockSpec((1, D_MODEL, tf), lambda i, j: (j, 0, 0)),
            pl.BlockSpec((tf, D_MODEL), lambda i, j: (j, 0)),
            pl.BlockSpec((1, D_MODEL), lambda i, j: (0, 0)),
        ],
        out_specs=pl.BlockSpec((tm, D_MODEL), lambda i, j: (i, 0)),
        out_shape=jax.ShapeDtypeStruct((tokens, D_MODEL), F32),
        scratch_shapes=[pltpu.VMEM((tm, D_MODEL), BF16)],
        compiler_params=_compiler_params(("parallel", "arbitrary"), 60),
        name="ffn",
    )(x1, g2, wg_bf, wu_bf, wd, gf)


def _gate_params(w_t, b_i, b_f):
    wgate = jnp.pad(w_t[N_MAIN:, :], ((0, LANES - 2 * N_HEADS), (0, 0)))
    bias = jnp.concatenate([b_i, b_f]).astype(F32)
    bias_row = jnp.pad(bias, (0, LANES - 2 * N_HEADS))[None, :]
    bias_col = jnp.broadcast_to(bias[:, None], (SUBLANES, LANES))
    return wgate, bias_row, bias_col


def _tail(x2d, y, w_out, g2, ffn_weights, gf, final_norm):
    x1 = _outproj(y, w_out, x2d)
    return _ffn(x1, g2[None, :], *ffn_weights, gf[None, :], final_norm)


def _stack(states, k):
    if len(states) == 1:
        return states[0][k][None]
    return jnp.stack([st[k] for st in states])


def kernel(x_prompt, x_sample, state_conv, state_mlstm_C, state_mlstm_n, state_mlstm_m,
           norm1_g, w_in, b_igate, b_fgate, conv_w, conv_out_g, mlstm_out_g, w_out,
           norm2_g, w_ffn_gate, w_ffn_up, w_ffn_down, final_norm_g):
    depth = w_in.shape[0]
    bp, sp, _ = x_prompt.shape
    bs, ss, _ = x_sample.shape
    hp = x_prompt.reshape(bp * sp, D_MODEL)
    hs = x_sample.reshape(bs * ss, D_MODEL)
    p_states, s_states = [], []
    for l in range(depth):
        final_norm = l == depth - 1
        w_t = w_in[l].T
        wgate, bias_row, bias_col = _gate_params(w_t, b_igate[l], b_fgate[l])
        w_bf = _cast_main_columns(w_t)
        g1 = norm1_g[l][None, :]
        conv_g = conv_out_g[l][None, :]
        ml_g = mlstm_out_g[l][None, :]

        y, cb, c1, n1, m1, wg_bf, wu_bf, w_out_bf = _fused_prompt(
            hp, g1, w_bf, wgate, bp, sp, conv_w[l], conv_g, ml_g, bias_row, bias_col,
            w_ffn_gate[l], w_ffn_up[l], w_out[l])
        tail = (w_out_bf, norm2_g[l], (wg_bf, wu_bf, w_ffn_down[l]), final_norm_g, final_norm)
        hp = _tail(hp, y, *tail)
        p_states.append((cb, c1, n1, m1[:, :, 0]))

        z, gcol, grow = _inproj(hs, g1, w_bf, wgate)
        m_tok = jnp.repeat(state_mlstm_m[l], ss, axis=0)
        mcol = jnp.pad(m_tok, ((0, 0), (0, LANES - N_HEADS)))
        mrow = jnp.pad(m_tok.T, ((0, SUBLANES - N_HEADS), (0, 0)))
        y, cb, c1, n1, m_row = _mix_sample(z, gcol, grow, mcol, mrow, state_conv[l],
                                           state_mlstm_C[l], state_mlstm_n[l], bs, ss,
                                           conv_w[l], conv_g, ml_g, bias_row, bias_col)
        hs = _tail(hs, y, *tail)
        s_states.append((cb, c1, n1, m_row[:N_HEADS, ::ss].T))

    return (hp.reshape(bp, sp, D_MODEL), hs.reshape(bs, ss, D_MODEL),
            _stack(p_states, 0), _stack(p_states, 1), _stack(p_states, 2), _stack(p_states, 3),
            _stack(s_states, 0), _stack(s_states, 1), _stack(s_states, 2), _stack(s_states, 3))
```

```python
import functools

import jax
import jax.numpy as jnp
from jax import lax
from jax.experimental import pallas as pl
from jax.experimental.pallas import tpu as pltpu

F32 = jnp.float32
BF16 = jnp.bfloat16

D_MODEL = 2048
D_CONV = 1024
CONV_W = 3
N_HEADS = 4
D_K = 128
D_V = 256
D_FF = 5632
EPS = 1e-6

SRC_Q = 3 * D_CONV
SRC_K = SRC_Q + N_HEADS * D_K
SRC_V = SRC_K + N_HEADS * D_K
SRC_O = SRC_V + N_HEADS * D_V
N_MAIN = SRC_O + N_HEADS * D_V

HEAD_COLS = 2 * D_K + 2 * D_V
N_MLSTM = N_HEADS * HEAD_COLS
CONV_HALF = D_CONV // 2
BG, CG, XT = 0, 1, 2


def _cols_conv(kind, half):
    start = N_MLSTM + (3 * half + kind) * CONV_HALF
    return slice(start, start + CONV_HALF)


def _conv_operand(z, kind, rows=slice(None)):
    return jnp.concatenate([z[rows, _cols_conv(kind, 0)], z[rows, _cols_conv(kind, 1)]], axis=1)


def _cols_q(h):
    return slice(h * HEAD_COLS, h * HEAD_COLS + D_K)


def _cols_k(h):
    return slice(h * HEAD_COLS + D_K, h * HEAD_COLS + 2 * D_K)


def _cols_v(h):
    return slice(h * HEAD_COLS + 2 * D_K, h * HEAD_COLS + 2 * D_K + D_V)


def _cols_o(h):
    return slice(h * HEAD_COLS + 2 * D_K + D_V, (h + 1) * HEAD_COLS)


def _source_lane_block(j):
    per_head = HEAD_COLS // LANES
    h, r = j // per_head, j % per_head
    v_blocks = D_V // LANES
    mlstm_src = jnp.where(
        r == 0, SRC_Q // LANES + h,
        jnp.where(r == 1, SRC_K // LANES + h,
                  jnp.where(r < 2 + v_blocks,
                            SRC_V // LANES + v_blocks * h + (r - 2),
                            SRC_O // LANES + v_blocks * h + (r - 2 - v_blocks))))
    per_kind = CONV_HALF // LANES
    c = j - N_MLSTM // LANES
    half, r = c // (3 * per_kind), c % (3 * per_kind)
    conv_src = (r // per_kind) * (D_CONV // LANES) + half * per_kind + r % per_kind
    return jnp.where(j < N_MLSTM // LANES, mlstm_src, conv_src)

LANES = 128
SUBLANES = 8
GATE_ROWS = 16
MIB = 1024 * 1024

PROMPT_CHUNK = 256
SAMPLE_GROUP = 16
NORM_ROWS = 64
NORM_GROUP = 4
W_SLAB = 512
FFN_TILE = 512
CAST_PIECE = 256


def _compiler_params(semantics, vmem_mib):
    return pltpu.CompilerParams(dimension_semantics=semantics,
                                vmem_limit_bytes=vmem_mib * MIB)


def _rmsnorm_piece(x, gain, dtype):
    ms = jnp.mean(x * x, axis=-1, keepdims=True)
    return ((x * lax.rsqrt(ms + EPS)) * gain).astype(dtype)


def _rmsnorm_rows(src_ref, gain, dst_ref, copy_ref=None):
    rows = src_ref.shape[0]
    chunk = min(NORM_ROWS, rows)

    def body(i, carry):
        r = pl.multiple_of(i * chunk, chunk)
        x = src_ref[pl.ds(r, chunk), :].astype(F32)
        if copy_ref is not None:
            copy_ref[pl.ds(r, chunk), :] = x
        dst_ref[pl.ds(r, chunk), :] = _rmsnorm_piece(x, gain, dst_ref.dtype)
        return carry

    lax.fori_loop(0, rows // chunk, body, 0, unroll=2)


def _rmsnorm_rows_inplace(ref, gain):
    rows = ref.shape[0]
    chunk = min(NORM_ROWS, rows)
    group = min(NORM_GROUP, rows // chunk)

    def body(i, carry):
        starts = [pl.multiple_of((i * group + k) * chunk, chunk) for k in range(group)]
        scales = []
        for r in starts:
            x = ref[pl.ds(r, chunk), :]
            scales.append(lax.rsqrt(jnp.mean(x * x, axis=-1, keepdims=True) + EPS))
        for r, scale in zip(starts, scales):
            ref[pl.ds(r, chunk), :] = (ref[pl.ds(r, chunk), :] * scale) * gain
        return carry

    lax.fori_loop(0, rows // (chunk * group), body, 0)


def _cast_kernel(*refs):
    *w_refs, o_ref = refs
    for r, w_ref in enumerate(w_refs):
        o_ref[:, r * LANES:(r + 1) * LANES] = w_ref[...].T.astype(o_ref.dtype)


def _cast_main_columns(w_t):
    per_slab = W_SLAB // LANES
    in_specs = [
        pl.BlockSpec((LANES, D_MODEL),
                     lambda j, r=r: (_source_lane_block(per_slab * j + r), 0))
        for r in range(per_slab)
    ]
    return pl.pallas_call(
        _cast_kernel,
        grid=(N_MAIN // W_SLAB,),
        in_specs=in_specs,
        out_specs=pl.BlockSpec((D_MODEL, W_SLAB), lambda j: (0, j)),
        out_shape=jax.ShapeDtypeStruct((D_MODEL, N_MAIN), BF16),
        compiler_params=_compiler_params(("parallel",), 32),
        name="cast_w_in",
    )(*([w_t] * per_slab))


def _gate_dot(xn, wgate_t_ref):
    return lax.dot_general(xn, wgate_t_ref[...].astype(BF16), (((1,), (1,)), ((), ())),
                           preferred_element_type=F32)


def _inproj_kernel(x_ref, g_ref, w_ref, wgate_ref, z_ref, gcol_ref, grow_ref, xn_ref):
    @pl.when(pl.program_id(1) == 0)
    def _():
        _rmsnorm_rows(x_ref, g_ref[...], xn_ref)
        gc = _gate_dot(xn_ref[...], wgate_ref)
        gcol_ref[...] = gc
        grow_ref[...] = gc.T[0:GATE_ROWS, :]

    z_ref[...] = jnp.dot(xn_ref[...], w_ref[...], preferred_element_type=F32)


def _inproj(x2d, g1, w_bf, wgate):
    tokens = x2d.shape[0]
    tm = min(1024, tokens)
    tn = W_SLAB
    return pl.pallas_call(
        _inproj_kernel,
        grid=(tokens // tm, N_MAIN // tn),
        in_specs=[
            pl.BlockSpec((tm, D_MODEL), lambda i, j: (i, 0)),
            pl.BlockSpec((1, D_MODEL), lambda i, j: (0, 0)),
            pl.BlockSpec((D_MODEL, tn), lambda i, j: (0, j)),
            pl.BlockSpec((LANES, D_MODEL), lambda i, j: (0, 0)),
        ],
        out_specs=[
            pl.BlockSpec((tm, tn), lambda i, j: (i, j)),
            pl.BlockSpec((tm, LANES), lambda i, j: (i, 0)),
            pl.BlockSpec((GATE_ROWS, tm), lambda i, j: (0, i)),
        ],
        out_shape=[
            jax.ShapeDtypeStruct((tokens, N_MAIN), F32),
            jax.ShapeDtypeStruct((tokens, LANES), F32),
            jax.ShapeDtypeStruct((GATE_ROWS, tokens), F32),
        ],
        scratch_shapes=[pltpu.VMEM((tm, D_MODEL), BF16)],
        compiler_params=_compiler_params(("parallel", "arbitrary"), 48),
        name="inproj",
    )(x2d, g1, w_bf, wgate)


def _log_sigmoid(x):
    return jnp.minimum(x, 0.0) - jnp.log1p(jnp.exp(-jnp.abs(x)))


def _dot_exact(a, b):
    return jnp.dot(a, b, precision=lax.Precision.HIGHEST, preferred_element_type=F32)


def _intra(q_bf, k_bf, b_col, b_row, li_row, m0_col, mask):
    dmat = jnp.where(mask, b_col - b_row + li_row, -jnp.inf)
    inter = b_col + m0_col
    m_t = jnp.maximum(inter, jnp.max(dmat, axis=1, keepdims=True))
    w = jnp.exp(dmat - m_t)
    a_inter = jnp.exp(inter - m_t)
    qk = lax.dot_general(q_bf, k_bf, (((1,), (1,)), ((), ())), preferred_element_type=F32)
    return qk * w, a_inter, m_t


def _head_output(num, den, m_t, o_gate, gain):
    denom = jnp.maximum(jnp.abs(den), jnp.exp(-m_t))
    h = num * (1.0 / denom)
    hn = (h * lax.rsqrt(jnp.mean(h * h, axis=1, keepdims=True) + EPS)) * gain
    return jax.nn.sigmoid(o_gate) * hn


def _conv_norm(bg, conv, gain):
    yc = bg * conv
    return (yc * lax.rsqrt(jnp.mean(yc * yc, axis=1, keepdims=True) + EPS)) * gain


def _prompt_conv_half(z, half, convw_ref, ubuf, yc_scr):
    L = z.shape[0]
    ch = slice(half * CONV_HALF, (half + 1) * CONV_HALF)
    u = z[:, _cols_conv(CG, half)] * z[:, _cols_conv(XT, half)]
    ubuf[SUBLANES:SUBLANES + L, ch] = u
    conv = (ubuf[SUBLANES - 2:SUBLANES - 2 + L, ch] * convw_ref[0:1, ch]
            + ubuf[SUBLANES - 1:SUBLANES - 1 + L, ch] * convw_ref[1:2, ch]
            + u * convw_ref[2:3, ch])
    ubuf[0:SUBLANES, ch] = ubuf[L:L + SUBLANES, ch]
    yc = z[:, _cols_conv(BG, half)] * conv
    yc_scr[:, ch] = yc
    return jnp.sum(yc * yc, axis=1, keepdims=True)


def _prompt_conv_finish(sumsq, yc_scr, convg_ref, y_ref):
    scale = lax.rsqrt(sumsq * (1.0 / D_CONV) + EPS)
    y_ref[:, 0:D_CONV] = ((yc_scr[...] * scale) * convg_ref[...]).astype(y_ref.dtype)


def _prompt_mlstm(z, gc, gr, mlg_ref, bias_row_ref, bias_col_ref, y_ref, c_scr, n_scr, m_scr,
                  fillers):
    L = z.shape[0]
    fillers = list(fillers)

    def fill():
        if fillers:
            fillers.pop(0)()

    pre_c = gc + bias_row_ref[...]
    pre_r = gr + bias_col_ref[:, 0:1]
    row_id = lax.broadcasted_iota(jnp.int32, (L, L), 0)
    col_id = lax.broadcasted_iota(jnp.int32, (L, L), 1)
    causal = col_id <= row_id
    b_c = _dot_exact(causal.astype(F32), _log_sigmoid(pre_c))
    b_r = _dot_exact(_log_sigmoid(pre_r), (row_id <= col_id).astype(F32))

    for h in range(N_HEADS):
        li_r = pre_r[h:h + 1, :]
        li_c = pre_c[:, h:h + 1]
        bc = b_c[:, N_HEADS + h:N_HEADS + h + 1]
        br = b_r[N_HEADS + h:N_HEADS + h + 1, :]
        m0 = m_scr[h, 0:1, 0:1]
        qf = z[:, _cols_q(h)] * (D_K ** -0.5)
        q_bf = qf.astype(BF16)
        kf = z[:, _cols_k(h)]
        v_bf = z[:, _cols_v(h)].astype(BF16)
        s, a_inter, m_t = _intra(q_bf, kf.astype(BF16), bc, br, li_r, m0, causal)
        fill()
        c_old = c_scr[h]
        n_old = n_scr[h]
        num = (a_inter * jnp.dot(q_bf, c_old.astype(BF16), preferred_element_type=F32)
               + jnp.dot(s.astype(BF16), v_bf, preferred_element_type=F32))
        den = (a_inter * jnp.sum(qf * n_old, axis=1, keepdims=True)
               + jnp.sum(s, axis=1, keepdims=True))
        fill()
        y_ref[:, D_CONV + h * D_V:D_CONV + (h + 1) * D_V] = _head_output(
            num, den, m_t, z[:, _cols_o(h)],
            mlg_ref[:, h * D_V:(h + 1) * D_V]).astype(y_ref.dtype)

        b_last = br[:, L - 1:L]
        m_new = jnp.maximum(b_last + m0,
                            jnp.max(b_last - br + li_r, axis=1, keepdims=True))
        decay = jnp.exp(b_last + m0 - m_new)
        kw = kf * jnp.exp(b_last - bc + li_c - m_new)
        c_scr[h] = decay * c_old + lax.dot_general(
            kw.astype(BF16), v_bf, (((0,), (0,)), ((), ())), preferred_element_type=F32)
        n_scr[h] = decay * n_old + jnp.sum(kw, axis=0, keepdims=True)
        m_scr[h] = jnp.broadcast_to(m_new, (SUBLANES, LANES))
        fill()
    while fillers:
        fill()


def _fused_prompt_kernel(x_ref, g1_ref, w_ref, wgate_ref, convw_ref, convg_ref, mlg_ref,
                         bias_row_ref, bias_col_ref, wg_ref, wu_ref, wout_ref,
                         y_ref, convs_ref, c_out_ref, n_out_ref, m_out_ref,
                         wg_bf_ref, wu_bf_ref, wout_bf_ref,
                         xn_scr, z_scr, ubuf, yc_scr, c_scr, n_scr, m_scr):
    chunk = pl.program_id(1)
    L = x_ref.shape[0]

    @pl.when(chunk == 0)
    def _():
        ubuf[0:SUBLANES, :] = jnp.zeros((SUBLANES, D_CONV), F32)
        c_scr[...] = jnp.zeros(c_scr.shape, F32)
        n_scr[...] = jnp.zeros(n_scr.shape, F32)
        m_scr[...] = jnp.zeros(m_scr.shape, F32)

    gain = g1_ref[...]
    for r in range(0, L, NORM_ROWS):
        xn_scr[r:r + NORM_ROWS, :] = _rmsnorm_piece(x_ref[r:r + NORM_ROWS, :], gain, BF16)
    xn = xn_scr[...]
    gc = _gate_dot(xn, wgate_ref)
    gr = gc.T[0:SUBLANES, :]

    def project(j):
        cols = slice(j * W_SLAB, (j + 1) * W_SLAB)
        z_scr[:, cols] = jnp.dot(xn_scr[...], w_ref[:, cols], preferred_element_type=F32)

    piece = lambda j: functools.partial(project, j)
    skip = lambda: None
    sumsq = []

    def conv_first_half_beside(j):
        project(j)
        sumsq.append(_prompt_conv_half(z_scr, 0, convw_ref, ubuf, yc_scr))

    project(0)
    wg_bf_ref[0] = wg_ref[...].astype(BF16)
    wout_bf_ref[...] = wout_ref[...].astype(BF16)
    project(1)
    wu_bf_ref[0] = wu_ref[...].astype(BF16)
    fillers = [piece(2), piece(6), piece(7),
               piece(3), piece(4), piece(8),
               piece(5), functools.partial(conv_first_half_beside, 9), skip,
               piece(10), piece(11), skip]
    _prompt_mlstm(z_scr, gc, gr, mlg_ref, bias_row_ref, bias_col_ref, y_ref,
                  c_scr, n_scr, m_scr, fillers)
    sumsq.append(_prompt_conv_half(z_scr, 1, convw_ref, ubuf, yc_scr))
    _prompt_conv_finish(sumsq[0] + sumsq[1], yc_scr, convg_ref, y_ref)

    @pl.when(chunk == pl.num_programs(1) - 1)
    def _():
        convs_ref[0] = ubuf[SUBLANES - 2:SUBLANES, :]
        c_out_ref[0] = c_scr[...]
        for h in range(N_HEADS):
            n_out_ref[0, h:h + 1, :] = n_scr[h]
            m_out_ref[0, h:h + 1, :] = m_scr[h, 0:1, :]


def _fused_prompt(x2d, g1, w_bf, wgate, batch, seq, conv_w, conv_g, ml_g, bias_row, bias_col,
                  wg, wu, w_out):
    L = PROMPT_CHUNK
    nc = seq // L
    const = lambda b, c: (0, 0)
    wout_rows = D_MODEL // (batch * nc)
    assert wout_rows * batch * nc == D_MODEL and wout_rows % (2 * SUBLANES) == 0
    wout_piece = pl.BlockSpec((wout_rows, D_MODEL), lambda b, c: (b * nc + c, 0))
    n_pieces = D_FF // CAST_PIECE
    assert batch * nc >= n_pieces, "not enough grid steps to cast the FFN weights"
    per_tile = FFN_TILE // CAST_PIECE
    piece_of = lambda b, c: jnp.minimum(b * nc + c, n_pieces - 1)
    tiled = lambda b, c: (piece_of(b, c) // per_tile, 0, piece_of(b, c) % per_tile)
    return pl.pallas_call(
        _fused_prompt_kernel,
        grid=(batch, nc),
        in_specs=[
            pl.BlockSpec((L, D_MODEL), lambda b, c: (b * nc + c, 0)),
            pl.BlockSpec((1, D_MODEL), const),
            pl.BlockSpec((D_MODEL, N_MAIN), const, pipeline_mode=pl.Buffered(1)),
            pl.BlockSpec((LANES, D_MODEL), const),
            pl.BlockSpec((CONV_W, D_CONV), const),
            pl.BlockSpec((1, D_CONV), const),
            pl.BlockSpec((1, N_HEADS * D_V), const),
            pl.BlockSpec((1, LANES), const),
            pl.BlockSpec((SUBLANES, LANES), const),
            pl.BlockSpec((D_MODEL, CAST_PIECE), lambda b, c: (0, piece_of(b, c))),
            pl.BlockSpec((D_MODEL, CAST_PIECE), lambda b, c: (0, piece_of(b, c))),
            wout_piece,
        ],
        out_specs=[
            pl.BlockSpec((L, D_MODEL), lambda b, c: (b * nc + c, 0)),
            pl.BlockSpec((1, CONV_W - 1, D_CONV), lambda b, c: (b, 0, 0)),
            pl.BlockSpec((1, N_HEADS, D_K, D_V), lambda b, c: (b, 0, 0, 0)),
            pl.BlockSpec((1, N_HEADS, D_K), lambda b, c: (b, 0, 0)),
            pl.BlockSpec((1, N_HEADS, LANES), lambda b, c: (b, 0, 0)),
            pl.BlockSpec((1, D_MODEL, CAST_PIECE), tiled),
            pl.BlockSpec((1, D_MODEL, CAST_PIECE), tiled),
            wout_piece,
        ],
        out_shape=[
            jax.ShapeDtypeStruct((batch * seq, D_MODEL), BF16),
            jax.ShapeDtypeStruct((batch, CONV_W - 1, D_CONV), F32),
            jax.ShapeDtypeStruct((batch, N_HEADS, D_K, D_V), F32),
            jax.ShapeDtypeStruct((batch, N_HEADS, D_K), F32),
            jax.ShapeDtypeStruct((batch, N_HEADS, LANES), F32),
            jax.ShapeDtypeStruct((D_FF // FFN_TILE, D_MODEL, FFN_TILE), BF16),
            jax.ShapeDtypeStruct((D_FF // FFN_TILE, D_MODEL, FFN_TILE), BF16),
            jax.ShapeDtypeStruct((D_MODEL, D_MODEL), BF16),
        ],
        scratch_shapes=[
            pltpu.VMEM((L, D_MODEL), BF16),
            pltpu.VMEM((L, N_MAIN), F32),
            pltpu.VMEM((L + 2 * SUBLANES, D_CONV), F32),
            pltpu.VMEM((L, D_CONV), F32),
            pltpu.VMEM((N_HEADS, D_K, D_V), F32),
            pltpu.VMEM((N_HEADS, 1, D_K), F32),
            pltpu.VMEM((N_HEADS, SUBLANES, LANES), F32),
        ],
        compiler_params=_compiler_params(("arbitrary", "arbitrary"), 62),
        name="fused_prompt",
    )(x2d, g1, w_bf, wgate, conv_w, conv_g, ml_g, bias_row, bias_col, wg, wu, w_out)


def _mix_sample_kernel(z_ref, gcol_ref, grow_ref, mcol_ref, mrow_ref,
                       convs_in_ref, c_in_ref, n_in_ref,
                       convw_ref, convg_ref, mlg_ref, bias_row_ref, bias_col_ref,
                       y_ref, convs_ref, c_out_ref, n_out_ref, m_out_ref,
                       pad_scr, conv_scr, *, seq):
    L = z_ref.shape[0]
    group = L // seq
    stride = seq + SUBLANES

    for g in range(group):
        rows = slice(g * seq, (g + 1) * seq)
        u = _conv_operand(z_ref, CG, rows) * _conv_operand(z_ref, XT, rows)
        base = g * stride
        pad_scr[base + SUBLANES - 2:base + SUBLANES, :] = convs_in_ref[g]
        pad_scr[base + SUBLANES:base + SUBLANES + seq, :] = u
        conv_scr[g * seq:(g + 1) * seq, :] = (
            pad_scr[base + SUBLANES - 2:base + SUBLANES - 2 + seq, :] * convw_ref[0:1, :]
            + pad_scr[base + SUBLANES - 1:base + SUBLANES - 1 + seq, :] * convw_ref[1:2, :]
            + u * convw_ref[2:3, :])
        convs_ref[g] = pad_scr[base + seq + SUBLANES - 2:base + seq + SUBLANES, :]
    y_ref[:, 0:D_CONV] = _conv_norm(_conv_operand(z_ref, BG), conv_scr[...],
                                    convg_ref[...]).astype(y_ref.dtype)

    pre_c = gcol_ref[...] + bias_row_ref[...]
    pre_r = grow_ref[0:SUBLANES, :] + bias_col_ref[:, 0:1]
    row_id = lax.broadcasted_iota(jnp.int32, (L, L), 0)
    col_id = lax.broadcasted_iota(jnp.int32, (L, L), 1)
    same = (row_id // seq) == (col_id // seq)
    causal = same & (col_id <= row_id)
    lf_c = _log_sigmoid(pre_c)
    lf_r = _log_sigmoid(pre_r)
    b_c = _dot_exact(causal.astype(F32), lf_c)
    b_r = _dot_exact(lf_r, (same & (row_id <= col_id)).astype(F32))
    same_f = same.astype(F32)
    tot_c = _dot_exact(same_f, lf_c)
    tot_r = _dot_exact(lf_r, same_f)
    lane_seq = lax.broadcasted_iota(jnp.int32, (1, L), 1) // seq

    for h in range(N_HEADS):
        li_r = pre_r[h:h + 1, :]
        li_c = pre_c[:, h:h + 1]
        bc = b_c[:, N_HEADS + h:N_HEADS + h + 1]
        br = b_r[N_HEADS + h:N_HEADS + h + 1, :]
        bl_c = tot_c[:, N_HEADS + h:N_HEADS + h + 1]
        bl_r = tot_r[N_HEADS + h:N_HEADS + h + 1, :]
        m0_c = mcol_ref[:, h:h + 1]
        m0_r = mrow_ref[h:h + 1, :]
        qf = z_ref[:, _cols_q(h)] * (D_K ** -0.5)
        q_bf = qf.astype(BF16)
        kf = z_ref[:, _cols_k(h)]
        v_bf = z_ref[:, _cols_v(h)].astype(BF16)
        s, a_inter, m_t = _intra(q_bf, kf.astype(BF16), bc, br, li_r, m0_c, causal)

        qc_rows, qn_rows = [], []
        for g in range(group):
            rows = slice(g * seq, (g + 1) * seq)
            qc_rows.append(jnp.dot(qf[rows], c_in_ref[g, h], preferred_element_type=F32))
            qn_rows.append(jnp.sum(qf[rows] * n_in_ref[g, h:h + 1, :], axis=1, keepdims=True))
        num = (a_inter * jnp.concatenate(qc_rows, axis=0)
               + jnp.dot(s.astype(BF16), v_bf, preferred_element_type=F32))
        den = (a_inter * jnp.concatenate(qn_rows, axis=0)
               + jnp.sum(s, axis=1, keepdims=True))
        y_ref[:, D_CONV + h * D_V:D_CONV + (h + 1) * D_V] = _head_output(
            num, den, m_t, z_ref[:, _cols_o(h)],
            mlg_ref[:, h * D_V:(h + 1) * D_V]).astype(y_ref.dtype)

        g_r = bl_r - br + li_r
        gmax_c = jnp.max(jnp.where(same, g_r, -jnp.inf), axis=1, keepdims=True)
        gmax_r = jnp.max(jnp.where(same, gmax_c, -jnp.inf), axis=0, keepdims=True)
        m_new_c = jnp.maximum(bl_c + m0_c, gmax_c)
        m_new_r = jnp.maximum(bl_r + m0_r, gmax_r)
        decay_r = jnp.exp(bl_r + m0_r - m_new_r)
        kw = kf * jnp.exp(bl_c - bc + li_c - m_new_c)
        kw_t = kw.T
        for g in range(group):
            decay = decay_r[:, g * seq:g * seq + 1]
            kw_g = jnp.where(lane_seq == g, kw_t, 0.0).astype(BF16)
            c_out_ref[g, h] = decay * c_in_ref[g, h] + jnp.dot(
                kw_g, v_bf, preferred_element_type=F32)
            n_out_ref[g, h:h + 1, :] = (decay * n_in_ref[g, h:h + 1, :]
                                        + jnp.sum(kw[g * seq:(g + 1) * seq], axis=0, keepdims=True))
        m_out_ref[h:h + 1, :] = m_new_r
    m_out_ref[N_HEADS:, :] = jnp.zeros((SUBLANES - N_HEADS, L), F32)


def _mix_sample(z, gcol, grow, mcol, mrow, convs, c_state, n_state, batch, seq,
                conv_w, conv_g, ml_g, bias_row, bias_col):
    group = SAMPLE_GROUP
    L = group * seq
    const = lambda i: (0, 0)
    return pl.pallas_call(
        functools.partial(_mix_sample_kernel, seq=seq),
        grid=(batch // group,),
        in_specs=[
            pl.BlockSpec((L, N_MAIN), lambda i: (i, 0)),
            pl.BlockSpec((L, LANES), lambda i: (i, 0)),
            pl.BlockSpec((GATE_ROWS, L), lambda i: (0, i)),
            pl.BlockSpec((L, LANES), lambda i: (i, 0)),
            pl.BlockSpec((SUBLANES, L), lambda i: (0, i)),
            pl.BlockSpec((group, CONV_W - 1, D_CONV), lambda i: (i, 0, 0)),
            pl.BlockSpec((group, N_HEADS, D_K, D_V), lambda i: (i, 0, 0, 0)),
            pl.BlockSpec((group, N_HEADS, D_K), lambda i: (i, 0, 0)),
            pl.BlockSpec((CONV_W, D_CONV), const),
            pl.BlockSpec((1, D_CONV), const),
            pl.BlockSpec((1, N_HEADS * D_V), const),
            pl.BlockSpec((1, LANES), const),
            pl.BlockSpec((SUBLANES, LANES), const),
        ],
        out_specs=[
            pl.BlockSpec((L, D_MODEL), lambda i: (i, 0)),
            pl.BlockSpec((group, CONV_W - 1, D_CONV), lambda i: (i, 0, 0)),
            pl.BlockSpec((group, N_HEADS, D_K, D_V), lambda i: (i, 0, 0, 0)),
            pl.BlockSpec((group, N_HEADS, D_K), lambda i: (i, 0, 0)),
            pl.BlockSpec((SUBLANES, L), lambda i: (0, i)),
        ],
        out_shape=[
            jax.ShapeDtypeStruct((batch * seq, D_MODEL), BF16),
            jax.ShapeDtypeStruct((batch, CONV_W - 1, D_CONV), F32),
            jax.ShapeDtypeStruct((batch, N_HEADS, D_K, D_V), F32),
            jax.ShapeDtypeStruct((batch, N_HEADS, D_K), F32),
            jax.ShapeDtypeStruct((SUBLANES, batch * seq), F32),
        ],
        scratch_shapes=[
            pltpu.VMEM((group * (seq + SUBLANES), D_CONV), F32),
            pltpu.VMEM((L, D_CONV), F32),
        ],
        compiler_params=_compiler_params(("parallel",), 56),
        name="mix_sample",
    )(z, gcol, grow, mcol, mrow, convs, c_state, n_state,
      conv_w, conv_g, ml_g, bias_row, bias_col)


def _outproj_kernel(y_ref, w_ref, x_ref, o_ref):
    o_ref[...] = x_ref[...] + jnp.dot(y_ref[...], w_ref[...], preferred_element_type=F32)


def _outproj(y, w_out_bf, x2d):
    tokens = x2d.shape[0]
    tm = min(512, tokens)
    return pl.pallas_call(
        _outproj_kernel,
        grid=(tokens // tm,),
        in_specs=[
            pl.BlockSpec((tm, D_MODEL), lambda i: (i, 0)),
            pl.BlockSpec((D_MODEL, D_MODEL), lambda i: (0, 0), pipeline_mode=pl.Buffered(1)),
            pl.BlockSpec((tm, D_MODEL), lambda i: (i, 0)),
        ],
        out_specs=pl.BlockSpec((tm, D_MODEL), lambda i: (i, 0)),
        out_shape=jax.ShapeDtypeStruct((tokens, D_MODEL), F32),
        compiler_params=_compiler_params(("parallel",), 40),
        name="outproj",
    )(y, w_out_bf, x2d)


def _ffn_kernel(x_ref, g2_ref, wg_ref, wu_ref, wd_ref, gf_ref, o_ref, hn_ref, *, final_norm):
    j = pl.program_id(1)

    @pl.when(j == 0)
    def _():
        _rmsnorm_rows(x_ref, g2_ref[...], hn_ref, copy_ref=o_ref)

    hn = hn_ref[...]
    half = wd_ref.shape[0] // 2
    halves = (slice(0, half), slice(half, 2 * half))
    pre = [(jnp.dot(hn, wg_ref[0, :, cols], preferred_element_type=F32),
            jnp.dot(hn, wu_ref[0, :, cols], preferred_element_type=F32)) for cols in halves]
    down = None
    for (gate, up), rows in zip(pre, halves):
        act = ((gate * jax.nn.sigmoid(gate)) * up).astype(BF16)
        part = jnp.dot(act, wd_ref[rows, :].astype(BF16), preferred_element_type=F32)
        down = part if down is None else down + part
    o_ref[...] += down

    if final_norm:
        @pl.when(j == pl.num_programs(1) - 1)
        def _():
            _rmsnorm_rows_inplace(o_ref, gf_ref[...])


def _ffn(x1, g2, wg_bf, wu_bf, wd, gf, final_norm):
    tokens = x1.shape[0]
    tm = min(1024, tokens)
    tf = FFN_TILE
    return pl.pallas_call(
        functools.partial(_ffn_kernel, final_norm=final_norm),
        grid=(tokens // tm, D_FF // tf),
        in_specs=[
            pl.BlockSpec((tm, D_MODEL), lambda i, j: (i, 0)),
            pl.BlockSpec((1, D_MODEL), lambda i, j: (0, 0)),
            pl.BlockSpec((1, D_MODEL, tf), lambda i, j: (j, 0, 0)),
            pl.BlockSpec((1, D_MODEL, tf), lambda i, j: (j, 0, 0)),
            pl.BlockSpec((tf, D_MODEL), lambda i, j: (j, 0)),
            pl.BlockSpec((1, D_MODEL), lambda i, j: (0, 0)),
        ],
        out_specs=pl.BlockSpec((tm, D_MODEL), lambda i, j: (i, 0)),
        out_shape=jax.ShapeDtypeStruct((tokens, D_MODEL), F32),
        scratch_shapes=[pltpu.VMEM((tm, D_MODEL), BF16)],
        compiler_params=_compiler_params(("parallel", "arbitrary"), 60),
        name="ffn",
    )(x1, g2, wg_bf, wu_bf, wd, gf)


def _gate_params(w_t, b_i, b_f):
    wgate = jnp.pad(w_t[N_MAIN:, :], ((0, LANES - 2 * N_HEADS), (0, 0)))
    bias = jnp.concatenate([b_i, b_f]).astype(F32)
    bias_row = jnp.pad(bias, (0, LANES - 2 * N_HEADS))[None, :]
    bias_col = jnp.broadcast_to(bias[:, None], (SUBLANES, LANES))
    return wgate, bias_row, bias_col


def _tail(x2d, y, w_out, g2, ffn_weights, gf, final_norm):
    x1 = _outproj(y, w_out, x2d)
    return _ffn(x1, g2[None, :], *ffn_weights, gf[None, :], final_norm)


def _stack(states, k):
    if len(states) == 1:
        return states[0][k][None]
    return jnp.stack([st[k] for st in states])


def kernel(x_prompt, x_sample, state_conv, state_mlstm_C, state_mlstm_n, state_mlstm_m,
           norm1_g, w_in, b_igate, b_fgate, conv_w, conv_out_g, mlstm_out_g, w_out,
           norm2_g, w_ffn_gate, w_ffn_up, w_ffn_down, final_norm_g):
    depth = w_in.shape[0]
    bp, sp, _ = x_prompt.shape
    bs, ss, _ = x_sample.shape
    hp = x_prompt.reshape(bp * sp, D_MODEL)
    hs = x_sample.reshape(bs * ss, D_MODEL)
    p_states, s_states = [], []
    for l in range(depth):
        final_norm = l == depth - 1
        w_t = w_in[l].T
        wgate, bias_row, bias_col = _gate_params(w_t, b_igate[l], b_fgate[l])
        w_bf = _cast_main_columns(w_t)
        g1 = norm1_g[l][None, :]
        conv_g = conv_out_g[l][None, :]
        ml_g = mlstm_out_g[l][None, :]

        y, cb, c1, n1, m1, wg_bf, wu_bf, w_out_bf = _fused_prompt(
            hp, g1, w_bf, wgate, bp, sp, conv_w[l], conv_g, ml_g, bias_row, bias_col,
            w_ffn_gate[l], w_ffn_up[l], w_out[l])
        tail = (w_out_bf, norm2_g[l], (wg_bf, wu_bf, w_ffn_down[l]), final_norm_g, final_norm)
        hp = _tail(hp, y, *tail)
        p_states.append((cb, c1, n1, m1[:, :, 0]))

        z, gcol, grow = _inproj(hs, g1, w_bf, wgate)
        m_tok = jnp.repeat(state_mlstm_m[l], ss, axis=0)
        mcol = jnp.pad(m_tok, ((0, 0), (0, LANES - N_HEADS)))
        mrow = jnp.pad(m_tok.T, ((0, SUBLANES - N_HEADS), (0, 0)))
        y, cb, c1, n1, m_row = _mix_sample(z, gcol, grow, mcol, mrow, state_conv[l],
                                           state_mlstm_C[l], state_mlstm_n[l], bs, ss,
                                           conv_w[l], conv_g, ml_g, bias_row, bias_col)
        hs = _tail(hs, y, *tail)
        s_states.append((cb, c1, n1, m_row[:N_HEADS, ::ss].T))

    return (hp.reshape(bp, sp, D_MODEL), hs.reshape(bs, ss, D_MODEL),
            _stack(p_states, 0), _stack(p_states, 1), _stack(p_states, 2), _stack(p_states, 3),
            _stack(s_states, 0), _stack(s_states, 1), _stack(s_states, 2), _stack(s_states, 3))
```

```python
import functools

import jax
import jax.numpy as jnp
from jax import lax
from jax.experimental import pallas as pl
from jax.experimental.pallas import tpu as pltpu

F32 = jnp.float32
BF16 = jnp.bfloat16

D_MODEL = 2048
D_CONV = 1024
CONV_W = 3
N_HEADS = 4
D_K = 128
D_V = 256
D_FF = 5632
EPS = 1e-6

LANES = 128
SUBLANES = 8
GATE_ROWS = 16
MIB = 1024 * 1024

PROMPT_CHUNK = 256
SAMPLE_GROUP = 16
NORM_ROWS = 64
NORM_GROUP = 4
W_SLAB = 512
FFN_TILE = 512
CAST_PIECE = 256

VMEM_MIB = {"cast_w_in": 32, "inproj": 48, "fused_prompt": 62, "mix_sample": 56,
            "outproj": 40, "ffn": 60}

SRC_Q = 3 * D_CONV
SRC_K = SRC_Q + N_HEADS * D_K
SRC_V = SRC_K + N_HEADS * D_K
SRC_O = SRC_V + N_HEADS * D_V
N_MAIN = SRC_O + N_HEADS * D_V

HEAD_COLS = 2 * D_K + 2 * D_V
N_MLSTM = N_HEADS * HEAD_COLS
CONV_HALF = D_CONV // 2
BG, CG, XT = 0, 1, 2


def _cols_conv(kind, half):
    start = N_MLSTM + (3 * half + kind) * CONV_HALF
    return slice(start, start + CONV_HALF)


def _conv_operand(z, kind, rows=slice(None)):
    return jnp.concatenate([z[rows, _cols_conv(kind, 0)], z[rows, _cols_conv(kind, 1)]], axis=1)


def _cols_q(h):
    return slice(h * HEAD_COLS, h * HEAD_COLS + D_K)


def _cols_k(h):
    return slice(h * HEAD_COLS + D_K, h * HEAD_COLS + 2 * D_K)


def _cols_v(h):
    return slice(h * HEAD_COLS + 2 * D_K, h * HEAD_COLS + 2 * D_K + D_V)


def _cols_o(h):
    return slice(h * HEAD_COLS + 2 * D_K + D_V, (h + 1) * HEAD_COLS)


def _source_lane_block(j):
    per_head = HEAD_COLS // LANES
    h, r = j // per_head, j % per_head
    v_blocks = D_V // LANES
    mlstm_src = jnp.where(
        r == 0, SRC_Q // LANES + h,
        jnp.where(r == 1, SRC_K // LANES + h,
                  jnp.where(r < 2 + v_blocks,
                            SRC_V // LANES + v_blocks * h + (r - 2),
                            SRC_O // LANES + v_blocks * h + (r - 2 - v_blocks))))
    per_kind = CONV_HALF // LANES
    c = j - N_MLSTM // LANES
    half, r = c // (3 * per_kind), c % (3 * per_kind)
    conv_src = (r // per_kind) * (D_CONV // LANES) + half * per_kind + r % per_kind
    return jnp.where(j < N_MLSTM // LANES, mlstm_src, conv_src)


def _compiler_params(name, semantics):
    return pltpu.CompilerParams(dimension_semantics=semantics,
                                vmem_limit_bytes=VMEM_MIB[name] * MIB)


def _rmsnorm_piece(x, gain, dtype):
    ms = jnp.mean(x * x, axis=-1, keepdims=True)
    return ((x * lax.rsqrt(ms + EPS)) * gain).astype(dtype)


def _rmsnorm_rows(src_ref, gain, dst_ref, copy_ref=None):
    rows = src_ref.shape[0]
    chunk = min(NORM_ROWS, rows)

    def body(i, carry):
        r = pl.multiple_of(i * chunk, chunk)
        x = src_ref[pl.ds(r, chunk), :].astype(F32)
        if copy_ref is not None:
            copy_ref[pl.ds(r, chunk), :] = x
        dst_ref[pl.ds(r, chunk), :] = _rmsnorm_piece(x, gain, dst_ref.dtype)
        return carry

    lax.fori_loop(0, rows // chunk, body, 0, unroll=2)


def _rmsnorm_rows_inplace(ref, gain):
    rows = ref.shape[0]
    chunk = min(NORM_ROWS, rows)
    group = min(NORM_GROUP, rows // chunk)

    def body(i, carry):
        starts = [pl.multiple_of((i * group + k) * chunk, chunk) for k in range(group)]
        scales = []
        for r in starts:
            x = ref[pl.ds(r, chunk), :]
            scales.append(lax.rsqrt(jnp.mean(x * x, axis=-1, keepdims=True) + EPS))
        for r, scale in zip(starts, scales):
            ref[pl.ds(r, chunk), :] = (ref[pl.ds(r, chunk), :] * scale) * gain
        return carry

    lax.fori_loop(0, rows // (chunk * group), body, 0)


def _cast_kernel(*refs):
    *w_refs, o_ref = refs
    for r, w_ref in enumerate(w_refs):
        o_ref[:, r * LANES:(r + 1) * LANES] = w_ref[...].T.astype(o_ref.dtype)


def _cast_main_columns(w_t):
    per_slab = W_SLAB // LANES
    in_specs = [
        pl.BlockSpec((LANES, D_MODEL),
                     lambda j, r=r: (_source_lane_block(per_slab * j + r), 0))
        for r in range(per_slab)
    ]
    return pl.pallas_call(
        _cast_kernel,
        grid=(N_MAIN // W_SLAB,),
        in_specs=in_specs,
        out_specs=pl.BlockSpec((D_MODEL, W_SLAB), lambda j: (0, j)),
        out_shape=jax.ShapeDtypeStruct((D_MODEL, N_MAIN), BF16),
        compiler_params=_compiler_params("cast_w_in", ("parallel",)),
        name="cast_w_in",
    )(*([w_t] * per_slab))


def _gate_dot(xn, wgate_t_ref):
    return lax.dot_general(xn, wgate_t_ref[...].astype(BF16), (((1,), (1,)), ((), ())),
                           preferred_element_type=F32)


def _inproj_kernel(x_ref, g_ref, w_ref, wgate_ref, z_ref, gcol_ref, grow_ref, xn_ref):
    @pl.when(pl.program_id(1) == 0)
    def _():
        _rmsnorm_rows(x_ref, g_ref[...], xn_ref)
        gc = _gate_dot(xn_ref[...], wgate_ref)
        gcol_ref[...] = gc
        grow_ref[...] = gc.T[0:GATE_ROWS, :]

    z_ref[...] = jnp.dot(xn_ref[...], w_ref[...], preferred_element_type=F32)


def _inproj(x2d, g1, w_bf, wgate):
    tokens = x2d.shape[0]
    tm = min(1024, tokens)
    tn = W_SLAB
    return pl.pallas_call(
        _inproj_kernel,
        grid=(tokens // tm, N_MAIN // tn),
        in_specs=[
            pl.BlockSpec((tm, D_MODEL), lambda i, j: (i, 0)),
            pl.BlockSpec((1, D_MODEL), lambda i, j: (0, 0)),
            pl.BlockSpec((D_MODEL, tn), lambda i, j: (0, j)),
            pl.BlockSpec((LANES, D_MODEL), lambda i, j: (0, 0)),
        ],
        out_specs=[
            pl.BlockSpec((tm, tn), lambda i, j: (i, j)),
            pl.BlockSpec((tm, LANES), lambda i, j: (i, 0)),
            pl.BlockSpec((GATE_ROWS, tm), lambda i, j: (0, i)),
        ],
        out_shape=[
            jax.ShapeDtypeStruct((tokens, N_MAIN), F32),
            jax.ShapeDtypeStruct((tokens, LANES), F32),
            jax.ShapeDtypeStruct((GATE_ROWS, tokens), F32),
        ],
        scratch_shapes=[pltpu.VMEM((tm, D_MODEL), BF16)],
        compiler_params=_compiler_params("inproj", ("parallel", "arbitrary")),
        name="inproj",
    )(x2d, g1, w_bf, wgate)


def _log_sigmoid(x):
    return jnp.minimum(x, 0.0) - jnp.log1p(jnp.exp(-jnp.abs(x)))


def _split3(x):
    hi = x.astype(BF16).astype(F32)
    mid = (x - hi).astype(BF16).astype(F32)
    lo = ((x - hi) - mid).astype(BF16).astype(F32)
    return hi, mid, lo


def _masked_sums(mask, lf_col, lf_row):
    m_bf = mask.astype(BF16)
    n, k = lf_col.shape[1], lf_row.shape[0]
    col_terms = jnp.concatenate(_split3(lf_col), axis=1).astype(BF16)
    row_terms = jnp.concatenate(_split3(lf_row) + (jnp.zeros_like(lf_row),),
                                axis=0).astype(BF16)
    cols = jnp.dot(m_bf, col_terms, preferred_element_type=F32)
    rows = lax.dot_general(row_terms, m_bf, (((1,), (1,)), ((), ())),
                           preferred_element_type=F32)
    return (cols[:, 0:n] + cols[:, n:2 * n] + cols[:, 2 * n:3 * n],
            rows[0:k] + rows[k:2 * k] + rows[2 * k:3 * k])


def _intra(q_bf, k_bf, b_col, b_row, li_row, m0_col, mask):
    dmat = jnp.where(mask, b_col - b_row + li_row, -jnp.inf)
    inter = b_col + m0_col
    m_t = jnp.maximum(inter, jnp.max(dmat, axis=1, keepdims=True))
    w = jnp.exp(dmat - m_t)
    a_inter = jnp.exp(inter - m_t)
    qk = lax.dot_general(q_bf, k_bf, (((1,), (1,)), ((), ())), preferred_element_type=F32)
    return qk * w, a_inter, m_t


def _head_output(num, den, m_t, o_gate, gain):
    denom = jnp.maximum(jnp.abs(den), jnp.exp(-m_t))
    h = num * (1.0 / denom)
    hn = (h * lax.rsqrt(jnp.mean(h * h, axis=1, keepdims=True) + EPS)) * gain
    return jax.nn.sigmoid(o_gate) * hn


def _conv_norm(bg, conv, gain):
    yc = bg * conv
    return (yc * lax.rsqrt(jnp.mean(yc * yc, axis=1, keepdims=True) + EPS)) * gain


def _prompt_conv_half(z, half, convw_ref, ubuf, yc_scr):
    L = z.shape[0]
    ch = slice(half * CONV_HALF, (half + 1) * CONV_HALF)
    u = z[:, _cols_conv(CG, half)] * z[:, _cols_conv(XT, half)]
    ubuf[SUBLANES:SUBLANES + L, ch] = u
    conv = (ubuf[SUBLANES - 2:SUBLANES - 2 + L, ch] * convw_ref[0:1, ch]
            + ubuf[SUBLANES - 1:SUBLANES - 1 + L, ch] * convw_ref[1:2, ch]
            + u * convw_ref[2:3, ch])
    ubuf[0:SUBLANES, ch] = ubuf[L:L + SUBLANES, ch]
    yc = z[:, _cols_conv(BG, half)] * conv
    yc_scr[:, ch] = yc
    return jnp.sum(yc * yc, axis=1, keepdims=True)


def _prompt_conv_finish(sumsq, yc_scr, convg_ref, y_ref):
    scale = lax.rsqrt(sumsq * (1.0 / D_CONV) + EPS)
    y_ref[:, 0:D_CONV] = ((yc_scr[...] * scale) * convg_ref[...]).astype(y_ref.dtype)


def _prompt_mlstm(z, gc, gr, mlg_ref, bias_row_ref, bias_col_ref, y_ref, c_scr, n_scr, m_scr,
                  fillers):
    L = z.shape[0]
    fillers = list(fillers)

    def fill():
        if fillers:
            fillers.pop(0)()

    pre_c = gc + bias_row_ref[...]
    pre_r = gr + bias_col_ref[:, 0:1]
    row_id = lax.broadcasted_iota(jnp.int32, (L, L), 0)
    col_id = lax.broadcasted_iota(jnp.int32, (L, L), 1)
    causal = col_id <= row_id
    b_c, b_r = _masked_sums(causal, _log_sigmoid(pre_c), _log_sigmoid(pre_r))

    for h in range(N_HEADS):
        li_r = pre_r[h:h + 1, :]
        li_c = pre_c[:, h:h + 1]
        bc = b_c[:, N_HEADS + h:N_HEADS + h + 1]
        br = b_r[N_HEADS + h:N_HEADS + h + 1, :]
        m0 = m_scr[h, 0:1, 0:1]
        qf = z[:, _cols_q(h)] * (D_K ** -0.5)
        q_bf = qf.astype(BF16)
        kf = z[:, _cols_k(h)]
        v_bf = z[:, _cols_v(h)].astype(BF16)
        s, a_inter, m_t = _intra(q_bf, kf.astype(BF16), bc, br, li_r, m0, causal)
        fill()
        c_old = c_scr[h]
        n_old = n_scr[h]
        num = (a_inter * jnp.dot(q_bf, c_old.astype(BF16), preferred_element_type=F32)
               + jnp.dot(s.astype(BF16), v_bf, preferred_element_type=F32))
        den = (a_inter * jnp.sum(qf * n_old, axis=1, keepdims=True)
               + jnp.sum(s, axis=1, keepdims=True))
        fill()
        y_ref[:, D_CONV + h * D_V:D_CONV + (h + 1) * D_V] = _head_output(
            num, den, m_t, z[:, _cols_o(h)],
            mlg_ref[:, h * D_V:(h + 1) * D_V]).astype(y_ref.dtype)

        b_last = br[:, L - 1:L]
        m_new = jnp.maximum(b_last + m0,
                            jnp.max(b_last - br + li_r, axis=1, keepdims=True))
        decay = jnp.exp(b_last + m0 - m_new)
        kw = kf * jnp.exp(b_last - bc + li_c - m_new)
        c_scr[h] = decay * c_old + lax.dot_general(
            kw.astype(BF16), v_bf, (((0,), (0,)), ((), ())), preferred_element_type=F32)
        n_scr[h] = decay * n_old + jnp.sum(kw, axis=0, keepdims=True)
        m_scr[h] = jnp.broadcast_to(m_new, (SUBLANES, LANES))
        fill()
    while fillers:
        fill()


def _fused_prompt_kernel(x_ref, g1_ref, w_ref, wgate_ref, convw_ref, convg_ref, mlg_ref,
                         bias_row_ref, bias_col_ref, wg_ref, wu_ref, wout_ref,
                         y_ref, convs_ref, c_out_ref, n_out_ref, m_out_ref,
                         wg_bf_ref, wu_bf_ref, wout_bf_ref,
                         xn_scr, z_scr, ubuf, yc_scr, c_scr, n_scr, m_scr):
    chunk = pl.program_id(1)
    L = x_ref.shape[0]

    @pl.when(chunk == 0)
    def _():
        ubuf[0:SUBLANES, :] = jnp.zeros((SUBLANES, D_CONV), F32)
        c_scr[...] = jnp.zeros(c_scr.shape, F32)
        n_scr[...] = jnp.zeros(n_scr.shape, F32)
        m_scr[...] = jnp.zeros(m_scr.shape, F32)

    gain = g1_ref[...]
    for r in range(0, L, NORM_ROWS):
        xn_scr[r:r + NORM_ROWS, :] = _rmsnorm_piece(x_ref[r:r + NORM_ROWS, :], gain, BF16)
    xn = xn_scr[...]
    gc = _gate_dot(xn, wgate_ref)
    gr = gc.T[0:SUBLANES, :]

    def project(j):
        cols = slice(j * W_SLAB, (j + 1) * W_SLAB)
        z_scr[:, cols] = jnp.dot(xn_scr[...], w_ref[:, cols], preferred_element_type=F32)

    piece = lambda j: functools.partial(project, j)
    skip = lambda: None
    sumsq = []

    def conv_first_half_beside(j):
        project(j)
        sumsq.append(_prompt_conv_half(z_scr, 0, convw_ref, ubuf, yc_scr))

    project(0)
    wg_bf_ref[0] = wg_ref[...].astype(BF16)
    wout_bf_ref[...] = wout_ref[...].astype(BF16)
    project(1)
    wu_bf_ref[0] = wu_ref[...].astype(BF16)
    fillers = [piece(2), piece(6), piece(7),
               piece(3), piece(4), piece(8),
               piece(5), functools.partial(conv_first_half_beside, 9), skip,
               piece(10), piece(11), skip]
    _prompt_mlstm(z_scr, gc, gr, mlg_ref, bias_row_ref, bias_col_ref, y_ref,
                  c_scr, n_scr, m_scr, fillers)
    sumsq.append(_prompt_conv_half(z_scr, 1, convw_ref, ubuf, yc_scr))
    _prompt_conv_finish(sumsq[0] + sumsq[1], yc_scr, convg_ref, y_ref)

    @pl.when(chunk == pl.num_programs(1) - 1)
    def _():
        convs_ref[0] = ubuf[SUBLANES - 2:SUBLANES, :]
        c_out_ref[0] = c_scr[...]
        for h in range(N_HEADS):
            n_out_ref[0, h:h + 1, :] = n_scr[h]
            m_out_ref[0, h:h + 1, :] = m_scr[h, 0:1, :]


def _fused_prompt(x2d, g1, w_bf, wgate, batch, seq, conv_w, conv_g, ml_g, bias_row, bias_col,
                  wg, wu, w_out):
    L = PROMPT_CHUNK
    nc = seq // L
    const = lambda b, c: (0, 0)
    wout_rows = D_MODEL // (batch * nc)
    assert wout_rows * batch * nc == D_MODEL and wout_rows % (2 * SUBLANES) == 0
    wout_piece = pl.BlockSpec((wout_rows, D_MODEL), lambda b, c: (b * nc + c, 0))
    n_pieces = D_FF // CAST_PIECE
    assert batch * nc >= n_pieces, "not enough grid steps to cast the FFN weights"
    per_tile = FFN_TILE // CAST_PIECE
    piece_of = lambda b, c: jnp.minimum(b * nc + c, n_pieces - 1)
    tiled = lambda b, c: (piece_of(b, c) // per_tile, 0, piece_of(b, c) % per_tile)
    return pl.pallas_call(
        _fused_prompt_kernel,
        grid=(batch, nc),
        in_specs=[
            pl.BlockSpec((L, D_MODEL), lambda b, c: (b * nc + c, 0)),
            pl.BlockSpec((1, D_MODEL), const),
            pl.BlockSpec((D_MODEL, N_MAIN), const, pipeline_mode=pl.Buffered(1)),
            pl.BlockSpec((LANES, D_MODEL), const),
            pl.BlockSpec((CONV_W, D_CONV), const),
            pl.BlockSpec((1, D_CONV), const),
            pl.BlockSpec((1, N_HEADS * D_V), const),
            pl.BlockSpec((1, LANES), const),
            pl.BlockSpec((SUBLANES, LANES), const),
            pl.BlockSpec((D_MODEL, CAST_PIECE), lambda b, c: (0, piece_of(b, c))),
            pl.BlockSpec((D_MODEL, CAST_PIECE), lambda b, c: (0, piece_of(b, c))),
            wout_piece,
        ],
        out_specs=[
            pl.BlockSpec((L, D_MODEL), lambda b, c: (b * nc + c, 0)),
            pl.BlockSpec((1, CONV_W - 1, D_CONV), lambda b, c: (b, 0, 0)),
            pl.BlockSpec((1, N_HEADS, D_K, D_V), lambda b, c: (b, 0, 0, 0)),
            pl.BlockSpec((1, N_HEADS, D_K), lambda b, c: (b, 0, 0)),
            pl.BlockSpec((1, N_HEADS, LANES), lambda b, c: (b, 0, 0)),
            pl.BlockSpec((1, D_MODEL, CAST_PIECE), tiled),
            pl.BlockSpec((1, D_MODEL, CAST_PIECE), tiled),
            wout_piece,
        ],
        out_shape=[
            jax.ShapeDtypeStruct((batch * seq, D_MODEL), BF16),
            jax.ShapeDtypeStruct((batch, CONV_W - 1, D_CONV), F32),
            jax.ShapeDtypeStruct((batch, N_HEADS, D_K, D_V), F32),
            jax.ShapeDtypeStruct((batch, N_HEADS, D_K), F32),
            jax.ShapeDtypeStruct((batch, N_HEADS, LANES), F32),
            jax.ShapeDtypeStruct((D_FF // FFN_TILE, D_MODEL, FFN_TILE), BF16),
            jax.ShapeDtypeStruct((D_FF // FFN_TILE, D_MODEL, FFN_TILE), BF16),
            jax.ShapeDtypeStruct((D_MODEL, D_MODEL), BF16),
        ],
        scratch_shapes=[
            pltpu.VMEM((L, D_MODEL), BF16),
            pltpu.VMEM((L, N_MAIN), F32),
            pltpu.VMEM((L + 2 * SUBLANES, D_CONV), F32),
            pltpu.VMEM((L, D_CONV), F32),
            pltpu.VMEM((N_HEADS, D_K, D_V), F32),
            pltpu.VMEM((N_HEADS, 1, D_K), F32),
            pltpu.VMEM((N_HEADS, SUBLANES, LANES), F32),
        ],
        compiler_params=_compiler_params("fused_prompt", ("arbitrary", "arbitrary")),
        name="fused_prompt",
    )(x2d, g1, w_bf, wgate, conv_w, conv_g, ml_g, bias_row, bias_col, wg, wu, w_out)


def _mix_sample_kernel(z_ref, gcol_ref, grow_ref, mcol_ref, mrow_ref,
                       convs_in_ref, c_in_ref, n_in_ref,
                       convw_ref, convg_ref, mlg_ref, bias_row_ref, bias_col_ref,
                       y_ref, convs_ref, c_out_ref, n_out_ref, m_out_ref,
                       pad_scr, conv_scr, *, seq):
    L = z_ref.shape[0]
    group = L // seq
    stride = seq + SUBLANES

    for g in range(group):
        rows = slice(g * seq, (g + 1) * seq)
        u = _conv_operand(z_ref, CG, rows) * _conv_operand(z_ref, XT, rows)
        base = g * stride
        pad_scr[base + SUBLANES - 2:base + SUBLANES, :] = convs_in_ref[g]
        pad_scr[base + SUBLANES:base + SUBLANES + seq, :] = u
        conv_scr[g * seq:(g + 1) * seq, :] = (
            pad_scr[base + SUBLANES - 2:base + SUBLANES - 2 + seq, :] * convw_ref[0:1, :]
            + pad_scr[base + SUBLANES - 1:base + SUBLANES - 1 + seq, :] * convw_ref[1:2, :]
            + u * convw_ref[2:3, :])
        convs_ref[g] = pad_scr[base + seq + SUBLANES - 2:base + seq + SUBLANES, :]
    y_ref[:, 0:D_CONV] = _conv_norm(_conv_operand(z_ref, BG), conv_scr[...],
                                    convg_ref[...]).astype(y_ref.dtype)

    pre_c = gcol_ref[...] + bias_row_ref[...]
    pre_r = grow_ref[0:SUBLANES, :] + bias_col_ref[:, 0:1]
    row_id = lax.broadcasted_iota(jnp.int32, (L, L), 0)
    col_id = lax.broadcasted_iota(jnp.int32, (L, L), 1)
    same = (row_id // seq) == (col_id // seq)
    causal = same & (col_id <= row_id)
    lf_c = _log_sigmoid(pre_c)
    lf_r = _log_sigmoid(pre_r)
    b_c, b_r = _masked_sums(causal, lf_c, lf_r)
    tot_c, tot_r = _masked_sums(same, lf_c, lf_r)
    lane_seq = lax.broadcasted_iota(jnp.int32, (1, L), 1) // seq

    for h in range(N_HEADS):
        li_r = pre_r[h:h + 1, :]
        li_c = pre_c[:, h:h + 1]
        bc = b_c[:, N_HEADS + h:N_HEADS + h + 1]
        br = b_r[N_HEADS + h:N_HEADS + h + 1, :]
        bl_c = tot_c[:, N_HEADS + h:N_HEADS + h + 1]
        bl_r = tot_r[N_HEADS + h:N_HEADS + h + 1, :]
        m0_c = mcol_ref[:, h:h + 1]
        m0_r = mrow_ref[h:h + 1, :]
        qf = z_ref[:, _cols_q(h)] * (D_K ** -0.5)
        q_bf = qf.astype(BF16)
        kf = z_ref[:, _cols_k(h)]
        v_bf = z_ref[:, _cols_v(h)].astype(BF16)
        s, a_inter, m_t = _intra(q_bf, kf.astype(BF16), bc, br, li_r, m0_c, causal)

        qc_rows, qn_rows = [], []
        for g in range(group):
            rows = slice(g * seq, (g + 1) * seq)
            qc_rows.append(jnp.dot(qf[rows], c_in_ref[g, h], preferred_element_type=F32))
            qn_rows.append(jnp.sum(qf[rows] * n_in_ref[g, h:h + 1, :], axis=1, keepdims=True))
        num = (a_inter * jnp.concatenate(qc_rows, axis=0)
               + jnp.dot(s.astype(BF16), v_bf, preferred_element_type=F32))
        den = (a_inter * jnp.concatenate(qn_rows, axis=0)
               + jnp.sum(s, axis=1, keepdims=True))
        y_ref[:, D_CONV + h * D_V:D_CONV + (h + 1) * D_V] = _head_output(
            num, den, m_t, z_ref[:, _cols_o(h)],
            mlg_ref[:, h * D_V:(h + 1) * D_V]).astype(y_ref.dtype)

        g_r = bl_r - br + li_r
        gmax_c = jnp.max(jnp.where(same, g_r, -jnp.inf), axis=1, keepdims=True)
        gmax_r = jnp.max(jnp.where(same, gmax_c, -jnp.inf), axis=0, keepdims=True)
        m_new_c = jnp.maximum(bl_c + m0_c, gmax_c)
        m_new_r = jnp.maximum(bl_r + m0_r, gmax_r)
        decay_r = jnp.exp(bl_r + m0_r - m_new_r)
        kw = kf * jnp.exp(bl_c - bc + li_c - m_new_c)
        kw_t = kw.T
        for g in range(group):
            decay = decay_r[:, g * seq:g * seq + 1]
            kw_g = jnp.where(lane_seq == g, kw_t, 0.0).astype(BF16)
            c_out_ref[g, h] = decay * c_in_ref[g, h] + jnp.dot(
                kw_g, v_bf, preferred_element_type=F32)
            n_out_ref[g, h:h + 1, :] = (decay * n_in_ref[g, h:h + 1, :]
                                        + jnp.sum(kw[g * seq:(g + 1) * seq], axis=0, keepdims=True))
        m_out_ref[h:h + 1, :] = m_new_r
    m_out_ref[N_HEADS:, :] = jnp.zeros((SUBLANES - N_HEADS, L), F32)


def _mix_sample(z, gcol, grow, mcol, mrow, convs, c_state, n_state, batch, seq,
                conv_w, conv_g, ml_g, bias_row, bias_col):
    group = SAMPLE_GROUP
    L = group * seq
    const = lambda i: (0, 0)
    return pl.pallas_call(
        functools.partial(_mix_sample_kernel, seq=seq),
        grid=(batch // group,),
        in_specs=[
            pl.BlockSpec((L, N_MAIN), lambda i: (i, 0)),
            pl.BlockSpec((L, LANES), lambda i: (i, 0)),
            pl.BlockSpec((GATE_ROWS, L), lambda i: (0, i)),
            pl.BlockSpec((L, LANES), lambda i: (i, 0)),
            pl.BlockSpec((SUBLANES, L), lambda i: (0, i)),
            pl.BlockSpec((group, CONV_W - 1, D_CONV), lambda i: (i, 0, 0)),
            pl.BlockSpec((group, N_HEADS, D_K, D_V), lambda i: (i, 0, 0, 0)),
            pl.BlockSpec((group, N_HEADS, D_K), lambda i: (i, 0, 0)),
            pl.BlockSpec((CONV_W, D_CONV), const),
            pl.BlockSpec((1, D_CONV), const),
            pl.BlockSpec((1, N_HEADS * D_V), const),
            pl.BlockSpec((1, LANES), const),
            pl.BlockSpec((SUBLANES, LANES), const),
        ],
        out_specs=[
            pl.BlockSpec((L, D_MODEL), lambda i: (i, 0)),
            pl.BlockSpec((group, CONV_W - 1, D_CONV), lambda i: (i, 0, 0)),
            pl.BlockSpec((group, N_HEADS, D_K, D_V), lambda i: (i, 0, 0, 0)),
            pl.BlockSpec((group, N_HEADS, D_K), lambda i: (i, 0, 0)),
            pl.BlockSpec((SUBLANES, L), lambda i: (0, i)),
        ],
        out_shape=[
            jax.ShapeDtypeStruct((batch * seq, D_MODEL), BF16),
            jax.ShapeDtypeStruct((batch, CONV_W - 1, D_CONV), F32),
            jax.ShapeDtypeStruct((batch, N_HEADS, D_K, D_V), F32),
            jax.ShapeDtypeStruct((batch, N_HEADS, D_K), F32),
            jax.ShapeDtypeStruct((SUBLANES, batch * seq), F32),
        ],
        scratch_shapes=[
            pltpu.VMEM((group * (seq + SUBLANES), D_CONV), F32),
            pltpu.VMEM((L, D_CONV), F32),
        ],
        compiler_params=_compiler_params("mix_sample", ("parallel",)),
        name="mix_sample",
    )(z, gcol, grow, mcol, mrow, convs, c_state, n_state,
      conv_w, conv_g, ml_g, bias_row, bias_col)


def _outproj_kernel(y_ref, w_ref, x_ref, o_ref):
    o_ref[...] = x_ref[...] + jnp.dot(y_ref[...], w_ref[...], preferred_element_type=F32)


def _outproj(y, w_out_bf, x2d):
    tokens = x2d.shape[0]
    tm = min(512, tokens)
    return pl.pallas_call(
        _outproj_kernel,
        grid=(tokens // tm,),
        in_specs=[
            pl.BlockSpec((tm, D_MODEL), lambda i: (i, 0)),
            pl.BlockSpec((D_MODEL, D_MODEL), lambda i: (0, 0), pipeline_mode=pl.Buffered(1)),
            pl.BlockSpec((tm, D_MODEL), lambda i: (i, 0)),
        ],
        out_specs=pl.BlockSpec((tm, D_MODEL), lambda i: (i, 0)),
        out_shape=jax.ShapeDtypeStruct((tokens, D_MODEL), F32),
        compiler_params=_compiler_params("outproj", ("parallel",)),
        name="outproj",
    )(y, w_out_bf, x2d)


def _ffn_kernel(x_ref, g2_ref, wg_ref, wu_ref, wd_ref, gf_ref, o_ref, hn_ref, *, final_norm):
    j = pl.program_id(1)

    @pl.when(j == 0)
    def _():
        _rmsnorm_rows(x_ref, g2_ref[...], hn_ref, copy_ref=o_ref)

    hn = hn_ref[...]
    half = wd_ref.shape[0] // 2
    halves = (slice(0, half), slice(half, 2 * half))
    pre = [(jnp.dot(hn, wg_ref[0, :, cols], preferred_element_type=F32),
            jnp.dot(hn, wu_ref[0, :, cols], preferred_element_type=F32)) for cols in halves]
    down = None
    for (gate, up), rows in zip(pre, halves):
        act = ((gate * jax.nn.sigmoid(gate)) * up).astype(BF16)
        part = jnp.dot(act, wd_ref[rows, :].astype(BF16), preferred_element_type=F32)
        down = part if down is None else down + part
    o_ref[...] += down

    if final_norm:
        @pl.when(j == pl.num_programs(1) - 1)
        def _():
            _rmsnorm_rows_inplace(o_ref, gf_ref[...])


def _ffn(x1, g2, wg_bf, wu_bf, wd, gf, final_norm):
    tokens = x1.shape[0]
    tm = min(1024, tokens)
    tf = FFN_TILE
    return pl.pallas_call(
        functools.partial(_ffn_kernel, final_norm=final_norm),
        grid=(tokens // tm, D_FF // tf),
        in_specs=[
            pl.BlockSpec((tm, D_MODEL), lambda i, j: (i, 0)),
            pl.BlockSpec((1, D_MODEL), lambda i, j: (0, 0)),
            pl.BlockSpec((1, D_MODEL, tf), lambda i, j: (j, 0, 0)),
            pl.BlockSpec((1, D_MODEL, tf), lambda i, j: (j, 0, 0)),
            pl.BlockSpec((tf, D_MODEL), lambda i, j: (j, 0)),
            pl.BlockSpec((1, D_MODEL), lambda i, j: (0, 0)),
        ],
        out_specs=pl.BlockSpec((tm, D_MODEL), lambda i, j: (i, 0)),
        out_shape=jax.ShapeDtypeStruct((tokens, D_MODEL), F32),
        scratch_shapes=[pltpu.VMEM((tm, D_MODEL), BF16)],
        compiler_params=_compiler_params("ffn", ("parallel", "arbitrary")),
        name="ffn",
    )(x1, g2, wg_bf, wu_bf, wd, gf)


def _gate_params(w_t, b_i, b_f):
    wgate = jnp.pad(w_t[N_MAIN:, :], ((0, LANES - 2 * N_HEADS), (0, 0)))
    bias = jnp.concatenate([b_i, b_f]).astype(F32)
    bias_row = jnp.pad(bias, (0, LANES - 2 * N_HEADS))[None, :]
    bias_col = jnp.broadcast_to(bias[:, None], (SUBLANES, LANES))
    return wgate, bias_row, bias_col


def _tail(x2d, y, w_out, g2, ffn_weights, gf, final_norm):
    x1 = _outproj(y, w_out, x2d)
    return _ffn(x1, g2[None, :], *ffn_weights, gf[None, :], final_norm)


def _stack(states, k):
    if len(states) == 1:
        return states[0][k][None]
    return jnp.stack([st[k] for st in states])


def kernel(x_prompt, x_sample, state_conv, state_mlstm_C, state_mlstm_n, state_mlstm_m,
           norm1_g, w_in, b_igate, b_fgate, conv_w, conv_out_g, mlstm_out_g, w_out,
           norm2_g, w_ffn_gate, w_ffn_up, w_ffn_down, final_norm_g):
    depth = w_in.shape[0]
    bp, sp, _ = x_prompt.shape
    bs, ss, _ = x_sample.shape
    hp = x_prompt.reshape(bp * sp, D_MODEL)
    hs = x_sample.reshape(bs * ss, D_MODEL)
    p_states, s_states = [], []
    for l in range(depth):
        final_norm = l == depth - 1
        w_t = w_in[l].T
        wgate, bias_row, bias_col = _gate_params(w_t, b_igate[l], b_fgate[l])
        w_bf = _cast_main_columns(w_t)
        g1 = norm1_g[l][None, :]
        conv_g = conv_out_g[l][None, :]
        ml_g = mlstm_out_g[l][None, :]

        y, cb, c1, n1, m1, wg_bf, wu_bf, w_out_bf = _fused_prompt(
            hp, g1, w_bf, wgate, bp, sp, conv_w[l], conv_g, ml_g, bias_row, bias_col,
            w_ffn_gate[l], w_ffn_up[l], w_out[l])
        tail = (w_out_bf, norm2_g[l], (wg_bf, wu_bf, w_ffn_down[l]), final_norm_g, final_norm)
        hp = _tail(hp, y, *tail)
        p_states.append((cb, c1, n1, m1[:, :, 0]))

        z, gcol, grow = _inproj(hs, g1, w_bf, wgate)
        m_tok = jnp.repeat(state_mlstm_m[l], ss, axis=0)
        mcol = jnp.pad(m_tok, ((0, 0), (0, LANES - N_HEADS)))
        mrow = jnp.pad(m_tok.T, ((0, SUBLANES - N_HEADS), (0, 0)))
        y, cb, c1, n1, m_row = _mix_sample(z, gcol, grow, mcol, mrow, state_conv[l],
                                           state_mlstm_C[l], state_mlstm_n[l], bs, ss,
                                           conv_w[l], conv_g, ml_g, bias_row, bias_col)
        hs = _tail(hs, y, *tail)
        s_states.append((cb, c1, n1, m_row[:N_HEADS, ::ss].T))

    return (hp.reshape(bp, sp, D_MODEL), hs.reshape(bs, ss, D_MODEL),
            _stack(p_states, 0), _stack(p_states, 1), _stack(p_states, 2), _stack(p_states, 3),
            _stack(s_states, 0), _stack(s_states, 1), _stack(s_states, 2), _stack(s_states, 3))
```

```python
import functools

import jax
import jax.numpy as jnp
from jax import lax
from jax.experimental import pallas as pl
from jax.experimental.pallas import tpu as pltpu

F32 = jnp.float32
BF16 = jnp.bfloat16

D_MODEL = 2048
D_CONV = 1024
CONV_W = 3
N_HEADS = 4
D_K = 128
D_V = 256
D_FF = 5632
EPS = 1e-6

LANES = 128
SUBLANES = 8
GATE_ROWS = 16
MIB = 1024 * 1024

PROMPT_CHUNK = 256
SAMPLE_GROUP = 16
NORM_ROWS = 64
W_SLAB = 512
FFN_TILE = 512
CAST_PIECE = 256

VMEM_MIB = {"cast_w_in": 32, "inproj": 48, "fused_prompt": 62, "mix_sample": 56,
            "outproj": 40, "ffn": 60}

SRC_Q = 3 * D_CONV
SRC_K = SRC_Q + N_HEADS * D_K
SRC_V = SRC_K + N_HEADS * D_K
SRC_O = SRC_V + N_HEADS * D_V
N_MAIN = SRC_O + N_HEADS * D_V

HEAD_COLS = 2 * D_K + 2 * D_V
N_MLSTM = N_HEADS * HEAD_COLS
CONV_PARTS = 4
CONV_PART = D_CONV // CONV_PARTS
BG, CG, XT = 0, 1, 2


def _cols_conv(kind, part):
    start = N_MLSTM + (3 * part + kind) * CONV_PART
    return slice(start, start + CONV_PART)


def _conv_operand(z, kind, rows=slice(None)):
    return jnp.concatenate([z[rows, _cols_conv(kind, p)] for p in range(CONV_PARTS)], axis=1)


def _cols_q(h):
    return slice(h * HEAD_COLS, h * HEAD_COLS + D_K)


def _cols_k(h):
    return slice(h * HEAD_COLS + D_K, h * HEAD_COLS + 2 * D_K)


def _cols_v(h):
    return slice(h * HEAD_COLS + 2 * D_K, h * HEAD_COLS + 2 * D_K + D_V)


def _cols_o(h):
    return slice(h * HEAD_COLS + 2 * D_K + D_V, (h + 1) * HEAD_COLS)


def _source_lane_block(j):
    per_head = HEAD_COLS // LANES
    h, r = j // per_head, j % per_head
    v_blocks = D_V // LANES
    mlstm_src = jnp.where(
        r == 0, SRC_Q // LANES + h,
        jnp.where(r == 1, SRC_K // LANES + h,
                  jnp.where(r < 2 + v_blocks,
                            SRC_V // LANES + v_blocks * h + (r - 2),
                            SRC_O // LANES + v_blocks * h + (r - 2 - v_blocks))))
    per_kind = CONV_PART // LANES
    c = j - N_MLSTM // LANES
    part, r = c // (3 * per_kind), c % (3 * per_kind)
    conv_src = (r // per_kind) * (D_CONV // LANES) + part * per_kind + r % per_kind
    return jnp.where(j < N_MLSTM // LANES, mlstm_src, conv_src)


def _compiler_params(name, semantics):
    return pltpu.CompilerParams(dimension_semantics=semantics,
                                vmem_limit_bytes=VMEM_MIB[name] * MIB)


def _rmsnorm_piece(x, gain, dtype):
    ms = jnp.mean(x * x, axis=-1, keepdims=True)
    return ((x * lax.rsqrt(ms + EPS)) * gain).astype(dtype)


def _rmsnorm_rows(src_ref, gain, dst_ref):
    rows = src_ref.shape[0]
    chunk = min(NORM_ROWS, rows)

    def body(i, carry):
        r = pl.multiple_of(i * chunk, chunk)
        dst_ref[pl.ds(r, chunk), :] = _rmsnorm_piece(
            src_ref[pl.ds(r, chunk), :].astype(F32), gain, dst_ref.dtype)
        return carry

    lax.fori_loop(0, rows // chunk, body, 0, unroll=2)


def _cast_kernel(*refs):
    *w_refs, o_ref = refs
    for r, w_ref in enumerate(w_refs):
        o_ref[:, r * LANES:(r + 1) * LANES] = w_ref[...].T.astype(o_ref.dtype)


def _cast_main_columns(w_t):
    per_slab = W_SLAB // LANES
    in_specs = [
        pl.BlockSpec((LANES, D_MODEL),
                     lambda j, r=r: (_source_lane_block(per_slab * j + r), 0))
        for r in range(per_slab)
    ]
    return pl.pallas_call(
        _cast_kernel,
        grid=(N_MAIN // W_SLAB,),
        in_specs=in_specs,
        out_specs=pl.BlockSpec((D_MODEL, W_SLAB), lambda j: (0, j)),
        out_shape=jax.ShapeDtypeStruct((D_MODEL, N_MAIN), BF16),
        compiler_params=_compiler_params("cast_w_in", ("parallel",)),
        name="cast_w_in",
    )(*([w_t] * per_slab))


def _gate_dot(xn, wgate_t_ref):
    return lax.dot_general(xn, wgate_t_ref[...].astype(BF16), (((1,), (1,)), ((), ())),
                           preferred_element_type=F32)


def _inproj_kernel(x_ref, g_ref, w_ref, wgate_ref, z_ref, gcol_ref, grow_ref, xn_ref):
    @pl.when(pl.program_id(1) == 0)
    def _():
        _rmsnorm_rows(x_ref, g_ref[...], xn_ref)
        gc = _gate_dot(xn_ref[...], wgate_ref)
        gcol_ref[...] = gc
        grow_ref[...] = gc.T[0:GATE_ROWS, :]

    z_ref[...] = jnp.dot(xn_ref[...], w_ref[...], preferred_element_type=F32)


def _inproj(x2d, g1, w_bf, wgate):
    tokens = x2d.shape[0]
    tm = min(1024, tokens)
    tn = W_SLAB
    return pl.pallas_call(
        _inproj_kernel,
        grid=(tokens // tm, N_MAIN // tn),
        in_specs=[
            pl.BlockSpec((tm, D_MODEL), lambda i, j: (i, 0)),
            pl.BlockSpec((1, D_MODEL), lambda i, j: (0, 0)),
            pl.BlockSpec((D_MODEL, tn), lambda i, j: (0, j)),
            pl.BlockSpec((LANES, D_MODEL), lambda i, j: (0, 0)),
        ],
        out_specs=[
            pl.BlockSpec((tm, tn), lambda i, j: (i, j)),
            pl.BlockSpec((tm, LANES), lambda i, j: (i, 0)),
            pl.BlockSpec((GATE_ROWS, tm), lambda i, j: (0, i)),
        ],
        out_shape=[
            jax.ShapeDtypeStruct((tokens, N_MAIN), F32),
            jax.ShapeDtypeStruct((tokens, LANES), F32),
            jax.ShapeDtypeStruct((GATE_ROWS, tokens), F32),
        ],
        scratch_shapes=[pltpu.VMEM((tm, D_MODEL), BF16)],
        compiler_params=_compiler_params("inproj", ("parallel", "arbitrary")),
        name="inproj",
    )(x2d, g1, w_bf, wgate)


def _log_sigmoid(x):
    return jnp.minimum(x, 0.0) - jnp.log1p(jnp.exp(-jnp.abs(x)))


def _split3(x):
    hi = x.astype(BF16).astype(F32)
    mid = (x - hi).astype(BF16).astype(F32)
    lo = ((x - hi) - mid).astype(BF16).astype(F32)
    return hi, mid, lo


def _masked_sums(mask, lf_col, lf_row):
    m_bf = mask.astype(BF16)
    n, k = lf_col.shape[1], lf_row.shape[0]
    col_terms = jnp.concatenate(_split3(lf_col), axis=1).astype(BF16)
    row_terms = jnp.concatenate(_split3(lf_row) + (jnp.zeros_like(lf_row),),
                                axis=0).astype(BF16)
    cols = jnp.dot(m_bf, col_terms, preferred_element_type=F32)
    rows = lax.dot_general(row_terms, m_bf, (((1,), (1,)), ((), ())),
                           preferred_element_type=F32)
    return (cols[:, 0:n] + cols[:, n:2 * n] + cols[:, 2 * n:3 * n],
            rows[0:k] + rows[k:2 * k] + rows[2 * k:3 * k])


def _intra(q_bf, k_bf, b_col, b_row, li_row, m0_col, mask):
    dmat = jnp.where(mask, b_col - b_row + li_row, -jnp.inf)
    inter = b_col + m0_col
    m_t = jnp.maximum(inter, jnp.max(dmat, axis=1, keepdims=True))
    w = jnp.exp(dmat - m_t)
    a_inter = jnp.exp(inter - m_t)
    qk = lax.dot_general(q_bf, k_bf, (((1,), (1,)), ((), ())), preferred_element_type=F32)
    return qk * w, a_inter, m_t


def _head_output(num, den, m_t, o_gate, gain):
    denom = jnp.maximum(jnp.abs(den), jnp.exp(-m_t))
    h = num * (1.0 / denom)
    hn = (h * lax.rsqrt(jnp.mean(h * h, axis=1, keepdims=True) + EPS)) * gain
    return jax.nn.sigmoid(o_gate) * hn


def _conv_norm(bg, conv, gain):
    yc = bg * conv
    return (yc * lax.rsqrt(jnp.mean(yc * yc, axis=1, keepdims=True) + EPS)) * gain


def _prompt_conv_part(z, part, convw_ref, ubuf, yc_scr):
    L = z.shape[0]
    ch = slice(part * CONV_PART, (part + 1) * CONV_PART)
    u = z[:, _cols_conv(CG, part)] * z[:, _cols_conv(XT, part)]
    ubuf[SUBLANES:SUBLANES + L, ch] = u
    conv = (ubuf[SUBLANES - 2:SUBLANES - 2 + L, ch] * convw_ref[0:1, ch]
            + ubuf[SUBLANES - 1:SUBLANES - 1 + L, ch] * convw_ref[1:2, ch]
            + u * convw_ref[2:3, ch])
    ubuf[0:SUBLANES, ch] = ubuf[L:L + SUBLANES, ch]
    yc = z[:, _cols_conv(BG, part)] * conv
    yc_scr[:, ch] = yc
    return jnp.sum(yc * yc, axis=1, keepdims=True)


def _prompt_conv_finish(sumsq, yc_scr, convg_ref, y_ref):
    scale = lax.rsqrt(sumsq * (1.0 / D_CONV) + EPS)
    y_ref[:, 0:D_CONV] = ((yc_scr[...] * scale) * convg_ref[...]).astype(y_ref.dtype)


def _prompt_mlstm(z, gc, gr, mlg_ref, bias_row_ref, bias_col_ref, y_ref, c_scr, n_scr, m_scr,
                  fillers):
    L = z.shape[0]
    fillers = list(fillers)

    def fill():
        if fillers:
            fillers.pop(0)()

    pre_c = gc + bias_row_ref[...]
    pre_r = gr + bias_col_ref[:, 0:1]
    row_id = lax.broadcasted_iota(jnp.int32, (L, L), 0)
    col_id = lax.broadcasted_iota(jnp.int32, (L, L), 1)
    causal = col_id <= row_id
    b_c, b_r = _masked_sums(causal, _log_sigmoid(pre_c), _log_sigmoid(pre_r))

    def scores(h):
        li_r = pre_r[h:h + 1, :]
        bc = b_c[:, N_HEADS + h:N_HEADS + h + 1]
        br = b_r[N_HEADS + h:N_HEADS + h + 1, :]
        m0 = m_scr[h, 0:1, 0:1]
        qf = z[:, _cols_q(h)] * (D_K ** -0.5)
        q_bf = qf.astype(BF16)
        kf = z[:, _cols_k(h)]
        v_bf = z[:, _cols_v(h)].astype(BF16)
        s, a_inter, m_t = _intra(q_bf, kf.astype(BF16), bc, br, li_r, m0, causal)
        return li_r, bc, br, m0, qf, q_bf, kf, v_bf, s, a_inter, m_t

    def finish(h, staged):
        li_r, bc, br, m0, qf, q_bf, kf, v_bf, s, a_inter, m_t = staged
        li_c = pre_c[:, h:h + 1]
        c_old = c_scr[h]
        n_old = n_scr[h]
        num = (a_inter * jnp.dot(q_bf, c_old.astype(BF16), preferred_element_type=F32)
               + jnp.dot(s.astype(BF16), v_bf, preferred_element_type=F32))
        den = (a_inter * jnp.sum(qf * n_old, axis=1, keepdims=True)
               + jnp.sum(s, axis=1, keepdims=True))
        fill()
        y_ref[:, D_CONV + h * D_V:D_CONV + (h + 1) * D_V] = _head_output(
            num, den, m_t, z[:, _cols_o(h)],
            mlg_ref[:, h * D_V:(h + 1) * D_V]).astype(y_ref.dtype)

        b_last = br[:, L - 1:L]
        m_new = jnp.maximum(b_last + m0,
                            jnp.max(b_last - br + li_r, axis=1, keepdims=True))
        decay = jnp.exp(b_last + m0 - m_new)
        kw = kf * jnp.exp(b_last - bc + li_c - m_new)
        c_scr[h] = decay * c_old + lax.dot_general(
            kw.astype(BF16), v_bf, (((0,), (0,)), ((), ())), preferred_element_type=F32)
        n_scr[h] = decay * n_old + jnp.sum(kw, axis=0, keepdims=True)
        m_scr[h] = jnp.broadcast_to(m_new, (SUBLANES, LANES))
        fill()

    staged = scores(0)
    fill()
    for h in range(N_HEADS):
        upcoming = None
        if h + 1 < N_HEADS:
            upcoming = scores(h + 1)
            fill()
        finish(h, staged)
        staged = upcoming
    while fillers:
        fill()


def _fused_prompt_kernel(x_ref, g1_ref, w_ref, wgate_ref, convw_ref, convg_ref, mlg_ref,
                         bias_row_ref, bias_col_ref, wg_ref, wu_ref, wout_ref,
                         y_ref, convs_ref, c_out_ref, n_out_ref, m_out_ref,
                         wg_bf_ref, wu_bf_ref, wout_bf_ref,
                         xn_scr, z_scr, ubuf, yc_scr, c_scr, n_scr, m_scr):
    chunk = pl.program_id(1)
    L = x_ref.shape[0]

    @pl.when(chunk == 0)
    def _():
        ubuf[0:SUBLANES, :] = jnp.zeros((SUBLANES, D_CONV), F32)
        c_scr[...] = jnp.zeros(c_scr.shape, F32)
        n_scr[...] = jnp.zeros(n_scr.shape, F32)
        m_scr[...] = jnp.zeros(m_scr.shape, F32)

    gain = g1_ref[...]
    for r in range(0, L, NORM_ROWS):
        xn_scr[r:r + NORM_ROWS, :] = _rmsnorm_piece(x_ref[r:r + NORM_ROWS, :], gain, BF16)
    xn = xn_scr[...]
    gc = _gate_dot(xn, wgate_ref)
    gr = gc.T[0:SUBLANES, :]

    def project(j):
        cols = slice(j * W_SLAB, (j + 1) * W_SLAB)
        z_scr[:, cols] = jnp.dot(xn_scr[...], w_ref[:, cols], preferred_element_type=F32)

    piece = lambda j: functools.partial(project, j)
    skip = lambda: None
    sumsq = []

    def then_conv(j, part):
        def emit():
            project(j)
            sumsq.append(_prompt_conv_part(z_scr, part, convw_ref, ubuf, yc_scr))
        return emit

    project(0)
    wg_bf_ref[0] = wg_ref[...].astype(BF16)
    wout_bf_ref[...] = wout_ref[...].astype(BF16)
    project(1)
    wu_bf_ref[0] = wu_ref[...].astype(BF16)
    fillers = [piece(2), piece(3), piece(4), piece(6),
               piece(5), then_conv(7, 0), then_conv(8, 1),
               piece(9), then_conv(10, 2), then_conv(11, 3),
               skip, skip]
    _prompt_mlstm(z_scr, gc, gr, mlg_ref, bias_row_ref, bias_col_ref, y_ref,
                  c_scr, n_scr, m_scr, fillers)
    _prompt_conv_finish(sum(sumsq[1:], sumsq[0]), yc_scr, convg_ref, y_ref)

    @pl.when(chunk == pl.num_programs(1) - 1)
    def _():
        convs_ref[0] = ubuf[SUBLANES - 2:SUBLANES, :]
        c_out_ref[0] = c_scr[...]
        for h in range(N_HEADS):
            n_out_ref[0, h:h + 1, :] = n_scr[h]
            m_out_ref[0, h:h + 1, :] = m_scr[h, 0:1, :]


def _fused_prompt(x2d, g1, w_bf, wgate, batch, seq, conv_w, conv_g, ml_g, bias_row, bias_col,
                  wg, wu, w_out):
    L = PROMPT_CHUNK
    nc = seq // L
    const = lambda b, c: (0, 0)
    wout_rows = D_MODEL // (batch * nc)
    assert wout_rows * batch * nc == D_MODEL and wout_rows % (2 * SUBLANES) == 0
    wout_piece = pl.BlockSpec((wout_rows, D_MODEL), lambda b, c: (b * nc + c, 0))
    n_pieces = D_FF // CAST_PIECE
    assert batch * nc >= n_pieces, "not enough grid steps to cast the FFN weights"
    per_tile = FFN_TILE // CAST_PIECE
    piece_of = lambda b, c: jnp.minimum(b * nc + c, n_pieces - 1)
    tiled = lambda b, c: (piece_of(b, c) // per_tile, 0, piece_of(b, c) % per_tile)
    return pl.pallas_call(
        _fused_prompt_kernel,
        grid=(batch, nc),
        in_specs=[
            pl.BlockSpec((L, D_MODEL), lambda b, c: (b * nc + c, 0)),
            pl.BlockSpec((1, D_MODEL), const),
            pl.BlockSpec((D_MODEL, N_MAIN), const, pipeline_mode=pl.Buffered(1)),
            pl.BlockSpec((LANES, D_MODEL), const),
            pl.BlockSpec((CONV_W, D_CONV), const),
            pl.BlockSpec((1, D_CONV), const),
            pl.BlockSpec((1, N_HEADS * D_V), const),
            pl.BlockSpec((1, LANES), const),
            pl.BlockSpec((SUBLANES, LANES), const),
            pl.BlockSpec((D_MODEL, CAST_PIECE), lambda b, c: (0, piece_of(b, c))),
            pl.BlockSpec((D_MODEL, CAST_PIECE), lambda b, c: (0, piece_of(b, c))),
            wout_piece,
        ],
        out_specs=[
            pl.BlockSpec((L, D_MODEL), lambda b, c: (b * nc + c, 0)),
            pl.BlockSpec((1, CONV_W - 1, D_CONV), lambda b, c: (b, 0, 0)),
            pl.BlockSpec((1, N_HEADS, D_K, D_V), lambda b, c: (b, 0, 0, 0)),
            pl.BlockSpec((1, N_HEADS, D_K), lambda b, c: (b, 0, 0)),
            pl.BlockSpec((1, N_HEADS, LANES), lambda b, c: (b, 0, 0)),
            pl.BlockSpec((1, D_MODEL, CAST_PIECE), tiled),
            pl.BlockSpec((1, D_MODEL, CAST_PIECE), tiled),
            wout_piece,
        ],
        out_shape=[
            jax.ShapeDtypeStruct((batch * seq, D_MODEL), BF16),
            jax.ShapeDtypeStruct((batch, CONV_W - 1, D_CONV), F32),
            jax.ShapeDtypeStruct((batch, N_HEADS, D_K, D_V), F32),
            jax.ShapeDtypeStruct((batch, N_HEADS, D_K), F32),
            jax.ShapeDtypeStruct((batch, N_HEADS, LANES), F32),
            jax.ShapeDtypeStruct((D_FF // FFN_TILE, D_MODEL, FFN_TILE), BF16),
            jax.ShapeDtypeStruct((D_FF // FFN_TILE, D_MODEL, FFN_TILE), BF16),
            jax.ShapeDtypeStruct((D_MODEL, D_MODEL), BF16),
        ],
        scratch_shapes=[
            pltpu.VMEM((L, D_MODEL), BF16),
            pltpu.VMEM((L, N_MAIN), F32),
            pltpu.VMEM((L + 2 * SUBLANES, D_CONV), F32),
            pltpu.VMEM((L, D_CONV), F32),
            pltpu.VMEM((N_HEADS, D_K, D_V), F32),
            pltpu.VMEM((N_HEADS, 1, D_K), F32),
            pltpu.VMEM((N_HEADS, SUBLANES, LANES), F32),
        ],
        compiler_params=_compiler_params("fused_prompt", ("arbitrary", "arbitrary")),
        name="fused_prompt",
    )(x2d, g1, w_bf, wgate, conv_w, conv_g, ml_g, bias_row, bias_col, wg, wu, w_out)


def _mix_sample_kernel(z_ref, gcol_ref, grow_ref, mcol_ref, mrow_ref,
                       convs_in_ref, c_in_ref, n_in_ref,
                       convw_ref, convg_ref, mlg_ref, bias_row_ref, bias_col_ref,
                       y_ref, convs_ref, c_out_ref, n_out_ref, m_out_ref,
                       pad_scr, conv_scr, *, seq):
    L = z_ref.shape[0]
    group = L // seq
    stride = seq + SUBLANES

    for g in range(group):
        rows = slice(g * seq, (g + 1) * seq)
        u = _conv_operand(z_ref, CG, rows) * _conv_operand(z_ref, XT, rows)
        base = g * stride
        pad_scr[base + SUBLANES - 2:base + SUBLANES, :] = convs_in_ref[g]
        pad_scr[base + SUBLANES:base + SUBLANES + seq, :] = u
        conv_scr[g * seq:(g + 1) * seq, :] = (
            pad_scr[base + SUBLANES - 2:base + SUBLANES - 2 + seq, :] * convw_ref[0:1, :]
            + pad_scr[base + SUBLANES - 1:base + SUBLANES - 1 + seq, :] * convw_ref[1:2, :]
            + u * convw_ref[2:3, :])
        convs_ref[g] = pad_scr[base + seq + SUBLANES - 2:base + seq + SUBLANES, :]
    y_ref[:, 0:D_CONV] = _conv_norm(_conv_operand(z_ref, BG), conv_scr[...],
                                    convg_ref[...]).astype(y_ref.dtype)

    pre_c = gcol_ref[...] + bias_row_ref[...]
    pre_r = grow_ref[0:SUBLANES, :] + bias_col_ref[:, 0:1]
    row_id = lax.broadcasted_iota(jnp.int32, (L, L), 0)
    col_id = lax.broadcasted_iota(jnp.int32, (L, L), 1)
    same = (row_id // seq) == (col_id // seq)
    causal = same & (col_id <= row_id)
    lf_c = _log_sigmoid(pre_c)
    lf_r = _log_sigmoid(pre_r)
    b_c, b_r = _masked_sums(causal, lf_c, lf_r)
    tot_c, tot_r = _masked_sums(same, lf_c, lf_r)
    lane_seq = lax.broadcasted_iota(jnp.int32, (1, L), 1) // seq

    for h in range(N_HEADS):
        li_r = pre_r[h:h + 1, :]
        li_c = pre_c[:, h:h + 1]
        bc = b_c[:, N_HEADS + h:N_HEADS + h + 1]
        br = b_r[N_HEADS + h:N_HEADS + h + 1, :]
        bl_c = tot_c[:, N_HEADS + h:N_HEADS + h + 1]
        bl_r = tot_r[N_HEADS + h:N_HEADS + h + 1, :]
        m0_c = mcol_ref[:, h:h + 1]
        m0_r = mrow_ref[h:h + 1, :]
        qf = z_ref[:, _cols_q(h)] * (D_K ** -0.5)
        q_bf = qf.astype(BF16)
        kf = z_ref[:, _cols_k(h)]
        v_bf = z_ref[:, _cols_v(h)].astype(BF16)
        s, a_inter, m_t = _intra(q_bf, kf.astype(BF16), bc, br, li_r, m0_c, causal)

        qc_rows, qn_rows = [], []
        for g in range(group):
            rows = slice(g * seq, (g + 1) * seq)
            qc_rows.append(jnp.dot(qf[rows], c_in_ref[g, h], preferred_element_type=F32))
            qn_rows.append(jnp.sum(qf[rows] * n_in_ref[g, h:h + 1, :], axis=1, keepdims=True))
        num = (a_inter * jnp.concatenate(qc_rows, axis=0)
               + jnp.dot(s.astype(BF16), v_bf, preferred_element_type=F32))
        den = (a_inter * jnp.concatenate(qn_rows, axis=0)
               + jnp.sum(s, axis=1, keepdims=True))
        y_ref[:, D_CONV + h * D_V:D_CONV + (h + 1) * D_V] = _head_output(
            num, den, m_t, z_ref[:, _cols_o(h)],
            mlg_ref[:, h * D_V:(h + 1) * D_V]).astype(y_ref.dtype)

        g_r = bl_r - br + li_r
        gmax_c = jnp.max(jnp.where(same, g_r, -jnp.inf), axis=1, keepdims=True)
        gmax_r = jnp.max(jnp.where(same, gmax_c, -jnp.inf), axis=0, keepdims=True)
        m_new_c = jnp.maximum(bl_c + m0_c, gmax_c)
        m_new_r = jnp.maximum(bl_r + m0_r, gmax_r)
        decay_r = jnp.exp(bl_r + m0_r - m_new_r)
        kw = kf * jnp.exp(bl_c - bc + li_c - m_new_c)
        kw_t = kw.T
        for g in range(group):
            decay = decay_r[:, g * seq:g * seq + 1]
            kw_g = jnp.where(lane_seq == g, kw_t, 0.0).astype(BF16)
            c_out_ref[g, h] = decay * c_in_ref[g, h] + jnp.dot(
                kw_g, v_bf, preferred_element_type=F32)
            n_out_ref[g, h:h + 1, :] = (decay * n_in_ref[g, h:h + 1, :]
                                        + jnp.sum(kw[g * seq:(g + 1) * seq], axis=0, keepdims=True))
        m_out_ref[h:h + 1, :] = m_new_r
    m_out_ref[N_HEADS:, :] = jnp.zeros((SUBLANES - N_HEADS, L), F32)


def _mix_sample(z, gcol, grow, mcol, mrow, convs, c_state, n_state, batch, seq,
                conv_w, conv_g, ml_g, bias_row, bias_col):
    group = SAMPLE_GROUP
    L = group * seq
    const = lambda i: (0, 0)
    return pl.pallas_call(
        functools.partial(_mix_sample_kernel, seq=seq),
        grid=(batch // group,),
        in_specs=[
            pl.BlockSpec((L, N_MAIN), lambda i: (i, 0)),
            pl.BlockSpec((L, LANES), lambda i: (i, 0)),
            pl.BlockSpec((GATE_ROWS, L), lambda i: (0, i)),
            pl.BlockSpec((L, LANES), lambda i: (i, 0)),
            pl.BlockSpec((SUBLANES, L), lambda i: (0, i)),
            pl.BlockSpec((group, CONV_W - 1, D_CONV), lambda i: (i, 0, 0)),
            pl.BlockSpec((group, N_HEADS, D_K, D_V), lambda i: (i, 0, 0, 0)),
            pl.BlockSpec((group, N_HEADS, D_K), lambda i: (i, 0, 0)),
            pl.BlockSpec((CONV_W, D_CONV), const),
            pl.BlockSpec((1, D_CONV), const),
            pl.BlockSpec((1, N_HEADS * D_V), const),
            pl.BlockSpec((1, LANES), const),
            pl.BlockSpec((SUBLANES, LANES), const),
        ],
        out_specs=[
            pl.BlockSpec((L, D_MODEL), lambda i: (i, 0)),
            pl.BlockSpec((group, CONV_W - 1, D_CONV), lambda i: (i, 0, 0)),
            pl.BlockSpec((group, N_HEADS, D_K, D_V), lambda i: (i, 0, 0, 0)),
            pl.BlockSpec((group, N_HEADS, D_K), lambda i: (i, 0, 0)),
            pl.BlockSpec((SUBLANES, L), lambda i: (0, i)),
        ],
        out_shape=[
            jax.ShapeDtypeStruct((batch * seq, D_MODEL), BF16),
            jax.ShapeDtypeStruct((batch, CONV_W - 1, D_CONV), F32),
            jax.ShapeDtypeStruct((batch, N_HEADS, D_K, D_V), F32),
            jax.ShapeDtypeStruct((batch, N_HEADS, D_K), F32),
            jax.ShapeDtypeStruct((SUBLANES, batch * seq), F32),
        ],
        scratch_shapes=[
            pltpu.VMEM((group * (seq + SUBLANES), D_CONV), F32),
            pltpu.VMEM((L, D_CONV), F32),
        ],
        compiler_params=_compiler_params("mix_sample", ("parallel",)),
        name="mix_sample",
    )(z, gcol, grow, mcol, mrow, convs, c_state, n_state,
      conv_w, conv_g, ml_g, bias_row, bias_col)


def _outproj_kernel(y_ref, w_ref, x_ref, o_ref):
    o_ref[...] = x_ref[...] + jnp.dot(y_ref[...], w_ref[...], preferred_element_type=F32)


def _outproj(y, w_out_bf, x2d):
    tokens = x2d.shape[0]
    tm = min(512, tokens)
    return pl.pallas_call(
        _outproj_kernel,
        grid=(tokens // tm,),
        in_specs=[
            pl.BlockSpec((tm, D_MODEL), lambda i: (i, 0)),
            pl.BlockSpec((D_MODEL, D_MODEL), lambda i: (0, 0), pipeline_mode=pl.Buffered(1)),
            pl.BlockSpec((tm, D_MODEL), lambda i: (i, 0)),
        ],
        out_specs=pl.BlockSpec((tm, D_MODEL), lambda i: (i, 0)),
        out_shape=jax.ShapeDtypeStruct((tokens, D_MODEL), F32),
        compiler_params=_compiler_params("outproj", ("parallel",)),
        name="outproj",
    )(y, w_out_bf, x2d)


def _ffn_kernel(x_ref, g2_ref, wg_ref, wu_ref, wd_ref, gf_ref, o_ref, hn_ref, *, final_norm):
    j = pl.program_id(1)
    last = pl.num_programs(1) - 1
    tm = o_ref.shape[0]
    half_f = wd_ref.shape[0] // 2
    f_halves = (slice(0, half_f), slice(half_f, 2 * half_f))
    whole = slice(0, tm)
    row_halves = (slice(0, tm // 2), slice(tm // 2, tm))

    def gate_up(rows):
        hn = hn_ref[rows, :]
        return [(jnp.dot(hn, wg_ref[0, :, cols], preferred_element_type=F32),
                 jnp.dot(hn, wu_ref[0, :, cols], preferred_element_type=F32))
                for cols in f_halves]

    def down_into(rows, pre):
        down = None
        for (gate, up), frows in zip(pre, f_halves):
            act = ((gate * jax.nn.sigmoid(gate)) * up).astype(BF16)
            part = jnp.dot(act, wd_ref[frows, :].astype(BF16), preferred_element_type=F32)
            down = part if down is None else down + part
        o_ref[rows, :] += down

    def norm_in(rows):
        gain = g2_ref[...]
        for r in range(rows.start, rows.stop, NORM_ROWS):
            x = x_ref[r:r + NORM_ROWS, :]
            o_ref[r:r + NORM_ROWS, :] = x
            hn_ref[r:r + NORM_ROWS, :] = _rmsnorm_piece(x, gain, BF16)

    def norm_out(rows):
        gain = gf_ref[...]
        for r in range(rows.start, rows.stop, NORM_ROWS):
            o_ref[r:r + NORM_ROWS, :] = _rmsnorm_piece(o_ref[r:r + NORM_ROWS, :], gain, F32)

    @pl.when(j == 0)
    def _():
        norm_in(row_halves[0])
        pre = gate_up(row_halves[0])
        norm_in(row_halves[1])
        down_into(row_halves[0], pre)
        down_into(row_halves[1], gate_up(row_halves[1]))

    @pl.when((j > 0) & ((j < last) | (not final_norm)))
    def _():
        down_into(whole, gate_up(whole))

    if final_norm:
        @pl.when(j == last)
        def _():
            down_into(row_halves[0], gate_up(row_halves[0]))
            pre = gate_up(row_halves[1])
            norm_out(row_halves[0])
            down_into(row_halves[1], pre)
            norm_out(row_halves[1])


def _ffn(x1, g2, wg_bf, wu_bf, wd, gf, final_norm):
    tokens = x1.shape[0]
    tm = min(1024, tokens)
    tf = FFN_TILE
    return pl.pallas_call(
        functools.partial(_ffn_kernel, final_norm=final_norm),
        grid=(tokens // tm, D_FF // tf),
        in_specs=[
            pl.BlockSpec((tm, D_MODEL), lambda i, j: (i, 0)),
            pl.BlockSpec((1, D_MODEL), lambda i, j: (0, 0)),
            pl.BlockSpec((1, D_MODEL, tf), lambda i, j: (j, 0, 0)),
            pl.BlockSpec((1, D_MODEL, tf), lambda i, j: (j, 0, 0)),
            pl.BlockSpec((tf, D_MODEL), lambda i, j: (j, 0)),
            pl.BlockSpec((1, D_MODEL), lambda i, j: (0, 0)),
        ],
        out_specs=pl.BlockSpec((tm, D_MODEL), lambda i, j: (i, 0)),
        out_shape=jax.ShapeDtypeStruct((tokens, D_MODEL), F32),
        scratch_shapes=[pltpu.VMEM((tm, D_MODEL), BF16)],
        compiler_params=_compiler_params("ffn", ("parallel", "arbitrary")),
        name="ffn",
    )(x1, g2, wg_bf, wu_bf, wd, gf)


def _gate_params(w_t, b_i, b_f):
    wgate = jnp.pad(w_t[N_MAIN:, :], ((0, LANES - 2 * N_HEADS), (0, 0)))
    bias = jnp.concatenate([b_i, b_f]).astype(F32)
    bias_row = jnp.pad(bias, (0, LANES - 2 * N_HEADS))[None, :]
    bias_col = jnp.broadcast_to(bias[:, None], (SUBLANES, LANES))
    return wgate, bias_row, bias_col


def _tail(x2d, y, w_out, g2, ffn_weights, gf, final_norm):
    x1 = _outproj(y, w_out, x2d)
    return _ffn(x1, g2[None, :], *ffn_weights, gf[None, :], final_norm)


def _stack(states, k):
    if len(states) == 1:
        return states[0][k][None]
    return jnp.stack([st[k] for st in states])


def kernel(x_prompt, x_sample, state_conv, state_mlstm_C, state_mlstm_n, state_mlstm_m,
           norm1_g, w_in, b_igate, b_fgate, conv_w, conv_out_g, mlstm_out_g, w_out,
           norm2_g, w_ffn_gate, w_ffn_up, w_ffn_down, final_norm_g):
    depth = w_in.shape[0]
    bp, sp, _ = x_prompt.shape
    bs, ss, _ = x_sample.shape
    hp = x_prompt.reshape(bp * sp, D_MODEL)
    hs = x_sample.reshape(bs * ss, D_MODEL)
    p_states, s_states = [], []
    for l in range(depth):
        final_norm = l == depth - 1
        w_t = w_in[l].T
        wgate, bias_row, bias_col = _gate_params(w_t, b_igate[l], b_fgate[l])
        w_bf = _cast_main_columns(w_t)
        g1 = norm1_g[l][None, :]
        conv_g = conv_out_g[l][None, :]
        ml_g = mlstm_out_g[l][None, :]

        y, cb, c1, n1, m1, wg_bf, wu_bf, w_out_bf = _fused_prompt(
            hp, g1, w_bf, wgate, bp, sp, conv_w[l], conv_g, ml_g, bias_row, bias_col,
            w_ffn_gate[l], w_ffn_up[l], w_out[l])
        tail = (w_out_bf, norm2_g[l], (wg_bf, wu_bf, w_ffn_down[l]), final_norm_g, final_norm)
        hp = _tail(hp, y, *tail)
        p_states.append((cb, c1, n1, m1[:, :, 0]))

        z, gcol, grow = _inproj(hs, g1, w_bf, wgate)
        m_tok = jnp.repeat(state_mlstm_m[l], ss, axis=0)
        mcol = jnp.pad(m_tok, ((0, 0), (0, LANES - N_HEADS)))
        mrow = jnp.pad(m_tok.T, ((0, SUBLANES - N_HEADS), (0, 0)))
        y, cb, c1, n1, m_row = _mix_sample(z, gcol, grow, mcol, mrow, state_conv[l],
                                           state_mlstm_C[l], state_mlstm_n[l], bs, ss,
                                           conv_w[l], conv_g, ml_g, bias_row, bias_col)
        hs = _tail(hs, y, *tail)
        s_states.append((cb, c1, n1, m_row[:N_HEADS, ::ss].T))

    return (hp.reshape(bp, sp, D_MODEL), hs.reshape(bs, ss, D_MODEL),
            _stack(p_states, 0), _stack(p_states, 1), _stack(p_states, 2), _stack(p_states, 3),
            _stack(s_states, 0), _stack(s_states, 1), _stack(s_states, 2), _stack(s_states, 3))
```

```python
import functools

import jax
import jax.numpy as jnp
from jax import lax
from jax.experimental import pallas as pl
from jax.experimental.pallas import tpu as pltpu

F32 = jnp.float32
BF16 = jnp.bfloat16

D_MODEL = 2048
D_CONV = 1024
CONV_W = 3
N_HEADS = 4
D_K = 128
D_V = 256
D_FF = 5632
EPS = 1e-6

LANES = 128
SUBLANES = 8
GATE_ROWS = 16
MIB = 1024 * 1024

PROMPT_CHUNK = 256
SAMPLE_GROUP = 16
NORM_ROWS = 64
W_SLAB = 512
FFN_TILE = 512
CAST_PIECE = 256

VMEM_MIB = {"cast_w_in": 32, "inproj": 48, "fused_prompt": 62, "mix_sample": 56,
            "outproj": 54, "ffn": 60}

SRC_Q = 3 * D_CONV
SRC_K = SRC_Q + N_HEADS * D_K
SRC_V = SRC_K + N_HEADS * D_K
SRC_O = SRC_V + N_HEADS * D_V
N_MAIN = SRC_O + N_HEADS * D_V

HEAD_COLS = 2 * D_K + 2 * D_V
N_MLSTM = N_HEADS * HEAD_COLS
CONV_PARTS = 4
CONV_PART = D_CONV // CONV_PARTS
BG, CG, XT = 0, 1, 2


def _cols_conv(kind, part):
    start = N_MLSTM + (3 * part + kind) * CONV_PART
    return slice(start, start + CONV_PART)


def _conv_operand(z, kind, rows=slice(None)):
    return jnp.concatenate([z[rows, _cols_conv(kind, p)] for p in range(CONV_PARTS)], axis=1)


def _cols_q(h):
    return slice(h * HEAD_COLS, h * HEAD_COLS + D_K)


def _cols_k(h):
    return slice(h * HEAD_COLS + D_K, h * HEAD_COLS + 2 * D_K)


def _cols_v(h):
    return slice(h * HEAD_COLS + 2 * D_K, h * HEAD_COLS + 2 * D_K + D_V)


def _cols_o(h):
    return slice(h * HEAD_COLS + 2 * D_K + D_V, (h + 1) * HEAD_COLS)


def _source_lane_block(j):
    per_head = HEAD_COLS // LANES
    h, r = j // per_head, j % per_head
    v_blocks = D_V // LANES
    mlstm_src = jnp.where(
        r == 0, SRC_Q // LANES + h,
        jnp.where(r == 1, SRC_K // LANES + h,
                  jnp.where(r < 2 + v_blocks,
                            SRC_V // LANES + v_blocks * h + (r - 2),
                            SRC_O // LANES + v_blocks * h + (r - 2 - v_blocks))))
    per_kind = CONV_PART // LANES
    c = j - N_MLSTM // LANES
    part, r = c // (3 * per_kind), c % (3 * per_kind)
    conv_src = (r // per_kind) * (D_CONV // LANES) + part * per_kind + r % per_kind
    return jnp.where(j < N_MLSTM // LANES, mlstm_src, conv_src)


def _compiler_params(name, semantics):
    return pltpu.CompilerParams(dimension_semantics=semantics,
                                vmem_limit_bytes=VMEM_MIB[name] * MIB)


def _rmsnorm_piece(x, gain, dtype):
    ms = jnp.mean(x * x, axis=-1, keepdims=True)
    return ((x * lax.rsqrt(ms + EPS)) * gain).astype(dtype)


def _rmsnorm_rows(src_ref, gain, dst_ref):
    rows = src_ref.shape[0]
    chunk = min(NORM_ROWS, rows)

    def body(i, carry):
        r = pl.multiple_of(i * chunk, chunk)
        dst_ref[pl.ds(r, chunk), :] = _rmsnorm_piece(
            src_ref[pl.ds(r, chunk), :].astype(F32), gain, dst_ref.dtype)
        return carry

    lax.fori_loop(0, rows // chunk, body, 0, unroll=2)


def _cast_kernel(*refs):
    *w_refs, o_ref = refs
    for r, w_ref in enumerate(w_refs):
        o_ref[:, r * LANES:(r + 1) * LANES] = w_ref[...].T.astype(o_ref.dtype)


def _cast_main_columns(w_t):
    per_slab = W_SLAB // LANES
    in_specs = [
        pl.BlockSpec((LANES, D_MODEL),
                     lambda j, r=r: (_source_lane_block(per_slab * j + r), 0))
        for r in range(per_slab)
    ]
    return pl.pallas_call(
        _cast_kernel,
        grid=(N_MAIN // W_SLAB,),
        in_specs=in_specs,
        out_specs=pl.BlockSpec((D_MODEL, W_SLAB), lambda j: (0, j)),
        out_shape=jax.ShapeDtypeStruct((D_MODEL, N_MAIN), BF16),
        compiler_params=_compiler_params("cast_w_in", ("parallel",)),
        name="cast_w_in",
    )(*([w_t] * per_slab))


def _gate_dot(xn, wgate_t_ref):
    return lax.dot_general(xn, wgate_t_ref[...].astype(BF16), (((1,), (1,)), ((), ())),
                           preferred_element_type=F32)


def _inproj_kernel(x_ref, g_ref, w_ref, wgate_ref, z_ref, gcol_ref, grow_ref, xn_ref):
    @pl.when(pl.program_id(1) == 0)
    def _():
        _rmsnorm_rows(x_ref, g_ref[...], xn_ref)
        gc = _gate_dot(xn_ref[...], wgate_ref)
        gcol_ref[...] = gc
        grow_ref[...] = gc.T[0:GATE_ROWS, :]

    z_ref[...] = jnp.dot(xn_ref[...], w_ref[...], preferred_element_type=F32)


def _inproj(x2d, g1, w_bf, wgate):
    tokens = x2d.shape[0]
    tm = min(1024, tokens)
    tn = W_SLAB
    return pl.pallas_call(
        _inproj_kernel,
        grid=(tokens // tm, N_MAIN // tn),
        in_specs=[
            pl.BlockSpec((tm, D_MODEL), lambda i, j: (i, 0)),
            pl.BlockSpec((1, D_MODEL), lambda i, j: (0, 0)),
            pl.BlockSpec((D_MODEL, tn), lambda i, j: (0, j)),
            pl.BlockSpec((LANES, D_MODEL), lambda i, j: (0, 0)),
        ],
        out_specs=[
            pl.BlockSpec((tm, tn), lambda i, j: (i, j)),
            pl.BlockSpec((tm, LANES), lambda i, j: (i, 0)),
            pl.BlockSpec((GATE_ROWS, tm), lambda i, j: (0, i)),
        ],
        out_shape=[
            jax.ShapeDtypeStruct((tokens, N_MAIN), F32),
            jax.ShapeDtypeStruct((tokens, LANES), F32),
            jax.ShapeDtypeStruct((GATE_ROWS, tokens), F32),
        ],
        scratch_shapes=[pltpu.VMEM((tm, D_MODEL), BF16)],
        compiler_params=_compiler_params("inproj", ("parallel", "arbitrary")),
        name="inproj",
    )(x2d, g1, w_bf, wgate)


def _log_sigmoid(x):
    return jnp.minimum(x, 0.0) - jnp.log1p(jnp.exp(-jnp.abs(x)))


def _split3(x):
    hi = x.astype(BF16).astype(F32)
    mid = (x - hi).astype(BF16).astype(F32)
    lo = ((x - hi) - mid).astype(BF16).astype(F32)
    return hi, mid, lo


def _masked_sums(mask, lf_col, lf_row):
    m_bf = mask.astype(BF16)
    n, k = lf_col.shape[1], lf_row.shape[0]
    col_terms = jnp.concatenate(_split3(lf_col), axis=1).astype(BF16)
    row_terms = jnp.concatenate(_split3(lf_row) + (jnp.zeros_like(lf_row),),
                                axis=0).astype(BF16)
    cols = jnp.dot(m_bf, col_terms, preferred_element_type=F32)
    rows = lax.dot_general(row_terms, m_bf, (((1,), (1,)), ((), ())),
                           preferred_element_type=F32)
    return (cols[:, 0:n] + cols[:, n:2 * n] + cols[:, 2 * n:3 * n],
            rows[0:k] + rows[k:2 * k] + rows[2 * k:3 * k])


def _intra(q_bf, k_bf, b_col, b_row, li_row, m0_col, mask):
    dmat = jnp.where(mask, b_col - b_row + li_row, -jnp.inf)
    inter = b_col + m0_col
    m_t = jnp.maximum(inter, jnp.max(dmat, axis=1, keepdims=True))
    w = jnp.exp(dmat - m_t)
    a_inter = jnp.exp(inter - m_t)
    qk = lax.dot_general(q_bf, k_bf, (((1,), (1,)), ((), ())), preferred_element_type=F32)
    return qk * w, a_inter, m_t


def _head_output(num, den, m_t, o_gate, gain):
    denom = jnp.maximum(jnp.abs(den), jnp.exp(-m_t))
    h = num * (1.0 / denom)
    hn = (h * lax.rsqrt(jnp.mean(h * h, axis=1, keepdims=True) + EPS)) * gain
    return jax.nn.sigmoid(o_gate) * hn


def _conv_norm(bg, conv, gain):
    yc = bg * conv
    return (yc * lax.rsqrt(jnp.mean(yc * yc, axis=1, keepdims=True) + EPS)) * gain


def _prompt_conv_part(z, part, convw_ref, ubuf, yc_scr):
    L = z.shape[0]
    ch = slice(part * CONV_PART, (part + 1) * CONV_PART)
    u = z[:, _cols_conv(CG, part)] * z[:, _cols_conv(XT, part)]
    ubuf[SUBLANES:SUBLANES + L, ch] = u
    conv = (ubuf[SUBLANES - 2:SUBLANES - 2 + L, ch] * convw_ref[0:1, ch]
            + ubuf[SUBLANES - 1:SUBLANES - 1 + L, ch] * convw_ref[1:2, ch]
            + u * convw_ref[2:3, ch])
    ubuf[0:SUBLANES, ch] = ubuf[L:L + SUBLANES, ch]
    yc = z[:, _cols_conv(BG, part)] * conv
    yc_scr[:, ch] = yc
    return jnp.sum(yc * yc, axis=1, keepdims=True)


def _prompt_conv_finish(sumsq, yc_scr, convg_ref, y_ref):
    scale = lax.rsqrt(sumsq * (1.0 / D_CONV) + EPS)
    y_ref[:, 0:D_CONV] = ((yc_scr[...] * scale) * convg_ref[...]).astype(y_ref.dtype)


def _prompt_mlstm(z, gc, gr, mlg_ref, bias_row_ref, bias_col_ref, y_ref, c_scr, n_scr, m_scr,
                  fillers):
    L = z.shape[0]
    fillers = list(fillers)

    def fill():
        if fillers:
            fillers.pop(0)()

    pre_c = gc + bias_row_ref[...]
    pre_r = gr + bias_col_ref[:, 0:1]
    row_id = lax.broadcasted_iota(jnp.int32, (L, L), 0)
    col_id = lax.broadcasted_iota(jnp.int32, (L, L), 1)
    causal = col_id <= row_id
    b_c, b_r = _masked_sums(causal, _log_sigmoid(pre_c), _log_sigmoid(pre_r))

    def scores(h):
        li_r = pre_r[h:h + 1, :]
        bc = b_c[:, N_HEADS + h:N_HEADS + h + 1]
        br = b_r[N_HEADS + h:N_HEADS + h + 1, :]
        m0 = m_scr[h, 0:1, 0:1]
        qf = z[:, _cols_q(h)] * (D_K ** -0.5)
        q_bf = qf.astype(BF16)
        kf = z[:, _cols_k(h)]
        v_bf = z[:, _cols_v(h)].astype(BF16)
        s, a_inter, m_t = _intra(q_bf, kf.astype(BF16), bc, br, li_r, m0, causal)
        return li_r, bc, br, m0, qf, q_bf, kf, v_bf, s, a_inter, m_t

    def finish(h, staged):
        li_r, bc, br, m0, qf, q_bf, kf, v_bf, s, a_inter, m_t = staged
        li_c = pre_c[:, h:h + 1]
        c_old = c_scr[h]
        n_old = n_scr[h]
        num = (a_inter * jnp.dot(q_bf, c_old.astype(BF16), preferred_element_type=F32)
               + jnp.dot(s.astype(BF16), v_bf, preferred_element_type=F32))
        den = (a_inter * jnp.sum(qf * n_old, axis=1, keepdims=True)
               + jnp.sum(s, axis=1, keepdims=True))
        fill()
        y_ref[:, D_CONV + h * D_V:D_CONV + (h + 1) * D_V] = _head_output(
            num, den, m_t, z[:, _cols_o(h)],
            mlg_ref[:, h * D_V:(h + 1) * D_V]).astype(y_ref.dtype)

        b_last = br[:, L - 1:L]
        m_new = jnp.maximum(b_last + m0,
                            jnp.max(b_last - br + li_r, axis=1, keepdims=True))
        decay = jnp.exp(b_last + m0 - m_new)
        kw = kf * jnp.exp(b_last - bc + li_c - m_new)
        c_scr[h] = decay * c_old + lax.dot_general(
            kw.astype(BF16), v_bf, (((0,), (0,)), ((), ())), preferred_element_type=F32)
        n_scr[h] = decay * n_old + jnp.sum(kw, axis=0, keepdims=True)
        m_scr[h] = jnp.broadcast_to(m_new, (SUBLANES, LANES))
        fill()

    staged = scores(0)
    fill()
    for h in range(N_HEADS):
        upcoming = None
        if h + 1 < N_HEADS:
            upcoming = scores(h + 1)
            fill()
        finish(h, staged)
        staged = upcoming
    while fillers:
        fill()


def _fused_prompt_kernel(x_ref, g1_ref, w_ref, wgate_ref, convw_ref, convg_ref, mlg_ref,
                         bias_row_ref, bias_col_ref, wg_ref, wu_ref, wout_ref,
                         y_ref, convs_ref, c_out_ref, n_out_ref, m_out_ref,
                         wg_bf_ref, wu_bf_ref, wout_bf_ref,
                         xn_scr, z_scr, ubuf, yc_scr, c_scr, n_scr, m_scr):
    chunk = pl.program_id(1)
    L = x_ref.shape[0]

    @pl.when(chunk == 0)
    def _():
        ubuf[0:SUBLANES, :] = jnp.zeros((SUBLANES, D_CONV), F32)
        c_scr[...] = jnp.zeros(c_scr.shape, F32)
        n_scr[...] = jnp.zeros(n_scr.shape, F32)
        m_scr[...] = jnp.zeros(m_scr.shape, F32)

    gain = g1_ref[...]
    for r in range(0, L, NORM_ROWS):
        xn_scr[r:r + NORM_ROWS, :] = _rmsnorm_piece(x_ref[r:r + NORM_ROWS, :], gain, BF16)
    xn = xn_scr[...]
    gc = _gate_dot(xn, wgate_ref)
    gr = gc.T[0:SUBLANES, :]

    def project(j):
        cols = slice(j * W_SLAB, (j + 1) * W_SLAB)
        z_scr[:, cols] = jnp.dot(xn_scr[...], w_ref[:, cols], preferred_element_type=F32)

    piece = lambda j: functools.partial(project, j)
    skip = lambda: None
    sumsq = []

    def then_conv(j, part):
        def emit():
            project(j)
            sumsq.append(_prompt_conv_part(z_scr, part, convw_ref, ubuf, yc_scr))
        return emit

    project(0)
    wg_bf_ref[0] = wg_ref[...].astype(BF16)
    wout_bf_ref[...] = wout_ref[...].astype(BF16)
    project(1)
    wu_bf_ref[0] = wu_ref[...].astype(BF16)
    fillers = [piece(2), piece(3), piece(4), piece(6),
               piece(5), then_conv(7, 0), then_conv(8, 1),
               piece(9), then_conv(10, 2), then_conv(11, 3),
               skip, skip]
    _prompt_mlstm(z_scr, gc, gr, mlg_ref, bias_row_ref, bias_col_ref, y_ref,
                  c_scr, n_scr, m_scr, fillers)
    _prompt_conv_finish(sum(sumsq[1:], sumsq[0]), yc_scr, convg_ref, y_ref)

    @pl.when(chunk == pl.num_programs(1) - 1)
    def _():
        convs_ref[0] = ubuf[SUBLANES - 2:SUBLANES, :]
        c_out_ref[0] = c_scr[...]
        for h in range(N_HEADS):
            n_out_ref[0, h:h + 1, :] = n_scr[h]
            m_out_ref[0, h:h + 1, :] = m_scr[h, 0:1, :]


def _fused_prompt(x2d, g1, w_bf, wgate, batch, seq, conv_w, conv_g, ml_g, bias_row, bias_col,
                  wg, wu, w_out):
    L = PROMPT_CHUNK
    nc = seq // L
    const = lambda b, c: (0, 0)
    wout_rows = D_MODEL // (batch * nc)
    assert wout_rows * batch * nc == D_MODEL and wout_rows % (2 * SUBLANES) == 0
    wout_piece = pl.BlockSpec((wout_rows, D_MODEL), lambda b, c: (b * nc + c, 0))
    n_pieces = D_FF // CAST_PIECE
    assert batch * nc >= n_pieces, "not enough grid steps to cast the FFN weights"
    per_tile = FFN_TILE // CAST_PIECE
    piece_of = lambda b, c: jnp.minimum(b * nc + c, n_pieces - 1)
    tiled = lambda b, c: (piece_of(b, c) // per_tile, 0, piece_of(b, c) % per_tile)
    return pl.pallas_call(
        _fused_prompt_kernel,
        grid=(batch, nc),
        in_specs=[
            pl.BlockSpec((L, D_MODEL), lambda b, c: (b * nc + c, 0)),
            pl.BlockSpec((1, D_MODEL), const),
            pl.BlockSpec((D_MODEL, N_MAIN), const, pipeline_mode=pl.Buffered(1)),
            pl.BlockSpec((LANES, D_MODEL), const),
            pl.BlockSpec((CONV_W, D_CONV), const),
            pl.BlockSpec((1, D_CONV), const),
            pl.BlockSpec((1, N_HEADS * D_V), const),
            pl.BlockSpec((1, LANES), const),
            pl.BlockSpec((SUBLANES, LANES), const),
            pl.BlockSpec((D_MODEL, CAST_PIECE), lambda b, c: (0, piece_of(b, c))),
            pl.BlockSpec((D_MODEL, CAST_PIECE), lambda b, c: (0, piece_of(b, c))),
            wout_piece,
        ],
        out_specs=[
            pl.BlockSpec((L, D_MODEL), lambda b, c: (b * nc + c, 0)),
            pl.BlockSpec((1, CONV_W - 1, D_CONV), lambda b, c: (b, 0, 0)),
            pl.BlockSpec((1, N_HEADS, D_K, D_V), lambda b, c: (b, 0, 0, 0)),
            pl.BlockSpec((1, N_HEADS, D_K), lambda b, c: (b, 0, 0)),
            pl.BlockSpec((1, N_HEADS, LANES), lambda b, c: (b, 0, 0)),
            pl.BlockSpec((1, D_MODEL, CAST_PIECE), tiled),
            pl.BlockSpec((1, D_MODEL, CAST_PIECE), tiled),
            wout_piece,
        ],
        out_shape=[
            jax.ShapeDtypeStruct((batch * seq, D_MODEL), BF16),
            jax.ShapeDtypeStruct((batch, CONV_W - 1, D_CONV), F32),
            jax.ShapeDtypeStruct((batch, N_HEADS, D_K, D_V), F32),
            jax.ShapeDtypeStruct((batch, N_HEADS, D_K), F32),
            jax.ShapeDtypeStruct((batch, N_HEADS, LANES), F32),
            jax.ShapeDtypeStruct((D_FF // FFN_TILE, D_MODEL, FFN_TILE), BF16),
            jax.ShapeDtypeStruct((D_FF // FFN_TILE, D_MODEL, FFN_TILE), BF16),
            jax.ShapeDtypeStruct((D_MODEL, D_MODEL), BF16),
        ],
        scratch_shapes=[
            pltpu.VMEM((L, D_MODEL), BF16),
            pltpu.VMEM((L, N_MAIN), F32),
            pltpu.VMEM((L + 2 * SUBLANES, D_CONV), F32),
            pltpu.VMEM((L, D_CONV), F32),
            pltpu.VMEM((N_HEADS, D_K, D_V), F32),
            pltpu.VMEM((N_HEADS, 1, D_K), F32),
            pltpu.VMEM((N_HEADS, SUBLANES, LANES), F32),
        ],
        compiler_params=_compiler_params("fused_prompt", ("arbitrary", "arbitrary")),
        name="fused_prompt",
    )(x2d, g1, w_bf, wgate, conv_w, conv_g, ml_g, bias_row, bias_col, wg, wu, w_out)


def _mix_sample_kernel(z_ref, gcol_ref, grow_ref, mcol_ref, mrow_ref,
                       convs_in_ref, c_in_ref, n_in_ref,
                       convw_ref, convg_ref, mlg_ref, bias_row_ref, bias_col_ref,
                       y_ref, convs_ref, c_out_ref, n_out_ref, m_out_ref,
                       pad_scr, conv_scr, *, seq):
    L = z_ref.shape[0]
    group = L // seq
    stride = seq + SUBLANES

    for g in range(group):
        rows = slice(g * seq, (g + 1) * seq)
        u = _conv_operand(z_ref, CG, rows) * _conv_operand(z_ref, XT, rows)
        base = g * stride
        pad_scr[base + SUBLANES - 2:base + SUBLANES, :] = convs_in_ref[g]
        pad_scr[base + SUBLANES:base + SUBLANES + seq, :] = u
        conv_scr[g * seq:(g + 1) * seq, :] = (
            pad_scr[base + SUBLANES - 2:base + SUBLANES - 2 + seq, :] * convw_ref[0:1, :]
            + pad_scr[base + SUBLANES - 1:base + SUBLANES - 1 + seq, :] * convw_ref[1:2, :]
            + u * convw_ref[2:3, :])
        convs_ref[g] = pad_scr[base + seq + SUBLANES - 2:base + seq + SUBLANES, :]
    y_ref[:, 0:D_CONV] = _conv_norm(_conv_operand(z_ref, BG), conv_scr[...],
                                    convg_ref[...]).astype(y_ref.dtype)

    pre_c = gcol_ref[...] + bias_row_ref[...]
    pre_r = grow_ref[0:SUBLANES, :] + bias_col_ref[:, 0:1]
    row_id = lax.broadcasted_iota(jnp.int32, (L, L), 0)
    col_id = lax.broadcasted_iota(jnp.int32, (L, L), 1)
    same = (row_id // seq) == (col_id // seq)
    causal = same & (col_id <= row_id)
    lf_c = _log_sigmoid(pre_c)
    lf_r = _log_sigmoid(pre_r)
    b_c, b_r = _masked_sums(causal, lf_c, lf_r)
    tot_c, tot_r = _masked_sums(same, lf_c, lf_r)
    lane_seq = lax.broadcasted_iota(jnp.int32, (1, L), 1) // seq

    for h in range(N_HEADS):
        li_r = pre_r[h:h + 1, :]
        li_c = pre_c[:, h:h + 1]
        bc = b_c[:, N_HEADS + h:N_HEADS + h + 1]
        br = b_r[N_HEADS + h:N_HEADS + h + 1, :]
        bl_c = tot_c[:, N_HEADS + h:N_HEADS + h + 1]
        bl_r = tot_r[N_HEADS + h:N_HEADS + h + 1, :]
        m0_c = mcol_ref[:, h:h + 1]
        m0_r = mrow_ref[h:h + 1, :]
        qf = z_ref[:, _cols_q(h)] * (D_K ** -0.5)
        q_bf = qf.astype(BF16)
        kf = z_ref[:, _cols_k(h)]
        v_bf = z_ref[:, _cols_v(h)].astype(BF16)
        s, a_inter, m_t = _intra(q_bf, kf.astype(BF16), bc, br, li_r, m0_c, causal)

        qc_rows, qn_rows = [], []
        for g in range(group):
            rows = slice(g * seq, (g + 1) * seq)
            qc_rows.append(jnp.dot(qf[rows], c_in_ref[g, h], preferred_element_type=F32))
            qn_rows.append(jnp.sum(qf[rows] * n_in_ref[g, h:h + 1, :], axis=1, keepdims=True))
        num = (a_inter * jnp.concatenate(qc_rows, axis=0)
               + jnp.dot(s.astype(BF16), v_bf, preferred_element_type=F32))
        den = (a_inter * jnp.concatenate(qn_rows, axis=0)
               + jnp.sum(s, axis=1, keepdims=True))
        y_ref[:, D_CONV + h * D_V:D_CONV + (h + 1) * D_V] = _head_output(
            num, den, m_t, z_ref[:, _cols_o(h)],
            mlg_ref[:, h * D_V:(h + 1) * D_V]).astype(y_ref.dtype)

        g_r = bl_r - br + li_r
        gmax_c = jnp.max(jnp.where(same, g_r, -jnp.inf), axis=1, keepdims=True)
        gmax_r = jnp.max(jnp.where(same, gmax_c, -jnp.inf), axis=0, keepdims=True)
        m_new_c = jnp.maximum(bl_c + m0_c, gmax_c)
        m_new_r = jnp.maximum(bl_r + m0_r, gmax_r)
        decay_r = jnp.exp(bl_r + m0_r - m_new_r)
        kw = kf * jnp.exp(bl_c - bc + li_c - m_new_c)
        kw_t = kw.T
        for g in range(group):
            decay = decay_r[:, g * seq:g * seq + 1]
            kw_g = jnp.where(lane_seq == g, kw_t, 0.0).astype(BF16)
            c_out_ref[g, h] = decay * c_in_ref[g, h] + jnp.dot(
                kw_g, v_bf, preferred_element_type=F32)
            n_out_ref[g, h:h + 1, :] = (decay * n_in_ref[g, h:h + 1, :]
                                        + jnp.sum(kw[g * seq:(g + 1) * seq], axis=0, keepdims=True))
        m_out_ref[h:h + 1, :] = m_new_r
    m_out_ref[N_HEADS:, :] = jnp.zeros((SUBLANES - N_HEADS, L), F32)


def _mix_sample(z, gcol, grow, mcol, mrow, convs, c_state, n_state, batch, seq,
                conv_w, conv_g, ml_g, bias_row, bias_col):
    group = SAMPLE_GROUP
    L = group * seq
    const = lambda i: (0, 0)
    return pl.pallas_call(
        functools.partial(_mix_sample_kernel, seq=seq),
        grid=(batch // group,),
        in_specs=[
            pl.BlockSpec((L, N_MAIN), lambda i: (i, 0)),
            pl.BlockSpec((L, LANES), lambda i: (i, 0)),
            pl.BlockSpec((GATE_ROWS, L), lambda i: (0, i)),
            pl.BlockSpec((L, LANES), lambda i: (i, 0)),
            pl.BlockSpec((SUBLANES, L), lambda i: (0, i)),
            pl.BlockSpec((group, CONV_W - 1, D_CONV), lambda i: (i, 0, 0)),
            pl.BlockSpec((group, N_HEADS, D_K, D_V), lambda i: (i, 0, 0, 0)),
            pl.BlockSpec((group, N_HEADS, D_K), lambda i: (i, 0, 0)),
            pl.BlockSpec((CONV_W, D_CONV), const),
            pl.BlockSpec((1, D_CONV), const),
            pl.BlockSpec((1, N_HEADS * D_V), const),
            pl.BlockSpec((1, LANES), const),
            pl.BlockSpec((SUBLANES, LANES), const),
        ],
        out_specs=[
            pl.BlockSpec((L, D_MODEL), lambda i: (i, 0)),
            pl.BlockSpec((group, CONV_W - 1, D_CONV), lambda i: (i, 0, 0)),
            pl.BlockSpec((group, N_HEADS, D_K, D_V), lambda i: (i, 0, 0, 0)),
            pl.BlockSpec((group, N_HEADS, D_K), lambda i: (i, 0, 0)),
            pl.BlockSpec((SUBLANES, L), lambda i: (0, i)),
        ],
        out_shape=[
            jax.ShapeDtypeStruct((batch * seq, D_MODEL), BF16),
            jax.ShapeDtypeStruct((batch, CONV_W - 1, D_CONV), F32),
            jax.ShapeDtypeStruct((batch, N_HEADS, D_K, D_V), F32),
            jax.ShapeDtypeStruct((batch, N_HEADS, D_K), F32),
            jax.ShapeDtypeStruct((SUBLANES, batch * seq), F32),
        ],
        scratch_shapes=[
            pltpu.VMEM((group * (seq + SUBLANES), D_CONV), F32),
            pltpu.VMEM((L, D_CONV), F32),
        ],
        compiler_params=_compiler_params("mix_sample", ("parallel",)),
        name="mix_sample",
    )(z, gcol, grow, mcol, mrow, convs, c_state, n_state,
      conv_w, conv_g, ml_g, bias_row, bias_col)


def _outproj_kernel(y_ref, w_ref, x_ref, o_ref):
    o_ref[...] = x_ref[...] + jnp.dot(y_ref[...], w_ref[...], preferred_element_type=F32)


def _outproj(y, w_out_bf, x2d):
    tokens = x2d.shape[0]
    tm = min(1024, tokens)
    return pl.pallas_call(
        _outproj_kernel,
        grid=(tokens // tm,),
        in_specs=[
            pl.BlockSpec((tm, D_MODEL), lambda i: (i, 0)),
            pl.BlockSpec((D_MODEL, D_MODEL), lambda i: (0, 0), pipeline_mode=pl.Buffered(1)),
            pl.BlockSpec((tm, D_MODEL), lambda i: (i, 0)),
        ],
        out_specs=pl.BlockSpec((tm, D_MODEL), lambda i: (i, 0)),
        out_shape=jax.ShapeDtypeStruct((tokens, D_MODEL), F32),
        compiler_params=_compiler_params("outproj", ("parallel",)),
        name="outproj",
    )(y, w_out_bf, x2d)


def _ffn_kernel(x_ref, g2_ref, wg_ref, wu_ref, wd_ref, gf_ref, o_ref, hn_ref, *, final_norm):
    j = pl.program_id(1)
    last = pl.num_programs(1) - 1
    tm = o_ref.shape[0]
    half_f = wd_ref.shape[0] // 2
    f_halves = (slice(0, half_f), slice(half_f, 2 * half_f))
    whole = slice(0, tm)
    row_halves = (slice(0, tm // 2), slice(tm // 2, tm))

    def gate_up(rows):
        hn = hn_ref[rows, :]
        return [(jnp.dot(hn, wg_ref[0, :, cols], preferred_element_type=F32),
                 jnp.dot(hn, wu_ref[0, :, cols], preferred_element_type=F32))
                for cols in f_halves]

    def down_into(rows, pre):
        down = None
        for (gate, up), frows in zip(pre, f_halves):
            act = ((gate * jax.nn.sigmoid(gate)) * up).astype(BF16)
            part = jnp.dot(act, wd_ref[frows, :].astype(BF16), preferred_element_type=F32)
            down = part if down is None else down + part
        o_ref[rows, :] += down

    def norm_in(rows):
        gain = g2_ref[...]
        for r in range(rows.start, rows.stop, NORM_ROWS):
            x = x_ref[r:r + NORM_ROWS, :]
            o_ref[r:r + NORM_ROWS, :] = x
            hn_ref[r:r + NORM_ROWS, :] = _rmsnorm_piece(x, gain, BF16)

    def norm_out(rows):
        gain = gf_ref[...]
        for r in range(rows.start, rows.stop, NORM_ROWS):
            o_ref[r:r + NORM_ROWS, :] = _rmsnorm_piece(o_ref[r:r + NORM_ROWS, :], gain, F32)

    @pl.when(j == 0)
    def _():
        norm_in(row_halves[0])
        pre = gate_up(row_halves[0])
        norm_in(row_halves[1])
        down_into(row_halves[0], pre)
        down_into(row_halves[1], gate_up(row_halves[1]))

    @pl.when((j > 0) & ((j < last) | (not final_norm)))
    def _():
        down_into(whole, gate_up(whole))

    if final_norm:
        @pl.when(j == last)
        def _():
            down_into(row_halves[0], gate_up(row_halves[0]))
            pre = gate_up(row_halves[1])
            norm_out(row_halves[0])
            down_into(row_halves[1], pre)
            norm_out(row_halves[1])


def _ffn(x1, g2, wg_bf, wu_bf, wd, gf, final_norm):
    tokens = x1.shape[0]
    tm = min(1024, tokens)
    tf = FFN_TILE
    return pl.pallas_call(
        functools.partial(_ffn_kernel, final_norm=final_norm),
        grid=(tokens // tm, D_FF // tf),
        in_specs=[
            pl.BlockSpec((tm, D_MODEL), lambda i, j: (i, 0)),
            pl.BlockSpec((1, D_MODEL), lambda i, j: (0, 0)),
            pl.BlockSpec((1, D_MODEL, tf), lambda i, j: (j, 0, 0)),
            pl.BlockSpec((1, D_MODEL, tf), lambda i, j: (j, 0, 0)),
            pl.BlockSpec((tf, D_MODEL), lambda i, j: (j, 0)),
            pl.BlockSpec((1, D_MODEL), lambda i, j: (0, 0)),
        ],
        out_specs=pl.BlockSpec((tm, D_MODEL), lambda i, j: (i, 0)),
        out_shape=jax.ShapeDtypeStruct((tokens, D_MODEL), F32),
        scratch_shapes=[pltpu.VMEM((tm, D_MODEL), BF16)],
        compiler_params=_compiler_params("ffn", ("parallel", "arbitrary")),
        name="ffn",
    )(x1, g2, wg_bf, wu_bf, wd, gf)


def _gate_params(w_t, b_i, b_f):
    wgate = jnp.pad(w_t[N_MAIN:, :], ((0, LANES - 2 * N_HEADS), (0, 0)))
    bias = jnp.concatenate([b_i, b_f]).astype(F32)
    bias_row = jnp.pad(bias, (0, LANES - 2 * N_HEADS))[None, :]
    bias_col = jnp.broadcast_to(bias[:, None], (SUBLANES, LANES))
    return wgate, bias_row, bias_col


def _tail(x2d, y, w_out, g2, ffn_weights, gf, final_norm):
    x1 = _outproj(y, w_out, x2d)
    return _ffn(x1, g2[None, :], *ffn_weights, gf[None, :], final_norm)


def _stack(states, k):
    if len(states) == 1:
        return states[0][k][None]
    return jnp.stack([st[k] for st in states])


def kernel(x_prompt, x_sample, state_conv, state_mlstm_C, state_mlstm_n, state_mlstm_m,
           norm1_g, w_in, b_igate, b_fgate, conv_w, conv_out_g, mlstm_out_g, w_out,
           norm2_g, w_ffn_gate, w_ffn_up, w_ffn_down, final_norm_g):
    depth = w_in.shape[0]
    bp, sp, _ = x_prompt.shape
    bs, ss, _ = x_sample.shape
    hp = x_prompt.reshape(bp * sp, D_MODEL)
    hs = x_sample.reshape(bs * ss, D_MODEL)
    p_states, s_states = [], []
    for l in range(depth):
        final_norm = l == depth - 1
        w_t = w_in[l].T
        wgate, bias_row, bias_col = _gate_params(w_t, b_igate[l], b_fgate[l])
        w_bf = _cast_main_columns(w_t)
        g1 = norm1_g[l][None, :]
        conv_g = conv_out_g[l][None, :]
        ml_g = mlstm_out_g[l][None, :]

        y, cb, c1, n1, m1, wg_bf, wu_bf, w_out_bf = _fused_prompt(
            hp, g1, w_bf, wgate, bp, sp, conv_w[l], conv_g, ml_g, bias_row, bias_col,
            w_ffn_gate[l], w_ffn_up[l], w_out[l])
        tail = (w_out_bf, norm2_g[l], (wg_bf, wu_bf, w_ffn_down[l]), final_norm_g, final_norm)
        hp = _tail(hp, y, *tail)
        p_states.append((cb, c1, n1, m1[:, :, 0]))

        z, gcol, grow = _inproj(hs, g1, w_bf, wgate)
        m_tok = jnp.repeat(state_mlstm_m[l], ss, axis=0)
        mcol = jnp.pad(m_tok, ((0, 0), (0, LANES - N_HEADS)))
        mrow = jnp.pad(m_tok.T, ((0, SUBLANES - N_HEADS), (0, 0)))
        y, cb, c1, n1, m_row = _mix_sample(z, gcol, grow, mcol, mrow, state_conv[l],
                                           state_mlstm_C[l], state_mlstm_n[l], bs, ss,
                                           conv_w[l], conv_g, ml_g, bias_row, bias_col)
        hs = _tail(hs, y, *tail)
        s_states.append((cb, c1, n1, m_row[:N_HEADS, ::ss].T))

    return (hp.reshape(bp, sp, D_MODEL), hs.reshape(bs, ss, D_MODEL),
            _stack(p_states, 0), _stack(p_states, 1), _stack(p_states, 2), _stack(p_states, 3),
            _stack(s_states, 0), _stack(s_states, 1), _stack(s_states, 2), _stack(s_states, 3))
```

```python
import functools

import jax
import jax.numpy as jnp
from jax import lax
from jax.experimental import pallas as pl
from jax.experimental.pallas import tpu as pltpu

F32 = jnp.float32
BF16 = jnp.bfloat16

D_MODEL = 2048
D_CONV = 1024
CONV_W = 3
N_HEADS = 4
D_K = 128
D_V = 256
D_FF = 5632
EPS = 1e-6

LANES = 128
SUBLANES = 8
GATE_ROWS = 16
MIB = 1024 * 1024

PROMPT_CHUNK = 256
SAMPLE_GROUP = 16
NORM_ROWS = 64
NORM_GROUP = 4
W_SLAB = 512
FFN_TILE = 512
CAST_PIECE = 256

VMEM_MIB = {"cast_w_in": 62, "inproj": 62, "fused_prompt": 62, "mix_sample": 62,
            "outproj": 62, "ffn": 62}

SRC_Q = 3 * D_CONV
SRC_K = SRC_Q + N_HEADS * D_K
SRC_V = SRC_K + N_HEADS * D_K
SRC_O = SRC_V + N_HEADS * D_V
N_MAIN = SRC_O + N_HEADS * D_V

HEAD_COLS = 2 * D_K + 2 * D_V
N_MLSTM = N_HEADS * HEAD_COLS
CONV_PARTS = 4
CONV_PART = D_CONV // CONV_PARTS
BG, CG, XT = 0, 1, 2


def _cols_conv(kind, part):
    start = N_MLSTM + (3 * part + kind) * CONV_PART
    return slice(start, start + CONV_PART)


def _conv_operand(z, kind, rows=slice(None)):
    return jnp.concatenate([z[rows, _cols_conv(kind, p)] for p in range(CONV_PARTS)], axis=1)


def _cols_q(h):
    return slice(h * HEAD_COLS, h * HEAD_COLS + D_K)


def _cols_k(h):
    return slice(h * HEAD_COLS + D_K, h * HEAD_COLS + 2 * D_K)


def _cols_v(h):
    return slice(h * HEAD_COLS + 2 * D_K, h * HEAD_COLS + 2 * D_K + D_V)


def _cols_o(h):
    return slice(h * HEAD_COLS + 2 * D_K + D_V, (h + 1) * HEAD_COLS)


def _source_lane_block(j):
    per_head = HEAD_COLS // LANES
    h, r = j // per_head, j % per_head
    v_blocks = D_V // LANES
    mlstm_src = jnp.where(
        r == 0, SRC_Q // LANES + h,
        jnp.where(r == 1, SRC_K // LANES + h,
                  jnp.where(r < 2 + v_blocks,
                            SRC_V // LANES + v_blocks * h + (r - 2),
                            SRC_O // LANES + v_blocks * h + (r - 2 - v_blocks))))
    per_kind = CONV_PART // LANES
    c = j - N_MLSTM // LANES
    part, r = c // (3 * per_kind), c % (3 * per_kind)
    conv_src = (r // per_kind) * (D_CONV // LANES) + part * per_kind + r % per_kind
    return jnp.where(j < N_MLSTM // LANES, mlstm_src, conv_src)


def _compiler_params(name, semantics):
    return pltpu.CompilerParams(dimension_semantics=semantics,
                                vmem_limit_bytes=VMEM_MIB[name] * MIB)


def _rmsnorm_piece(x, gain, dtype):
    ms = jnp.mean(x * x, axis=-1, keepdims=True)
    return ((x * lax.rsqrt(ms + EPS)) * gain).astype(dtype)


def _rmsnorm_rows(src_ref, gain, dst_ref, copy_ref=None):
    rows = src_ref.shape[0]
    chunk = min(NORM_ROWS, rows)

    def body(i, carry):
        r = pl.multiple_of(i * chunk, chunk)
        x = src_ref[pl.ds(r, chunk), :].astype(F32)
        if copy_ref is not None:
            copy_ref[pl.ds(r, chunk), :] = x
        dst_ref[pl.ds(r, chunk), :] = _rmsnorm_piece(x, gain, dst_ref.dtype)
        return carry

    lax.fori_loop(0, rows // chunk, body, 0, unroll=2)


def _rmsnorm_rows_inplace(ref, gain):
    rows = ref.shape[0]
    chunk = min(NORM_ROWS, rows)
    group = min(NORM_GROUP, rows // chunk)

    def body(i, carry):
        starts = [pl.multiple_of((i * group + k) * chunk, chunk) for k in range(group)]
        scales = []
        for r in starts:
            x = ref[pl.ds(r, chunk), :]
            scales.append(lax.rsqrt(jnp.mean(x * x, axis=-1, keepdims=True) + EPS))
        for r, scale in zip(starts, scales):
            ref[pl.ds(r, chunk), :] = (ref[pl.ds(r, chunk), :] * scale) * gain
        return carry

    lax.fori_loop(0, rows // (chunk * group), body, 0)


def _cast_kernel(*refs):
    *w_refs, o_ref = refs
    for r, w_ref in enumerate(w_refs):
        o_ref[:, r * LANES:(r + 1) * LANES] = w_ref[...].T.astype(o_ref.dtype)


def _cast_main_columns(w_t):
    per_slab = W_SLAB // LANES
    in_specs = [
        pl.BlockSpec((LANES, D_MODEL),
                     lambda j, r=r: (_source_lane_block(per_slab * j + r), 0))
        for r in range(per_slab)
    ]
    return pl.pallas_call(
        _cast_kernel,
        grid=(N_MAIN // W_SLAB,),
        in_specs=in_specs,
        out_specs=pl.BlockSpec((D_MODEL, W_SLAB), lambda j: (0, j)),
        out_shape=jax.ShapeDtypeStruct((D_MODEL, N_MAIN), BF16),
        compiler_params=_compiler_params("cast_w_in", ("parallel",)),
        name="cast_w_in",
    )(*([w_t] * per_slab))


def _gate_dot(xn, wgate_t_ref):
    return lax.dot_general(xn, wgate_t_ref[...].astype(BF16), (((1,), (1,)), ((), ())),
                           preferred_element_type=F32)


def _inproj_kernel(x_ref, g_ref, w_ref, wgate_ref, z_ref, gcol_ref, grow_ref, xn_ref):
    @pl.when(pl.program_id(1) == 0)
    def _():
        _rmsnorm_rows(x_ref, g_ref[...], xn_ref)
        gc = _gate_dot(xn_ref[...], wgate_ref)
        gcol_ref[...] = gc
        grow_ref[...] = gc.T[0:GATE_ROWS, :]

    z_ref[...] = jnp.dot(xn_ref[...], w_ref[...], preferred_element_type=F32)


def _inproj(x2d, g1, w_bf, wgate):
    tokens = x2d.shape[0]
    tm = min(1024, tokens)
    tn = W_SLAB
    assert tokens % tm == 0 and tm % NORM_ROWS == 0, tokens
    return pl.pallas_call(
        _inproj_kernel,
        grid=(tokens // tm, N_MAIN // tn),
        in_specs=[
            pl.BlockSpec((tm, D_MODEL), lambda i, j: (i, 0)),
            pl.BlockSpec((1, D_MODEL), lambda i, j: (0, 0)),
            pl.BlockSpec((D_MODEL, tn), lambda i, j: (0, j)),
            pl.BlockSpec((LANES, D_MODEL), lambda i, j: (0, 0)),
        ],
        out_specs=[
            pl.BlockSpec((tm, tn), lambda i, j: (i, j)),
            pl.BlockSpec((tm, LANES), lambda i, j: (i, 0)),
            pl.BlockSpec((GATE_ROWS, tm), lambda i, j: (0, i)),
        ],
        out_shape=[
            jax.ShapeDtypeStruct((tokens, N_MAIN), F32),
            jax.ShapeDtypeStruct((tokens, LANES), F32),
            jax.ShapeDtypeStruct((GATE_ROWS, tokens), F32),
        ],
        scratch_shapes=[pltpu.VMEM((tm, D_MODEL), BF16)],
        compiler_params=_compiler_params("inproj", ("parallel", "arbitrary")),
        name="inproj",
    )(x2d, g1, w_bf, wgate)


def _log_sigmoid(x):
    return jnp.minimum(x, 0.0) - jnp.log1p(jnp.exp(-jnp.abs(x)))


def _split3(x):
    hi = x.astype(BF16).astype(F32)
    mid = (x - hi).astype(BF16).astype(F32)
    lo = ((x - hi) - mid).astype(BF16).astype(F32)
    return hi, mid, lo


def _masked_sums(mask, lf_col, lf_row):
    m_bf = mask.astype(BF16)
    n, k = lf_col.shape[1], lf_row.shape[0]
    col_terms = jnp.concatenate(_split3(lf_col), axis=1).astype(BF16)
    row_terms = jnp.concatenate(_split3(lf_row) + (jnp.zeros_like(lf_row),),
                                axis=0).astype(BF16)
    cols = jnp.dot(m_bf, col_terms, preferred_element_type=F32)
    rows = lax.dot_general(row_terms, m_bf, (((1,), (1,)), ((), ())),
                           preferred_element_type=F32)
    return (cols[:, 0:n] + cols[:, n:2 * n] + cols[:, 2 * n:3 * n],
            rows[0:k] + rows[k:2 * k] + rows[2 * k:3 * k])


def _intra(q_bf, k_bf, b_col, b_row, li_row, m0_col, mask):
    dmat = jnp.where(mask, b_col - b_row + li_row, -jnp.inf)
    inter = b_col + m0_col
    m_t = jnp.maximum(inter, jnp.max(dmat, axis=1, keepdims=True))
    w = jnp.exp(dmat - m_t)
    a_inter = jnp.exp(inter - m_t)
    qk = lax.dot_general(q_bf, k_bf, (((1,), (1,)), ((), ())), preferred_element_type=F32)
    return qk * w, a_inter, m_t


def _head_output(num, den, m_t, o_gate, gain):
    denom = jnp.maximum(jnp.abs(den), jnp.exp(-m_t))
    h = num * (1.0 / denom)
    hn = (h * lax.rsqrt(jnp.mean(h * h, axis=1, keepdims=True) + EPS)) * gain
    return jax.nn.sigmoid(o_gate) * hn


def _conv_norm(bg, conv, gain):
    yc = bg * conv
    return (yc * lax.rsqrt(jnp.mean(yc * yc, axis=1, keepdims=True) + EPS)) * gain


def _prompt_conv_part(z, part, convw_ref, ubuf, yc_scr):
    L = z.shape[0]
    ch = slice(part * CONV_PART, (part + 1) * CONV_PART)
    u = z[:, _cols_conv(CG, part)] * z[:, _cols_conv(XT, part)]
    ubuf[SUBLANES:SUBLANES + L, ch] = u
    conv = (ubuf[SUBLANES - 2:SUBLANES - 2 + L, ch] * convw_ref[0:1, ch]
            + ubuf[SUBLANES - 1:SUBLANES - 1 + L, ch] * convw_ref[1:2, ch]
            + u * convw_ref[2:3, ch])
    ubuf[0:SUBLANES, ch] = ubuf[L:L + SUBLANES, ch]
    yc = z[:, _cols_conv(BG, part)] * conv
    yc_scr[:, ch] = yc
    return jnp.sum(yc * yc, axis=1, keepdims=True)


def _prompt_conv_finish(sumsq, yc_scr, convg_ref, y_ref):
    scale = lax.rsqrt(sumsq * (1.0 / D_CONV) + EPS)
    y_ref[:, 0:D_CONV] = ((yc_scr[...] * scale) * convg_ref[...]).astype(y_ref.dtype)


def _prompt_mlstm(z, gc, gr, mlg_ref, bias_row_ref, bias_col_ref, y_ref, c_scr, n_scr, m_scr,
                  fillers):
    L = z.shape[0]
    fillers = list(fillers)

    def fill():
        if fillers:
            fillers.pop(0)()

    pre_c = gc + bias_row_ref[...]
    pre_r = gr + bias_col_ref[:, 0:1]
    row_id = lax.broadcasted_iota(jnp.int32, (L, L), 0)
    col_id = lax.broadcasted_iota(jnp.int32, (L, L), 1)
    causal = col_id <= row_id
    b_c, b_r = _masked_sums(causal, _log_sigmoid(pre_c), _log_sigmoid(pre_r))

    def scores(h):
        li_r = pre_r[h:h + 1, :]
        bc = b_c[:, N_HEADS + h:N_HEADS + h + 1]
        br = b_r[N_HEADS + h:N_HEADS + h + 1, :]
        m0 = m_scr[h, 0:1, 0:1]
        qf = z[:, _cols_q(h)] * (D_K ** -0.5)
        q_bf = qf.astype(BF16)
        kf = z[:, _cols_k(h)]
        v_bf = z[:, _cols_v(h)].astype(BF16)
        s, a_inter, m_t = _intra(q_bf, kf.astype(BF16), bc, br, li_r, m0, causal)
        return li_r, bc, br, m0, qf, q_bf, kf, v_bf, s, a_inter, m_t

    def finish(h, staged):
        li_r, bc, br, m0, qf, q_bf, kf, v_bf, s, a_inter, m_t = staged
        li_c = pre_c[:, h:h + 1]
        c_old = c_scr[h]
        n_old = n_scr[h]
        num = (a_inter * jnp.dot(q_bf, c_old.astype(BF16), preferred_element_type=F32)
               + jnp.dot(s.astype(BF16), v_bf, preferred_element_type=F32))
        den = (a_inter * jnp.sum(qf * n_old, axis=1, keepdims=True)
               + jnp.sum(s, axis=1, keepdims=True))
        fill()
        y_ref[:, D_CONV + h * D_V:D_CONV + (h + 1) * D_V] = _head_output(
            num, den, m_t, z[:, _cols_o(h)],
            mlg_ref[:, h * D_V:(h + 1) * D_V]).astype(y_ref.dtype)

        b_last = br[:, L - 1:L]
        m_new = jnp.maximum(b_last + m0,
                            jnp.max(b_last - br + li_r, axis=1, keepdims=True))
        decay = jnp.exp(b_last + m0 - m_new)
        kw = kf * jnp.exp(b_last - bc + li_c - m_new)
        c_scr[h] = decay * c_old + lax.dot_general(
            kw.astype(BF16), v_bf, (((0,), (0,)), ((), ())), preferred_element_type=F32)
        n_scr[h] = decay * n_old + jnp.sum(kw, axis=0, keepdims=True)
        m_scr[h] = jnp.broadcast_to(m_new, (SUBLANES, LANES))
        fill()

    staged = scores(0)
    fill()
    for h in range(N_HEADS):
        upcoming = None
        if h + 1 < N_HEADS:
            upcoming = scores(h + 1)
            fill()
        finish(h, staged)
        staged = upcoming
    while fillers:
        fill()


def _fused_prompt_kernel(x_ref, g1_ref, w_ref, wgate_ref, convw_ref, convg_ref, mlg_ref,
                         bias_row_ref, bias_col_ref, wg_ref, wu_ref, wout_ref,
                         y_ref, convs_ref, c_out_ref, n_out_ref, m_out_ref,
                         wgu_bf_ref, wout_bf_ref,
                         xn_scr, z_scr, ubuf, yc_scr, c_scr, n_scr, m_scr):
    chunk = pl.program_id(1)
    L = x_ref.shape[0]

    @pl.when(chunk == 0)
    def _():
        ubuf[0:SUBLANES, :] = jnp.zeros((SUBLANES, D_CONV), F32)
        c_scr[...] = jnp.zeros(c_scr.shape, F32)
        n_scr[...] = jnp.zeros(n_scr.shape, F32)
        m_scr[...] = jnp.zeros(m_scr.shape, F32)

    gain = g1_ref[...]
    for r in range(0, L, NORM_ROWS):
        xn_scr[r:r + NORM_ROWS, :] = _rmsnorm_piece(x_ref[r:r + NORM_ROWS, :], gain, BF16)
    xn = xn_scr[...]
    gc = _gate_dot(xn, wgate_ref)
    gr = gc.T[0:SUBLANES, :]

    def project(j):
        cols = slice(j * W_SLAB, (j + 1) * W_SLAB)
        z_scr[:, cols] = jnp.dot(xn_scr[...], w_ref[:, cols], preferred_element_type=F32)

    piece = lambda j: functools.partial(project, j)
    skip = lambda: None
    sumsq = []

    def then_conv(j, part):
        def emit():
            project(j)
            sumsq.append(_prompt_conv_part(z_scr, part, convw_ref, ubuf, yc_scr))
        return emit

    project(0)
    wgu_bf_ref[0, :, 0:CAST_PIECE] = wg_ref[...].astype(BF16)
    wout_bf_ref[...] = wout_ref[...].astype(BF16)
    project(1)
    wgu_bf_ref[0, :, CAST_PIECE:2 * CAST_PIECE] = wu_ref[...].astype(BF16)
    fillers = [piece(2), piece(3), piece(4), piece(6),
               piece(5), then_conv(7, 0), then_conv(8, 1),
               piece(9), then_conv(10, 2), then_conv(11, 3),
               skip, skip]
    _prompt_mlstm(z_scr, gc, gr, mlg_ref, bias_row_ref, bias_col_ref, y_ref,
                  c_scr, n_scr, m_scr, fillers)
    _prompt_conv_finish(sum(sumsq[1:], sumsq[0]), yc_scr, convg_ref, y_ref)

    @pl.when(chunk == pl.num_programs(1) - 1)
    def _():
        convs_ref[0] = ubuf[SUBLANES - 2:SUBLANES, :]
        c_out_ref[0] = c_scr[...]
        for h in range(N_HEADS):
            n_out_ref[0, h:h + 1, :] = n_scr[h]
            m_out_ref[0, h:h + 1, :] = m_scr[h, 0:1, :]


def _fused_prompt(x2d, g1, w_bf, wgate, batch, seq, conv_w, conv_g, ml_g, bias_row, bias_col,
                  wg, wu, w_out):
    L = PROMPT_CHUNK
    nc = seq // L
    assert nc * L == seq, seq
    const = lambda b, c: (0, 0)
    wout_rows = D_MODEL // (batch * nc)
    assert wout_rows * batch * nc == D_MODEL and wout_rows % (2 * SUBLANES) == 0
    wout_piece = pl.BlockSpec((wout_rows, D_MODEL), lambda b, c: (b * nc + c, 0))
    n_pieces = D_FF // CAST_PIECE
    assert batch * nc >= n_pieces, "not enough grid steps to cast the FFN weights"
    per_tile = FFN_TILE // CAST_PIECE
    piece_of = lambda b, c: jnp.minimum(b * nc + c, n_pieces - 1)
    tiled = lambda b, c: (piece_of(b, c) // per_tile, 0, piece_of(b, c) % per_tile)
    return pl.pallas_call(
        _fused_prompt_kernel,
        grid=(batch, nc),
        in_specs=[
            pl.BlockSpec((L, D_MODEL), lambda b, c: (b * nc + c, 0)),
            pl.BlockSpec((1, D_MODEL), const),
            pl.BlockSpec((D_MODEL, N_MAIN), const, pipeline_mode=pl.Buffered(1)),
            pl.BlockSpec((LANES, D_MODEL), const),
            pl.BlockSpec((CONV_W, D_CONV), const),
            pl.BlockSpec((1, D_CONV), const),
            pl.BlockSpec((1, N_HEADS * D_V), const),
            pl.BlockSpec((1, LANES), const),
            pl.BlockSpec((SUBLANES, LANES), const),
            pl.BlockSpec((D_MODEL, CAST_PIECE), lambda b, c: (0, piece_of(b, c))),
            pl.BlockSpec((D_MODEL, CAST_PIECE), lambda b, c: (0, piece_of(b, c))),
            wout_piece,
        ],
        out_specs=[
            pl.BlockSpec((L, D_MODEL), lambda b, c: (b * nc + c, 0)),
            pl.BlockSpec((1, CONV_W - 1, D_CONV), lambda b, c: (b, 0, 0)),
            pl.BlockSpec((1, N_HEADS, D_K, D_V), lambda b, c: (b, 0, 0, 0)),
            pl.BlockSpec((1, N_HEADS, D_K), lambda b, c: (b, 0, 0)),
            pl.BlockSpec((1, N_HEADS, LANES), lambda b, c: (b, 0, 0)),
            pl.BlockSpec((1, D_MODEL, 2 * CAST_PIECE), tiled),
            wout_piece,
        ],
        out_shape=[
            jax.ShapeDtypeStruct((batch * seq, D_MODEL), BF16),
            jax.ShapeDtypeStruct((batch, CONV_W - 1, D_CONV), F32),
            jax.ShapeDtypeStruct((batch, N_HEADS, D_K, D_V), F32),
            jax.ShapeDtypeStruct((batch, N_HEADS, D_K), F32),
            jax.ShapeDtypeStruct((batch, N_HEADS, LANES), F32),
            jax.ShapeDtypeStruct((D_FF // FFN_TILE, D_MODEL, 2 * FFN_TILE), BF16),
            jax.ShapeDtypeStruct((D_MODEL, D_MODEL), BF16),
        ],
        scratch_shapes=[
            pltpu.VMEM((L, D_MODEL), BF16),
            pltpu.VMEM((L, N_MAIN), F32),
            pltpu.VMEM((L + 2 * SUBLANES, D_CONV), F32),
            pltpu.VMEM((L, D_CONV), F32),
            pltpu.VMEM((N_HEADS, D_K, D_V), F32),
            pltpu.VMEM((N_HEADS, 1, D_K), F32),
            pltpu.VMEM((N_HEADS, SUBLANES, LANES), F32),
        ],
        compiler_params=_compiler_params("fused_prompt", ("arbitrary", "arbitrary")),
        name="fused_prompt",
    )(x2d, g1, w_bf, wgate, conv_w, conv_g, ml_g, bias_row, bias_col, wg, wu, w_out)


def _mix_sample_kernel(z_ref, gcol_ref, grow_ref, mcol_ref, mrow_ref,
                       convs_in_ref, c_in_ref, n_in_ref,
                       convw_ref, convg_ref, mlg_ref, bias_row_ref, bias_col_ref,
                       y_ref, convs_ref, c_out_ref, n_out_ref, m_out_ref,
                       pad_scr, conv_scr, *, seq):
    L = z_ref.shape[0]
    group = L // seq
    stride = seq + SUBLANES

    for g in range(group):
        rows = slice(g * seq, (g + 1) * seq)
        u = _conv_operand(z_ref, CG, rows) * _conv_operand(z_ref, XT, rows)
        base = g * stride
        pad_scr[base + SUBLANES - 2:base + SUBLANES, :] = convs_in_ref[g]
        pad_scr[base + SUBLANES:base + SUBLANES + seq, :] = u
        conv_scr[g * seq:(g + 1) * seq, :] = (
            pad_scr[base + SUBLANES - 2:base + SUBLANES - 2 + seq, :] * convw_ref[0:1, :]
            + pad_scr[base + SUBLANES - 1:base + SUBLANES - 1 + seq, :] * convw_ref[1:2, :]
            + u * convw_ref[2:3, :])
        convs_ref[g] = pad_scr[base + seq + SUBLANES - 2:base + seq + SUBLANES, :]
    y_ref[:, 0:D_CONV] = _conv_norm(_conv_operand(z_ref, BG), conv_scr[...],
                                    convg_ref[...]).astype(y_ref.dtype)

    pre_c = gcol_ref[...] + bias_row_ref[...]
    pre_r = grow_ref[0:SUBLANES, :] + bias_col_ref[:, 0:1]
    row_id = lax.broadcasted_iota(jnp.int32, (L, L), 0)
    col_id = lax.broadcasted_iota(jnp.int32, (L, L), 1)
    same = (row_id // seq) == (col_id // seq)
    causal = same & (col_id <= row_id)
    lf_c = _log_sigmoid(pre_c)
    lf_r = _log_sigmoid(pre_r)
    b_c, b_r = _masked_sums(causal, lf_c, lf_r)
    tot_c, tot_r = _masked_sums(same, lf_c, lf_r)
    lane_seq = lax.broadcasted_iota(jnp.int32, (1, L), 1) // seq

    for h in range(N_HEADS):
        li_r = pre_r[h:h + 1, :]
        li_c = pre_c[:, h:h + 1]
        bc = b_c[:, N_HEADS + h:N_HEADS + h + 1]
        br = b_r[N_HEADS + h:N_HEADS + h + 1, :]
        bl_c = tot_c[:, N_HEADS + h:N_HEADS + h + 1]
        bl_r = tot_r[N_HEADS + h:N_HEADS + h + 1, :]
        m0_c = mcol_ref[:, h:h + 1]
        m0_r = mrow_ref[h:h + 1, :]
        qf = z_ref[:, _cols_q(h)] * (D_K ** -0.5)
        q_bf = qf.astype(BF16)
        kf = z_ref[:, _cols_k(h)]
        v_bf = z_ref[:, _cols_v(h)].astype(BF16)
        s, a_inter, m_t = _intra(q_bf, kf.astype(BF16), bc, br, li_r, m0_c, causal)

        qc_rows, qn_rows = [], []
        for g in range(group):
            rows = slice(g * seq, (g + 1) * seq)
            qc_rows.append(jnp.dot(qf[rows], c_in_ref[g, h], preferred_element_type=F32))
            qn_rows.append(jnp.sum(qf[rows] * n_in_ref[g, h:h + 1, :], axis=1, keepdims=True))
        num = (a_inter * jnp.concatenate(qc_rows, axis=0)
               + jnp.dot(s.astype(BF16), v_bf, preferred_element_type=F32))
        den = (a_inter * jnp.concatenate(qn_rows, axis=0)
               + jnp.sum(s, axis=1, keepdims=True))
        y_ref[:, D_CONV + h * D_V:D_CONV + (h + 1) * D_V] = _head_output(
            num, den, m_t, z_ref[:, _cols_o(h)],
            mlg_ref[:, h * D_V:(h + 1) * D_V]).astype(y_ref.dtype)

        g_r = bl_r - br + li_r
        gmax_c = jnp.max(jnp.where(same, g_r, -jnp.inf), axis=1, keepdims=True)
        gmax_r = jnp.max(jnp.where(same, gmax_c, -jnp.inf), axis=0, keepdims=True)
        m_new_c = jnp.maximum(bl_c + m0_c, gmax_c)
        m_new_r = jnp.maximum(bl_r + m0_r, gmax_r)
        decay_r = jnp.exp(bl_r + m0_r - m_new_r)
        kw = kf * jnp.exp(bl_c - bc + li_c - m_new_c)
        kw_t = kw.T
        for g in range(group):
            decay = decay_r[:, g * seq:g * seq + 1]
            kw_g = jnp.where(lane_seq == g, kw_t, 0.0).astype(BF16)
            c_out_ref[g, h] = decay * c_in_ref[g, h] + jnp.dot(
                kw_g, v_bf, preferred_element_type=F32)
            n_out_ref[g, h:h + 1, :] = (decay * n_in_ref[g, h:h + 1, :]
                                        + jnp.sum(kw[g * seq:(g + 1) * seq], axis=0, keepdims=True))
        m_out_ref[h:h + 1, :] = m_new_r
    m_out_ref[N_HEADS:, :] = jnp.zeros((SUBLANES - N_HEADS, L), F32)


def _mix_sample(z, gcol, grow, mcol, mrow, convs, c_state, n_state, batch, seq,
                conv_w, conv_g, ml_g, bias_row, bias_col):
    group = SAMPLE_GROUP
    L = group * seq
    assert seq == SUBLANES and L == LANES and batch % group == 0, (batch, seq)
    const = lambda i: (0, 0)
    return pl.pallas_call(
        functools.partial(_mix_sample_kernel, seq=seq),
        grid=(batch // group,),
        in_specs=[
            pl.BlockSpec((L, N_MAIN), lambda i: (i, 0)),
            pl.BlockSpec((L, LANES), lambda i: (i, 0)),
            pl.BlockSpec((GATE_ROWS, L), lambda i: (0, i)),
            pl.BlockSpec((L, LANES), lambda i: (i, 0)),
            pl.BlockSpec((SUBLANES, L), lambda i: (0, i)),
            pl.BlockSpec((group, CONV_W - 1, D_CONV), lambda i: (i, 0, 0)),
            pl.BlockSpec((group, N_HEADS, D_K, D_V), lambda i: (i, 0, 0, 0)),
            pl.BlockSpec((group, N_HEADS, D_K), lambda i: (i, 0, 0)),
            pl.BlockSpec((CONV_W, D_CONV), const),
            pl.BlockSpec((1, D_CONV), const),
            pl.BlockSpec((1, N_HEADS * D_V), const),
            pl.BlockSpec((1, LANES), const),
            pl.BlockSpec((SUBLANES, LANES), const),
        ],
        out_specs=[
            pl.BlockSpec((L, D_MODEL), lambda i: (i, 0)),
            pl.BlockSpec((group, CONV_W - 1, D_CONV), lambda i: (i, 0, 0)),
            pl.BlockSpec((group, N_HEADS, D_K, D_V), lambda i: (i, 0, 0, 0)),
            pl.BlockSpec((group, N_HEADS, D_K), lambda i: (i, 0, 0)),
            pl.BlockSpec((SUBLANES, L), lambda i: (0, i)),
        ],
        out_shape=[
            jax.ShapeDtypeStruct((batch * seq, D_MODEL), BF16),
            jax.ShapeDtypeStruct((batch, CONV_W - 1, D_CONV), F32),
            jax.ShapeDtypeStruct((batch, N_HEADS, D_K, D_V), F32),
            jax.ShapeDtypeStruct((batch, N_HEADS, D_K), F32),
            jax.ShapeDtypeStruct((SUBLANES, batch * seq), F32),
        ],
        scratch_shapes=[
            pltpu.VMEM((group * (seq + SUBLANES), D_CONV), F32),
            pltpu.VMEM((L, D_CONV), F32),
        ],
        compiler_params=_compiler_params("mix_sample", ("parallel",)),
        name="mix_sample",
    )(z, gcol, grow, mcol, mrow, convs, c_state, n_state,
      conv_w, conv_g, ml_g, bias_row, bias_col)


def _outproj_kernel(y_ref, w_ref, x_ref, o_ref):
    o_ref[...] = x_ref[...] + jnp.dot(y_ref[...], w_ref[...], preferred_element_type=F32)


def _outproj(y, w_out_bf, x2d):
    tokens = x2d.shape[0]
    tm = min(512, tokens)
    assert tokens % tm == 0, tokens
    return pl.pallas_call(
        _outproj_kernel,
        grid=(tokens // tm,),
        in_specs=[
            pl.BlockSpec((tm, D_MODEL), lambda i: (i, 0)),
            pl.BlockSpec((D_MODEL, D_MODEL), lambda i: (0, 0), pipeline_mode=pl.Buffered(1)),
            pl.BlockSpec((tm, D_MODEL), lambda i: (i, 0)),
        ],
        out_specs=pl.BlockSpec((tm, D_MODEL), lambda i: (i, 0)),
        out_shape=jax.ShapeDtypeStruct((tokens, D_MODEL), F32),
        compiler_params=_compiler_params("outproj", ("parallel",)),
        name="outproj",
    )(y, w_out_bf, x2d)


def _ffn_kernel(x_ref, g2_ref, wgu_ref, wd_ref, gf_ref, o_ref, hn_ref, *, final_norm):
    j = pl.program_id(1)

    @pl.when(j == 0)
    def _():
        _rmsnorm_rows(x_ref, g2_ref[...], hn_ref, copy_ref=o_ref)

    hn = hn_ref[...]
    half = wd_ref.shape[0] // 2
    halves = (slice(0, half), slice(half, 2 * half))
    pre = []
    for a in range(2):
        gate_up = jnp.dot(hn, wgu_ref[0, :, 2 * a * half:2 * (a + 1) * half],
                          preferred_element_type=F32)
        pre.append((gate_up[:, 0:half], gate_up[:, half:2 * half]))
    down = None
    for (gate, up), rows in zip(pre, halves):
        act = ((gate * jax.nn.sigmoid(gate)) * up).astype(BF16)
        part = jnp.dot(act, wd_ref[rows, :].astype(BF16), preferred_element_type=F32)
        down = part if down is None else down + part
    o_ref[...] += down

    if final_norm:
        @pl.when(j == pl.num_programs(1) - 1)
        def _():
            _rmsnorm_rows_inplace(o_ref, gf_ref[...])


def _ffn(x1, g2, wgu_bf, wd, gf, final_norm):
    tokens = x1.shape[0]
    tm = min(1024, tokens)
    tf = FFN_TILE
    assert tokens % tm == 0 and tm % (NORM_ROWS * NORM_GROUP) == 0, tokens
    return pl.pallas_call(
        functools.partial(_ffn_kernel, final_norm=final_norm),
        grid=(tokens // tm, D_FF // tf),
        in_specs=[
            pl.BlockSpec((tm, D_MODEL), lambda i, j: (i, 0)),
            pl.BlockSpec((1, D_MODEL), lambda i, j: (0, 0)),
            pl.BlockSpec((1, D_MODEL, 2 * tf), lambda i, j: (j, 0, 0)),
            pl.BlockSpec((tf, D_MODEL), lambda i, j: (j, 0)),
            pl.BlockSpec((1, D_MODEL), lambda i, j: (0, 0)),
        ],
        out_specs=pl.BlockSpec((tm, D_MODEL), lambda i, j: (i, 0)),
        out_shape=jax.ShapeDtypeStruct((tokens, D_MODEL), F32),
        scratch_shapes=[pltpu.VMEM((tm, D_MODEL), BF16)],
        compiler_params=_compiler_params("ffn", ("parallel", "arbitrary")),
        name="ffn",
    )(x1, g2, wgu_bf, wd, gf)


def _gate_params(w_t, b_i, b_f):
    wgate = jnp.pad(w_t[N_MAIN:, :], ((0, LANES - 2 * N_HEADS), (0, 0)))
    bias = jnp.concatenate([b_i, b_f]).astype(F32)
    bias_row = jnp.pad(bias, (0, LANES - 2 * N_HEADS))[None, :]
    bias_col = jnp.broadcast_to(bias[:, None], (SUBLANES, LANES))
    return wgate, bias_row, bias_col


def _tail(x2d, y, w_out, g2, ffn_weights, gf, final_norm):
    x1 = _outproj(y, w_out, x2d)
    return _ffn(x1, g2[None, :], *ffn_weights, gf[None, :], final_norm)


def _stack(states, k):
    if len(states) == 1:
        return states[0][k][None]
    return jnp.stack([st[k] for st in states])


def kernel(x_prompt, x_sample, state_conv, state_mlstm_C, state_mlstm_n, state_mlstm_m,
           norm1_g, w_in, b_igate, b_fgate, conv_w, conv_out_g, mlstm_out_g, w_out,
           norm2_g, w_ffn_gate, w_ffn_up, w_ffn_down, final_norm_g):
    depth = w_in.shape[0]
    bp, sp, _ = x_prompt.shape
    bs, ss, _ = x_sample.shape
    hp = x_prompt.reshape(bp * sp, D_MODEL)
    hs = x_sample.reshape(bs * ss, D_MODEL)
    p_states, s_states = [], []
    for l in range(depth):
        final_norm = l == depth - 1
        w_t = w_in[l].T
        wgate, bias_row, bias_col = _gate_params(w_t, b_igate[l], b_fgate[l])
        w_bf = _cast_main_columns(w_t)
        g1 = norm1_g[l][None, :]
        conv_g = conv_out_g[l][None, :]
        ml_g = mlstm_out_g[l][None, :]

        y, cb, c1, n1, m1, wgu_bf, w_out_bf = _fused_prompt(
            hp, g1, w_bf, wgate, bp, sp, conv_w[l], conv_g, ml_g, bias_row, bias_col,
            w_ffn_gate[l], w_ffn_up[l], w_out[l])
        tail = (w_out_bf, norm2_g[l], (wgu_bf, w_ffn_down[l]), final_norm_g, final_norm)
        hp = _tail(hp, y, *tail)
        p_states.append((cb, c1, n1, m1[:, :, 0]))

        z, gcol, grow = _inproj(hs, g1, w_bf, wgate)
        m_tok = jnp.repeat(state_mlstm_m[l], ss, axis=0)
        mcol = jnp.pad(m_tok, ((0, 0), (0, LANES - N_HEADS)))
        mrow = jnp.pad(m_tok.T, ((0, SUBLANES - N_HEADS), (0, 0)))
        y, cb, c1, n1, m_row = _mix_sample(z, gcol, grow, mcol, mrow, state_conv[l],
                                           state_mlstm_C[l], state_mlstm_n[l], bs, ss,
                                           conv_w[l], conv_g, ml_g, bias_row, bias_col)
        hs = _tail(hs, y, *tail)
        s_states.append((cb, c1, n1, m_row[:N_HEADS, ::ss].T))

    return (hp.reshape(bp, sp, D_MODEL), hs.reshape(bs, ss, D_MODEL),
            _stack(p_states, 0), _stack(p_states, 1), _stack(p_states, 2), _stack(p_states, 3),
            _stack(s_states, 0), _stack(s_states, 1), _stack(s_states, 2), _stack(s_states, 3))
```

```python
import functools

import jax
import jax.numpy as jnp
from jax import lax
from jax.experimental import pallas as pl
from jax.experimental.pallas import tpu as pltpu

F32 = jnp.float32
BF16 = jnp.bfloat16

D_MODEL = 2048
D_CONV = 1024
CONV_W = 3
N_HEADS = 4
D_K = 128
D_V = 256
D_FF = 5632
EPS = 1e-6

LANES = 128
SUBLANES = 8
GATE_ROWS = 16
MIB = 1024 * 1024

PROMPT_CHUNK = 256
SAMPLE_GROUP = 16
NORM_ROWS = 64
NORM_GROUP = 4
W_SLAB = 512
FFN_TILE = 512
CAST_PIECE = 256

VMEM_MIB = {"cast_w_in": 62, "inproj": 62, "fused_prompt": 62, "mix_sample": 62,
            "outproj": 62, "ffn": 62}

SRC_Q = 3 * D_CONV
SRC_K = SRC_Q + N_HEADS * D_K
SRC_V = SRC_K + N_HEADS * D_K
SRC_O = SRC_V + N_HEADS * D_V
N_MAIN = SRC_O + N_HEADS * D_V

HEAD_COLS = 2 * D_K + 2 * D_V
N_MLSTM = N_HEADS * HEAD_COLS
CONV_PARTS = 4
CONV_PART = D_CONV // CONV_PARTS
BG, CG, XT = 0, 1, 2


def _cols_conv(kind, part):
    start = N_MLSTM + (3 * part + kind) * CONV_PART
    return slice(start, start + CONV_PART)


def _conv_operand(z, kind, rows=slice(None)):
    return jnp.concatenate([z[rows, _cols_conv(kind, p)] for p in range(CONV_PARTS)], axis=1)


def _cols_q(h):
    return slice(h * HEAD_COLS, h * HEAD_COLS + D_K)


def _cols_k(h):
    return slice(h * HEAD_COLS + D_K, h * HEAD_COLS + 2 * D_K)


def _cols_v(h):
    return slice(h * HEAD_COLS + 2 * D_K, h * HEAD_COLS + 2 * D_K + D_V)


def _cols_o(h):
    return slice(h * HEAD_COLS + 2 * D_K + D_V, (h + 1) * HEAD_COLS)


def _source_lane_block(j):
    per_head = HEAD_COLS // LANES
    h, r = j // per_head, j % per_head
    v_blocks = D_V // LANES
    mlstm_src = jnp.where(
        r == 0, SRC_Q // LANES + h,
        jnp.where(r == 1, SRC_K // LANES + h,
                  jnp.where(r < 2 + v_blocks,
                            SRC_V // LANES + v_blocks * h + (r - 2),
                            SRC_O // LANES + v_blocks * h + (r - 2 - v_blocks))))
    per_kind = CONV_PART // LANES
    c = j - N_MLSTM // LANES
    part, r = c // (3 * per_kind), c % (3 * per_kind)
    conv_src = (r // per_kind) * (D_CONV // LANES) + part * per_kind + r % per_kind
    return jnp.where(j < N_MLSTM // LANES, mlstm_src, conv_src)


def _compiler_params(name, semantics):
    return pltpu.CompilerParams(dimension_semantics=semantics,
                                vmem_limit_bytes=VMEM_MIB[name] * MIB)


def _rmsnorm_piece(x, gain, dtype):
    ms = jnp.mean(x * x, axis=-1, keepdims=True)
    return ((x * lax.rsqrt(ms + EPS)) * gain).astype(dtype)


def _rmsnorm_rows(src_ref, gain, dst_ref, copy_ref=None):
    rows = src_ref.shape[0]
    chunk = min(NORM_ROWS, rows)

    def body(i, carry):
        r = pl.multiple_of(i * chunk, chunk)
        x = src_ref[pl.ds(r, chunk), :].astype(F32)
        if copy_ref is not None:
            copy_ref[pl.ds(r, chunk), :] = x
        dst_ref[pl.ds(r, chunk), :] = _rmsnorm_piece(x, gain, dst_ref.dtype)
        return carry

    lax.fori_loop(0, rows // chunk, body, 0, unroll=2)


def _rmsnorm_rows_inplace(ref, gain):
    rows = ref.shape[0]
    chunk = min(NORM_ROWS, rows)
    group = min(NORM_GROUP, rows // chunk)

    def body(i, carry):
        starts = [pl.multiple_of((i * group + k) * chunk, chunk) for k in range(group)]
        scales = []
        for r in starts:
            x = ref[pl.ds(r, chunk), :]
            scales.append(lax.rsqrt(jnp.mean(x * x, axis=-1, keepdims=True) + EPS))
        for r, scale in zip(starts, scales):
            ref[pl.ds(r, chunk), :] = (ref[pl.ds(r, chunk), :] * scale) * gain
        return carry

    lax.fori_loop(0, rows // (chunk * group), body, 0)


def _cast_kernel(*refs):
    *w_refs, o_ref = refs
    for r, w_ref in enumerate(w_refs):
        o_ref[:, r * LANES:(r + 1) * LANES] = w_ref[...].T.astype(o_ref.dtype)


def _cast_main_columns(w_t):
    per_slab = W_SLAB // LANES
    in_specs = [
        pl.BlockSpec((LANES, D_MODEL),
                     lambda j, r=r: (_source_lane_block(per_slab * j + r), 0))
        for r in range(per_slab)
    ]
    return pl.pallas_call(
        _cast_kernel,
        grid=(N_MAIN // W_SLAB,),
        in_specs=in_specs,
        out_specs=pl.BlockSpec((D_MODEL, W_SLAB), lambda j: (0, j)),
        out_shape=jax.ShapeDtypeStruct((D_MODEL, N_MAIN), BF16),
        compiler_params=_compiler_params("cast_w_in", ("parallel",)),
        name="cast_w_in",
    )(*([w_t] * per_slab))


def _gate_dot(xn, wgate_t_ref):
    return lax.dot_general(xn, wgate_t_ref[...].astype(BF16), (((1,), (1,)), ((), ())),
                           preferred_element_type=F32)


def _inproj_kernel(x_ref, g_ref, w_ref, wgate_ref, z_ref, gcol_ref, grow_ref, xn_ref):
    @pl.when(pl.program_id(1) == 0)
    def _():
        _rmsnorm_rows(x_ref, g_ref[...], xn_ref)
        gc = _gate_dot(xn_ref[...], wgate_ref)
        gcol_ref[...] = gc
        grow_ref[...] = gc.T[0:GATE_ROWS, :]

    z_ref[...] = jnp.dot(xn_ref[...], w_ref[...], preferred_element_type=F32)


def _inproj(x2d, g1, w_bf, wgate):
    tokens = x2d.shape[0]
    tm = min(1024, tokens)
    tn = W_SLAB
    assert tokens % tm == 0 and tm % NORM_ROWS == 0, tokens
    return pl.pallas_call(
        _inproj_kernel,
        grid=(tokens // tm, N_MAIN // tn),
        in_specs=[
            pl.BlockSpec((tm, D_MODEL), lambda i, j: (i, 0)),
            pl.BlockSpec((1, D_MODEL), lambda i, j: (0, 0)),
            pl.BlockSpec((D_MODEL, tn), lambda i, j: (0, j)),
            pl.BlockSpec((LANES, D_MODEL), lambda i, j: (0, 0)),
        ],
        out_specs=[
            pl.BlockSpec((tm, tn), lambda i, j: (i, j)),
            pl.BlockSpec((tm, LANES), lambda i, j: (i, 0)),
            pl.BlockSpec((GATE_ROWS, tm), lambda i, j: (0, i)),
        ],
        out_shape=[
            jax.ShapeDtypeStruct((tokens, N_MAIN), F32),
            jax.ShapeDtypeStruct((tokens, LANES), F32),
            jax.ShapeDtypeStruct((GATE_ROWS, tokens), F32),
        ],
        scratch_shapes=[pltpu.VMEM((tm, D_MODEL), BF16)],
        compiler_params=_compiler_params("inproj", ("parallel", "arbitrary")),
        name="inproj",
    )(x2d, g1, w_bf, wgate)


def _log_sigmoid(x):
    return jnp.minimum(x, 0.0) - jnp.log1p(jnp.exp(-jnp.abs(x)))


def _split3(x):
    hi = x.astype(BF16).astype(F32)
    mid = (x - hi).astype(BF16).astype(F32)
    lo = ((x - hi) - mid).astype(BF16).astype(F32)
    return hi, mid, lo


def _masked_sums(mask, lf_col, lf_row):
    m_bf = mask.astype(BF16)
    n, k = lf_col.shape[1], lf_row.shape[0]
    col_terms = jnp.concatenate(_split3(lf_col), axis=1).astype(BF16)
    row_terms = jnp.concatenate(_split3(lf_row) + (jnp.zeros_like(lf_row),),
                                axis=0).astype(BF16)
    cols = jnp.dot(m_bf, col_terms, preferred_element_type=F32)
    rows = lax.dot_general(row_terms, m_bf, (((1,), (1,)), ((), ())),
                           preferred_element_type=F32)
    return (cols[:, 0:n] + cols[:, n:2 * n] + cols[:, 2 * n:3 * n],
            rows[0:k] + rows[k:2 * k] + rows[2 * k:3 * k])


def _intra(q_bf, k_bf, b_col, b_row, li_row, m0_col, mask):
    dmat = jnp.where(mask, b_col - b_row + li_row, -jnp.inf)
    inter = b_col + m0_col
    m_t = jnp.maximum(inter, jnp.max(dmat, axis=1, keepdims=True))
    w = jnp.exp(dmat - m_t)
    a_inter = jnp.exp(inter - m_t)
    qk = lax.dot_general(q_bf, k_bf, (((1,), (1,)), ((), ())), preferred_element_type=F32)
    return qk * w, a_inter, m_t


def _head_output(num, den, m_t, o_gate, gain):
    denom = jnp.maximum(jnp.abs(den), jnp.exp(-m_t))
    h = num * (1.0 / denom)
    hn = (h * lax.rsqrt(jnp.mean(h * h, axis=1, keepdims=True) + EPS)) * gain
    return jax.nn.sigmoid(o_gate) * hn


def _conv_norm(bg, conv, gain):
    yc = bg * conv
    return (yc * lax.rsqrt(jnp.mean(yc * yc, axis=1, keepdims=True) + EPS)) * gain


def _prompt_conv_part(z, part, convw_ref, ubuf, yc_scr):
    L = z.shape[0]
    ch = slice(part * CONV_PART, (part + 1) * CONV_PART)
    u = z[:, _cols_conv(CG, part)] * z[:, _cols_conv(XT, part)]
    ubuf[SUBLANES:SUBLANES + L, ch] = u
    conv = (ubuf[SUBLANES - 2:SUBLANES - 2 + L, ch] * convw_ref[0:1, ch]
            + ubuf[SUBLANES - 1:SUBLANES - 1 + L, ch] * convw_ref[1:2, ch]
            + u * convw_ref[2:3, ch])
    ubuf[0:SUBLANES, ch] = ubuf[L:L + SUBLANES, ch]
    yc = z[:, _cols_conv(BG, part)] * conv
    yc_scr[:, ch] = yc
    return jnp.sum(yc * yc, axis=1, keepdims=True)


def _prompt_conv_finish(sumsq, yc_scr, convg_ref, y_ref):
    scale = lax.rsqrt(sumsq * (1.0 / D_CONV) + EPS)
    y_ref[:, 0:D_CONV] = ((yc_scr[...] * scale) * convg_ref[...]).astype(y_ref.dtype)


def _prompt_mlstm(z, gc, gr, mlg_ref, bias_row_ref, bias_col_ref, y_ref, c_scr, n_scr, m_scr,
                  fillers):
    L = z.shape[0]
    fillers = list(fillers)

    def fill():
        if fillers:
            fillers.pop(0)()

    pre_c = gc + bias_row_ref[...]
    pre_r = gr + bias_col_ref[:, 0:1]
    row_id = lax.broadcasted_iota(jnp.int32, (L, L), 0)
    col_id = lax.broadcasted_iota(jnp.int32, (L, L), 1)
    causal = col_id <= row_id
    b_c, b_r = _masked_sums(causal, _log_sigmoid(pre_c), _log_sigmoid(pre_r))

    def scores(h):
        li_r = pre_r[h:h + 1, :]
        bc = b_c[:, N_HEADS + h:N_HEADS + h + 1]
        br = b_r[N_HEADS + h:N_HEADS + h + 1, :]
        m0 = m_scr[h, 0:1, 0:1]
        qf = z[:, _cols_q(h)] * (D_K ** -0.5)
        q_bf = qf.astype(BF16)
        kf = z[:, _cols_k(h)]
        v_bf = z[:, _cols_v(h)].astype(BF16)
        s, a_inter, m_t = _intra(q_bf, kf.astype(BF16), bc, br, li_r, m0, causal)
        return li_r, bc, br, m0, qf, q_bf, kf, v_bf, s, a_inter, m_t

    def finish(h, staged):
        li_r, bc, br, m0, qf, q_bf, kf, v_bf, s, a_inter, m_t = staged
        li_c = pre_c[:, h:h + 1]
        c_old = c_scr[h]
        n_old = n_scr[h]
        num = (a_inter * jnp.dot(q_bf, c_old.astype(BF16), preferred_element_type=F32)
               + jnp.dot(s.astype(BF16), v_bf, preferred_element_type=F32))
        den = (a_inter * jnp.sum(qf * n_old, axis=1, keepdims=True)
               + jnp.sum(s, axis=1, keepdims=True))
        fill()
        y_ref[:, D_CONV + h * D_V:D_CONV + (h + 1) * D_V] = _head_output(
            num, den, m_t, z[:, _cols_o(h)],
            mlg_ref[:, h * D_V:(h + 1) * D_V]).astype(y_ref.dtype)

        b_last = br[:, L - 1:L]
        m_new = jnp.maximum(b_last + m0,
                            jnp.max(b_last - br + li_r, axis=1, keepdims=True))
        decay = jnp.exp(b_last + m0 - m_new)
        kw = kf * jnp.exp(b_last - bc + li_c - m_new)
        c_scr[h] = decay * c_old + lax.dot_general(
            kw.astype(BF16), v_bf, (((0,), (0,)), ((), ())), preferred_element_type=F32)
        n_scr[h] = decay * n_old + jnp.sum(kw, axis=0, keepdims=True)
        m_scr[h] = jnp.broadcast_to(m_new, (SUBLANES, LANES))
        fill()

    staged = scores(0)
    fill()
    for h in range(N_HEADS):
        upcoming = None
        if h + 1 < N_HEADS:
            upcoming = scores(h + 1)
            fill()
        finish(h, staged)
        staged = upcoming
    while fillers:
        fill()


def _fused_prompt_kernel(x_ref, g1_ref, w_ref, wgate_ref, convw_ref, convg_ref, mlg_ref,
                         bias_row_ref, bias_col_ref, wg_ref, wu_ref, wout_ref,
                         y_ref, convs_ref, c_out_ref, n_out_ref, m_out_ref,
                         wgu_bf_ref, wout_bf_ref,
                         xn_scr, z_scr, ubuf, yc_scr, c_scr, n_scr, m_scr):
    chunk = pl.program_id(1)
    L = x_ref.shape[0]

    @pl.when(chunk == 0)
    def _():
        ubuf[0:SUBLANES, :] = jnp.zeros((SUBLANES, D_CONV), F32)
        c_scr[...] = jnp.zeros(c_scr.shape, F32)
        n_scr[...] = jnp.zeros(n_scr.shape, F32)
        m_scr[...] = jnp.zeros(m_scr.shape, F32)

    gain = g1_ref[...]
    for r in range(0, L, NORM_ROWS):
        xn_scr[r:r + NORM_ROWS, :] = _rmsnorm_piece(x_ref[r:r + NORM_ROWS, :], gain, BF16)
    xn = xn_scr[...]
    gc = _gate_dot(xn, wgate_ref)
    gr = gc.T[0:SUBLANES, :]

    def project(j):
        cols = slice(j * W_SLAB, (j + 1) * W_SLAB)
        z_scr[:, cols] = jnp.dot(xn_scr[...], w_ref[:, cols], preferred_element_type=F32)

    piece = lambda j: functools.partial(project, j)
    skip = lambda: None
    sumsq = []

    def then_conv(j, part):
        def emit():
            project(j)
            sumsq.append(_prompt_conv_part(z_scr, part, convw_ref, ubuf, yc_scr))
        return emit

    project(0)
    wgu_bf_ref[0, :, 0:CAST_PIECE] = wg_ref[...].astype(BF16)
    wout_bf_ref[...] = wout_ref[...].astype(BF16)
    project(1)
    wgu_bf_ref[0, :, CAST_PIECE:2 * CAST_PIECE] = wu_ref[...].astype(BF16)
    fillers = [piece(2), piece(3), piece(4), piece(6),
               piece(5), then_conv(7, 0), then_conv(8, 1),
               piece(9), then_conv(10, 2), then_conv(11, 3),
               skip, skip]
    _prompt_mlstm(z_scr, gc, gr, mlg_ref, bias_row_ref, bias_col_ref, y_ref,
                  c_scr, n_scr, m_scr, fillers)
    _prompt_conv_finish(sum(sumsq[1:], sumsq[0]), yc_scr, convg_ref, y_ref)

    @pl.when(chunk == pl.num_programs(1) - 1)
    def _():
        convs_ref[0] = ubuf[SUBLANES - 2:SUBLANES, :]
        c_out_ref[0] = c_scr[...]
        for h in range(N_HEADS):
            n_out_ref[0, h:h + 1, :] = n_scr[h]
            m_out_ref[0, h:h + 1, :] = m_scr[h, 0:1, :]


def _fused_prompt(x2d, g1, w_bf, wgate, batch, seq, conv_w, conv_g, ml_g, bias_row, bias_col,
                  wg, wu, w_out):
    L = PROMPT_CHUNK
    nc = seq // L
    assert nc * L == seq, seq
    const = lambda b, c: (0, 0)
    wout_rows = D_MODEL // (batch * nc)
    assert wout_rows * batch * nc == D_MODEL and wout_rows % (2 * SUBLANES) == 0
    wout_piece = pl.BlockSpec((wout_rows, D_MODEL), lambda b, c: (b * nc + c, 0))
    n_pieces = D_FF // CAST_PIECE
    assert batch * nc >= n_pieces, "not enough grid steps to cast the FFN weights"
    per_tile = FFN_TILE // CAST_PIECE
    piece_of = lambda b, c: jnp.minimum(b * nc + c, n_pieces - 1)
    tiled = lambda b, c: (piece_of(b, c) // per_tile, 0, piece_of(b, c) % per_tile)
    return pl.pallas_call(
        _fused_prompt_kernel,
        grid=(batch, nc),
        in_specs=[
            pl.BlockSpec((L, D_MODEL), lambda b, c: (b * nc + c, 0)),
            pl.BlockSpec((1, D_MODEL), const),
            pl.BlockSpec((D_MODEL, N_MAIN), const, pipeline_mode=pl.Buffered(1)),
            pl.BlockSpec((LANES, D_MODEL), const),
            pl.BlockSpec((CONV_W, D_CONV), const),
            pl.BlockSpec((1, D_CONV), const),
            pl.BlockSpec((1, N_HEADS * D_V), const),
            pl.BlockSpec((1, LANES), const),
            pl.BlockSpec((SUBLANES, LANES), const),
            pl.BlockSpec((D_MODEL, CAST_PIECE), lambda b, c: (0, piece_of(b, c))),
            pl.BlockSpec((D_MODEL, CAST_PIECE), lambda b, c: (0, piece_of(b, c))),
            wout_piece,
        ],
        out_specs=[
            pl.BlockSpec((L, D_MODEL), lambda b, c: (b * nc + c, 0)),
            pl.BlockSpec((1, CONV_W - 1, D_CONV), lambda b, c: (b, 0, 0)),
            pl.BlockSpec((1, N_HEADS, D_K, D_V), lambda b, c: (b, 0, 0, 0)),
            pl.BlockSpec((1, N_HEADS, D_K), lambda b, c: (b, 0, 0)),
            pl.BlockSpec((1, N_HEADS, LANES), lambda b, c: (b, 0, 0)),
            pl.BlockSpec((1, D_MODEL, 2 * CAST_PIECE), tiled),
            wout_piece,
        ],
        out_shape=[
            jax.ShapeDtypeStruct((batch * seq, D_MODEL), BF16),
            jax.ShapeDtypeStruct((batch, CONV_W - 1, D_CONV), F32),
            jax.ShapeDtypeStruct((batch, N_HEADS, D_K, D_V), F32),
            jax.ShapeDtypeStruct((batch, N_HEADS, D_K), F32),
            jax.ShapeDtypeStruct((batch, N_HEADS, LANES), F32),
            jax.ShapeDtypeStruct((D_FF // FFN_TILE, D_MODEL, 2 * FFN_TILE), BF16),
            jax.ShapeDtypeStruct((D_MODEL, D_MODEL), BF16),
        ],
        scratch_shapes=[
            pltpu.VMEM((L, D_MODEL), BF16),
            pltpu.VMEM((L, N_MAIN), F32),
            pltpu.VMEM((L + 2 * SUBLANES, D_CONV), F32),
            pltpu.VMEM((L, D_CONV), F32),
            pltpu.VMEM((N_HEADS, D_K, D_V), F32),
            pltpu.VMEM((N_HEADS, 1, D_K), F32),
            pltpu.VMEM((N_HEADS, SUBLANES, LANES), F32),
        ],
        compiler_params=_compiler_params("fused_prompt", ("arbitrary", "arbitrary")),
        name="fused_prompt",
    )(x2d, g1, w_bf, wgate, conv_w, conv_g, ml_g, bias_row, bias_col, wg, wu, w_out)


def _mix_sample_kernel(z_ref, gcol_ref, grow_ref, mcol_ref, mrow_ref,
                       convs_in_ref, c_in_ref, n_in_ref,
                       convw_ref, convg_ref, mlg_ref, bias_row_ref, bias_col_ref,
                       y_ref, convs_ref, c_out_ref, n_out_ref, m_out_ref,
                       pad_scr, conv_scr, *, seq):
    L = z_ref.shape[0]
    group = L // seq
    stride = seq + SUBLANES

    for g in range(group):
        rows = slice(g * seq, (g + 1) * seq)
        u = _conv_operand(z_ref, CG, rows) * _conv_operand(z_ref, XT, rows)
        base = g * stride
        pad_scr[base + SUBLANES - 2:base + SUBLANES, :] = convs_in_ref[g]
        pad_scr[base + SUBLANES:base + SUBLANES + seq, :] = u
        conv_scr[g * seq:(g + 1) * seq, :] = (
            pad_scr[base + SUBLANES - 2:base + SUBLANES - 2 + seq, :] * convw_ref[0:1, :]
            + pad_scr[base + SUBLANES - 1:base + SUBLANES - 1 + seq, :] * convw_ref[1:2, :]
            + u * convw_ref[2:3, :])
        convs_ref[g] = pad_scr[base + seq + SUBLANES - 2:base + seq + SUBLANES, :]
    y_ref[:, 0:D_CONV] = _conv_norm(_conv_operand(z_ref, BG), conv_scr[...],
                                    convg_ref[...]).astype(y_ref.dtype)

    pre_c = gcol_ref[...] + bias_row_ref[...]
    pre_r = grow_ref[0:SUBLANES, :] + bias_col_ref[:, 0:1]
    row_id = lax.broadcasted_iota(jnp.int32, (L, L), 0)
    col_id = lax.broadcasted_iota(jnp.int32, (L, L), 1)
    same = (row_id // seq) == (col_id // seq)
    causal = same & (col_id <= row_id)
    lf_c = _log_sigmoid(pre_c)
    lf_r = _log_sigmoid(pre_r)
    b_c, b_r = _masked_sums(causal, lf_c, lf_r)
    tot_c, tot_r = _masked_sums(same, lf_c, lf_r)
    lane_seq = lax.broadcasted_iota(jnp.int32, (1, L), 1) // seq

    for h in range(N_HEADS):
        li_r = pre_r[h:h + 1, :]
        li_c = pre_c[:, h:h + 1]
        bc = b_c[:, N_HEADS + h:N_HEADS + h + 1]
        br = b_r[N_HEADS + h:N_HEADS + h + 1, :]
        bl_c = tot_c[:, N_HEADS + h:N_HEADS + h + 1]
        bl_r = tot_r[N_HEADS + h:N_HEADS + h + 1, :]
        m0_c = mcol_ref[:, h:h + 1]
        m0_r = mrow_ref[h:h + 1, :]
        qf = z_ref[:, _cols_q(h)] * (D_K ** -0.5)
        q_bf = qf.astype(BF16)
        kf = z_ref[:, _cols_k(h)]
        v_bf = z_ref[:, _cols_v(h)].astype(BF16)
        s, a_inter, m_t = _intra(q_bf, kf.astype(BF16), bc, br, li_r, m0_c, causal)

        qc_rows, qn_rows = [], []
        for g in range(group):
            rows = slice(g * seq, (g + 1) * seq)
            qc_rows.append(jnp.dot(qf[rows], c_in_ref[g, h], preferred_element_type=F32))
            qn_rows.append(jnp.sum(qf[rows] * n_in_ref[g, h:h + 1, :], axis=1, keepdims=True))
        num = (a_inter * jnp.concatenate(qc_rows, axis=0)
               + jnp.dot(s.astype(BF16), v_bf, preferred_element_type=F32))
        den = (a_inter * jnp.concatenate(qn_rows, axis=0)
               + jnp.sum(s, axis=1, keepdims=True))
        y_ref[:, D_CONV + h * D_V:D_CONV + (h + 1) * D_V] = _head_output(
            num, den, m_t, z_ref[:, _cols_o(h)],
            mlg_ref[:, h * D_V:(h + 1) * D_V]).astype(y_ref.dtype)

        g_r = bl_r - br + li_r
        gmax_c = jnp.max(jnp.where(same, g_r, -jnp.inf), axis=1, keepdims=True)
        gmax_r = jnp.max(jnp.where(same, gmax_c, -jnp.inf), axis=0, keepdims=True)
        m_new_c = jnp.maximum(bl_c + m0_c, gmax_c)
        m_new_r = jnp.maximum(bl_r + m0_r, gmax_r)
        decay_r = jnp.exp(bl_r + m0_r - m_new_r)
        kw = kf * jnp.exp(bl_c - bc + li_c - m_new_c)
        kw_t = kw.T
        for g in range(group):
            decay = decay_r[:, g * seq:g * seq + 1]
            kw_g = jnp.where(lane_seq == g, kw_t, 0.0).astype(BF16)
            c_out_ref[g, h] = decay * c_in_ref[g, h] + jnp.dot(
                kw_g, v_bf, preferred_element_type=F32)
            n_out_ref[g, h:h + 1, :] = (decay * n_in_ref[g, h:h + 1, :]
                                        + jnp.sum(kw[g * seq:(g + 1) * seq], axis=0, keepdims=True))
        m_out_ref[h:h + 1, :] = m_new_r
    m_out_ref[N_HEADS:, :] = jnp.zeros((SUBLANES - N_HEADS, L), F32)


def _mix_sample(z, gcol, grow, mcol, mrow, convs, c_state, n_state, batch, seq,
                conv_w, conv_g, ml_g, bias_row, bias_col):
    group = SAMPLE_GROUP
    L = group * seq
    assert seq == SUBLANES and L == LANES and batch % group == 0, (batch, seq)
    const = lambda i: (0, 0)
    return pl.pallas_call(
        functools.partial(_mix_sample_kernel, seq=seq),
        grid=(batch // group,),
        in_specs=[
            pl.BlockSpec((L, N_MAIN), lambda i: (i, 0)),
            pl.BlockSpec((L, LANES), lambda i: (i, 0)),
            pl.BlockSpec((GATE_ROWS, L), lambda i: (0, i)),
            pl.BlockSpec((L, LANES), lambda i: (i, 0)),
            pl.BlockSpec((SUBLANES, L), lambda i: (0, i)),
            pl.BlockSpec((group, CONV_W - 1, D_CONV), lambda i: (i, 0, 0)),
            pl.BlockSpec((group, N_HEADS, D_K, D_V), lambda i: (i, 0, 0, 0)),
            pl.BlockSpec((group, N_HEADS, D_K), lambda i: (i, 0, 0)),
            pl.BlockSpec((CONV_W, D_CONV), const),
            pl.BlockSpec((1, D_CONV), const),
            pl.BlockSpec((1, N_HEADS * D_V), const),
            pl.BlockSpec((1, LANES), const),
            pl.BlockSpec((SUBLANES, LANES), const),
        ],
        out_specs=[
            pl.BlockSpec((L, D_MODEL), lambda i: (i, 0)),
            pl.BlockSpec((group, CONV_W - 1, D_CONV), lambda i: (i, 0, 0)),
            pl.BlockSpec((group, N_HEADS, D_K, D_V), lambda i: (i, 0, 0, 0)),
            pl.BlockSpec((group, N_HEADS, D_K), lambda i: (i, 0, 0)),
            pl.BlockSpec((SUBLANES, L), lambda i: (0, i)),
        ],
        out_shape=[
            jax.ShapeDtypeStruct((batch * seq, D_MODEL), BF16),
            jax.ShapeDtypeStruct((batch, CONV_W - 1, D_CONV), F32),
            jax.ShapeDtypeStruct((batch, N_HEADS, D_K, D_V), F32),
            jax.ShapeDtypeStruct((batch, N_HEADS, D_K), F32),
            jax.ShapeDtypeStruct((SUBLANES, batch * seq), F32),
        ],
        scratch_shapes=[
            pltpu.VMEM((group * (seq + SUBLANES), D_CONV), F32),
            pltpu.VMEM((L, D_CONV), F32),
        ],
        compiler_params=_compiler_params("mix_sample", ("parallel",)),
        name="mix_sample",
    )(z, gcol, grow, mcol, mrow, convs, c_state, n_state,
      conv_w, conv_g, ml_g, bias_row, bias_col)


def _outproj_kernel(ya_ref, yb_ref, w_ref, xa_ref, xb_ref, o_ref, *, tiles_a):
    i = pl.program_id(0)

    @pl.when(i < tiles_a)
    def _():
        o_ref[...] = xa_ref[...] + jnp.dot(ya_ref[...], w_ref[...], preferred_element_type=F32)

    @pl.when(i >= tiles_a)
    def _():
        o_ref[...] = xb_ref[...] + jnp.dot(yb_ref[...], w_ref[...], preferred_element_type=F32)


def _outproj(ya, yb, w_out_bf, xa, xb):
    tm = 512
    assert xa.shape[0] % tm == 0 and xb.shape[0] % tm == 0, (xa.shape, xb.shape)
    tiles_a, tiles_b = xa.shape[0] // tm, xb.shape[0] // tm
    rows_a = pl.BlockSpec((tm, D_MODEL), lambda i: (jnp.minimum(i, tiles_a - 1), 0))
    rows_b = pl.BlockSpec((tm, D_MODEL), lambda i: (jnp.maximum(i - tiles_a, 0), 0))
    return pl.pallas_call(
        functools.partial(_outproj_kernel, tiles_a=tiles_a),
        grid=(tiles_a + tiles_b,),
        in_specs=[
            rows_a, rows_b,
            pl.BlockSpec((D_MODEL, D_MODEL), lambda i: (0, 0), pipeline_mode=pl.Buffered(1)),
            rows_a, rows_b,
        ],
        out_specs=pl.BlockSpec((tm, D_MODEL), lambda i: (i, 0)),
        out_shape=jax.ShapeDtypeStruct(((tiles_a + tiles_b) * tm, D_MODEL), F32),
        compiler_params=_compiler_params("outproj", ("parallel",)),
        name="outproj",
    )(ya, yb, w_out_bf, xa, xb)


def _ffn_kernel(x_ref, g2_ref, wgu_ref, wd_ref, gf_ref, o_ref, hn_ref, *, final_norm):
    j = pl.program_id(1)

    @pl.when(j == 0)
    def _():
        _rmsnorm_rows(x_ref, g2_ref[...], hn_ref, copy_ref=o_ref)

    hn = hn_ref[...]
    half = wd_ref.shape[0] // 2
    halves = (slice(0, half), slice(half, 2 * half))
    pre = []
    for a in range(2):
        gate_up = jnp.dot(hn, wgu_ref[0, :, 2 * a * half:2 * (a + 1) * half],
                          preferred_element_type=F32)
        pre.append((gate_up[:, 0:half], gate_up[:, half:2 * half]))
    down = None
    for (gate, up), rows in zip(pre, halves):
        act = ((gate * jax.nn.sigmoid(gate)) * up).astype(BF16)
        part = jnp.dot(act, wd_ref[rows, :].astype(BF16), preferred_element_type=F32)
        down = part if down is None else down + part
    o_ref[...] += down

    if final_norm:
        @pl.when(j == pl.num_programs(1) - 1)
        def _():
            _rmsnorm_rows_inplace(o_ref, gf_ref[...])


def _ffn(x1, first_row, tokens, g2, wgu_bf, wd, gf, final_norm):
    tm = min(1024, tokens)
    tf = FFN_TILE
    assert tokens % tm == 0 and first_row % tm == 0, (first_row, tokens)
    assert tm % (NORM_ROWS * NORM_GROUP) == 0, tm
    first_tile = first_row // tm
    return pl.pallas_call(
        functools.partial(_ffn_kernel, final_norm=final_norm),
        grid=(tokens // tm, D_FF // tf),
        in_specs=[
            pl.BlockSpec((tm, D_MODEL), lambda i, j: (i + first_tile, 0)),
            pl.BlockSpec((1, D_MODEL), lambda i, j: (0, 0)),
            pl.BlockSpec((1, D_MODEL, 2 * tf), lambda i, j: (j, 0, 0)),
            pl.BlockSpec((tf, D_MODEL), lambda i, j: (j, 0)),
            pl.BlockSpec((1, D_MODEL), lambda i, j: (0, 0)),
        ],
        out_specs=pl.BlockSpec((tm, D_MODEL), lambda i, j: (i, 0)),
        out_shape=jax.ShapeDtypeStruct((tokens, D_MODEL), F32),
        scratch_shapes=[pltpu.VMEM((tm, D_MODEL), BF16)],
        compiler_params=_compiler_params("ffn", ("parallel", "arbitrary")),
        name="ffn",
    )(x1, g2, wgu_bf, wd, gf)


def _gate_params(w_t, b_i, b_f):
    wgate = jnp.pad(w_t[N_MAIN:, :], ((0, LANES - 2 * N_HEADS), (0, 0)))
    bias = jnp.concatenate([b_i, b_f]).astype(F32)
    bias_row = jnp.pad(bias, (0, LANES - 2 * N_HEADS))[None, :]
    bias_col = jnp.broadcast_to(bias[:, None], (SUBLANES, LANES))
    return wgate, bias_row, bias_col


def _stack(states, k):
    if len(states) == 1:
        return states[0][k][None]
    return jnp.stack([st[k] for st in states])


def kernel(x_prompt, x_sample, state_conv, state_mlstm_C, state_mlstm_n, state_mlstm_m,
           norm1_g, w_in, b_igate, b_fgate, conv_w, conv_out_g, mlstm_out_g, w_out,
           norm2_g, w_ffn_gate, w_ffn_up, w_ffn_down, final_norm_g):
    depth = w_in.shape[0]
    bp, sp, _ = x_prompt.shape
    bs, ss, _ = x_sample.shape
    hp = x_prompt.reshape(bp * sp, D_MODEL)
    hs = x_sample.reshape(bs * ss, D_MODEL)
    p_states, s_states = [], []
    for l in range(depth):
        final_norm = l == depth - 1
        w_t = w_in[l].T
        wgate, bias_row, bias_col = _gate_params(w_t, b_igate[l], b_fgate[l])
        w_bf = _cast_main_columns(w_t)
        g1 = norm1_g[l][None, :]
        conv_g = conv_out_g[l][None, :]
        ml_g = mlstm_out_g[l][None, :]

        y_p, cb, c1, n1, m1, wgu_bf, w_out_bf = _fused_prompt(
            hp, g1, w_bf, wgate, bp, sp, conv_w[l], conv_g, ml_g, bias_row, bias_col,
            w_ffn_gate[l], w_ffn_up[l], w_out[l])
        p_states.append((cb, c1, n1, m1[:, :, 0]))

        z, gcol, grow = _inproj(hs, g1, w_bf, wgate)
        m_tok = jnp.repeat(state_mlstm_m[l], ss, axis=0)
        mcol = jnp.pad(m_tok, ((0, 0), (0, LANES - N_HEADS)))
        mrow = jnp.pad(m_tok.T, ((0, SUBLANES - N_HEADS), (0, 0)))
        y_s, cb, c1, n1, m_row = _mix_sample(z, gcol, grow, mcol, mrow, state_conv[l],
                                             state_mlstm_C[l], state_mlstm_n[l], bs, ss,
                                             conv_w[l], conv_g, ml_g, bias_row, bias_col)
        s_states.append((cb, c1, n1, m_row[:N_HEADS, ::ss].T))

        x1 = _outproj(y_p, y_s, w_out_bf, hp, hs)
        ffn = (norm2_g[l][None, :], wgu_bf, w_ffn_down[l], final_norm_g[None, :], final_norm)
        hp = _ffn(x1, 0, bp * sp, *ffn)
        hs = _ffn(x1, bp * sp, bs * ss, *ffn)

    return (hp.reshape(bp, sp, D_MODEL), hs.reshape(bs, ss, D_MODEL),
            _stack(p_states, 0), _stack(p_states, 1), _stack(p_states, 2), _stack(p_states, 3),
            _stack(s_states, 0), _stack(s_states, 1), _stack(s_states, 2), _stack(s_states, 3))
```

```python
import functools

import jax
import jax.numpy as jnp
from jax import lax
from jax.experimental import pallas as pl
from jax.experimental.pallas import tpu as pltpu

F32 = jnp.float32
BF16 = jnp.bfloat16

D_MODEL = 2048
D_CONV = 1024
CONV_W = 3
N_HEADS = 4
D_K = 128
D_V = 256
D_FF = 5632
EPS = 1e-6

LANES = 128
SUBLANES = 8
GATE_ROWS = 16
MIB = 1024 * 1024

PROMPT_CHUNK = 256
SAMPLE_GROUP = 16
NORM_ROWS = 64
NORM_GROUP = 4
W_SLAB = 512
FFN_TILE = 512
CAST_PIECE = 256

VMEM_MIB = {"cast_w_in": 62, "inproj": 62, "fused_prompt": 62, "mix_sample": 62,
            "outproj": 62, "ffn": 62}

SRC_Q = 3 * D_CONV
SRC_K = SRC_Q + N_HEADS * D_K
SRC_V = SRC_K + N_HEADS * D_K
SRC_O = SRC_V + N_HEADS * D_V
N_MAIN = SRC_O + N_HEADS * D_V

HEAD_COLS = 2 * D_K + 2 * D_V
N_MLSTM = N_HEADS * HEAD_COLS
CONV_PARTS = 4
CONV_PART = D_CONV // CONV_PARTS
BG, CG, XT = 0, 1, 2


def _cols_conv(kind, part):
    start = N_MLSTM + (3 * part + kind) * CONV_PART
    return slice(start, start + CONV_PART)


def _conv_operand(z, kind, rows=slice(None)):
    return jnp.concatenate([z[rows, _cols_conv(kind, p)] for p in range(CONV_PARTS)], axis=1)


def _cols_q(h):
    return slice(h * HEAD_COLS, h * HEAD_COLS + D_K)


def _cols_k(h):
    return slice(h * HEAD_COLS + D_K, h * HEAD_COLS + 2 * D_K)


def _cols_v(h):
    return slice(h * HEAD_COLS + 2 * D_K, h * HEAD_COLS + 2 * D_K + D_V)


def _cols_o(h):
    return slice(h * HEAD_COLS + 2 * D_K + D_V, (h + 1) * HEAD_COLS)


def _source_lane_block(j):
    per_head = HEAD_COLS // LANES
    h, r = j // per_head, j % per_head
    v_blocks = D_V // LANES
    mlstm_src = jnp.where(
        r == 0, SRC_Q // LANES + h,
        jnp.where(r == 1, SRC_K // LANES + h,
                  jnp.where(r < 2 + v_blocks,
                            SRC_V // LANES + v_blocks * h + (r - 2),
                            SRC_O // LANES + v_blocks * h + (r - 2 - v_blocks))))
    per_kind = CONV_PART // LANES
    c = j - N_MLSTM // LANES
    part, r = c // (3 * per_kind), c % (3 * per_kind)
    conv_src = (r // per_kind) * (D_CONV // LANES) + part * per_kind + r % per_kind
    return jnp.where(j < N_MLSTM // LANES, mlstm_src, conv_src)


def _compiler_params(name, semantics):
    return pltpu.CompilerParams(dimension_semantics=semantics,
                                vmem_limit_bytes=VMEM_MIB[name] * MIB)


def _rmsnorm_piece(x, gain, dtype):
    ms = jnp.mean(x * x, axis=-1, keepdims=True)
    return ((x * lax.rsqrt(ms + EPS)) * gain).astype(dtype)


def _rmsnorm_rows(src_ref, gain, dst_ref, copy_ref=None):
    rows = src_ref.shape[0]
    chunk = min(NORM_ROWS, rows)

    def body(i, carry):
        r = pl.multiple_of(i * chunk, chunk)
        x = src_ref[pl.ds(r, chunk), :].astype(F32)
        if copy_ref is not None:
            copy_ref[pl.ds(r, chunk), :] = x
        dst_ref[pl.ds(r, chunk), :] = _rmsnorm_piece(x, gain, dst_ref.dtype)
        return carry

    lax.fori_loop(0, rows // chunk, body, 0, unroll=2)


def _rmsnorm_rows_inplace(ref, gain):
    rows = ref.shape[0]
    chunk = min(NORM_ROWS, rows)
    group = min(NORM_GROUP, rows // chunk)

    def body(i, carry):
        starts = [pl.multiple_of((i * group + k) * chunk, chunk) for k in range(group)]
        scales = []
        for r in starts:
            x = ref[pl.ds(r, chunk), :]
            scales.append(lax.rsqrt(jnp.mean(x * x, axis=-1, keepdims=True) + EPS))
        for r, scale in zip(starts, scales):
            ref[pl.ds(r, chunk), :] = (ref[pl.ds(r, chunk), :] * scale) * gain
        return carry

    lax.fori_loop(0, rows // (chunk * group), body, 0)


def _cast_kernel(*refs):
    *w_refs, o_ref = refs
    for r, w_ref in enumerate(w_refs):
        o_ref[:, r * LANES:(r + 1) * LANES] = w_ref[...].T.astype(o_ref.dtype)


def _cast_main_columns(w_t):
    per_slab = W_SLAB // LANES
    in_specs = [
        pl.BlockSpec((LANES, D_MODEL),
                     lambda j, r=r: (_source_lane_block(per_slab * j + r), 0))
        for r in range(per_slab)
    ]
    return pl.pallas_call(
        _cast_kernel,
        grid=(N_MAIN // W_SLAB,),
        in_specs=in_specs,
        out_specs=pl.BlockSpec((D_MODEL, W_SLAB), lambda j: (0, j)),
        out_shape=jax.ShapeDtypeStruct((D_MODEL, N_MAIN), BF16),
        compiler_params=_compiler_params("cast_w_in", ("parallel",)),
        name="cast_w_in",
    )(*([w_t] * per_slab))


GATE_BLOCK = pl.BlockSpec((2 * N_HEADS, D_MODEL), lambda *_: (N_MAIN // (2 * N_HEADS), 0))


def _gate_dot(xn, wgate_t_ref):
    rows = wgate_t_ref[...]
    padded = jnp.concatenate([rows, jnp.zeros((LANES - rows.shape[0], D_MODEL), F32)], axis=0)
    return lax.dot_general(xn, padded.astype(BF16), (((1,), (1,)), ((), ())),
                           preferred_element_type=F32)


def _inproj_kernel(x_ref, g_ref, w_ref, wgate_ref, z_ref, gcol_ref, grow_ref, xn_ref):
    @pl.when(pl.program_id(1) == 0)
    def _():
        _rmsnorm_rows(x_ref, g_ref[...], xn_ref)
        gc = _gate_dot(xn_ref[...], wgate_ref)
        gcol_ref[...] = gc
        grow_ref[...] = gc.T[0:GATE_ROWS, :]

    z_ref[...] = jnp.dot(xn_ref[...], w_ref[...], preferred_element_type=F32)


def _inproj(x2d, g1, w_bf, w_t):
    tokens = x2d.shape[0]
    tm = min(1024, tokens)
    tn = W_SLAB
    assert tokens % tm == 0 and tm % NORM_ROWS == 0, tokens
    return pl.pallas_call(
        _inproj_kernel,
        grid=(tokens // tm, N_MAIN // tn),
        in_specs=[
            pl.BlockSpec((tm, D_MODEL), lambda i, j: (i, 0)),
            pl.BlockSpec((1, D_MODEL), lambda i, j: (0, 0)),
            pl.BlockSpec((D_MODEL, tn), lambda i, j: (0, j)),
            GATE_BLOCK,
        ],
        out_specs=[
            pl.BlockSpec((tm, tn), lambda i, j: (i, j)),
            pl.BlockSpec((tm, LANES), lambda i, j: (i, 0)),
            pl.BlockSpec((GATE_ROWS, tm), lambda i, j: (0, i)),
        ],
        out_shape=[
            jax.ShapeDtypeStruct((tokens, N_MAIN), F32),
            jax.ShapeDtypeStruct((tokens, LANES), F32),
            jax.ShapeDtypeStruct((GATE_ROWS, tokens), F32),
        ],
        scratch_shapes=[pltpu.VMEM((tm, D_MODEL), BF16)],
        compiler_params=_compiler_params("inproj", ("parallel", "arbitrary")),
        name="inproj",
    )(x2d, g1, w_bf, w_t)


def _log_sigmoid(x):
    return jnp.minimum(x, 0.0) - jnp.log1p(jnp.exp(-jnp.abs(x)))


def _split3(x):
    hi = x.astype(BF16).astype(F32)
    mid = (x - hi).astype(BF16).astype(F32)
    lo = ((x - hi) - mid).astype(BF16).astype(F32)
    return hi, mid, lo


def _masked_sums(mask, lf_col, lf_row):
    m_bf = mask.astype(BF16)
    n, k = lf_col.shape[1], lf_row.shape[0]
    col_terms = jnp.concatenate(_split3(lf_col), axis=1).astype(BF16)
    row_terms = jnp.concatenate(_split3(lf_row) + (jnp.zeros_like(lf_row),),
                                axis=0).astype(BF16)
    cols = jnp.dot(m_bf, col_terms, preferred_element_type=F32)
    rows = lax.dot_general(row_terms, m_bf, (((1,), (1,)), ((), ())),
                           preferred_element_type=F32)
    return (cols[:, 0:n] + cols[:, n:2 * n] + cols[:, 2 * n:3 * n],
            rows[0:k] + rows[k:2 * k] + rows[2 * k:3 * k])


def _intra(q_bf, k_bf, b_col, b_row, li_row, m0_col, mask):
    dmat = jnp.where(mask, b_col - b_row + li_row, -jnp.inf)
    inter = b_col + m0_col
    m_t = jnp.maximum(inter, jnp.max(dmat, axis=1, keepdims=True))
    w = jnp.exp(dmat - m_t)
    a_inter = jnp.exp(inter - m_t)
    qk = lax.dot_general(q_bf, k_bf, (((1,), (1,)), ((), ())), preferred_element_type=F32)
    return qk * w, a_inter, m_t


def _head_output(num, den, m_t, o_gate, gain):
    denom = jnp.maximum(jnp.abs(den), jnp.exp(-m_t))
    h = num * (1.0 / denom)
    hn = (h * lax.rsqrt(jnp.mean(h * h, axis=1, keepdims=True) + EPS)) * gain
    return jax.nn.sigmoid(o_gate) * hn


def _conv_norm(bg, conv, gain):
    yc = bg * conv
    return (yc * lax.rsqrt(jnp.mean(yc * yc, axis=1, keepdims=True) + EPS)) * gain


def _prompt_conv_part(z, part, convw_ref, ubuf, yc_scr):
    L = z.shape[0]
    ch = slice(part * CONV_PART, (part + 1) * CONV_PART)
    u = z[:, _cols_conv(CG, part)] * z[:, _cols_conv(XT, part)]
    ubuf[SUBLANES:SUBLANES + L, ch] = u
    conv = (ubuf[SUBLANES - 2:SUBLANES - 2 + L, ch] * convw_ref[0:1, ch]
            + ubuf[SUBLANES - 1:SUBLANES - 1 + L, ch] * convw_ref[1:2, ch]
            + u * convw_ref[2:3, ch])
    ubuf[0:SUBLANES, ch] = ubuf[L:L + SUBLANES, ch]
    yc = z[:, _cols_conv(BG, part)] * conv
    yc_scr[:, ch] = yc
    return jnp.sum(yc * yc, axis=1, keepdims=True)


def _prompt_conv_finish(sumsq, yc_scr, convg_ref, y_ref):
    scale = lax.rsqrt(sumsq * (1.0 / D_CONV) + EPS)
    y_ref[:, 0:D_CONV] = ((yc_scr[...] * scale) * convg_ref[...]).astype(y_ref.dtype)


def _prompt_mlstm(z, gc, gr, mlg_ref, bias_row_ref, bias_col_ref, y_ref, c_scr, n_scr, m_scr,
                  fillers):
    L = z.shape[0]
    fillers = list(fillers)

    def fill():
        if fillers:
            fillers.pop(0)()

    pre_c = gc + bias_row_ref[...]
    pre_r = gr + bias_col_ref[:, 0:1]
    row_id = lax.broadcasted_iota(jnp.int32, (L, L), 0)
    col_id = lax.broadcasted_iota(jnp.int32, (L, L), 1)
    causal = col_id <= row_id
    b_c, b_r = _masked_sums(causal, _log_sigmoid(pre_c), _log_sigmoid(pre_r))

    def scores(h):
        li_r = pre_r[h:h + 1, :]
        bc = b_c[:, N_HEADS + h:N_HEADS + h + 1]
        br = b_r[N_HEADS + h:N_HEADS + h + 1, :]
        m0 = m_scr[h, 0:1, 0:1]
        qf = z[:, _cols_q(h)] * (D_K ** -0.5)
        q_bf = qf.astype(BF16)
        kf = z[:, _cols_k(h)]
        v_bf = z[:, _cols_v(h)].astype(BF16)
        s, a_inter, m_t = _intra(q_bf, kf.astype(BF16), bc, br, li_r, m0, causal)
        return li_r, bc, br, m0, qf, q_bf, kf, v_bf, s, a_inter, m_t

    def finish(h, staged):
        li_r, bc, br, m0, qf, q_bf, kf, v_bf, s, a_inter, m_t = staged
        li_c = pre_c[:, h:h + 1]
        c_old = c_scr[h]
        n_old = n_scr[h]
        num = (a_inter * jnp.dot(q_bf, c_old.astype(BF16), preferred_element_type=F32)
               + jnp.dot(s.astype(BF16), v_bf, preferred_element_type=F32))
        den = (a_inter * jnp.sum(qf * n_old, axis=1, keepdims=True)
               + jnp.sum(s, axis=1, keepdims=True))
        fill()
        y_ref[:, D_CONV + h * D_V:D_CONV + (h + 1) * D_V] = _head_output(
            num, den, m_t, z[:, _cols_o(h)],
            mlg_ref[:, h * D_V:(h + 1) * D_V]).astype(y_ref.dtype)

        b_last = br[:, L - 1:L]
        m_new = jnp.maximum(b_last + m0,
                            jnp.max(b_last - br + li_r, axis=1, keepdims=True))
        decay = jnp.exp(b_last + m0 - m_new)
        kw = kf * jnp.exp(b_last - bc + li_c - m_new)
        c_scr[h] = decay * c_old + lax.dot_general(
            kw.astype(BF16), v_bf, (((0,), (0,)), ((), ())), preferred_element_type=F32)
        n_scr[h] = decay * n_old + jnp.sum(kw, axis=0, keepdims=True)
        m_scr[h] = jnp.broadcast_to(m_new, (SUBLANES, LANES))
        fill()

    staged = scores(0)
    fill()
    for h in range(N_HEADS):
        upcoming = None
        if h + 1 < N_HEADS:
            upcoming = scores(h + 1)
            fill()
        finish(h, staged)
        staged = upcoming
    while fillers:
        fill()


def _fused_prompt_kernel(x_ref, g1_ref, w_ref, wgate_ref, convw_ref, convg_ref, mlg_ref,
                         bias_row_ref, bias_col_ref, wg_ref, wu_ref, wout_ref,
                         y_ref, convs_ref, c_out_ref, n_out_ref, m_out_ref,
                         wgu_bf_ref, wout_bf_ref,
                         xn_scr, z_scr, ubuf, yc_scr, c_scr, n_scr, m_scr):
    chunk = pl.program_id(1)
    L = x_ref.shape[0]

    @pl.when(chunk == 0)
    def _():
        ubuf[0:SUBLANES, :] = jnp.zeros((SUBLANES, D_CONV), F32)
        c_scr[...] = jnp.zeros(c_scr.shape, F32)
        n_scr[...] = jnp.zeros(n_scr.shape, F32)
        m_scr[...] = jnp.zeros(m_scr.shape, F32)

    gain = g1_ref[...]
    for r in range(0, L, NORM_ROWS):
        xn_scr[r:r + NORM_ROWS, :] = _rmsnorm_piece(x_ref[r:r + NORM_ROWS, :], gain, BF16)
    xn = xn_scr[...]
    gc = _gate_dot(xn, wgate_ref)
    gr = gc.T[0:SUBLANES, :]

    def project(j):
        cols = slice(j * W_SLAB, (j + 1) * W_SLAB)
        z_scr[:, cols] = jnp.dot(xn_scr[...], w_ref[:, cols], preferred_element_type=F32)

    piece = lambda j: functools.partial(project, j)
    skip = lambda: None
    sumsq = []

    def then_conv(j, part):
        def emit():
            project(j)
            sumsq.append(_prompt_conv_part(z_scr, part, convw_ref, ubuf, yc_scr))
        return emit

    project(0)
    wgu_bf_ref[0, :, 0:CAST_PIECE] = wg_ref[...].astype(BF16)
    wout_bf_ref[...] = wout_ref[...].astype(BF16)
    project(1)
    wgu_bf_ref[0, :, CAST_PIECE:2 * CAST_PIECE] = wu_ref[...].astype(BF16)
    fillers = [piece(2), piece(3), piece(4), piece(6),
               piece(5), then_conv(7, 0), then_conv(8, 1),
               piece(9), then_conv(10, 2), then_conv(11, 3),
               skip, skip]
    _prompt_mlstm(z_scr, gc, gr, mlg_ref, bias_row_ref, bias_col_ref, y_ref,
                  c_scr, n_scr, m_scr, fillers)
    _prompt_conv_finish(sum(sumsq[1:], sumsq[0]), yc_scr, convg_ref, y_ref)

    @pl.when(chunk == pl.num_programs(1) - 1)
    def _():
        convs_ref[0] = ubuf[SUBLANES - 2:SUBLANES, :]
        c_out_ref[0] = c_scr[...]
        for h in range(N_HEADS):
            n_out_ref[0, h:h + 1, :] = n_scr[h]
            m_out_ref[0, h:h + 1, :] = m_scr[h, 0:1, :]


def _fused_prompt(x2d, g1, w_bf, w_t, batch, seq, conv_w, conv_g, ml_g, bias_row, bias_col,
                  wg, wu, w_out):
    L = PROMPT_CHUNK
    nc = seq // L
    assert nc * L == seq, seq
    const = lambda b, c: (0, 0)
    wout_rows = D_MODEL // (batch * nc)
    assert wout_rows * batch * nc == D_MODEL and wout_rows % (2 * SUBLANES) == 0
    wout_piece = pl.BlockSpec((wout_rows, D_MODEL), lambda b, c: (b * nc + c, 0))
    n_pieces = D_FF // CAST_PIECE
    assert batch * nc >= n_pieces, "not enough grid steps to cast the FFN weights"
    per_tile = FFN_TILE // CAST_PIECE
    piece_of = lambda b, c: jnp.minimum(b * nc + c, n_pieces - 1)
    tiled = lambda b, c: (piece_of(b, c) // per_tile, 0, piece_of(b, c) % per_tile)
    return pl.pallas_call(
        _fused_prompt_kernel,
        grid=(batch, nc),
        in_specs=[
            pl.BlockSpec((L, D_MODEL), lambda b, c: (b * nc + c, 0)),
            pl.BlockSpec((1, D_MODEL), const),
            pl.BlockSpec((D_MODEL, N_MAIN), const, pipeline_mode=pl.Buffered(1)),
            GATE_BLOCK,
            pl.BlockSpec((CONV_W, D_CONV), const),
            pl.BlockSpec((1, D_CONV), const),
            pl.BlockSpec((1, N_HEADS * D_V), const),
            pl.BlockSpec((1, LANES), const),
            pl.BlockSpec((SUBLANES, LANES), const),
            pl.BlockSpec((D_MODEL, CAST_PIECE), lambda b, c: (0, piece_of(b, c))),
            pl.BlockSpec((D_MODEL, CAST_PIECE), lambda b, c: (0, piece_of(b, c))),
            wout_piece,
        ],
        out_specs=[
            pl.BlockSpec((L, D_MODEL), lambda b, c: (b * nc + c, 0)),
            pl.BlockSpec((1, CONV_W - 1, D_CONV), lambda b, c: (b, 0, 0)),
            pl.BlockSpec((1, N_HEADS, D_K, D_V), lambda b, c: (b, 0, 0, 0)),
            pl.BlockSpec((1, N_HEADS, D_K), lambda b, c: (b, 0, 0)),
            pl.BlockSpec((1, N_HEADS, LANES), lambda b, c: (b, 0, 0)),
            pl.BlockSpec((1, D_MODEL, 2 * CAST_PIECE), tiled),
            wout_piece,
        ],
        out_shape=[
            jax.ShapeDtypeStruct((batch * seq, D_MODEL), BF16),
            jax.ShapeDtypeStruct((batch, CONV_W - 1, D_CONV), F32),
            jax.ShapeDtypeStruct((batch, N_HEADS, D_K, D_V), F32),
            jax.ShapeDtypeStruct((batch, N_HEADS, D_K), F32),
            jax.ShapeDtypeStruct((batch, N_HEADS, LANES), F32),
            jax.ShapeDtypeStruct((D_FF // FFN_TILE, D_MODEL, 2 * FFN_TILE), BF16),
            jax.ShapeDtypeStruct((D_MODEL, D_MODEL), BF16),
        ],
        scratch_shapes=[
            pltpu.VMEM((L, D_MODEL), BF16),
            pltpu.VMEM((L, N_MAIN), F32),
            pltpu.VMEM((L + 2 * SUBLANES, D_CONV), F32),
            pltpu.VMEM((L, D_CONV), F32),
            pltpu.VMEM((N_HEADS, D_K, D_V), F32),
            pltpu.VMEM((N_HEADS, 1, D_K), F32),
            pltpu.VMEM((N_HEADS, SUBLANES, LANES), F32),
        ],
        compiler_params=_compiler_params("fused_prompt", ("arbitrary", "arbitrary")),
        name="fused_prompt",
    )(x2d, g1, w_bf, w_t, conv_w, conv_g, ml_g, bias_row, bias_col, wg, wu, w_out)


def _mix_sample_kernel(z_ref, gcol_ref, grow_ref, mrow_ref,
                       convs_in_ref, c_in_ref, n_in_ref,
                       convw_ref, convg_ref, mlg_ref, bias_row_ref, bias_col_ref,
                       y_ref, convs_ref, c_out_ref, n_out_ref, m_out_ref,
                       pad_scr, conv_scr, *, seq):
    L = z_ref.shape[0]
    group = L // seq
    stride = seq + SUBLANES

    for g in range(group):
        rows = slice(g * seq, (g + 1) * seq)
        u = _conv_operand(z_ref, CG, rows) * _conv_operand(z_ref, XT, rows)
        base = g * stride
        pad_scr[base + SUBLANES - 2:base + SUBLANES, :] = convs_in_ref[g]
        pad_scr[base + SUBLANES:base + SUBLANES + seq, :] = u
        conv_scr[g * seq:(g + 1) * seq, :] = (
            pad_scr[base + SUBLANES - 2:base + SUBLANES - 2 + seq, :] * convw_ref[0:1, :]
            + pad_scr[base + SUBLANES - 1:base + SUBLANES - 1 + seq, :] * convw_ref[1:2, :]
            + u * convw_ref[2:3, :])
        convs_ref[g] = pad_scr[base + seq + SUBLANES - 2:base + seq + SUBLANES, :]
    y_ref[:, 0:D_CONV] = _conv_norm(_conv_operand(z_ref, BG), conv_scr[...],
                                    convg_ref[...]).astype(y_ref.dtype)

    pre_c = gcol_ref[...] + bias_row_ref[...]
    pre_r = grow_ref[0:SUBLANES, :] + bias_col_ref[:, 0:1]
    row_id = lax.broadcasted_iota(jnp.int32, (L, L), 0)
    col_id = lax.broadcasted_iota(jnp.int32, (L, L), 1)
    same = (row_id // seq) == (col_id // seq)
    causal = same & (col_id <= row_id)
    lf_c = _log_sigmoid(pre_c)
    lf_r = _log_sigmoid(pre_r)
    b_c, b_r = _masked_sums(causal, lf_c, lf_r)
    tot_c, tot_r = _masked_sums(same, lf_c, lf_r)
    lane_seq = lax.broadcasted_iota(jnp.int32, (1, L), 1) // seq
    m0_cols = mrow_ref[...].T

    for h in range(N_HEADS):
        li_r = pre_r[h:h + 1, :]
        li_c = pre_c[:, h:h + 1]
        bc = b_c[:, N_HEADS + h:N_HEADS + h + 1]
        br = b_r[N_HEADS + h:N_HEADS + h + 1, :]
        bl_c = tot_c[:, N_HEADS + h:N_HEADS + h + 1]
        bl_r = tot_r[N_HEADS + h:N_HEADS + h + 1, :]
        m0_c = m0_cols[:, h:h + 1]
        m0_r = mrow_ref[h:h + 1, :]
        qf = z_ref[:, _cols_q(h)] * (D_K ** -0.5)
        q_bf = qf.astype(BF16)
        kf = z_ref[:, _cols_k(h)]
        v_bf = z_ref[:, _cols_v(h)].astype(BF16)
        s, a_inter, m_t = _intra(q_bf, kf.astype(BF16), bc, br, li_r, m0_c, causal)

        qc_rows, qn_rows = [], []
        for g in range(group):
            rows = slice(g * seq, (g + 1) * seq)
            qc_rows.append(jnp.dot(qf[rows], c_in_ref[g, h], preferred_element_type=F32))
            qn_rows.append(jnp.sum(qf[rows] * n_in_ref[g, h:h + 1, :], axis=1, keepdims=True))
        num = (a_inter * jnp.concatenate(qc_rows, axis=0)
               + jnp.dot(s.astype(BF16), v_bf, preferred_element_type=F32))
        den = (a_inter * jnp.concatenate(qn_rows, axis=0)
               + jnp.sum(s, axis=1, keepdims=True))
        y_ref[:, D_CONV + h * D_V:D_CONV + (h + 1) * D_V] = _head_output(
            num, den, m_t, z_ref[:, _cols_o(h)],
            mlg_ref[:, h * D_V:(h + 1) * D_V]).astype(y_ref.dtype)

        g_r = bl_r - br + li_r
        gmax_c = jnp.max(jnp.where(same, g_r, -jnp.inf), axis=1, keepdims=True)
        gmax_r = jnp.max(jnp.where(same, gmax_c, -jnp.inf), axis=0, keepdims=True)
        m_new_c = jnp.maximum(bl_c + m0_c, gmax_c)
        m_new_r = jnp.maximum(bl_r + m0_r, gmax_r)
        decay_r = jnp.exp(bl_r + m0_r - m_new_r)
        kw = kf * jnp.exp(bl_c - bc + li_c - m_new_c)
        kw_t = kw.T
        for g in range(group):
            decay = decay_r[:, g * seq:g * seq + 1]
            kw_g = jnp.where(lane_seq == g, kw_t, 0.0).astype(BF16)
            c_out_ref[g, h] = decay * c_in_ref[g, h] + jnp.dot(
                kw_g, v_bf, preferred_element_type=F32)
            n_out_ref[g, h:h + 1, :] = (decay * n_in_ref[g, h:h + 1, :]
                                        + jnp.sum(kw[g * seq:(g + 1) * seq], axis=0, keepdims=True))
        m_out_ref[h:h + 1, :] = m_new_r
    m_out_ref[N_HEADS:, :] = jnp.zeros((SUBLANES - N_HEADS, L), F32)


def _mix_sample(z, gcol, grow, mrow, convs, c_state, n_state, batch, seq,
                conv_w, conv_g, ml_g, bias_row, bias_col):
    group = SAMPLE_GROUP
    L = group * seq
    assert seq == SUBLANES and L == LANES and batch % group == 0, (batch, seq)
    const = lambda i: (0, 0)
    return pl.pallas_call(
        functools.partial(_mix_sample_kernel, seq=seq),
        grid=(batch // group,),
        in_specs=[
            pl.BlockSpec((L, N_MAIN), lambda i: (i, 0)),
            pl.BlockSpec((L, LANES), lambda i: (i, 0)),
            pl.BlockSpec((GATE_ROWS, L), lambda i: (0, i)),
            pl.BlockSpec((SUBLANES, L), lambda i: (0, i)),
            pl.BlockSpec((group, CONV_W - 1, D_CONV), lambda i: (i, 0, 0)),
            pl.BlockSpec((group, N_HEADS, D_K, D_V), lambda i: (i, 0, 0, 0)),
            pl.BlockSpec((group, N_HEADS, D_K), lambda i: (i, 0, 0)),
            pl.BlockSpec((CONV_W, D_CONV), const),
            pl.BlockSpec((1, D_CONV), const),
            pl.BlockSpec((1, N_HEADS * D_V), const),
            pl.BlockSpec((1, LANES), const),
            pl.BlockSpec((SUBLANES, LANES), const),
        ],
        out_specs=[
            pl.BlockSpec((L, D_MODEL), lambda i: (i, 0)),
            pl.BlockSpec((group, CONV_W - 1, D_CONV), lambda i: (i, 0, 0)),
            pl.BlockSpec((group, N_HEADS, D_K, D_V), lambda i: (i, 0, 0, 0)),
            pl.BlockSpec((group, N_HEADS, D_K), lambda i: (i, 0, 0)),
            pl.BlockSpec((SUBLANES, L), lambda i: (0, i)),
        ],
        out_shape=[
            jax.ShapeDtypeStruct((batch * seq, D_MODEL), BF16),
            jax.ShapeDtypeStruct((batch, CONV_W - 1, D_CONV), F32),
            jax.ShapeDtypeStruct((batch, N_HEADS, D_K, D_V), F32),
            jax.ShapeDtypeStruct((batch, N_HEADS, D_K), F32),
            jax.ShapeDtypeStruct((SUBLANES, batch * seq), F32),
        ],
        scratch_shapes=[
            pltpu.VMEM((group * (seq + SUBLANES), D_CONV), F32),
            pltpu.VMEM((L, D_CONV), F32),
        ],
        compiler_params=_compiler_params("mix_sample", ("parallel",)),
        name="mix_sample",
    )(z, gcol, grow, mrow, convs, c_state, n_state,
      conv_w, conv_g, ml_g, bias_row, bias_col)


def _outproj_kernel(ya_ref, yb_ref, w_ref, xa_ref, xb_ref, o_ref, *, tiles_a):
    i = pl.program_id(0)

    @pl.when(i < tiles_a)
    def _():
        o_ref[...] = xa_ref[...] + jnp.dot(ya_ref[...], w_ref[...], preferred_element_type=F32)

    @pl.when(i >= tiles_a)
    def _():
        o_ref[...] = xb_ref[...] + jnp.dot(yb_ref[...], w_ref[...], preferred_element_type=F32)


def _outproj(ya, yb, w_out_bf, xa, xb):
    tm = 512
    assert xa.shape[0] % tm == 0 and xb.shape[0] % tm == 0, (xa.shape, xb.shape)
    tiles_a, tiles_b = xa.shape[0] // tm, xb.shape[0] // tm
    rows_a = pl.BlockSpec((tm, D_MODEL), lambda i: (jnp.minimum(i, tiles_a - 1), 0))
    rows_b = pl.BlockSpec((tm, D_MODEL), lambda i: (jnp.maximum(i - tiles_a, 0), 0))
    return pl.pallas_call(
        functools.partial(_outproj_kernel, tiles_a=tiles_a),
        grid=(tiles_a + tiles_b,),
        in_specs=[
            rows_a, rows_b,
            pl.BlockSpec((D_MODEL, D_MODEL), lambda i: (0, 0), pipeline_mode=pl.Buffered(1)),
            rows_a, rows_b,
        ],
        out_specs=pl.BlockSpec((tm, D_MODEL), lambda i: (i, 0)),
        out_shape=jax.ShapeDtypeStruct(((tiles_a + tiles_b) * tm, D_MODEL), F32),
        compiler_params=_compiler_params("outproj", ("parallel",)),
        name="outproj",
    )(ya, yb, w_out_bf, xa, xb)


def _ffn_kernel(x_ref, g2_ref, wgu_ref, wd_ref, gf_ref, o_ref, hn_ref, *, final_norm):
    j = pl.program_id(1)

    @pl.when(j == 0)
    def _():
        _rmsnorm_rows(x_ref, g2_ref[...], hn_ref, copy_ref=o_ref)

    hn = hn_ref[...]
    half = wd_ref.shape[0] // 2
    halves = (slice(0, half), slice(half, 2 * half))
    pre = []
    for a in range(2):
        gate_up = jnp.dot(hn, wgu_ref[0, :, 2 * a * half:2 * (a + 1) * half],
                          preferred_element_type=F32)
        pre.append((gate_up[:, 0:half], gate_up[:, half:2 * half]))
    down = None
    for (gate, up), rows in zip(pre, halves):
        act = ((gate * jax.nn.sigmoid(gate)) * up).astype(BF16)
        part = jnp.dot(act, wd_ref[rows, :].astype(BF16), preferred_element_type=F32)
        down = part if down is None else down + part
    o_ref[...] += down

    if final_norm:
        @pl.when(j == pl.num_programs(1) - 1)
        def _():
            _rmsnorm_rows_inplace(o_ref, gf_ref[...])


def _ffn(x1, first_row, tokens, g2, wgu_bf, wd, gf, final_norm):
    tm = min(1024, tokens)
    tf = FFN_TILE
    assert tokens % tm == 0 and first_row % tm == 0, (first_row, tokens)
    assert tm % (NORM_ROWS * NORM_GROUP) == 0, tm
    first_tile = first_row // tm
    return pl.pallas_call(
        functools.partial(_ffn_kernel, final_norm=final_norm),
        grid=(tokens // tm, D_FF // tf),
        in_specs=[
            pl.BlockSpec((tm, D_MODEL), lambda i, j: (i + first_tile, 0)),
            pl.BlockSpec((1, D_MODEL), lambda i, j: (0, 0)),
            pl.BlockSpec((1, D_MODEL, 2 * tf), lambda i, j: (j, 0, 0)),
            pl.BlockSpec((tf, D_MODEL), lambda i, j: (j, 0)),
            pl.BlockSpec((1, D_MODEL), lambda i, j: (0, 0)),
        ],
        out_specs=pl.BlockSpec((tm, D_MODEL), lambda i, j: (i, 0)),
        out_shape=jax.ShapeDtypeStruct((tokens, D_MODEL), F32),
        scratch_shapes=[pltpu.VMEM((tm, D_MODEL), BF16)],
        compiler_params=_compiler_params("ffn", ("parallel", "arbitrary")),
        name="ffn",
    )(x1, g2, wgu_bf, wd, gf)


def _gate_bias(b_i, b_f):
    bias = jnp.concatenate([b_i, b_f]).astype(F32)
    bias_row = jnp.pad(bias, (0, LANES - 2 * N_HEADS))[None, :]
    bias_col = jnp.broadcast_to(bias[:, None], (SUBLANES, LANES))
    return bias_row, bias_col


def _stack(states, k):
    if len(states) == 1:
        return states[0][k][None]
    return jnp.stack([st[k] for st in states])


def kernel(x_prompt, x_sample, state_conv, state_mlstm_C, state_mlstm_n, state_mlstm_m,
           norm1_g, w_in, b_igate, b_fgate, conv_w, conv_out_g, mlstm_out_g, w_out,
           norm2_g, w_ffn_gate, w_ffn_up, w_ffn_down, final_norm_g):
    depth = w_in.shape[0]
    bp, sp, _ = x_prompt.shape
    bs, ss, _ = x_sample.shape
    hp = x_prompt.reshape(bp * sp, D_MODEL)
    hs = x_sample.reshape(bs * ss, D_MODEL)
    p_states, s_states = [], []
    for l in range(depth):
        final_norm = l == depth - 1
        w_t = w_in[l].T
        bias_row, bias_col = _gate_bias(b_igate[l], b_fgate[l])
        w_bf = _cast_main_columns(w_t)
        g1 = norm1_g[l][None, :]
        conv_g = conv_out_g[l][None, :]
        ml_g = mlstm_out_g[l][None, :]

        y_p, cb, c1, n1, m1, wgu_bf, w_out_bf = _fused_prompt(
            hp, g1, w_bf, w_t, bp, sp, conv_w[l], conv_g, ml_g, bias_row, bias_col,
            w_ffn_gate[l], w_ffn_up[l], w_out[l])
        p_states.append((cb, c1, n1, m1[:, :, 0]))

        z, gcol, grow = _inproj(hs, g1, w_bf, w_t)
        m_tok = jnp.repeat(state_mlstm_m[l], ss, axis=0)
        mrow = jnp.pad(m_tok.T, ((0, SUBLANES - N_HEADS), (0, 0)))
        y_s, cb, c1, n1, m_row = _mix_sample(z, gcol, grow, mrow, state_conv[l],
                                             state_mlstm_C[l], state_mlstm_n[l], bs, ss,
                                             conv_w[l], conv_g, ml_g, bias_row, bias_col)
        s_states.append((cb, c1, n1, m_row[:N_HEADS, ::ss].T))

        x1 = _outproj(y_p, y_s, w_out_bf, hp, hs)
        ffn = (norm2_g[l][None, :], wgu_bf, w_ffn_down[l], final_norm_g[None, :], final_norm)
        hp = _ffn(x1, 0, bp * sp, *ffn)
        hs = _ffn(x1, bp * sp, bs * ss, *ffn)

    return (hp.reshape(bp, sp, D_MODEL), hs.reshape(bs, ss, D_MODEL),
            _stack(p_states, 0), _stack(p_states, 1), _stack(p_states, 2), _stack(p_states, 3),
            _stack(s_states, 0), _stack(s_states, 1), _stack(s_states, 2), _stack(s_states, 3))
```

```python
import functools

import jax
import jax.numpy as jnp
from jax import lax
from jax.experimental import pallas as pl
from jax.experimental.pallas import tpu as pltpu

F32 = jnp.float32
BF16 = jnp.bfloat16

D_MODEL = 2048
D_CONV = 1024
CONV_W = 3
N_HEADS = 4
D_K = 128
D_V = 256
D_FF = 5632
EPS = 1e-6

LANES = 128
SUBLANES = 8
GATE_ROWS = 16
MIB = 1024 * 1024

PROMPT_CHUNK = 256
SAMPLE_GROUP = 16
NORM_ROWS = 64
NORM_GROUP = 4
W_SLAB = 512
FFN_TILE = 512
CAST_PIECE = 256

VMEM_MIB = {"cast_w_in": 62, "inproj": 62, "fused_prompt": 62, "mix_sample": 62,
            "outproj": 62, "ffn": 62}

SRC_Q = 3 * D_CONV
SRC_K = SRC_Q + N_HEADS * D_K
SRC_V = SRC_K + N_HEADS * D_K
SRC_O = SRC_V + N_HEADS * D_V
N_MAIN = SRC_O + N_HEADS * D_V

HEAD_COLS = 2 * D_K + 2 * D_V
N_MLSTM = N_HEADS * HEAD_COLS
CONV_PARTS = 4
CONV_PART = D_CONV // CONV_PARTS
BG, CG, XT = 0, 1, 2


def _cols_conv(kind, part):
    start = N_MLSTM + (3 * part + kind) * CONV_PART
    return slice(start, start + CONV_PART)


def _conv_operand(z, kind, rows=slice(None)):
    return jnp.concatenate([z[rows, _cols_conv(kind, p)] for p in range(CONV_PARTS)], axis=1)


def _cols_q(h):
    return slice(h * HEAD_COLS, h * HEAD_COLS + D_K)


def _cols_k(h):
    return slice(h * HEAD_COLS + D_K, h * HEAD_COLS + 2 * D_K)


def _cols_v(h):
    return slice(h * HEAD_COLS + 2 * D_K, h * HEAD_COLS + 2 * D_K + D_V)


def _cols_o(h):
    return slice(h * HEAD_COLS + 2 * D_K + D_V, (h + 1) * HEAD_COLS)


def _source_lane_block(j):
    per_head = HEAD_COLS // LANES
    h, r = j // per_head, j % per_head
    v_blocks = D_V // LANES
    mlstm_src = jnp.where(
        r == 0, SRC_Q // LANES + h,
        jnp.where(r == 1, SRC_K // LANES + h,
                  jnp.where(r < 2 + v_blocks,
                            SRC_V // LANES + v_blocks * h + (r - 2),
                            SRC_O // LANES + v_blocks * h + (r - 2 - v_blocks))))
    per_kind = CONV_PART // LANES
    c = j - N_MLSTM // LANES
    part, r = c // (3 * per_kind), c % (3 * per_kind)
    conv_src = (r // per_kind) * (D_CONV // LANES) + part * per_kind + r % per_kind
    return jnp.where(j < N_MLSTM // LANES, mlstm_src, conv_src)


def _compiler_params(name, semantics):
    return pltpu.CompilerParams(dimension_semantics=semantics,
                                vmem_limit_bytes=VMEM_MIB[name] * MIB)


def _rmsnorm_piece(x, gain, dtype):
    ms = jnp.mean(x * x, axis=-1, keepdims=True)
    return ((x * lax.rsqrt(ms + EPS)) * gain).astype(dtype)


def _rmsnorm_rows(src_ref, gain, dst_ref, copy_ref=None):
    rows = src_ref.shape[0]
    chunk = min(NORM_ROWS, rows)

    def body(i, carry):
        r = pl.multiple_of(i * chunk, chunk)
        x = src_ref[pl.ds(r, chunk), :].astype(F32)
        if copy_ref is not None:
            copy_ref[pl.ds(r, chunk), :] = x
        dst_ref[pl.ds(r, chunk), :] = _rmsnorm_piece(x, gain, dst_ref.dtype)
        return carry

    lax.fori_loop(0, rows // chunk, body, 0, unroll=2)


def _rmsnorm_rows_inplace(ref, gain):
    rows = ref.shape[0]
    chunk = min(NORM_ROWS, rows)
    group = min(NORM_GROUP, rows // chunk)

    def body(i, carry):
        starts = [pl.multiple_of((i * group + k) * chunk, chunk) for k in range(group)]
        scales = []
        for r in starts:
            x = ref[pl.ds(r, chunk), :]
            scales.append(lax.rsqrt(jnp.mean(x * x, axis=-1, keepdims=True) + EPS))
        for r, scale in zip(starts, scales):
            ref[pl.ds(r, chunk), :] = (ref[pl.ds(r, chunk), :] * scale) * gain
        return carry

    lax.fori_loop(0, rows // (chunk * group), body, 0)


def _cast_kernel(*refs):
    *w_refs, o_ref = refs
    for r, w_ref in enumerate(w_refs):
        o_ref[:, r * LANES:(r + 1) * LANES] = w_ref[...].T.astype(o_ref.dtype)


def _cast_main_columns(w_t):
    per_slab = W_SLAB // LANES
    in_specs = [
        pl.BlockSpec((LANES, D_MODEL),
                     lambda j, r=r: (_source_lane_block(per_slab * j + r), 0))
        for r in range(per_slab)
    ]
    return pl.pallas_call(
        _cast_kernel,
        grid=(N_MAIN // W_SLAB,),
        in_specs=in_specs,
        out_specs=pl.BlockSpec((D_MODEL, W_SLAB), lambda j: (0, j)),
        out_shape=jax.ShapeDtypeStruct((D_MODEL, N_MAIN), BF16),
        compiler_params=_compiler_params("cast_w_in", ("parallel",)),
        name="cast_w_in",
    )(*([w_t] * per_slab))


GATE_BLOCK = pl.BlockSpec((2 * N_HEADS, D_MODEL), lambda *_: (N_MAIN // (2 * N_HEADS), 0))


def _gate_dot(xn, wgate_t_ref):
    rows = wgate_t_ref[...]
    padded = jnp.concatenate([rows, jnp.zeros((LANES - rows.shape[0], D_MODEL), F32)], axis=0)
    return lax.dot_general(xn, padded.astype(BF16), (((1,), (1,)), ((), ())),
                           preferred_element_type=F32)


def _inproj_kernel(x_ref, g_ref, w_ref, wgate_ref, z_ref, gcol_ref, grow_ref, xn_ref):
    j = pl.program_id(1)

    @pl.when(j == 0)
    def _():
        gain = g_ref[...]
        piece = NORM_ROWS * NORM_GROUP
        for r in range(0, x_ref.shape[0], piece):
            for c in range(r, r + piece, NORM_ROWS):
                xn_ref[c:c + NORM_ROWS, :] = _rmsnorm_piece(x_ref[c:c + NORM_ROWS, :], gain, BF16)
            z_ref[r:r + piece, :] = jnp.dot(xn_ref[r:r + piece, :], w_ref[...],
                                            preferred_element_type=F32)
        gc = _gate_dot(xn_ref[...], wgate_ref)
        gcol_ref[...] = gc
        grow_ref[...] = gc.T[0:GATE_ROWS, :]

    @pl.when(j > 0)
    def _():
        z_ref[...] = jnp.dot(xn_ref[...], w_ref[...], preferred_element_type=F32)


def _inproj(x2d, g1, w_bf, w_t):
    tokens = x2d.shape[0]
    tm = min(1024, tokens)
    tn = W_SLAB
    assert tokens % tm == 0 and tm % (NORM_ROWS * NORM_GROUP) == 0, tokens
    return pl.pallas_call(
        _inproj_kernel,
        grid=(tokens // tm, N_MAIN // tn),
        in_specs=[
            pl.BlockSpec((tm, D_MODEL), lambda i, j: (i, 0)),
            pl.BlockSpec((1, D_MODEL), lambda i, j: (0, 0)),
            pl.BlockSpec((D_MODEL, tn), lambda i, j: (0, j)),
            GATE_BLOCK,
        ],
        out_specs=[
            pl.BlockSpec((tm, tn), lambda i, j: (i, j)),
            pl.BlockSpec((tm, LANES), lambda i, j: (i, 0)),
            pl.BlockSpec((GATE_ROWS, tm), lambda i, j: (0, i)),
        ],
        out_shape=[
            jax.ShapeDtypeStruct((tokens, N_MAIN), F32),
            jax.ShapeDtypeStruct((tokens, LANES), F32),
            jax.ShapeDtypeStruct((GATE_ROWS, tokens), F32),
        ],
        scratch_shapes=[pltpu.VMEM((tm, D_MODEL), BF16)],
        compiler_params=_compiler_params("inproj", ("parallel", "arbitrary")),
        name="inproj",
    )(x2d, g1, w_bf, w_t)


def _log_sigmoid(x):
    return jnp.minimum(x, 0.0) - jnp.log1p(jnp.exp(-jnp.abs(x)))


def _split3(x):
    hi = x.astype(BF16).astype(F32)
    mid = (x - hi).astype(BF16).astype(F32)
    lo = ((x - hi) - mid).astype(BF16).astype(F32)
    return hi, mid, lo


def _masked_sums(mask, lf_col, lf_row):
    m_bf = mask.astype(BF16)
    n, k = lf_col.shape[1], lf_row.shape[0]
    col_terms = jnp.concatenate(_split3(lf_col), axis=1).astype(BF16)
    row_terms = jnp.concatenate(_split3(lf_row) + (jnp.zeros_like(lf_row),),
                                axis=0).astype(BF16)
    cols = jnp.dot(m_bf, col_terms, preferred_element_type=F32)
    rows = lax.dot_general(row_terms, m_bf, (((1,), (1,)), ((), ())),
                           preferred_element_type=F32)
    return (cols[:, 0:n] + cols[:, n:2 * n] + cols[:, 2 * n:3 * n],
            rows[0:k] + rows[k:2 * k] + rows[2 * k:3 * k])


def _intra(q_bf, k_bf, b_col, b_row, li_row, m0_col, mask):
    dmat = jnp.where(mask, b_col - b_row + li_row, -jnp.inf)
    inter = b_col + m0_col
    m_t = jnp.maximum(inter, jnp.max(dmat, axis=1, keepdims=True))
    w = jnp.exp(dmat - m_t)
    a_inter = jnp.exp(inter - m_t)
    qk = lax.dot_general(q_bf, k_bf, (((1,), (1,)), ((), ())), preferred_element_type=F32)
    return qk * w, a_inter, m_t


def _head_output(num, den, m_t, o_gate, gain):
    denom = jnp.maximum(jnp.abs(den), jnp.exp(-m_t))
    h = num * (1.0 / denom)
    hn = (h * lax.rsqrt(jnp.mean(h * h, axis=1, keepdims=True) + EPS)) * gain
    return jax.nn.sigmoid(o_gate) * hn


def _conv_norm(bg, conv, gain):
    yc = bg * conv
    return (yc * lax.rsqrt(jnp.mean(yc * yc, axis=1, keepdims=True) + EPS)) * gain


def _prompt_conv_part(z, part, convw_ref, ubuf, yc_scr):
    L = z.shape[0]
    ch = slice(part * CONV_PART, (part + 1) * CONV_PART)
    u = z[:, _cols_conv(CG, part)] * z[:, _cols_conv(XT, part)]
    ubuf[SUBLANES:SUBLANES + L, ch] = u
    conv = (ubuf[SUBLANES - 2:SUBLANES - 2 + L, ch] * convw_ref[0:1, ch]
            + ubuf[SUBLANES - 1:SUBLANES - 1 + L, ch] * convw_ref[1:2, ch]
            + u * convw_ref[2:3, ch])
    ubuf[0:SUBLANES, ch] = ubuf[L:L + SUBLANES, ch]
    yc = z[:, _cols_conv(BG, part)] * conv
    yc_scr[:, ch] = yc
    return jnp.sum(yc * yc, axis=1, keepdims=True)


def _prompt_conv_finish(sumsq, yc_scr, convg_ref, y_ref):
    scale = lax.rsqrt(sumsq * (1.0 / D_CONV) + EPS)
    y_ref[:, 0:D_CONV] = ((yc_scr[...] * scale) * convg_ref[...]).astype(y_ref.dtype)


def _prompt_mlstm(z, gc, gr, mlg_ref, bias_row_ref, bias_col_ref, y_ref, c_scr, n_scr, m_scr,
                  fillers):
    L = z.shape[0]
    fillers = list(fillers)

    def fill():
        if fillers:
            fillers.pop(0)()

    pre_c = gc + bias_row_ref[...]
    pre_r = gr + bias_col_ref[:, 0:1]
    row_id = lax.broadcasted_iota(jnp.int32, (L, L), 0)
    col_id = lax.broadcasted_iota(jnp.int32, (L, L), 1)
    causal = col_id <= row_id
    b_c, b_r = _masked_sums(causal, _log_sigmoid(pre_c), _log_sigmoid(pre_r))

    def scores(h):
        li_r = pre_r[h:h + 1, :]
        bc = b_c[:, N_HEADS + h:N_HEADS + h + 1]
        br = b_r[N_HEADS + h:N_HEADS + h + 1, :]
        m0 = m_scr[h, 0:1, 0:1]
        qf = z[:, _cols_q(h)] * (D_K ** -0.5)
        q_bf = qf.astype(BF16)
        kf = z[:, _cols_k(h)]
        v_bf = z[:, _cols_v(h)].astype(BF16)
        s, a_inter, m_t = _intra(q_bf, kf.astype(BF16), bc, br, li_r, m0, causal)
        return li_r, bc, br, m0, qf, q_bf, kf, v_bf, s, a_inter, m_t

    def finish(h, staged):
        li_r, bc, br, m0, qf, q_bf, kf, v_bf, s, a_inter, m_t = staged
        li_c = pre_c[:, h:h + 1]
        c_old = c_scr[h]
        n_old = n_scr[h]
        num = (a_inter * jnp.dot(q_bf, c_old.astype(BF16), preferred_element_type=F32)
               + jnp.dot(s.astype(BF16), v_bf, preferred_element_type=F32))
        den = (a_inter * jnp.sum(qf * n_old, axis=1, keepdims=True)
               + jnp.sum(s, axis=1, keepdims=True))
        fill()
        y_ref[:, D_CONV + h * D_V:D_CONV + (h + 1) * D_V] = _head_output(
            num, den, m_t, z[:, _cols_o(h)],
            mlg_ref[:, h * D_V:(h + 1) * D_V]).astype(y_ref.dtype)

        b_last = br[:, L - 1:L]
        m_new = jnp.maximum(b_last + m0,
                            jnp.max(b_last - br + li_r, axis=1, keepdims=True))
        decay = jnp.exp(b_last + m0 - m_new)
        kw = kf * jnp.exp(b_last - bc + li_c - m_new)
        c_scr[h] = decay * c_old + lax.dot_general(
            kw.astype(BF16), v_bf, (((0,), (0,)), ((), ())), preferred_element_type=F32)
        n_scr[h] = decay * n_old + jnp.sum(kw, axis=0, keepdims=True)
        m_scr[h] = jnp.broadcast_to(m_new, (SUBLANES, LANES))
        fill()

    staged = scores(0)
    fill()
    for h in range(N_HEADS):
        upcoming = None
        if h + 1 < N_HEADS:
            upcoming = scores(h + 1)
            fill()
        finish(h, staged)
        staged = upcoming
    while fillers:
        fill()


def _fused_prompt_kernel(x_ref, g1_ref, w_ref, wgate_ref, convw_ref, convg_ref, mlg_ref,
                         bias_row_ref, bias_col_ref, wg_ref, wu_ref, wout_ref,
                         y_ref, convs_ref, c_out_ref, n_out_ref, m_out_ref,
                         wgu_bf_ref, wout_bf_ref,
                         xn_scr, z_scr, ubuf, yc_scr, c_scr, n_scr, m_scr):
    chunk = pl.program_id(1)
    L = x_ref.shape[0]

    @pl.when(chunk == 0)
    def _():
        ubuf[0:SUBLANES, :] = jnp.zeros((SUBLANES, D_CONV), F32)
        c_scr[...] = jnp.zeros(c_scr.shape, F32)
        n_scr[...] = jnp.zeros(n_scr.shape, F32)
        m_scr[...] = jnp.zeros(m_scr.shape, F32)

    gain = g1_ref[...]
    for r in range(0, L, NORM_ROWS):
        xn_scr[r:r + NORM_ROWS, :] = _rmsnorm_piece(x_ref[r:r + NORM_ROWS, :], gain, BF16)
    xn = xn_scr[...]
    gc = _gate_dot(xn, wgate_ref)
    gr = gc.T[0:SUBLANES, :]

    def project(j):
        cols = slice(j * W_SLAB, (j + 1) * W_SLAB)
        z_scr[:, cols] = jnp.dot(xn_scr[...], w_ref[:, cols], preferred_element_type=F32)

    piece = lambda j: functools.partial(project, j)
    skip = lambda: None
    sumsq = []

    def then_conv(j, part):
        def emit():
            project(j)
            sumsq.append(_prompt_conv_part(z_scr, part, convw_ref, ubuf, yc_scr))
        return emit

    project(0)
    wgu_bf_ref[0, :, 0:CAST_PIECE] = wg_ref[...].astype(BF16)
    wout_bf_ref[...] = wout_ref[...].astype(BF16)
    project(1)
    wgu_bf_ref[0, :, CAST_PIECE:2 * CAST_PIECE] = wu_ref[...].astype(BF16)
    fillers = [piece(2), piece(3), piece(4), piece(6),
               piece(5), then_conv(7, 0), then_conv(8, 1),
               piece(9), then_conv(10, 2), then_conv(11, 3),
               skip, skip]
    _prompt_mlstm(z_scr, gc, gr, mlg_ref, bias_row_ref, bias_col_ref, y_ref,
                  c_scr, n_scr, m_scr, fillers)
    _prompt_conv_finish(sum(sumsq[1:], sumsq[0]), yc_scr, convg_ref, y_ref)

    @pl.when(chunk == pl.num_programs(1) - 1)
    def _():
        convs_ref[0] = ubuf[SUBLANES - 2:SUBLANES, :]
        c_out_ref[0] = c_scr[...]
        for h in range(N_HEADS):
            n_out_ref[0, h:h + 1, :] = n_scr[h]
            m_out_ref[0, h:h + 1, :] = m_scr[h, 0:1, :]


def _fused_prompt(x2d, g1, w_bf, w_t, batch, seq, conv_w, conv_g, ml_g, bias_row, bias_col,
                  wg, wu, w_out):
    L = PROMPT_CHUNK
    nc = seq // L
    assert nc * L == seq, seq
    const = lambda b, c: (0, 0)
    wout_rows = D_MODEL // (batch * nc)
    assert wout_rows * batch * nc == D_MODEL and wout_rows % (2 * SUBLANES) == 0
    wout_piece = pl.BlockSpec((wout_rows, D_MODEL), lambda b, c: (b * nc + c, 0))
    n_pieces = D_FF // CAST_PIECE
    assert batch * nc >= n_pieces, "not enough grid steps to cast the FFN weights"
    per_tile = FFN_TILE // CAST_PIECE
    piece_of = lambda b, c: jnp.minimum(b * nc + c, n_pieces - 1)
    tiled = lambda b, c: (piece_of(b, c) // per_tile, 0, piece_of(b, c) % per_tile)
    return pl.pallas_call(
        _fused_prompt_kernel,
        grid=(batch, nc),
        in_specs=[
            pl.BlockSpec((L, D_MODEL), lambda b, c: (b * nc + c, 0)),
            pl.BlockSpec((1, D_MODEL), const),
            pl.BlockSpec((D_MODEL, N_MAIN), const, pipeline_mode=pl.Buffered(1)),
            GATE_BLOCK,
            pl.BlockSpec((CONV_W, D_CONV), const),
            pl.BlockSpec((1, D_CONV), const),
            pl.BlockSpec((1, N_HEADS * D_V), const),
            pl.BlockSpec((1, LANES), const),
            pl.BlockSpec((SUBLANES, LANES), const),
            pl.BlockSpec((D_MODEL, CAST_PIECE), lambda b, c: (0, piece_of(b, c))),
            pl.BlockSpec((D_MODEL, CAST_PIECE), lambda b, c: (0, piece_of(b, c))),
            wout_piece,
        ],
        out_specs=[
            pl.BlockSpec((L, D_MODEL), lambda b, c: (b * nc + c, 0)),
            pl.BlockSpec((1, CONV_W - 1, D_CONV), lambda b, c: (b, 0, 0)),
            pl.BlockSpec((1, N_HEADS, D_K, D_V), lambda b, c: (b, 0, 0, 0)),
            pl.BlockSpec((1, N_HEADS, D_K), lambda b, c: (b, 0, 0)),
            pl.BlockSpec((1, N_HEADS, LANES), lambda b, c: (b, 0, 0)),
            pl.BlockSpec((1, D_MODEL, 2 * CAST_PIECE), tiled),
            wout_piece,
        ],
        out_shape=[
            jax.ShapeDtypeStruct((batch * seq, D_MODEL), BF16),
            jax.ShapeDtypeStruct((batch, CONV_W - 1, D_CONV), F32),
            jax.ShapeDtypeStruct((batch, N_HEADS, D_K, D_V), F32),
            jax.ShapeDtypeStruct((batch, N_HEADS, D_K), F32),
            jax.ShapeDtypeStruct((batch, N_HEADS, LANES), F32),
            jax.ShapeDtypeStruct((D_FF // FFN_TILE, D_MODEL, 2 * FFN_TILE), BF16),
            jax.ShapeDtypeStruct((D_MODEL, D_MODEL), BF16),
        ],
        scratch_shapes=[
            pltpu.VMEM((L, D_MODEL), BF16),
            pltpu.VMEM((L, N_MAIN), F32),
            pltpu.VMEM((L + 2 * SUBLANES, D_CONV), F32),
            pltpu.VMEM((L, D_CONV), F32),
            pltpu.VMEM((N_HEADS, D_K, D_V), F32),
            pltpu.VMEM((N_HEADS, 1, D_K), F32),
            pltpu.VMEM((N_HEADS, SUBLANES, LANES), F32),
        ],
        compiler_params=_compiler_params("fused_prompt", ("arbitrary", "arbitrary")),
        name="fused_prompt",
    )(x2d, g1, w_bf, w_t, conv_w, conv_g, ml_g, bias_row, bias_col, wg, wu, w_out)


def _mix_sample_kernel(z_ref, gcol_ref, grow_ref, mcol_ref, mrow_ref,
                       convs_in_ref, c_in_ref, n_in_ref,
                       convw_ref, convg_ref, mlg_ref, bias_row_ref, bias_col_ref,
                       y_ref, convs_ref, c_out_ref, n_out_ref, m_out_ref,
                       pad_scr, conv_scr, *, seq):
    L = z_ref.shape[0]
    group = L // seq
    stride = seq + SUBLANES

    for g in range(group):
        rows = slice(g * seq, (g + 1) * seq)
        u = _conv_operand(z_ref, CG, rows) * _conv_operand(z_ref, XT, rows)
        base = g * stride
        pad_scr[base + SUBLANES - 2:base + SUBLANES, :] = convs_in_ref[g]
        pad_scr[base + SUBLANES:base + SUBLANES + seq, :] = u
        conv_scr[g * seq:(g + 1) * seq, :] = (
            pad_scr[base + SUBLANES - 2:base + SUBLANES - 2 + seq, :] * convw_ref[0:1, :]
            + pad_scr[base + SUBLANES - 1:base + SUBLANES - 1 + seq, :] * convw_ref[1:2, :]
            + u * convw_ref[2:3, :])
        convs_ref[g] = pad_scr[base + seq + SUBLANES - 2:base + seq + SUBLANES, :]
    y_ref[:, 0:D_CONV] = _conv_norm(_conv_operand(z_ref, BG), conv_scr[...],
                                    convg_ref[...]).astype(y_ref.dtype)

    pre_c = gcol_ref[...] + bias_row_ref[...]
    pre_r = grow_ref[0:SUBLANES, :] + bias_col_ref[:, 0:1]
    row_id = lax.broadcasted_iota(jnp.int32, (L, L), 0)
    col_id = lax.broadcasted_iota(jnp.int32, (L, L), 1)
    same = (row_id // seq) == (col_id // seq)
    causal = same & (col_id <= row_id)
    lf_c = _log_sigmoid(pre_c)
    lf_r = _log_sigmoid(pre_r)
    b_c, b_r = _masked_sums(causal, lf_c, lf_r)
    tot_c, tot_r = _masked_sums(same, lf_c, lf_r)
    lane_seq = lax.broadcasted_iota(jnp.int32, (1, L), 1) // seq

    for h in range(N_HEADS):
        li_r = pre_r[h:h + 1, :]
        li_c = pre_c[:, h:h + 1]
        bc = b_c[:, N_HEADS + h:N_HEADS + h + 1]
        br = b_r[N_HEADS + h:N_HEADS + h + 1, :]
        bl_c = tot_c[:, N_HEADS + h:N_HEADS + h + 1]
        bl_r = tot_r[N_HEADS + h:N_HEADS + h + 1, :]
        m0_c = mcol_ref[:, h:h + 1]
        m0_r = mrow_ref[h:h + 1, :]
        qf = z_ref[:, _cols_q(h)] * (D_K ** -0.5)
        q_bf = qf.astype(BF16)
        kf = z_ref[:, _cols_k(h)]
        v_bf = z_ref[:, _cols_v(h)].astype(BF16)
        s, a_inter, m_t = _intra(q_bf, kf.astype(BF16), bc, br, li_r, m0_c, causal)

        qc_rows, qn_rows = [], []
        for g in range(group):
            rows = slice(g * seq, (g + 1) * seq)
            qc_rows.append(jnp.dot(qf[rows], c_in_ref[g, h], preferred_element_type=F32))
            qn_rows.append(jnp.sum(qf[rows] * n_in_ref[g, h:h + 1, :], axis=1, keepdims=True))
        num = (a_inter * jnp.concatenate(qc_rows, axis=0)
               + jnp.dot(s.astype(BF16), v_bf, preferred_element_type=F32))
        den = (a_inter * jnp.concatenate(qn_rows, axis=0)
               + jnp.sum(s, axis=1, keepdims=True))
        y_ref[:, D_CONV + h * D_V:D_CONV + (h + 1) * D_V] = _head_output(
            num, den, m_t, z_ref[:, _cols_o(h)],
            mlg_ref[:, h * D_V:(h + 1) * D_V]).astype(y_ref.dtype)

        g_r = bl_r - br + li_r
        gmax_c = jnp.max(jnp.where(same, g_r, -jnp.inf), axis=1, keepdims=True)
        gmax_r = jnp.max(jnp.where(same, gmax_c, -jnp.inf), axis=0, keepdims=True)
        m_new_c = jnp.maximum(bl_c + m0_c, gmax_c)
        m_new_r = jnp.maximum(bl_r + m0_r, gmax_r)
        decay_r = jnp.exp(bl_r + m0_r - m_new_r)
        kw = kf * jnp.exp(bl_c - bc + li_c - m_new_c)
        kw_t = kw.T
        for g in range(group):
            decay = decay_r[:, g * seq:g * seq + 1]
            kw_g = jnp.where(lane_seq == g, kw_t, 0.0).astype(BF16)
            c_out_ref[g, h] = decay * c_in_ref[g, h] + jnp.dot(
                kw_g, v_bf, preferred_element_type=F32)
            n_out_ref[g, h:h + 1, :] = (decay * n_in_ref[g, h:h + 1, :]
                                        + jnp.sum(kw[g * seq:(g + 1) * seq], axis=0, keepdims=True))
        m_out_ref[h:h + 1, :] = m_new_r
    m_out_ref[N_HEADS:, :] = jnp.zeros((SUBLANES - N_HEADS, L), F32)


def _mix_sample(z, gcol, grow, mcol, mrow, convs, c_state, n_state, batch, seq,
                conv_w, conv_g, ml_g, bias_row, bias_col):
    group = SAMPLE_GROUP
    L = group * seq
    assert seq == SUBLANES and L == LANES and batch % group == 0, (batch, seq)
    const = lambda i: (0, 0)
    return pl.pallas_call(
        functools.partial(_mix_sample_kernel, seq=seq),
        grid=(batch // group,),
        in_specs=[
            pl.BlockSpec((L, N_MAIN), lambda i: (i, 0)),
            pl.BlockSpec((L, LANES), lambda i: (i, 0)),
            pl.BlockSpec((GATE_ROWS, L), lambda i: (0, i)),
            pl.BlockSpec((L, LANES), lambda i: (i, 0)),
            pl.BlockSpec((SUBLANES, L), lambda i: (0, i)),
            pl.BlockSpec((group, CONV_W - 1, D_CONV), lambda i: (i, 0, 0)),
            pl.BlockSpec((group, N_HEADS, D_K, D_V), lambda i: (i, 0, 0, 0)),
            pl.BlockSpec((group, N_HEADS, D_K), lambda i: (i, 0, 0)),
            pl.BlockSpec((CONV_W, D_CONV), const),
            pl.BlockSpec((1, D_CONV), const),
            pl.BlockSpec((1, N_HEADS * D_V), const),
            pl.BlockSpec((1, LANES), const),
            pl.BlockSpec((SUBLANES, LANES), const),
        ],
        out_specs=[
            pl.BlockSpec((L, D_MODEL), lambda i: (i, 0)),
            pl.BlockSpec((group, CONV_W - 1, D_CONV), lambda i: (i, 0, 0)),
            pl.BlockSpec((group, N_HEADS, D_K, D_V), lambda i: (i, 0, 0, 0)),
            pl.BlockSpec((group, N_HEADS, D_K), lambda i: (i, 0, 0)),
            pl.BlockSpec((SUBLANES, L), lambda i: (0, i)),
        ],
        out_shape=[
            jax.ShapeDtypeStruct((batch * seq, D_MODEL), BF16),
            jax.ShapeDtypeStruct((batch, CONV_W - 1, D_CONV), F32),
            jax.ShapeDtypeStruct((batch, N_HEADS, D_K, D_V), F32),
            jax.ShapeDtypeStruct((batch, N_HEADS, D_K), F32),
            jax.ShapeDtypeStruct((SUBLANES, batch * seq), F32),
        ],
        scratch_shapes=[
            pltpu.VMEM((group * (seq + SUBLANES), D_CONV), F32),
            pltpu.VMEM((L, D_CONV), F32),
        ],
        compiler_params=_compiler_params("mix_sample", ("parallel",)),
        name="mix_sample",
    )(z, gcol, grow, mcol, mrow, convs, c_state, n_state,
      conv_w, conv_g, ml_g, bias_row, bias_col)


def _outproj_kernel(ya_ref, yb_ref, w_ref, xa_ref, xb_ref, o_ref, *, tiles_a):
    i = pl.program_id(0)

    @pl.when(i < tiles_a)
    def _():
        o_ref[...] = xa_ref[...] + jnp.dot(ya_ref[...], w_ref[...], preferred_element_type=F32)

    @pl.when(i >= tiles_a)
    def _():
        o_ref[...] = xb_ref[...] + jnp.dot(yb_ref[...], w_ref[...], preferred_element_type=F32)


def _outproj(ya, yb, w_out_bf, xa, xb):
    tm = 512
    assert xa.shape[0] % tm == 0 and xb.shape[0] % tm == 0, (xa.shape, xb.shape)
    tiles_a, tiles_b = xa.shape[0] // tm, xb.shape[0] // tm
    rows_a = pl.BlockSpec((tm, D_MODEL), lambda i: (jnp.minimum(i, tiles_a - 1), 0))
    rows_b = pl.BlockSpec((tm, D_MODEL), lambda i: (jnp.maximum(i - tiles_a, 0), 0))
    return pl.pallas_call(
        functools.partial(_outproj_kernel, tiles_a=tiles_a),
        grid=(tiles_a + tiles_b,),
        in_specs=[
            rows_a, rows_b,
            pl.BlockSpec((D_MODEL, D_MODEL), lambda i: (0, 0), pipeline_mode=pl.Buffered(1)),
            rows_a, rows_b,
        ],
        out_specs=pl.BlockSpec((tm, D_MODEL), lambda i: (i, 0)),
        out_shape=jax.ShapeDtypeStruct(((tiles_a + tiles_b) * tm, D_MODEL), F32),
        compiler_params=_compiler_params("outproj", ("parallel",)),
        name="outproj",
    )(ya, yb, w_out_bf, xa, xb)


def _ffn_kernel(x_ref, g2_ref, wgu_ref, wd_ref, gf_ref, o_ref, hn_ref, *, final_norm):
    j = pl.program_id(1)

    @pl.when(j == 0)
    def _():
        _rmsnorm_rows(x_ref, g2_ref[...], hn_ref, copy_ref=o_ref)

    hn = hn_ref[...]
    half = wd_ref.shape[0] // 2
    halves = (slice(0, half), slice(half, 2 * half))
    pre = []
    for a in range(2):
        gate_up = jnp.dot(hn, wgu_ref[0, :, 2 * a * half:2 * (a + 1) * half],
                          preferred_element_type=F32)
        pre.append((gate_up[:, 0:half], gate_up[:, half:2 * half]))
    down = None
    for (gate, up), rows in zip(pre, halves):
        act = ((gate * jax.nn.sigmoid(gate)) * up).astype(BF16)
        part = jnp.dot(act, wd_ref[rows, :].astype(BF16), preferred_element_type=F32)
        down = part if down is None else down + part
    o_ref[...] += down

    if final_norm:
        @pl.when(j == pl.num_programs(1) - 1)
        def _():
            _rmsnorm_rows_inplace(o_ref, gf_ref[...])


def _ffn(x1, first_row, tokens, g2, wgu_bf, wd, gf, final_norm):
    tm = min(1024, tokens)
    tf = FFN_TILE
    assert tokens % tm == 0 and first_row % tm == 0, (first_row, tokens)
    assert tm % (NORM_ROWS * NORM_GROUP) == 0, tm
    first_tile = first_row // tm
    return pl.pallas_call(
        functools.partial(_ffn_kernel, final_norm=final_norm),
        grid=(tokens // tm, D_FF // tf),
        in_specs=[
            pl.BlockSpec((tm, D_MODEL), lambda i, j: (i + first_tile, 0)),
            pl.BlockSpec((1, D_MODEL), lambda i, j: (0, 0)),
            pl.BlockSpec((1, D_MODEL, 2 * tf), lambda i, j: (j, 0, 0)),
            pl.BlockSpec((tf, D_MODEL), lambda i, j: (j, 0)),
            pl.BlockSpec((1, D_MODEL), lambda i, j: (0, 0)),
        ],
        out_specs=pl.BlockSpec((tm, D_MODEL), lambda i, j: (i, 0)),
        out_shape=jax.ShapeDtypeStruct((tokens, D_MODEL), F32),
        scratch_shapes=[pltpu.VMEM((tm, D_MODEL), BF16)],
        compiler_params=_compiler_params("ffn", ("parallel", "arbitrary")),
        name="ffn",
    )(x1, g2, wgu_bf, wd, gf)


def _gate_bias(b_i, b_f):
    bias = jnp.concatenate([b_i, b_f]).astype(F32)
    bias_row = jnp.pad(bias, (0, LANES - 2 * N_HEADS))[None, :]
    bias_col = jnp.broadcast_to(bias[:, None], (SUBLANES, LANES))
    return bias_row, bias_col


def _stack(states, k):
    if len(states) == 1:
        return states[0][k][None]
    return jnp.stack([st[k] for st in states])


def kernel(x_prompt, x_sample, state_conv, state_mlstm_C, state_mlstm_n, state_mlstm_m,
           norm1_g, w_in, b_igate, b_fgate, conv_w, conv_out_g, mlstm_out_g, w_out,
           norm2_g, w_ffn_gate, w_ffn_up, w_ffn_down, final_norm_g):
    depth = w_in.shape[0]
    bp, sp, _ = x_prompt.shape
    bs, ss, _ = x_sample.shape
    hp = x_prompt.reshape(bp * sp, D_MODEL)
    hs = x_sample.reshape(bs * ss, D_MODEL)
    p_states, s_states = [], []
    for l in range(depth):
        final_norm = l == depth - 1
        w_t = w_in[l].T
        bias_row, bias_col = _gate_bias(b_igate[l], b_fgate[l])
        w_bf = _cast_main_columns(w_t)
        g1 = norm1_g[l][None, :]
        conv_g = conv_out_g[l][None, :]
        ml_g = mlstm_out_g[l][None, :]

        y_p, cb, c1, n1, m1, wgu_bf, w_out_bf = _fused_prompt(
            hp, g1, w_bf, w_t, bp, sp, conv_w[l], conv_g, ml_g, bias_row, bias_col,
            w_ffn_gate[l], w_ffn_up[l], w_out[l])
        p_states.append((cb, c1, n1, m1[:, :, 0]))

        z, gcol, grow = _inproj(hs, g1, w_bf, w_t)
        m_tok = jnp.repeat(state_mlstm_m[l], ss, axis=0)
        mcol = jnp.pad(m_tok, ((0, 0), (0, LANES - N_HEADS)))
        mrow = jnp.pad(m_tok.T, ((0, SUBLANES - N_HEADS), (0, 0)))
        y_s, cb, c1, n1, m_row = _mix_sample(z, gcol, grow, mcol, mrow, state_conv[l],
                                             state_mlstm_C[l], state_mlstm_n[l], bs, ss,
                                             conv_w[l], conv_g, ml_g, bias_row, bias_col)
        s_states.append((cb, c1, n1, m_row[:N_HEADS, ::ss].T))

        x1 = _outproj(y_p, y_s, w_out_bf, hp, hs)
        ffn = (norm2_g[l][None, :], wgu_bf, w_ffn_down[l], final_norm_g[None, :], final_norm)
        hp = _ffn(x1, 0, bp * sp, *ffn)
        hs = _ffn(x1, bp * sp, bs * ss, *ffn)

    return (hp.reshape(bp, sp, D_MODEL), hs.reshape(bs, ss, D_MODEL),
            _stack(p_states, 0), _stack(p_states, 1), _stack(p_states, 2), _stack(p_states, 3),
            _stack(s_states, 0), _stack(s_states, 1), _stack(s_states, 2), _stack(s_states, 3))
```

```python
import functools

import jax
import jax.numpy as jnp
from jax import lax
from jax.experimental import pallas as pl
from jax.experimental.pallas import tpu as pltpu

F32 = jnp.float32
BF16 = jnp.bfloat16

D_MODEL = 2048
D_CONV = 1024
CONV_W = 3
N_HEADS = 4
D_K = 128
D_V = 256
D_FF = 5632
EPS = 1e-6

LANES = 128
SUBLANES = 8
GATE_ROWS = 16
MIB = 1024 * 1024

PROMPT_CHUNK = 256
SAMPLE_GROUP = 16
NORM_ROWS = 64
NORM_GROUP = 4
W_SLAB = 512
CAST_SLAB = 1024
FFN_TILE = 512
CAST_PIECE = 256

VMEM_LIMIT_MIB = 62

SRC_Q = 3 * D_CONV
SRC_K = SRC_Q + N_HEADS * D_K
SRC_V = SRC_K + N_HEADS * D_K
SRC_O = SRC_V + N_HEADS * D_V
N_MAIN = SRC_O + N_HEADS * D_V

HEAD_COLS = 2 * D_K + 2 * D_V
N_MLSTM = N_HEADS * HEAD_COLS
CONV_PARTS = 4
CONV_PART = D_CONV // CONV_PARTS
BG, CG, XT = 0, 1, 2


def _cols_conv(kind, part):
    start = N_MLSTM + (3 * part + kind) * CONV_PART
    return slice(start, start + CONV_PART)


def _conv_operand(z, kind, rows=slice(None)):
    return jnp.concatenate([z[rows, _cols_conv(kind, p)] for p in range(CONV_PARTS)], axis=1)


def _cols_q(h):
    return slice(h * HEAD_COLS, h * HEAD_COLS + D_K)


def _cols_k(h):
    return slice(h * HEAD_COLS + D_K, h * HEAD_COLS + 2 * D_K)


def _cols_v(h):
    return slice(h * HEAD_COLS + 2 * D_K, h * HEAD_COLS + 2 * D_K + D_V)


def _cols_o(h):
    return slice(h * HEAD_COLS + 2 * D_K + D_V, (h + 1) * HEAD_COLS)


def _source_lane_block(j):
    per_head = HEAD_COLS // LANES
    h, r = j // per_head, j % per_head
    v_blocks = D_V // LANES
    mlstm_src = jnp.where(
        r == 0, SRC_Q // LANES + h,
        jnp.where(r == 1, SRC_K // LANES + h,
                  jnp.where(r < 2 + v_blocks,
                            SRC_V // LANES + v_blocks * h + (r - 2),
                            SRC_O // LANES + v_blocks * h + (r - 2 - v_blocks))))
    per_kind = CONV_PART // LANES
    c = j - N_MLSTM // LANES
    part, r = c // (3 * per_kind), c % (3 * per_kind)
    conv_src = (r // per_kind) * (D_CONV // LANES) + part * per_kind + r % per_kind
    return jnp.where(j < N_MLSTM // LANES, mlstm_src, conv_src)


def _compiler_params(semantics):
    return pltpu.CompilerParams(dimension_semantics=semantics,
                                vmem_limit_bytes=VMEM_LIMIT_MIB * MIB)


def _rmsnorm_piece(x, gain, dtype):
    ms = jnp.mean(x * x, axis=-1, keepdims=True)
    return ((x * lax.rsqrt(ms + EPS)) * gain).astype(dtype)


def _rmsnorm_rows(src_ref, gain, dst_ref, copy_ref=None):
    rows = src_ref.shape[0]
    chunk = min(NORM_ROWS, rows)

    def body(i, carry):
        r = pl.multiple_of(i * chunk, chunk)
        x = src_ref[pl.ds(r, chunk), :].astype(F32)
        if copy_ref is not None:
            copy_ref[pl.ds(r, chunk), :] = x
        dst_ref[pl.ds(r, chunk), :] = _rmsnorm_piece(x, gain, dst_ref.dtype)
        return carry

    lax.fori_loop(0, rows // chunk, body, 0, unroll=2)


def _rmsnorm_rows_inplace(ref, gain):
    rows = ref.shape[0]
    chunk = min(NORM_ROWS, rows)
    group = min(NORM_GROUP, rows // chunk)

    def body(i, carry):
        starts = [pl.multiple_of((i * group + k) * chunk, chunk) for k in range(group)]
        scales = []
        for r in starts:
            x = ref[pl.ds(r, chunk), :]
            scales.append(lax.rsqrt(jnp.mean(x * x, axis=-1, keepdims=True) + EPS))
        for r, scale in zip(starts, scales):
            ref[pl.ds(r, chunk), :] = (ref[pl.ds(r, chunk), :] * scale) * gain
        return carry

    lax.fori_loop(0, rows // (chunk * group), body, 0)


def _cast_kernel(*refs):
    *w_refs, o_ref = refs
    for r, w_ref in enumerate(w_refs):
        o_ref[:, r * LANES:(r + 1) * LANES] = w_ref[...].T.astype(o_ref.dtype)


def _cast_main_columns(w_t):
    per_slab = CAST_SLAB // LANES
    in_specs = [
        pl.BlockSpec((LANES, D_MODEL),
                     lambda j, r=r: (_source_lane_block(per_slab * j + r), 0))
        for r in range(per_slab)
    ]
    return pl.pallas_call(
        _cast_kernel,
        grid=(N_MAIN // CAST_SLAB,),
        in_specs=in_specs,
        out_specs=pl.BlockSpec((D_MODEL, CAST_SLAB), lambda j: (0, j)),
        out_shape=jax.ShapeDtypeStruct((D_MODEL, N_MAIN), BF16),
        compiler_params=_compiler_params(("parallel",)),
        name="cast_w_in",
    )(*([w_t] * per_slab))


GATE_BLOCK = pl.BlockSpec((2 * N_HEADS, D_MODEL), lambda *_: (N_MAIN // (2 * N_HEADS), 0))
CONV_W_BLOCK = pl.BlockSpec((CONV_W, 1, D_CONV), lambda *_: (0, 0, 0))


def _gate_bias(bi_ref, bf_ref):
    lane = lax.broadcasted_iota(jnp.int32, (1, LANES), 1)
    sub = lax.broadcasted_iota(jnp.int32, (2 * N_HEADS, 1), 0)
    row = jnp.zeros((1, LANES), F32)
    col = jnp.zeros((2 * N_HEADS, 1), F32)
    for k in range(N_HEADS):
        for pos, value in ((k, bi_ref[k]), (N_HEADS + k, bf_ref[k])):
            row = jnp.where(lane == pos, value, row)
            col = jnp.where(sub == pos, value, col)
    return row, col


def _gate_dot(xn, wgate_t_ref):
    rows = wgate_t_ref[...]
    padded = jnp.concatenate([rows, jnp.zeros((LANES - rows.shape[0], D_MODEL), F32)], axis=0)
    return lax.dot_general(xn, padded.astype(BF16), (((1,), (1,)), ((), ())),
                           preferred_element_type=F32)


def _inproj_kernel(x_ref, g_ref, w_ref, wgate_ref, z_ref, gcol_ref, grow_ref, xn_ref):
    j = pl.program_id(1)

    @pl.when(j == 0)
    def _():
        gain = g_ref[...]
        piece = NORM_ROWS * NORM_GROUP
        for r in range(0, x_ref.shape[0], piece):
            for c in range(r, r + piece, NORM_ROWS):
                xn_ref[c:c + NORM_ROWS, :] = _rmsnorm_piece(x_ref[c:c + NORM_ROWS, :], gain, BF16)
            z_ref[r:r + piece, :] = jnp.dot(xn_ref[r:r + piece, :], w_ref[...],
                                            preferred_element_type=F32)
        gc = _gate_dot(xn_ref[...], wgate_ref)
        gcol_ref[...] = gc
        grow_ref[...] = gc.T[0:GATE_ROWS, :]

    @pl.when(j > 0)
    def _():
        z_ref[...] = jnp.dot(xn_ref[...], w_ref[...], preferred_element_type=F32)


def _inproj(x2d, g1, w_bf, w_t):
    tokens = x2d.shape[0]
    tm = min(1024, tokens)
    tn = W_SLAB
    assert tokens % tm == 0 and tm % (NORM_ROWS * NORM_GROUP) == 0, tokens
    return pl.pallas_call(
        _inproj_kernel,
        grid=(tokens // tm, N_MAIN // tn),
        in_specs=[
            pl.BlockSpec((tm, D_MODEL), lambda i, j: (i, 0)),
            pl.BlockSpec((1, D_MODEL), lambda i, j: (0, 0)),
            pl.BlockSpec((D_MODEL, tn), lambda i, j: (0, j)),
            GATE_BLOCK,
        ],
        out_specs=[
            pl.BlockSpec((tm, tn), lambda i, j: (i, j)),
            pl.BlockSpec((tm, LANES), lambda i, j: (i, 0)),
            pl.BlockSpec((GATE_ROWS, tm), lambda i, j: (0, i)),
        ],
        out_shape=[
            jax.ShapeDtypeStruct((tokens, N_MAIN), F32),
            jax.ShapeDtypeStruct((tokens, LANES), F32),
            jax.ShapeDtypeStruct((GATE_ROWS, tokens), F32),
        ],
        scratch_shapes=[pltpu.VMEM((tm, D_MODEL), BF16)],
        compiler_params=_compiler_params(("parallel", "arbitrary")),
        name="inproj",
    )(x2d, g1, w_bf, w_t)


def _log_sigmoid(x):
    return jnp.minimum(x, 0.0) - jnp.log1p(jnp.exp(-jnp.abs(x)))


def _split3(x):
    hi = x.astype(BF16).astype(F32)
    mid = (x - hi).astype(BF16).astype(F32)
    lo = ((x - hi) - mid).astype(BF16).astype(F32)
    return hi, mid, lo


def _masked_sums(mask, lf_col, lf_row):
    m_bf = mask.astype(BF16)
    n, k = lf_col.shape[1], lf_row.shape[0]
    col_terms = jnp.concatenate(_split3(lf_col), axis=1).astype(BF16)
    row_terms = jnp.concatenate(_split3(lf_row) + (jnp.zeros_like(lf_row),),
                                axis=0).astype(BF16)
    cols = jnp.dot(m_bf, col_terms, preferred_element_type=F32)
    rows = lax.dot_general(row_terms, m_bf, (((1,), (1,)), ((), ())),
                           preferred_element_type=F32)
    return (cols[:, 0:n] + cols[:, n:2 * n] + cols[:, 2 * n:3 * n],
            rows[0:k] + rows[k:2 * k] + rows[2 * k:3 * k])


def _intra(q_bf, k_bf, b_col, b_row, li_row, m0_col, mask):
    dmat = jnp.where(mask, b_col - b_row + li_row, -jnp.inf)
    inter = b_col + m0_col
    m_t = jnp.maximum(inter, jnp.max(dmat, axis=1, keepdims=True))
    w = jnp.exp(dmat - m_t)
    a_inter = jnp.exp(inter - m_t)
    qk = lax.dot_general(q_bf, k_bf, (((1,), (1,)), ((), ())), preferred_element_type=F32)
    return qk * w, a_inter, m_t


def _head_output(num, den, m_t, o_gate, gain):
    denom = jnp.maximum(jnp.abs(den), jnp.exp(-m_t))
    h = num * (1.0 / denom)
    hn = (h * lax.rsqrt(jnp.mean(h * h, axis=1, keepdims=True) + EPS)) * gain
    return jax.nn.sigmoid(o_gate) * hn


def _conv_norm(bg, conv, gain):
    yc = bg * conv
    return (yc * lax.rsqrt(jnp.mean(yc * yc, axis=1, keepdims=True) + EPS)) * gain


def _prompt_conv_part(z, part, convw_ref, ubuf, yc_scr):
    L = z.shape[0]
    ch = slice(part * CONV_PART, (part + 1) * CONV_PART)
    u = z[:, _cols_conv(CG, part)] * z[:, _cols_conv(XT, part)]
    ubuf[SUBLANES:SUBLANES + L, ch] = u
    conv = (ubuf[SUBLANES - 2:SUBLANES - 2 + L, ch] * convw_ref[0, :, ch]
            + ubuf[SUBLANES - 1:SUBLANES - 1 + L, ch] * convw_ref[1, :, ch]
            + u * convw_ref[2, :, ch])
    ubuf[0:SUBLANES, ch] = ubuf[L:L + SUBLANES, ch]
    yc = z[:, _cols_conv(BG, part)] * conv
    yc_scr[:, ch] = yc
    return jnp.sum(yc * yc, axis=1, keepdims=True)


def _prompt_conv_finish(sumsq, yc_scr, convg_ref, y_ref):
    scale = lax.rsqrt(sumsq * (1.0 / D_CONV) + EPS)
    y_ref[:, 0:D_CONV] = ((yc_scr[...] * scale) * convg_ref[...]).astype(y_ref.dtype)


def _prompt_mlstm(z, gc, gr, mlg_ref, bi_ref, bf_ref, y_ref, c_scr, n_scr, m_scr,
                  fillers):
    L = z.shape[0]
    fillers = list(fillers)

    def fill():
        if fillers:
            fillers.pop(0)()

    bias_row, bias_col = _gate_bias(bi_ref, bf_ref)
    pre_c = gc + bias_row
    pre_r = gr + bias_col
    row_id = lax.broadcasted_iota(jnp.int32, (L, L), 0)
    col_id = lax.broadcasted_iota(jnp.int32, (L, L), 1)
    causal = col_id <= row_id
    b_c, b_r = _masked_sums(causal, _log_sigmoid(pre_c), _log_sigmoid(pre_r))

    def scores(h):
        li_r = pre_r[h:h + 1, :]
        bc = b_c[:, N_HEADS + h:N_HEADS + h + 1]
        br = b_r[N_HEADS + h:N_HEADS + h + 1, :]
        m0 = m_scr[h, 0:1, 0:1]
        qf = z[:, _cols_q(h)] * (D_K ** -0.5)
        q_bf = qf.astype(BF16)
        kf = z[:, _cols_k(h)]
        v_bf = z[:, _cols_v(h)].astype(BF16)
        s, a_inter, m_t = _intra(q_bf, kf.astype(BF16), bc, br, li_r, m0, causal)
        return li_r, bc, br, m0, qf, q_bf, kf, v_bf, s, a_inter, m_t

    def finish(h, staged):
        li_r, bc, br, m0, qf, q_bf, kf, v_bf, s, a_inter, m_t = staged
        li_c = pre_c[:, h:h + 1]
        c_old = c_scr[h]
        n_old = n_scr[h]
        num = (a_inter * jnp.dot(q_bf, c_old.astype(BF16), preferred_element_type=F32)
               + jnp.dot(s.astype(BF16), v_bf, preferred_element_type=F32))
        den = (a_inter * jnp.sum(qf * n_old, axis=1, keepdims=True)
               + jnp.sum(s, axis=1, keepdims=True))
        fill()
        y_ref[:, D_CONV + h * D_V:D_CONV + (h + 1) * D_V] = _head_output(
            num, den, m_t, z[:, _cols_o(h)],
            mlg_ref[:, h * D_V:(h + 1) * D_V]).astype(y_ref.dtype)

        b_last = br[:, L - 1:L]
        m_new = jnp.maximum(b_last + m0,
                            jnp.max(b_last - br + li_r, axis=1, keepdims=True))
        decay = jnp.exp(b_last + m0 - m_new)
        kw = kf * jnp.exp(b_last - bc + li_c - m_new)
        c_scr[h] = decay * c_old + lax.dot_general(
            kw.astype(BF16), v_bf, (((0,), (0,)), ((), ())), preferred_element_type=F32)
        n_scr[h] = decay * n_old + jnp.sum(kw, axis=0, keepdims=True)
        m_scr[h] = jnp.broadcast_to(m_new, (SUBLANES, LANES))
        fill()

    staged = scores(0)
    fill()
    for h in range(N_HEADS):
        upcoming = None
        if h + 1 < N_HEADS:
            upcoming = scores(h + 1)
            fill()
        finish(h, staged)
        staged = upcoming
    while fillers:
        fill()


def _fused_prompt_kernel(x_ref, g1_ref, w_hbm, wgate_ref, convw_ref, convg_ref, mlg_ref,
                         bi_ref, bf_ref, wg_ref, wu_ref, wout_ref,
                         y_ref, convs_ref, c_out_ref, n_out_ref, m_out_ref,
                         wgu_bf_ref, wout_bf_ref,
                         w_scr, w_sem, xn_scr, z_scr, ubuf, yc_scr, c_scr, n_scr, m_scr):
    chunk = pl.program_id(1)
    L = x_ref.shape[0]

    @pl.when(chunk == 0)
    def _():
        ubuf[0:SUBLANES, :] = jnp.zeros((SUBLANES, D_CONV), F32)
        c_scr[...] = jnp.zeros(c_scr.shape, F32)
        n_scr[...] = jnp.zeros(n_scr.shape, F32)
        m_scr[...] = jnp.zeros(m_scr.shape, F32)

    def slab_copy(j):
        cols = pl.ds(j * W_SLAB, W_SLAB)
        return pltpu.make_async_copy(w_hbm.at[:, cols], w_scr.at[:, cols], w_sem.at[j])

    def step(first_step):
        if first_step:
            for j in range(N_MAIN // W_SLAB):
                slab_copy(j).start()

        gain = g1_ref[...]
        for r in range(0, L, NORM_ROWS):
            xn_scr[r:r + NORM_ROWS, :] = _rmsnorm_piece(x_ref[r:r + NORM_ROWS, :], gain, BF16)
        xn = xn_scr[...]
        gc = _gate_dot(xn, wgate_ref)
        gr = gc.T[0:SUBLANES, :]

        def project(j):
            if first_step:
                slab_copy(j).wait()
            cols = slice(j * W_SLAB, (j + 1) * W_SLAB)
            z_scr[:, cols] = jnp.dot(xn_scr[...], w_scr[:, cols], preferred_element_type=F32)

        piece = lambda j: functools.partial(project, j)
        skip = lambda: None
        sumsq = []

        def then_conv(j, part):
            def emit():
                project(j)
                sumsq.append(_prompt_conv_part(z_scr, part, convw_ref, ubuf, yc_scr))
            return emit

        project(0)
        wgu_bf_ref[0, :, 0:CAST_PIECE] = wg_ref[...].astype(BF16)
        wout_bf_ref[...] = wout_ref[...].astype(BF16)
        project(1)
        wgu_bf_ref[0, :, CAST_PIECE:2 * CAST_PIECE] = wu_ref[...].astype(BF16)
        fillers = [piece(2), piece(3), piece(4), piece(6),
                   piece(5), then_conv(7, 0), then_conv(8, 1),
                   piece(9), then_conv(10, 2), then_conv(11, 3),
                   skip, skip]
        _prompt_mlstm(z_scr, gc, gr, mlg_ref, bi_ref, bf_ref, y_ref,
                      c_scr, n_scr, m_scr, fillers)
        _prompt_conv_finish(sum(sumsq[1:], sumsq[0]), yc_scr, convg_ref, y_ref)

    is_first = (pl.program_id(0) == 0) & (chunk == 0)
    pl.when(is_first)(functools.partial(step, True))
    pl.when(jnp.logical_not(is_first))(functools.partial(step, False))

    @pl.when(chunk == pl.num_programs(1) - 1)
    def _():
        convs_ref[0] = ubuf[SUBLANES - 2:SUBLANES, :]
        c_out_ref[0] = c_scr[...]
        for h in range(N_HEADS):
            n_out_ref[0, h:h + 1, :] = n_scr[h]
            m_out_ref[0, h:h + 1, :] = m_scr[h, 0:1, :]


def _fused_prompt(x2d, g1, w_bf, w_t, batch, seq, conv_w, conv_g, ml_g, b_i, b_f,
                  wg, wu, w_out):
    L = PROMPT_CHUNK
    nc = seq // L
    assert nc * L == seq, seq
    const = lambda b, c: (0, 0)
    wout_rows = D_MODEL // (batch * nc)
    assert wout_rows * batch * nc == D_MODEL and wout_rows % (2 * SUBLANES) == 0
    wout_piece = pl.BlockSpec((wout_rows, D_MODEL), lambda b, c: (b * nc + c, 0))
    n_pieces = D_FF // CAST_PIECE
    assert batch * nc >= n_pieces, "not enough grid steps to cast the FFN weights"
    per_tile = FFN_TILE // CAST_PIECE
    piece_of = lambda b, c: jnp.minimum(b * nc + c, n_pieces - 1)
    tiled = lambda b, c: (piece_of(b, c) // per_tile, 0, piece_of(b, c) % per_tile)
    return pl.pallas_call(
        _fused_prompt_kernel,
        grid=(batch, nc),
        in_specs=[
            pl.BlockSpec((L, D_MODEL), lambda b, c: (b * nc + c, 0)),
            pl.BlockSpec((1, D_MODEL), const),
            pl.BlockSpec(memory_space=pl.ANY),
            GATE_BLOCK,
            CONV_W_BLOCK,
            pl.BlockSpec((1, D_CONV), const),
            pl.BlockSpec((1, N_HEADS * D_V), const),
            pl.BlockSpec(memory_space=pltpu.SMEM),
            pl.BlockSpec(memory_space=pltpu.SMEM),
            pl.BlockSpec((D_MODEL, CAST_PIECE), lambda b, c: (0, piece_of(b, c))),
            pl.BlockSpec((D_MODEL, CAST_PIECE), lambda b, c: (0, piece_of(b, c))),
            wout_piece,
        ],
        out_specs=[
            pl.BlockSpec((L, D_MODEL), lambda b, c: (b * nc + c, 0)),
            pl.BlockSpec((1, CONV_W - 1, D_CONV), lambda b, c: (b, 0, 0)),
            pl.BlockSpec((1, N_HEADS, D_K, D_V), lambda b, c: (b, 0, 0, 0)),
            pl.BlockSpec((1, N_HEADS, D_K), lambda b, c: (b, 0, 0)),
            pl.BlockSpec((1, N_HEADS, LANES), lambda b, c: (b, 0, 0)),
            pl.BlockSpec((1, D_MODEL, 2 * CAST_PIECE), tiled),
            wout_piece,
        ],
        out_shape=[
            jax.ShapeDtypeStruct((batch * seq, D_MODEL), BF16),
            jax.ShapeDtypeStruct((batch, CONV_W - 1, D_CONV), F32),
            jax.ShapeDtypeStruct((batch, N_HEADS, D_K, D_V), F32),
            jax.ShapeDtypeStruct((batch, N_HEADS, D_K), F32),
            jax.ShapeDtypeStruct((batch, N_HEADS, LANES), F32),
            jax.ShapeDtypeStruct((D_FF // FFN_TILE, D_MODEL, 2 * FFN_TILE), BF16),
            jax.ShapeDtypeStruct((D_MODEL, D_MODEL), BF16),
        ],
        scratch_shapes=[
            pltpu.VMEM((D_MODEL, N_MAIN), BF16),
            pltpu.SemaphoreType.DMA((N_MAIN // W_SLAB,)),
            pltpu.VMEM((L, D_MODEL), BF16),
            pltpu.VMEM((L, N_MAIN), F32),
            pltpu.VMEM((L + 2 * SUBLANES, D_CONV), F32),
            pltpu.VMEM((L, D_CONV), F32),
            pltpu.VMEM((N_HEADS, D_K, D_V), F32),
            pltpu.VMEM((N_HEADS, 1, D_K), F32),
            pltpu.VMEM((N_HEADS, SUBLANES, LANES), F32),
        ],
        compiler_params=_compiler_params(("arbitrary", "arbitrary")),
        name="fused_prompt",
    )(x2d, g1, w_bf, w_t, conv_w, conv_g, ml_g, b_i, b_f, wg, wu, w_out)


def _mix_sample_kernel(z_ref, gcol_ref, grow_ref, mcol_ref, mrow_ref,
                       convs_in_ref, c_in_ref, n_in_ref,
                       convw_ref, convg_ref, mlg_ref, bi_ref, bf_ref,
                       y_ref, convs_ref, c_out_ref, n_out_ref, m_out_ref,
                       pad_scr, conv_scr, *, seq):
    L = z_ref.shape[0]
    group = L // seq
    stride = seq + SUBLANES

    for g in range(group):
        rows = slice(g * seq, (g + 1) * seq)
        u = _conv_operand(z_ref, CG, rows) * _conv_operand(z_ref, XT, rows)
        base = g * stride
        pad_scr[base + SUBLANES - 2:base + SUBLANES, :] = convs_in_ref[g]
        pad_scr[base + SUBLANES:base + SUBLANES + seq, :] = u
        conv_scr[g * seq:(g + 1) * seq, :] = (
            pad_scr[base + SUBLANES - 2:base + SUBLANES - 2 + seq, :] * convw_ref[0]
            + pad_scr[base + SUBLANES - 1:base + SUBLANES - 1 + seq, :] * convw_ref[1]
            + u * convw_ref[2])
        convs_ref[g] = pad_scr[base + seq + SUBLANES - 2:base + seq + SUBLANES, :]
    y_ref[:, 0:D_CONV] = _conv_norm(_conv_operand(z_ref, BG), conv_scr[...],
                                    convg_ref[...]).astype(y_ref.dtype)

    bias_row, bias_col = _gate_bias(bi_ref, bf_ref)
    pre_c = gcol_ref[...] + bias_row
    pre_r = grow_ref[0:SUBLANES, :] + bias_col
    row_id = lax.broadcasted_iota(jnp.int32, (L, L), 0)
    col_id = lax.broadcasted_iota(jnp.int32, (L, L), 1)
    same = (row_id // seq) == (col_id // seq)
    causal = same & (col_id <= row_id)
    lf_c = _log_sigmoid(pre_c)
    lf_r = _log_sigmoid(pre_r)
    b_c, b_r = _masked_sums(causal, lf_c, lf_r)
    tot_c, tot_r = _masked_sums(same, lf_c, lf_r)
    lane_seq = lax.broadcasted_iota(jnp.int32, (1, L), 1) // seq

    for h in range(N_HEADS):
        li_r = pre_r[h:h + 1, :]
        li_c = pre_c[:, h:h + 1]
        bc = b_c[:, N_HEADS + h:N_HEADS + h + 1]
        br = b_r[N_HEADS + h:N_HEADS + h + 1, :]
        bl_c = tot_c[:, N_HEADS + h:N_HEADS + h + 1]
        bl_r = tot_r[N_HEADS + h:N_HEADS + h + 1, :]
        m0_c = mcol_ref[:, h:h + 1]
        m0_r = mrow_ref[h:h + 1, :]
        qf = z_ref[:, _cols_q(h)] * (D_K ** -0.5)
        q_bf = qf.astype(BF16)
        kf = z_ref[:, _cols_k(h)]
        v_bf = z_ref[:, _cols_v(h)].astype(BF16)
        s, a_inter, m_t = _intra(q_bf, kf.astype(BF16), bc, br, li_r, m0_c, causal)

        qc_rows, qn_rows = [], []
        for g in range(group):
            rows = slice(g * seq, (g + 1) * seq)
            qc_rows.append(jnp.dot(qf[rows], c_in_ref[g, h], preferred_element_type=F32))
            qn_rows.append(jnp.sum(qf[rows] * n_in_ref[g, h:h + 1, :], axis=1, keepdims=True))
        num = (a_inter * jnp.concatenate(qc_rows, axis=0)
               + jnp.dot(s.astype(BF16), v_bf, preferred_element_type=F32))
        den = (a_inter * jnp.concatenate(qn_rows, axis=0)
               + jnp.sum(s, axis=1, keepdims=True))
        y_ref[:, D_CONV + h * D_V:D_CONV + (h + 1) * D_V] = _head_output(
            num, den, m_t, z_ref[:, _cols_o(h)],
            mlg_ref[:, h * D_V:(h + 1) * D_V]).astype(y_ref.dtype)

        g_r = bl_r - br + li_r
        gmax_c = jnp.max(jnp.where(same, g_r, -jnp.inf), axis=1, keepdims=True)
        gmax_r = jnp.max(jnp.where(same, gmax_c, -jnp.inf), axis=0, keepdims=True)
        m_new_c = jnp.maximum(bl_c + m0_c, gmax_c)
        m_new_r = jnp.maximum(bl_r + m0_r, gmax_r)
        decay_r = jnp.exp(bl_r + m0_r - m_new_r)
        kw = kf * jnp.exp(bl_c - bc + li_c - m_new_c)
        kw_t = kw.T
        for g in range(group):
            decay = decay_r[:, g * seq:g * seq + 1]
            kw_g = jnp.where(lane_seq == g, kw_t, 0.0).astype(BF16)
            c_out_ref[g, h] = decay * c_in_ref[g, h] + jnp.dot(
                kw_g, v_bf, preferred_element_type=F32)
            n_out_ref[g, h:h + 1, :] = (decay * n_in_ref[g, h:h + 1, :]
                                        + jnp.sum(kw[g * seq:(g + 1) * seq], axis=0, keepdims=True))
        m_out_ref[h:h + 1, :] = m_new_r
    m_out_ref[N_HEADS:, :] = jnp.zeros((SUBLANES - N_HEADS, L), F32)


def _mix_sample(z, gcol, grow, mcol, mrow, convs, c_state, n_state, batch, seq,
                conv_w, conv_g, ml_g, b_i, b_f):
    group = SAMPLE_GROUP
    L = group * seq
    assert seq == SUBLANES and L == LANES and batch % group == 0, (batch, seq)
    const = lambda i: (0, 0)
    return pl.pallas_call(
        functools.partial(_mix_sample_kernel, seq=seq),
        grid=(batch // group,),
        in_specs=[
            pl.BlockSpec((L, N_MAIN), lambda i: (i, 0)),
            pl.BlockSpec((L, LANES), lambda i: (i, 0)),
            pl.BlockSpec((GATE_ROWS, L), lambda i: (0, i)),
            pl.BlockSpec((L, LANES), lambda i: (i, 0)),
            pl.BlockSpec((SUBLANES, L), lambda i: (0, i)),
            pl.BlockSpec((group, CONV_W - 1, D_CONV), lambda i: (i, 0, 0)),
            pl.BlockSpec((group, N_HEADS, D_K, D_V), lambda i: (i, 0, 0, 0)),
            pl.BlockSpec((group, N_HEADS, D_K), lambda i: (i, 0, 0)),
            CONV_W_BLOCK,
            pl.BlockSpec((1, D_CONV), const),
            pl.BlockSpec((1, N_HEADS * D_V), const),
            pl.BlockSpec(memory_space=pltpu.SMEM),
            pl.BlockSpec(memory_space=pltpu.SMEM),
        ],
        out_specs=[
            pl.BlockSpec((L, D_MODEL), lambda i: (i, 0)),
            pl.BlockSpec((group, CONV_W - 1, D_CONV), lambda i: (i, 0, 0)),
            pl.BlockSpec((group, N_HEADS, D_K, D_V), lambda i: (i, 0, 0, 0)),
            pl.BlockSpec((group, N_HEADS, D_K), lambda i: (i, 0, 0)),
            pl.BlockSpec((SUBLANES, L), lambda i: (0, i)),
        ],
        out_shape=[
            jax.ShapeDtypeStruct((batch * seq, D_MODEL), BF16),
            jax.ShapeDtypeStruct((batch, CONV_W - 1, D_CONV), F32),
            jax.ShapeDtypeStruct((batch, N_HEADS, D_K, D_V), F32),
            jax.ShapeDtypeStruct((batch, N_HEADS, D_K), F32),
            jax.ShapeDtypeStruct((SUBLANES, batch * seq), F32),
        ],
        scratch_shapes=[
            pltpu.VMEM((group * (seq + SUBLANES), D_CONV), F32),
            pltpu.VMEM((L, D_CONV), F32),
        ],
        compiler_params=_compiler_params(("parallel",)),
        name="mix_sample",
    )(z, gcol, grow, mcol, mrow, convs, c_state, n_state,
      conv_w, conv_g, ml_g, b_i, b_f)


def _outproj_kernel(ya_ref, yb_ref, w_ref, xa_ref, xb_ref, o_ref, *, tiles_a):
    i = pl.program_id(0)

    @pl.when(i < tiles_a)
    def _():
        o_ref[...] = xa_ref[...] + jnp.dot(ya_ref[...], w_ref[...], preferred_element_type=F32)

    @pl.when(i >= tiles_a)
    def _():
        o_ref[...] = xb_ref[...] + jnp.dot(yb_ref[...], w_ref[...], preferred_element_type=F32)


def _outproj(ya, yb, w_out_bf, xa, xb):
    tm = 512
    assert xa.shape[0] % tm == 0 and xb.shape[0] % tm == 0, (xa.shape, xb.shape)
    tiles_a, tiles_b = xa.shape[0] // tm, xb.shape[0] // tm
    rows_a = pl.BlockSpec((tm, D_MODEL), lambda i: (jnp.minimum(i, tiles_a - 1), 0))
    rows_b = pl.BlockSpec((tm, D_MODEL), lambda i: (jnp.maximum(i - tiles_a, 0), 0))
    return pl.pallas_call(
        functools.partial(_outproj_kernel, tiles_a=tiles_a),
        grid=(tiles_a + tiles_b,),
        in_specs=[
            rows_a, rows_b,
            pl.BlockSpec((D_MODEL, D_MODEL), lambda i: (0, 0), pipeline_mode=pl.Buffered(1)),
            rows_a, rows_b,
        ],
        out_specs=pl.BlockSpec((tm, D_MODEL), lambda i: (i, 0)),
        out_shape=jax.ShapeDtypeStruct(((tiles_a + tiles_b) * tm, D_MODEL), F32),
        compiler_params=_compiler_params(("parallel",)),
        name="outproj",
    )(ya, yb, w_out_bf, xa, xb)


def _ffn_kernel(x_ref, g2_ref, wgu_ref, wd_ref, gf_ref, o_ref, hn_ref, *, final_norm):
    j = pl.program_id(1)

    @pl.when(j == 0)
    def _():
        _rmsnorm_rows(x_ref, g2_ref[...], hn_ref, copy_ref=o_ref)

    hn = hn_ref[...]
    half = wd_ref.shape[0] // 2
    halves = (slice(0, half), slice(half, 2 * half))
    pre = []
    for a in range(2):
        gate_up = jnp.dot(hn, wgu_ref[0, :, 2 * a * half:2 * (a + 1) * half],
                          preferred_element_type=F32)
        pre.append((gate_up[:, 0:half], gate_up[:, half:2 * half]))
    down = None
    for (gate, up), rows in zip(pre, halves):
        act = ((gate * jax.nn.sigmoid(gate)) * up).astype(BF16)
        part = jnp.dot(act, wd_ref[rows, :].astype(BF16), preferred_element_type=F32)
        down = part if down is None else down + part
    o_ref[...] += down

    if final_norm:
        @pl.when(j == pl.num_programs(1) - 1)
        def _():
            _rmsnorm_rows_inplace(o_ref, gf_ref[...])


def _ffn(x1, first_row, tokens, g2, wgu_bf, wd, gf, final_norm):
    tm = min(1024, tokens)
    tf = FFN_TILE
    assert tokens % tm == 0 and first_row % tm == 0, (first_row, tokens)
    assert tm % (NORM_ROWS * NORM_GROUP) == 0, tm
    first_tile = first_row // tm
    return pl.pallas_call(
        functools.partial(_ffn_kernel, final_norm=final_norm),
        grid=(tokens // tm, D_FF // tf),
        in_specs=[
            pl.BlockSpec((tm, D_MODEL), lambda i, j: (i + first_tile, 0)),
            pl.BlockSpec((1, D_MODEL), lambda i, j: (0, 0)),
            pl.BlockSpec((1, D_MODEL, 2 * tf), lambda i, j: (j, 0, 0)),
            pl.BlockSpec((tf, D_MODEL), lambda i, j: (j, 0)),
            pl.BlockSpec((1, D_MODEL), lambda i, j: (0, 0)),
        ],
        out_specs=pl.BlockSpec((tm, D_MODEL), lambda i, j: (i, 0)),
        out_shape=jax.ShapeDtypeStruct((tokens, D_MODEL), F32),
        scratch_shapes=[pltpu.VMEM((tm, D_MODEL), BF16)],
        compiler_params=_compiler_params(("parallel", "arbitrary")),
        name="ffn",
    )(x1, g2, wgu_bf, wd, gf)


def _stack(states, k):
    if len(states) == 1:
        return states[0][k][None]
    return jnp.stack([st[k] for st in states])


def kernel(x_prompt, x_sample, state_conv, state_mlstm_C, state_mlstm_n, state_mlstm_m,
           norm1_g, w_in, b_igate, b_fgate, conv_w, conv_out_g, mlstm_out_g, w_out,
           norm2_g, w_ffn_gate, w_ffn_up, w_ffn_down, final_norm_g):
    depth = w_in.shape[0]
    bp, sp, _ = x_prompt.shape
    bs, ss, _ = x_sample.shape
    hp = x_prompt.reshape(bp * sp, D_MODEL)
    hs = x_sample.reshape(bs * ss, D_MODEL)
    p_states, s_states = [], []
    for l in range(depth):
        final_norm = l == depth - 1
        w_t = w_in[l].T
        gate_b = (b_igate[l], b_fgate[l])
        conv_taps = conv_w[l][:, None, :]
        w_bf = _cast_main_columns(w_t)
        g1 = norm1_g[l][None, :]
        conv_g = conv_out_g[l][None, :]
        ml_g = mlstm_out_g[l][None, :]

        y_p, cb, c1, n1, m1, wgu_bf, w_out_bf = _fused_prompt(
            hp, g1, w_bf, w_t, bp, sp, conv_taps, conv_g, ml_g, *gate_b,
            w_ffn_gate[l], w_ffn_up[l], w_out[l])
        p_states.append((cb, c1, n1, m1[:, :, 0]))

        z, gcol, grow = _inproj(hs, g1, w_bf, w_t)
        m_tok = jnp.repeat(state_mlstm_m[l], ss, axis=0)
        mcol = jnp.pad(m_tok, ((0, 0), (0, LANES - N_HEADS)))
        mrow = jnp.pad(m_tok.T, ((0, SUBLANES - N_HEADS), (0, 0)))
        y_s, cb, c1, n1, m_row = _mix_sample(z, gcol, grow, mcol, mrow, state_conv[l],
                                             state_mlstm_C[l], state_mlstm_n[l], bs, ss,
                                             conv_taps, conv_g, ml_g, *gate_b)
        s_states.append((cb, c1, n1, m_row[:N_HEADS, ::ss].T))

        x1 = _outproj(y_p, y_s, w_out_bf, hp, hs)
        ffn = (norm2_g[l][None, :], wgu_bf, w_ffn_down[l], final_norm_g[None, :], final_norm)
        hp = _ffn(x1, 0, bp * sp, *ffn)
        hs = _ffn(x1, bp * sp, bs * ss, *ffn)

    return (hp.reshape(bp, sp, D_MODEL), hs.reshape(bs, ss, D_MODEL),
            _stack(p_states, 0), _stack(p_states, 1), _stack(p_states, 2), _stack(p_states, 3),
            _stack(s_states, 0), _stack(s_states, 1), _stack(s_states, 2), _stack(s_states, 3))
```

```python
import functools

import jax
import jax.numpy as jnp
from jax import lax
from jax.experimental import pallas as pl
from jax.experimental.pallas import tpu as pltpu

F32 = jnp.float32
BF16 = jnp.bfloat16

D_MODEL = 2048
D_CONV = 1024
CONV_W = 3
N_HEADS = 4
D_K = 128
D_V = 256
D_FF = 5632
EPS = 1e-6

LANES = 128
SUBLANES = 8
GATE_ROWS = 16
MIB = 1024 * 1024

PROMPT_CHUNK = 256
SAMPLE_GROUP = 16
NORM_ROWS = 64
NORM_GROUP = 4
W_SLAB = 512
CAST_SLAB = 1024
FFN_TILE = 512
CAST_PIECE = 256

VMEM_LIMIT_MIB = 62

SRC_Q = 3 * D_CONV
SRC_K = SRC_Q + N_HEADS * D_K
SRC_V = SRC_K + N_HEADS * D_K
SRC_O = SRC_V + N_HEADS * D_V
N_MAIN = SRC_O + N_HEADS * D_V

HEAD_COLS = 2 * D_K + 2 * D_V
N_MLSTM = N_HEADS * HEAD_COLS
CONV_PARTS = 4
CONV_PART = D_CONV // CONV_PARTS
BG, CG, XT = 0, 1, 2


def _cols_conv(kind, part):
    start = N_MLSTM + (3 * part + kind) * CONV_PART
    return slice(start, start + CONV_PART)


def _conv_operand(z, kind, rows=slice(None)):
    return jnp.concatenate([z[rows, _cols_conv(kind, p)] for p in range(CONV_PARTS)], axis=1)


def _cols_q(h):
    return slice(h * HEAD_COLS, h * HEAD_COLS + D_K)


def _cols_k(h):
    return slice(h * HEAD_COLS + D_K, h * HEAD_COLS + 2 * D_K)


def _cols_v(h):
    return slice(h * HEAD_COLS + 2 * D_K, h * HEAD_COLS + 2 * D_K + D_V)


def _cols_o(h):
    return slice(h * HEAD_COLS + 2 * D_K + D_V, (h + 1) * HEAD_COLS)


def _source_lane_block(j):
    per_head = HEAD_COLS // LANES
    h, r = j // per_head, j % per_head
    v_blocks = D_V // LANES
    mlstm_src = jnp.where(
        r == 0, SRC_Q // LANES + h,
        jnp.where(r == 1, SRC_K // LANES + h,
                  jnp.where(r < 2 + v_blocks,
                            SRC_V // LANES + v_blocks * h + (r - 2),
                            SRC_O // LANES + v_blocks * h + (r - 2 - v_blocks))))
    per_kind = CONV_PART // LANES
    c = j - N_MLSTM // LANES
    part, r = c // (3 * per_kind), c % (3 * per_kind)
    conv_src = (r // per_kind) * (D_CONV // LANES) + part * per_kind + r % per_kind
    return jnp.where(j < N_MLSTM // LANES, mlstm_src, conv_src)


def _compiler_params(semantics):
    return pltpu.CompilerParams(dimension_semantics=semantics,
                                vmem_limit_bytes=VMEM_LIMIT_MIB * MIB)


def _rmsnorm_piece(x, gain, dtype):
    ms = jnp.mean(x * x, axis=-1, keepdims=True)
    return ((x * lax.rsqrt(ms + EPS)) * gain).astype(dtype)


def _rmsnorm_rows(src_ref, gain, dst_ref, copy_ref=None):
    rows = src_ref.shape[0]
    chunk = min(NORM_ROWS, rows)

    def body(i, carry):
        r = pl.multiple_of(i * chunk, chunk)
        x = src_ref[pl.ds(r, chunk), :].astype(F32)
        if copy_ref is not None:
            copy_ref[pl.ds(r, chunk), :] = x
        dst_ref[pl.ds(r, chunk), :] = _rmsnorm_piece(x, gain, dst_ref.dtype)
        return carry

    lax.fori_loop(0, rows // chunk, body, 0, unroll=2)


def _rmsnorm_rows_inplace(ref, gain):
    rows = ref.shape[0]
    chunk = min(NORM_ROWS, rows)
    group = min(NORM_GROUP, rows // chunk)

    def body(i, carry):
        starts = [pl.multiple_of((i * group + k) * chunk, chunk) for k in range(group)]
        scales = []
        for r in starts:
            x = ref[pl.ds(r, chunk), :]
            scales.append(lax.rsqrt(jnp.mean(x * x, axis=-1, keepdims=True) + EPS))
        for r, scale in zip(starts, scales):
            ref[pl.ds(r, chunk), :] = (ref[pl.ds(r, chunk), :] * scale) * gain
        return carry

    lax.fori_loop(0, rows // (chunk * group), body, 0)


def _cast_kernel(*refs):
    *w_refs, o_ref = refs
    for r, w_ref in enumerate(w_refs):
        o_ref[:, r * LANES:(r + 1) * LANES] = w_ref[...].T.astype(o_ref.dtype)


def _cast_main_columns(w_t):
    per_slab = CAST_SLAB // LANES
    in_specs = [
        pl.BlockSpec((LANES, D_MODEL),
                     lambda j, r=r: (_source_lane_block(per_slab * j + r), 0))
        for r in range(per_slab)
    ]
    return pl.pallas_call(
        _cast_kernel,
        grid=(N_MAIN // CAST_SLAB,),
        in_specs=in_specs,
        out_specs=pl.BlockSpec((D_MODEL, CAST_SLAB), lambda j: (0, j)),
        out_shape=jax.ShapeDtypeStruct((D_MODEL, N_MAIN), BF16),
        compiler_params=_compiler_params(("parallel",)),
        name="cast_w_in",
    )(*([w_t] * per_slab))


GATE_BLOCK = pl.BlockSpec((2 * N_HEADS, D_MODEL), lambda *_: (N_MAIN // (2 * N_HEADS), 0))
CONV_W_BLOCK = pl.BlockSpec((CONV_W, 1, D_CONV), lambda *_: (0, 0, 0))


def _gate_bias(bi_ref, bf_ref):
    lane = lax.broadcasted_iota(jnp.int32, (1, LANES), 1)
    sub = lax.broadcasted_iota(jnp.int32, (2 * N_HEADS, 1), 0)
    row = jnp.zeros((1, LANES), F32)
    col = jnp.zeros((2 * N_HEADS, 1), F32)
    for k in range(N_HEADS):
        for pos, value in ((k, bi_ref[k]), (N_HEADS + k, bf_ref[k])):
            row = jnp.where(lane == pos, value, row)
            col = jnp.where(sub == pos, value, col)
    return row, col


def _gate_dot(xn, wgate_t_ref):
    rows = wgate_t_ref[...]
    padded = jnp.concatenate([rows, jnp.zeros((LANES - rows.shape[0], D_MODEL), F32)], axis=0)
    return lax.dot_general(xn, padded.astype(BF16), (((1,), (1,)), ((), ())),
                           preferred_element_type=F32)


def _inproj_kernel(x_ref, g_ref, w_ref, wgate_ref, z_ref, gcol_ref, grow_ref, xn_ref):
    j = pl.program_id(1)

    @pl.when(j == 0)
    def _():
        gain = g_ref[...]
        piece = NORM_ROWS * NORM_GROUP
        for r in range(0, x_ref.shape[0], piece):
            for c in range(r, r + piece, NORM_ROWS):
                xn_ref[c:c + NORM_ROWS, :] = _rmsnorm_piece(x_ref[c:c + NORM_ROWS, :], gain, BF16)
            z_ref[r:r + piece, :] = jnp.dot(xn_ref[r:r + piece, :], w_ref[...],
                                            preferred_element_type=F32)
        gc = _gate_dot(xn_ref[...], wgate_ref)
        gcol_ref[...] = gc
        grow_ref[...] = gc.T[0:GATE_ROWS, :]

    @pl.when(j > 0)
    def _():
        z_ref[...] = jnp.dot(xn_ref[...], w_ref[...], preferred_element_type=F32)


def _inproj(x2d, g1, w_bf, w_t):
    tokens = x2d.shape[0]
    tm = min(1024, tokens)
    tn = W_SLAB
    assert tokens % tm == 0 and tm % (NORM_ROWS * NORM_GROUP) == 0, tokens
    return pl.pallas_call(
        _inproj_kernel,
        grid=(tokens // tm, N_MAIN // tn),
        in_specs=[
            pl.BlockSpec((tm, D_MODEL), lambda i, j: (i, 0)),
            pl.BlockSpec((1, D_MODEL), lambda i, j: (0, 0)),
            pl.BlockSpec((D_MODEL, tn), lambda i, j: (0, j)),
            GATE_BLOCK,
        ],
        out_specs=[
            pl.BlockSpec((tm, tn), lambda i, j: (i, j)),
            pl.BlockSpec((tm, LANES), lambda i, j: (i, 0)),
            pl.BlockSpec((GATE_ROWS, tm), lambda i, j: (0, i)),
        ],
        out_shape=[
            jax.ShapeDtypeStruct((tokens, N_MAIN), F32),
            jax.ShapeDtypeStruct((tokens, LANES), F32),
            jax.ShapeDtypeStruct((GATE_ROWS, tokens), F32),
        ],
        scratch_shapes=[pltpu.VMEM((tm, D_MODEL), BF16)],
        compiler_params=_compiler_params(("parallel", "arbitrary")),
        name="inproj",
    )(x2d, g1, w_bf, w_t)


def _log_sigmoid(x):
    return jnp.minimum(x, 0.0) - jnp.log1p(jnp.exp(-jnp.abs(x)))


def _split3(x):
    hi = x.astype(BF16).astype(F32)
    mid = (x - hi).astype(BF16).astype(F32)
    lo = ((x - hi) - mid).astype(BF16).astype(F32)
    return hi, mid, lo


def _masked_sums(mask, lf_col, lf_row):
    m_bf = mask.astype(BF16)
    n, k = lf_col.shape[1], lf_row.shape[0]
    col_terms = jnp.concatenate(_split3(lf_col), axis=1).astype(BF16)
    row_terms = jnp.concatenate(_split3(lf_row) + (jnp.zeros_like(lf_row),),
                                axis=0).astype(BF16)
    cols = jnp.dot(m_bf, col_terms, preferred_element_type=F32)
    rows = lax.dot_general(row_terms, m_bf, (((1,), (1,)), ((), ())),
                           preferred_element_type=F32)
    return (cols[:, 0:n] + cols[:, n:2 * n] + cols[:, 2 * n:3 * n],
            rows[0:k] + rows[k:2 * k] + rows[2 * k:3 * k])


def _intra(q_bf, k_bf, b_col, b_row, li_row, m0_col, mask):
    dmat = jnp.where(mask, b_col - b_row + li_row, -jnp.inf)
    inter = b_col + m0_col
    m_t = jnp.maximum(inter, jnp.max(dmat, axis=1, keepdims=True))
    w = jnp.exp(dmat - m_t)
    a_inter = jnp.exp(inter - m_t)
    qk = lax.dot_general(q_bf, k_bf, (((1,), (1,)), ((), ())), preferred_element_type=F32)
    return qk * w, a_inter, m_t


def _head_output(num, den, m_t, o_gate, gain):
    denom = jnp.maximum(jnp.abs(den), jnp.exp(-m_t))
    h = num * (1.0 / denom)
    hn = (h * lax.rsqrt(jnp.mean(h * h, axis=1, keepdims=True) + EPS)) * gain
    return jax.nn.sigmoid(o_gate) * hn


def _conv_norm(bg, conv, gain):
    yc = bg * conv
    return (yc * lax.rsqrt(jnp.mean(yc * yc, axis=1, keepdims=True) + EPS)) * gain


def _prompt_conv_part(z, part, convw_ref, ubuf, yc_scr):
    L = z.shape[0]
    ch = slice(part * CONV_PART, (part + 1) * CONV_PART)
    u = z[:, _cols_conv(CG, part)] * z[:, _cols_conv(XT, part)]
    ubuf[SUBLANES:SUBLANES + L, ch] = u
    conv = (ubuf[SUBLANES - 2:SUBLANES - 2 + L, ch] * convw_ref[0, :, ch]
            + ubuf[SUBLANES - 1:SUBLANES - 1 + L, ch] * convw_ref[1, :, ch]
            + u * convw_ref[2, :, ch])
    ubuf[0:SUBLANES, ch] = ubuf[L:L + SUBLANES, ch]
    yc = z[:, _cols_conv(BG, part)] * conv
    yc_scr[:, ch] = yc
    return jnp.sum(yc * yc, axis=1, keepdims=True)


def _prompt_conv_finish(sumsq, yc_scr, convg_ref, y_ref):
    scale = lax.rsqrt(sumsq * (1.0 / D_CONV) + EPS)
    y_ref[:, 0:D_CONV] = ((yc_scr[...] * scale) * convg_ref[...]).astype(y_ref.dtype)


def _prompt_mlstm(z, gc, gr, mlg_ref, bi_ref, bf_ref, y_ref, c_scr, n_scr, m_scr,
                  fillers):
    L = z.shape[0]
    fillers = list(fillers)

    def fill():
        if fillers:
            fillers.pop(0)()

    bias_row, bias_col = _gate_bias(bi_ref, bf_ref)
    pre_c = gc + bias_row
    pre_r = gr + bias_col
    row_id = lax.broadcasted_iota(jnp.int32, (L, L), 0)
    col_id = lax.broadcasted_iota(jnp.int32, (L, L), 1)
    causal = col_id <= row_id
    b_c, b_r = _masked_sums(causal, _log_sigmoid(pre_c), _log_sigmoid(pre_r))

    def scores(h):
        li_r = pre_r[h:h + 1, :]
        bc = b_c[:, N_HEADS + h:N_HEADS + h + 1]
        br = b_r[N_HEADS + h:N_HEADS + h + 1, :]
        m0 = m_scr[h, 0:1, 0:1]
        qf = z[:, _cols_q(h)] * (D_K ** -0.5)
        q_bf = qf.astype(BF16)
        kf = z[:, _cols_k(h)]
        v_bf = z[:, _cols_v(h)].astype(BF16)
        s, a_inter, m_t = _intra(q_bf, kf.astype(BF16), bc, br, li_r, m0, causal)
        return li_r, bc, br, m0, qf, q_bf, kf, v_bf, s, a_inter, m_t

    def finish(h, staged):
        li_r, bc, br, m0, qf, q_bf, kf, v_bf, s, a_inter, m_t = staged
        li_c = pre_c[:, h:h + 1]
        c_old = c_scr[h]
        n_old = n_scr[h]
        num = (a_inter * jnp.dot(q_bf, c_old.astype(BF16), preferred_element_type=F32)
               + jnp.dot(s.astype(BF16), v_bf, preferred_element_type=F32))
        den = (a_inter * jnp.sum(qf * n_old, axis=1, keepdims=True)
               + jnp.sum(s, axis=1, keepdims=True))
        fill()
        y_ref[:, D_CONV + h * D_V:D_CONV + (h + 1) * D_V] = _head_output(
            num, den, m_t, z[:, _cols_o(h)],
            mlg_ref[:, h * D_V:(h + 1) * D_V]).astype(y_ref.dtype)

        b_last = br[:, L - 1:L]
        m_new = jnp.maximum(b_last + m0,
                            jnp.max(b_last - br + li_r, axis=1, keepdims=True))
        decay = jnp.exp(b_last + m0 - m_new)
        kw = kf * jnp.exp(b_last - bc + li_c - m_new)
        c_scr[h] = decay * c_old + lax.dot_general(
            kw.astype(BF16), v_bf, (((0,), (0,)), ((), ())), preferred_element_type=F32)
        n_scr[h] = decay * n_old + jnp.sum(kw, axis=0, keepdims=True)
        m_scr[h] = jnp.broadcast_to(m_new, (SUBLANES, LANES))
        fill()

    staged = scores(0)
    fill()
    for h in range(N_HEADS):
        upcoming = None
        if h + 1 < N_HEADS:
            upcoming = scores(h + 1)
            fill()
        finish(h, staged)
        staged = upcoming
    while fillers:
        fill()


def _fused_prompt_kernel(x_ref, g1_ref, w_ref, wgate_ref, convw_ref, convg_ref, mlg_ref,
                         bi_ref, bf_ref, wg_ref, wu_ref, wout_ref,
                         y_ref, convs_ref, c_out_ref, n_out_ref, m_out_ref,
                         wgu_bf_ref, wout_bf_ref,
                         xn_scr, z_scr, ubuf, yc_scr, c_scr, n_scr, m_scr):
    chunk = pl.program_id(1)
    L = x_ref.shape[0]

    @pl.when(chunk == 0)
    def _():
        ubuf[0:SUBLANES, :] = jnp.zeros((SUBLANES, D_CONV), F32)
        c_scr[...] = jnp.zeros(c_scr.shape, F32)
        n_scr[...] = jnp.zeros(n_scr.shape, F32)
        m_scr[...] = jnp.zeros(m_scr.shape, F32)

    gain = g1_ref[...]
    for r in range(0, L, NORM_ROWS):
        xn_scr[r:r + NORM_ROWS, :] = _rmsnorm_piece(x_ref[r:r + NORM_ROWS, :], gain, BF16)
    xn = xn_scr[...]
    gc = _gate_dot(xn, wgate_ref)
    gr = gc.T[0:SUBLANES, :]

    def project(j):
        cols = slice(j * W_SLAB, (j + 1) * W_SLAB)
        z_scr[:, cols] = jnp.dot(xn_scr[...], w_ref[:, cols], preferred_element_type=F32)

    piece = lambda j: functools.partial(project, j)
    skip = lambda: None
    sumsq = []

    def then_conv(j, part):
        def emit():
            project(j)
            sumsq.append(_prompt_conv_part(z_scr, part, convw_ref, ubuf, yc_scr))
        return emit

    project(0)
    wgu_bf_ref[0, :, 0:CAST_PIECE] = wg_ref[...].astype(BF16)
    wout_bf_ref[...] = wout_ref[...].astype(BF16)
    project(1)
    wgu_bf_ref[0, :, CAST_PIECE:2 * CAST_PIECE] = wu_ref[...].astype(BF16)
    fillers = [piece(2), piece(3), piece(4), piece(6),
               piece(5), then_conv(7, 0), then_conv(8, 1),
               piece(9), then_conv(10, 2), then_conv(11, 3),
               skip, skip]
    _prompt_mlstm(z_scr, gc, gr, mlg_ref, bi_ref, bf_ref, y_ref,
                  c_scr, n_scr, m_scr, fillers)
    _prompt_conv_finish(sum(sumsq[1:], sumsq[0]), yc_scr, convg_ref, y_ref)

    @pl.when(chunk == pl.num_programs(1) - 1)
    def _():
        convs_ref[0] = ubuf[SUBLANES - 2:SUBLANES, :]
        c_out_ref[0] = c_scr[...]
        for h in range(N_HEADS):
            n_out_ref[0, h:h + 1, :] = n_scr[h]
            m_out_ref[0, h:h + 1, :] = m_scr[h, 0:1, :]


def _fused_prompt(x2d, g1, w_bf, w_t, batch, seq, conv_w, conv_g, ml_g, b_i, b_f,
                  wg, wu, w_out):
    L = PROMPT_CHUNK
    nc = seq // L
    assert nc * L == seq, seq
    const = lambda b, c: (0, 0)
    wout_rows = D_MODEL // (batch * nc)
    assert wout_rows * batch * nc == D_MODEL and wout_rows % (2 * SUBLANES) == 0
    wout_piece = pl.BlockSpec((wout_rows, D_MODEL), lambda b, c: (b * nc + c, 0))
    n_pieces = D_FF // CAST_PIECE
    assert batch * nc >= n_pieces, "not enough grid steps to cast the FFN weights"
    per_tile = FFN_TILE // CAST_PIECE
    piece_of = lambda b, c: jnp.minimum(b * nc + c, n_pieces - 1)
    tiled = lambda b, c: (piece_of(b, c) // per_tile, 0, piece_of(b, c) % per_tile)
    return pl.pallas_call(
        _fused_prompt_kernel,
        grid=(batch, nc),
        in_specs=[
            pl.BlockSpec((L, D_MODEL), lambda b, c: (b * nc + c, 0)),
            pl.BlockSpec((1, D_MODEL), const),
            pl.BlockSpec((D_MODEL, N_MAIN), const, pipeline_mode=pl.Buffered(1)),
            GATE_BLOCK,
            CONV_W_BLOCK,
            pl.BlockSpec((1, D_CONV), const),
            pl.BlockSpec((1, N_HEADS * D_V), const),
            pl.BlockSpec(memory_space=pltpu.SMEM),
            pl.BlockSpec(memory_space=pltpu.SMEM),
            pl.BlockSpec((D_MODEL, CAST_PIECE), lambda b, c: (0, piece_of(b, c))),
            pl.BlockSpec((D_MODEL, CAST_PIECE), lambda b, c: (0, piece_of(b, c))),
            wout_piece,
        ],
        out_specs=[
            pl.BlockSpec((L, D_MODEL), lambda b, c: (b * nc + c, 0)),
            pl.BlockSpec((1, CONV_W - 1, D_CONV), lambda b, c: (b, 0, 0)),
            pl.BlockSpec((1, N_HEADS, D_K, D_V), lambda b, c: (b, 0, 0, 0)),
            pl.BlockSpec((1, N_HEADS, D_K), lambda b, c: (b, 0, 0)),
            pl.BlockSpec((1, N_HEADS, LANES), lambda b, c: (b, 0, 0)),
            pl.BlockSpec((1, D_MODEL, 2 * CAST_PIECE), tiled),
            wout_piece,
        ],
        out_shape=[
            jax.ShapeDtypeStruct((batch * seq, D_MODEL), BF16),
            jax.ShapeDtypeStruct((batch, CONV_W - 1, D_CONV), F32),
            jax.ShapeDtypeStruct((batch, N_HEADS, D_K, D_V), F32),
            jax.ShapeDtypeStruct((batch, N_HEADS, D_K), F32),
            jax.ShapeDtypeStruct((batch, N_HEADS, LANES), F32),
            jax.ShapeDtypeStruct((D_FF // FFN_TILE, D_MODEL, 2 * FFN_TILE), BF16),
            jax.ShapeDtypeStruct((D_MODEL, D_MODEL), BF16),
        ],
        scratch_shapes=[
            pltpu.VMEM((L, D_MODEL), BF16),
            pltpu.VMEM((L, N_MAIN), F32),
            pltpu.VMEM((L + 2 * SUBLANES, D_CONV), F32),
            pltpu.VMEM((L, D_CONV), F32),
            pltpu.VMEM((N_HEADS, D_K, D_V), F32),
            pltpu.VMEM((N_HEADS, 1, D_K), F32),
            pltpu.VMEM((N_HEADS, SUBLANES, LANES), F32),
        ],
        compiler_params=_compiler_params(("arbitrary", "arbitrary")),
        name="fused_prompt",
    )(x2d, g1, w_bf, w_t, conv_w, conv_g, ml_g, b_i, b_f, wg, wu, w_out)


def _mix_sample_kernel(z_ref, gcol_ref, grow_ref, mcol_ref, mrow_ref,
                       convs_in_ref, c_in_ref, n_in_ref,
                       convw_ref, convg_ref, mlg_ref, bi_ref, bf_ref,
                       y_ref, convs_ref, c_out_ref, n_out_ref, m_out_ref,
                       pad_scr, conv_scr, *, seq):
    L = z_ref.shape[0]
    group = L // seq
    stride = seq + SUBLANES

    for g in range(group):
        rows = slice(g * seq, (g + 1) * seq)
        u = _conv_operand(z_ref, CG, rows) * _conv_operand(z_ref, XT, rows)
        base = g * stride
        pad_scr[base + SUBLANES - 2:base + SUBLANES, :] = convs_in_ref[g]
        pad_scr[base + SUBLANES:base + SUBLANES + seq, :] = u
        conv_scr[g * seq:(g + 1) * seq, :] = (
            pad_scr[base + SUBLANES - 2:base + SUBLANES - 2 + seq, :] * convw_ref[0]
            + pad_scr[base + SUBLANES - 1:base + SUBLANES - 1 + seq, :] * convw_ref[1]
            + u * convw_ref[2])
        convs_ref[g] = pad_scr[base + seq + SUBLANES - 2:base + seq + SUBLANES, :]
    y_ref[:, 0:D_CONV] = _conv_norm(_conv_operand(z_ref, BG), conv_scr[...],
                                    convg_ref[...]).astype(y_ref.dtype)

    bias_row, bias_col = _gate_bias(bi_ref, bf_ref)
    pre_c = gcol_ref[...] + bias_row
    pre_r = grow_ref[0:SUBLANES, :] + bias_col
    row_id = lax.broadcasted_iota(jnp.int32, (L, L), 0)
    col_id = lax.broadcasted_iota(jnp.int32, (L, L), 1)
    same = (row_id // seq) == (col_id // seq)
    causal = same & (col_id <= row_id)
    lf_c = _log_sigmoid(pre_c)
    lf_r = _log_sigmoid(pre_r)
    b_c, b_r = _masked_sums(causal, lf_c, lf_r)
    tot_c, tot_r = _masked_sums(same, lf_c, lf_r)
    lane_seq = lax.broadcasted_iota(jnp.int32, (1, L), 1) // seq

    for h in range(N_HEADS):
        li_r = pre_r[h:h + 1, :]
        li_c = pre_c[:, h:h + 1]
        bc = b_c[:, N_HEADS + h:N_HEADS + h + 1]
        br = b_r[N_HEADS + h:N_HEADS + h + 1, :]
        bl_c = tot_c[:, N_HEADS + h:N_HEADS + h + 1]
        bl_r = tot_r[N_HEADS + h:N_HEADS + h + 1, :]
        m0_c = mcol_ref[:, h:h + 1]
        m0_r = mrow_ref[h:h + 1, :]
        qf = z_ref[:, _cols_q(h)] * (D_K ** -0.5)
        q_bf = qf.astype(BF16)
        kf = z_ref[:, _cols_k(h)]
        v_bf = z_ref[:, _cols_v(h)].astype(BF16)
        s, a_inter, m_t = _intra(q_bf, kf.astype(BF16), bc, br, li_r, m0_c, causal)

        qc_rows, qn_rows = [], []
        for g in range(group):
            rows = slice(g * seq, (g + 1) * seq)
            qc_rows.append(jnp.dot(qf[rows], c_in_ref[g, h], preferred_element_type=F32))
            qn_rows.append(jnp.sum(qf[rows] * n_in_ref[g, h:h + 1, :], axis=1, keepdims=True))
        num = (a_inter * jnp.concatenate(qc_rows, axis=0)
               + jnp.dot(s.astype(BF16), v_bf, preferred_element_type=F32))
        den = (a_inter * jnp.concatenate(qn_rows, axis=0)
               + jnp.sum(s, axis=1, keepdims=True))
        y_ref[:, D_CONV + h * D_V:D_CONV + (h + 1) * D_V] = _head_output(
            num, den, m_t, z_ref[:, _cols_o(h)],
            mlg_ref[:, h * D_V:(h + 1) * D_V]).astype(y_ref.dtype)

        g_r = bl_r - br + li_r
        gmax_c = jnp.max(jnp.where(same, g_r, -jnp.inf), axis=1, keepdims=True)
        gmax_r = jnp.max(jnp.where(same, gmax_c, -jnp.inf), axis=0, keepdims=True)
        m_new_c = jnp.maximum(bl_c + m0_c, gmax_c)
        m_new_r = jnp.maximum(bl_r + m0_r, gmax_r)
        decay_r = jnp.exp(bl_r + m0_r - m_new_r)
        kw = kf * jnp.exp(bl_c - bc + li_c - m_new_c)
        kw_t = kw.T
        for g in range(group):
            decay = decay_r[:, g * seq:g * seq + 1]
            kw_g = jnp.where(lane_seq == g, kw_t, 0.0).astype(BF16)
            c_out_ref[g, h] = decay * c_in_ref[g, h] + jnp.dot(
                kw_g, v_bf, preferred_element_type=F32)
            n_out_ref[g, h:h + 1, :] = (decay * n_in_ref[g, h:h + 1, :]
                                        + jnp.sum(kw[g * seq:(g + 1) * seq], axis=0, keepdims=True))
        m_out_ref[h:h + 1, :] = m_new_r
    m_out_ref[N_HEADS:, :] = jnp.zeros((SUBLANES - N_HEADS, L), F32)


def _mix_sample(z, gcol, grow, mcol, mrow, convs, c_state, n_state, batch, seq,
                conv_w, conv_g, ml_g, b_i, b_f):
    group = SAMPLE_GROUP
    L = group * seq
    assert seq == SUBLANES and L == LANES and batch % group == 0, (batch, seq)
    const = lambda i: (0, 0)
    return pl.pallas_call(
        functools.partial(_mix_sample_kernel, seq=seq),
        grid=(batch // group,),
        in_specs=[
            pl.BlockSpec((L, N_MAIN), lambda i: (i, 0)),
            pl.BlockSpec((L, LANES), lambda i: (i, 0)),
            pl.BlockSpec((GATE_ROWS, L), lambda i: (0, i)),
            pl.BlockSpec((L, LANES), lambda i: (i, 0)),
            pl.BlockSpec((SUBLANES, L), lambda i: (0, i)),
            pl.BlockSpec((group, CONV_W - 1, D_CONV), lambda i: (i, 0, 0)),
            pl.BlockSpec((group, N_HEADS, D_K, D_V), lambda i: (i, 0, 0, 0)),
            pl.BlockSpec((group, N_HEADS, D_K), lambda i: (i, 0, 0)),
            CONV_W_BLOCK,
            pl.BlockSpec((1, D_CONV), const),
            pl.BlockSpec((1, N_HEADS * D_V), const),
            pl.BlockSpec(memory_space=pltpu.SMEM),
            pl.BlockSpec(memory_space=pltpu.SMEM),
        ],
        out_specs=[
            pl.BlockSpec((L, D_MODEL), lambda i: (i, 0)),
            pl.BlockSpec((group, CONV_W - 1, D_CONV), lambda i: (i, 0, 0)),
            pl.BlockSpec((group, N_HEADS, D_K, D_V), lambda i: (i, 0, 0, 0)),
            pl.BlockSpec((group, N_HEADS, D_K), lambda i: (i, 0, 0)),
            pl.BlockSpec((SUBLANES, L), lambda i: (0, i)),
        ],
        out_shape=[
            jax.ShapeDtypeStruct((batch * seq, D_MODEL), BF16),
            jax.ShapeDtypeStruct((batch, CONV_W - 1, D_CONV), F32),
            jax.ShapeDtypeStruct((batch, N_HEADS, D_K, D_V), F32),
            jax.ShapeDtypeStruct((batch, N_HEADS, D_K), F32),
            jax.ShapeDtypeStruct((SUBLANES, batch * seq), F32),
        ],
        scratch_shapes=[
            pltpu.VMEM((group * (seq + SUBLANES), D_CONV), F32),
            pltpu.VMEM((L, D_CONV), F32),
        ],
        compiler_params=_compiler_params(("parallel",)),
        name="mix_sample",
    )(z, gcol, grow, mcol, mrow, convs, c_state, n_state,
      conv_w, conv_g, ml_g, b_i, b_f)


def _outproj_kernel(ya_ref, yb_ref, w_ref, xa_ref, xb_ref, o_ref, *, tiles_a):
    i = pl.program_id(0)

    @pl.when(i < tiles_a)
    def _():
        o_ref[...] = xa_ref[...] + jnp.dot(ya_ref[...], w_ref[...], preferred_element_type=F32)

    @pl.when(i >= tiles_a)
    def _():
        o_ref[...] = xb_ref[...] + jnp.dot(yb_ref[...], w_ref[...], preferred_element_type=F32)


def _outproj(ya, yb, w_out_bf, xa, xb):
    tm = 512
    assert xa.shape[0] % tm == 0 and xb.shape[0] % tm == 0, (xa.shape, xb.shape)
    tiles_a, tiles_b = xa.shape[0] // tm, xb.shape[0] // tm
    rows_a = pl.BlockSpec((tm, D_MODEL), lambda i: (jnp.minimum(i, tiles_a - 1), 0))
    rows_b = pl.BlockSpec((tm, D_MODEL), lambda i: (jnp.maximum(i - tiles_a, 0), 0))
    return pl.pallas_call(
        functools.partial(_outproj_kernel, tiles_a=tiles_a),
        grid=(tiles_a + tiles_b,),
        in_specs=[
            rows_a, rows_b,
            pl.BlockSpec((D_MODEL, D_MODEL), lambda i: (0, 0), pipeline_mode=pl.Buffered(1)),
            rows_a, rows_b,
        ],
        out_specs=pl.BlockSpec((tm, D_MODEL), lambda i: (i, 0)),
        out_shape=jax.ShapeDtypeStruct(((tiles_a + tiles_b) * tm, D_MODEL), F32),
        compiler_params=_compiler_params(("parallel",)),
        name="outproj",
    )(ya, yb, w_out_bf, xa, xb)


def _ffn_kernel(x_ref, g2_ref, wgu_ref, wd_ref, gf_ref, outa_hbm, outb_hbm,
                acc_ref, hn_ref, sem, *, final_norm, tiles_a, tiles_b):
    i = pl.program_id(0)
    j = pl.program_id(1)
    tm = x_ref.shape[0]
    slot = i % 2
    o_ref = acc_ref.at[slot]

    def out_copy(tile, s, group_b):
        dst = outb_hbm if group_b else outa_hbm
        row = (tile - (tiles_a if group_b else 0)) * tm
        if not isinstance(row, int):
            row = pl.multiple_of(row, tm)
        return pltpu.make_async_copy(acc_ref.at[s], dst.at[pl.ds(row, tm), :], sem.at[s])

    def for_tile(tile, s, action):
        if isinstance(tile, int):
            action(out_copy(tile, s, tile >= tiles_a))
        else:
            pl.when(tile < tiles_a)(lambda: action(out_copy(tile, s, False)))
            pl.when(tile >= tiles_a)(lambda: action(out_copy(tile, s, True)))

    @pl.when(j == 0)
    def _():
        pl.when(i >= 2)(lambda: for_tile(i - 2, slot, lambda c: c.wait()))
        _rmsnorm_rows(x_ref, g2_ref[...], hn_ref, copy_ref=o_ref)

    hn = hn_ref[...]
    half = wd_ref.shape[0] // 2
    halves = (slice(0, half), slice(half, 2 * half))
    pre = []
    for a in range(2):
        gate_up = jnp.dot(hn, wgu_ref[0, :, 2 * a * half:2 * (a + 1) * half],
                          preferred_element_type=F32)
        pre.append((gate_up[:, 0:half], gate_up[:, half:2 * half]))
    down = None
    for (gate, up), rows in zip(pre, halves):
        act = ((gate * jax.nn.sigmoid(gate)) * up).astype(BF16)
        part = jnp.dot(act, wd_ref[rows, :].astype(BF16), preferred_element_type=F32)
        down = part if down is None else down + part
    o_ref[...] += down

    @pl.when(j == pl.num_programs(1) - 1)
    def _():
        if final_norm:
            _rmsnorm_rows_inplace(o_ref, gf_ref[...])
        for_tile(i, slot, lambda c: c.start())

        @pl.when(i == tiles_a + tiles_b - 1)
        def _():
            last = tiles_a + tiles_b - 1
            for tile in range(max(last - 1, 0), last + 1):
                for_tile(tile, tile % 2, lambda c: c.wait())


def _ffn(x1, rows_a, g2, wgu_bf, wd, gf, final_norm):
    tm = 1024
    tf = FFN_TILE
    rows_b = x1.shape[0] - rows_a
    assert rows_a % tm == 0 and rows_b % tm == 0 and rows_a > 0 and rows_b > 0, (rows_a, rows_b)
    assert tm % (NORM_ROWS * NORM_GROUP) == 0, tm
    tiles_a, tiles_b = rows_a // tm, rows_b // tm
    return pl.pallas_call(
        functools.partial(_ffn_kernel, final_norm=final_norm, tiles_a=tiles_a, tiles_b=tiles_b),
        grid=(tiles_a + tiles_b, D_FF // tf),
        in_specs=[
            pl.BlockSpec((tm, D_MODEL), lambda i, j: (i, 0)),
            pl.BlockSpec((1, D_MODEL), lambda i, j: (0, 0)),
            pl.BlockSpec((1, D_MODEL, 2 * tf), lambda i, j: (j, 0, 0)),
            pl.BlockSpec((tf, D_MODEL), lambda i, j: (j, 0)),
            pl.BlockSpec((1, D_MODEL), lambda i, j: (0, 0)),
        ],
        out_specs=[pl.BlockSpec(memory_space=pl.ANY), pl.BlockSpec(memory_space=pl.ANY)],
        out_shape=[jax.ShapeDtypeStruct((rows_a, D_MODEL), F32),
                   jax.ShapeDtypeStruct((rows_b, D_MODEL), F32)],
        scratch_shapes=[
            pltpu.VMEM((2, tm, D_MODEL), F32),
            pltpu.VMEM((tm, D_MODEL), BF16),
            pltpu.SemaphoreType.DMA((2,)),
        ],
        compiler_params=_compiler_params(("arbitrary", "arbitrary")),
        name="ffn",
    )(x1, g2, wgu_bf, wd, gf)


def _stack(states, k):
    if len(states) == 1:
        return states[0][k][None]
    return jnp.stack([st[k] for st in states])


def kernel(x_prompt, x_sample, state_conv, state_mlstm_C, state_mlstm_n, state_mlstm_m,
           norm1_g, w_in, b_igate, b_fgate, conv_w, conv_out_g, mlstm_out_g, w_out,
           norm2_g, w_ffn_gate, w_ffn_up, w_ffn_down, final_norm_g):
    depth = w_in.shape[0]
    bp, sp, _ = x_prompt.shape
    bs, ss, _ = x_sample.shape
    hp = x_prompt.reshape(bp * sp, D_MODEL)
    hs = x_sample.reshape(bs * ss, D_MODEL)
    p_states, s_states = [], []
    for l in range(depth):
        final_norm = l == depth - 1
        w_t = w_in[l].T
        gate_b = (b_igate[l], b_fgate[l])
        conv_taps = conv_w[l][:, None, :]
        w_bf = _cast_main_columns(w_t)
        g1 = norm1_g[l][None, :]
        conv_g = conv_out_g[l][None, :]
        ml_g = mlstm_out_g[l][None, :]

        y_p, cb, c1, n1, m1, wgu_bf, w_out_bf = _fused_prompt(
            hp, g1, w_bf, w_t, bp, sp, conv_taps, conv_g, ml_g, *gate_b,
            w_ffn_gate[l], w_ffn_up[l], w_out[l])
        p_states.append((cb, c1, n1, m1[:, :, 0]))

        z, gcol, grow = _inproj(hs, g1, w_bf, w_t)
        m_tok = jnp.repeat(state_mlstm_m[l], ss, axis=0)
        mcol = jnp.pad(m_tok, ((0, 0), (0, LANES - N_HEADS)))
        mrow = jnp.pad(m_tok.T, ((0, SUBLANES - N_HEADS), (0, 0)))
        y_s, cb, c1, n1, m_row = _mix_sample(z, gcol, grow, mcol, mrow, state_conv[l],
                                             state_mlstm_C[l], state_mlstm_n[l], bs, ss,
                                             conv_taps, conv_g, ml_g, *gate_b)
        s_states.append((cb, c1, n1, m_row[:N_HEADS, ::ss].T))

        x1 = _outproj(y_p, y_s, w_out_bf, hp, hs)
        hp, hs = _ffn(x1, bp * sp, norm2_g[l][None, :], wgu_bf, w_ffn_down[l],
                      final_norm_g[None, :], final_norm)

    return (hp.reshape(bp, sp, D_MODEL), hs.reshape(bs, ss, D_MODEL),
            _stack(p_states, 0), _stack(p_states, 1), _stack(p_states, 2), _stack(p_states, 3),
            _stack(s_states, 0), _stack(s_states, 1), _stack(s_states, 2), _stack(s_states, 3))
```

```python
import functools

import jax
import jax.numpy as jnp
from jax import lax
from jax.experimental import pallas as pl
from jax.experimental.pallas import tpu as pltpu

F32 = jnp.float32
BF16 = jnp.bfloat16

D_MODEL = 2048
D_CONV = 1024
CONV_W = 3
N_HEADS = 4
D_K = 128
D_V = 256
D_FF = 5632
EPS = 1e-6

LANES = 128
SUBLANES = 8
GATE_ROWS = 16
MIB = 1024 * 1024

PROMPT_CHUNK = 256
SAMPLE_GROUP = 16
NORM_ROWS = 64
NORM_GROUP = 4
W_SLAB = 512
CAST_SLAB = 1024
FFN_TILE = 512
CAST_PIECE = 256

VMEM_LIMIT_MIB = 62

SRC_Q = 3 * D_CONV
SRC_K = SRC_Q + N_HEADS * D_K
SRC_V = SRC_K + N_HEADS * D_K
SRC_O = SRC_V + N_HEADS * D_V
N_MAIN = SRC_O + N_HEADS * D_V

HEAD_COLS = 2 * D_K + 2 * D_V
N_MLSTM = N_HEADS * HEAD_COLS
CONV_PARTS = 4
CONV_PART = D_CONV // CONV_PARTS
BG, CG, XT = 0, 1, 2


def _cols_conv(kind, part):
    start = N_MLSTM + (3 * part + kind) * CONV_PART
    return slice(start, start + CONV_PART)


def _conv_operand(z, kind, rows=slice(None)):
    return jnp.concatenate([z[rows, _cols_conv(kind, p)] for p in range(CONV_PARTS)], axis=1)


def _cols_q(h):
    return slice(h * HEAD_COLS, h * HEAD_COLS + D_K)


def _cols_k(h):
    return slice(h * HEAD_COLS + D_K, h * HEAD_COLS + 2 * D_K)


def _cols_v(h):
    return slice(h * HEAD_COLS + 2 * D_K, h * HEAD_COLS + 2 * D_K + D_V)


def _cols_o(h):
    return slice(h * HEAD_COLS + 2 * D_K + D_V, (h + 1) * HEAD_COLS)


def _source_lane_block(j):
    per_head = HEAD_COLS // LANES
    h, r = j // per_head, j % per_head
    v_blocks = D_V // LANES
    mlstm_src = jnp.where(
        r == 0, SRC_Q // LANES + h,
        jnp.where(r == 1, SRC_K // LANES + h,
                  jnp.where(r < 2 + v_blocks,
                            SRC_V // LANES + v_blocks * h + (r - 2),
                            SRC_O // LANES + v_blocks * h + (r - 2 - v_blocks))))
    per_kind = CONV_PART // LANES
    c = j - N_MLSTM // LANES
    part, r = c // (3 * per_kind), c % (3 * per_kind)
    conv_src = (r // per_kind) * (D_CONV // LANES) + part * per_kind + r % per_kind
    return jnp.where(j < N_MLSTM // LANES, mlstm_src, conv_src)


def _compiler_params(semantics):
    return pltpu.CompilerParams(dimension_semantics=semantics,
                                vmem_limit_bytes=VMEM_LIMIT_MIB * MIB)


def _rmsnorm_piece(x, gain, dtype):
    ms = jnp.mean(x * x, axis=-1, keepdims=True)
    return ((x * lax.rsqrt(ms + EPS)) * gain).astype(dtype)


def _rmsnorm_rows(src_ref, gain, dst_ref, copy_ref=None):
    rows = src_ref.shape[0]
    chunk = min(NORM_ROWS, rows)

    def body(i, carry):
        r = pl.multiple_of(i * chunk, chunk)
        x = src_ref[pl.ds(r, chunk), :].astype(F32)
        if copy_ref is not None:
            copy_ref[pl.ds(r, chunk), :] = x
        dst_ref[pl.ds(r, chunk), :] = _rmsnorm_piece(x, gain, dst_ref.dtype)
        return carry

    lax.fori_loop(0, rows // chunk, body, 0, unroll=2)


def _rmsnorm_rows_inplace(ref, gain):
    rows = ref.shape[0]
    chunk = min(NORM_ROWS, rows)
    group = min(NORM_GROUP, rows // chunk)

    def body(i, carry):
        starts = [pl.multiple_of((i * group + k) * chunk, chunk) for k in range(group)]
        scales = []
        for r in starts:
            x = ref[pl.ds(r, chunk), :]
            scales.append(lax.rsqrt(jnp.mean(x * x, axis=-1, keepdims=True) + EPS))
        for r, scale in zip(starts, scales):
            ref[pl.ds(r, chunk), :] = (ref[pl.ds(r, chunk), :] * scale) * gain
        return carry

    lax.fori_loop(0, rows // (chunk * group), body, 0)


GATE_BLOCK = pl.BlockSpec((2 * N_HEADS, D_MODEL), lambda *_: (N_MAIN // (2 * N_HEADS), 0))
CONV_W_BLOCK = pl.BlockSpec((CONV_W, 1, D_CONV), lambda *_: (0, 0, 0))


def _gate_bias(bi_ref, bf_ref):
    lane = lax.broadcasted_iota(jnp.int32, (1, LANES), 1)
    sub = lax.broadcasted_iota(jnp.int32, (2 * N_HEADS, 1), 0)
    row = jnp.zeros((1, LANES), F32)
    col = jnp.zeros((2 * N_HEADS, 1), F32)
    for k in range(N_HEADS):
        for pos, value in ((k, bi_ref[k]), (N_HEADS + k, bf_ref[k])):
            row = jnp.where(lane == pos, value, row)
            col = jnp.where(sub == pos, value, col)
    return row, col


def _gate_dot(xn, wgate_t_ref):
    rows = wgate_t_ref[...]
    padded = jnp.concatenate([rows, jnp.zeros((LANES - rows.shape[0], D_MODEL), F32)], axis=0)
    return lax.dot_general(xn, padded.astype(BF16), (((1,), (1,)), ((), ())),
                           preferred_element_type=F32)


def _cast_project_kernel(*refs, n_src):
    w_refs = refs[:n_src]
    x_ref, g_ref, wgate_ref, wbf_ref, z_ref, gcol_ref, grow_ref, xn_ref = refs[n_src:]

    @pl.when(pl.program_id(0) == 0)
    def _():
        _rmsnorm_rows(x_ref, g_ref[...], xn_ref)
        gc = _gate_dot(xn_ref[...], wgate_ref)
        gcol_ref[...] = gc
        grow_ref[...] = gc.T[0:GATE_ROWS, :]

    per = W_SLAB // LANES
    for s in range(n_src // per):
        for r in range(s * per, (s + 1) * per):
            wbf_ref[:, r * LANES:(r + 1) * LANES] = w_refs[r][...].T.astype(wbf_ref.dtype)
        cols = slice(s * W_SLAB, (s + 1) * W_SLAB)
        z_ref[:, cols] = jnp.dot(xn_ref[...], wbf_ref[:, cols], preferred_element_type=F32)


def _cast_and_project(w_t, x2d, g1):
    tokens = x2d.shape[0]
    assert tokens <= 1024 and tokens % LANES == 0, tokens
    per_slab = CAST_SLAB // LANES
    const = lambda j: (0, 0)
    w_blocks = [
        pl.BlockSpec((LANES, D_MODEL),
                     lambda j, r=r: (_source_lane_block(per_slab * j + r), 0))
        for r in range(per_slab)
    ]
    return pl.pallas_call(
        functools.partial(_cast_project_kernel, n_src=per_slab),
        grid=(N_MAIN // CAST_SLAB,),
        in_specs=w_blocks + [
            pl.BlockSpec((tokens, D_MODEL), const),
            pl.BlockSpec((1, D_MODEL), const),
            GATE_BLOCK,
        ],
        out_specs=[
            pl.BlockSpec((D_MODEL, CAST_SLAB), lambda j: (0, j)),
            pl.BlockSpec((tokens, CAST_SLAB), lambda j: (0, j)),
            pl.BlockSpec((tokens, LANES), const),
            pl.BlockSpec((GATE_ROWS, tokens), const),
        ],
        out_shape=[
            jax.ShapeDtypeStruct((D_MODEL, N_MAIN), BF16),
            jax.ShapeDtypeStruct((tokens, N_MAIN), F32),
            jax.ShapeDtypeStruct((tokens, LANES), F32),
            jax.ShapeDtypeStruct((GATE_ROWS, tokens), F32),
        ],
        scratch_shapes=[pltpu.VMEM((tokens, D_MODEL), BF16)],
        compiler_params=_compiler_params(("arbitrary",)),
        name="cast_project",
    )(*([w_t] * per_slab), x2d, g1, w_t)


def _log_sigmoid(x):
    return jnp.minimum(x, 0.0) - jnp.log1p(jnp.exp(-jnp.abs(x)))


def _split3(x):
    hi = x.astype(BF16).astype(F32)
    mid = (x - hi).astype(BF16).astype(F32)
    lo = ((x - hi) - mid).astype(BF16).astype(F32)
    return hi, mid, lo


def _masked_sums(mask, lf_col, lf_row):
    m_bf = mask.astype(BF16)
    n, k = lf_col.shape[1], lf_row.shape[0]
    col_terms = jnp.concatenate(_split3(lf_col), axis=1).astype(BF16)
    row_terms = jnp.concatenate(_split3(lf_row) + (jnp.zeros_like(lf_row),),
                                axis=0).astype(BF16)
    cols = jnp.dot(m_bf, col_terms, preferred_element_type=F32)
    rows = lax.dot_general(row_terms, m_bf, (((1,), (1,)), ((), ())),
                           preferred_element_type=F32)
    return (cols[:, 0:n] + cols[:, n:2 * n] + cols[:, 2 * n:3 * n],
            rows[0:k] + rows[k:2 * k] + rows[2 * k:3 * k])


def _intra(q_bf, k_bf, b_col, b_row, li_row, m0_col, mask):
    dmat = jnp.where(mask, b_col - b_row + li_row, -jnp.inf)
    inter = b_col + m0_col
    m_t = jnp.maximum(inter, jnp.max(dmat, axis=1, keepdims=True))
    w = jnp.exp(dmat - m_t)
    a_inter = jnp.exp(inter - m_t)
    qk = lax.dot_general(q_bf, k_bf, (((1,), (1,)), ((), ())), preferred_element_type=F32)
    return qk * w, a_inter, m_t


def _head_output(num, den, m_t, o_gate, gain):
    denom = jnp.maximum(jnp.abs(den), jnp.exp(-m_t))
    h = num * (1.0 / denom)
    hn = (h * lax.rsqrt(jnp.mean(h * h, axis=1, keepdims=True) + EPS)) * gain
    return jax.nn.sigmoid(o_gate) * hn


def _conv_norm(bg, conv, gain):
    yc = bg * conv
    return (yc * lax.rsqrt(jnp.mean(yc * yc, axis=1, keepdims=True) + EPS)) * gain


def _prompt_conv_part(z, part, convw_ref, ubuf, yc_scr):
    L = z.shape[0]
    ch = slice(part * CONV_PART, (part + 1) * CONV_PART)
    u = z[:, _cols_conv(CG, part)] * z[:, _cols_conv(XT, part)]
    ubuf[SUBLANES:SUBLANES + L, ch] = u
    conv = (ubuf[SUBLANES - 2:SUBLANES - 2 + L, ch] * convw_ref[0, :, ch]
            + ubuf[SUBLANES - 1:SUBLANES - 1 + L, ch] * convw_ref[1, :, ch]
            + u * convw_ref[2, :, ch])
    ubuf[0:SUBLANES, ch] = ubuf[L:L + SUBLANES, ch]
    yc = z[:, _cols_conv(BG, part)] * conv
    yc_scr[:, ch] = yc
    return jnp.sum(yc * yc, axis=1, keepdims=True)


def _prompt_conv_finish(sumsq, yc_scr, convg_ref, y_ref):
    scale = lax.rsqrt(sumsq * (1.0 / D_CONV) + EPS)
    y_ref[:, 0:D_CONV] = ((yc_scr[...] * scale) * convg_ref[...]).astype(y_ref.dtype)


def _prompt_mlstm(z, gc, gr, mlg_ref, bi_ref, bf_ref, y_ref, c_scr, n_scr, m_scr,
                  fillers):
    L = z.shape[0]
    fillers = list(fillers)

    def fill():
        if fillers:
            fillers.pop(0)()

    bias_row, bias_col = _gate_bias(bi_ref, bf_ref)
    pre_c = gc + bias_row
    pre_r = gr + bias_col
    row_id = lax.broadcasted_iota(jnp.int32, (L, L), 0)
    col_id = lax.broadcasted_iota(jnp.int32, (L, L), 1)
    causal = col_id <= row_id
    b_c, b_r = _masked_sums(causal, _log_sigmoid(pre_c), _log_sigmoid(pre_r))

    def scores(h):
        li_r = pre_r[h:h + 1, :]
        bc = b_c[:, N_HEADS + h:N_HEADS + h + 1]
        br = b_r[N_HEADS + h:N_HEADS + h + 1, :]
        m0 = m_scr[h, 0:1, 0:1]
        qf = z[:, _cols_q(h)] * (D_K ** -0.5)
        q_bf = qf.astype(BF16)
        kf = z[:, _cols_k(h)]
        v_bf = z[:, _cols_v(h)].astype(BF16)
        s, a_inter, m_t = _intra(q_bf, kf.astype(BF16), bc, br, li_r, m0, causal)
        return li_r, bc, br, m0, qf, q_bf, kf, v_bf, s, a_inter, m_t

    def finish(h, staged):
        li_r, bc, br, m0, qf, q_bf, kf, v_bf, s, a_inter, m_t = staged
        li_c = pre_c[:, h:h + 1]
        c_old = c_scr[h]
        n_old = n_scr[h]
        num = (a_inter * jnp.dot(q_bf, c_old.astype(BF16), preferred_element_type=F32)
               + jnp.dot(s.astype(BF16), v_bf, preferred_element_type=F32))
        den = (a_inter * jnp.sum(qf * n_old, axis=1, keepdims=True)
               + jnp.sum(s, axis=1, keepdims=True))
        fill()
        y_ref[:, D_CONV + h * D_V:D_CONV + (h + 1) * D_V] = _head_output(
            num, den, m_t, z[:, _cols_o(h)],
            mlg_ref[:, h * D_V:(h + 1) * D_V]).astype(y_ref.dtype)

        b_last = br[:, L - 1:L]
        m_new = jnp.maximum(b_last + m0,
                            jnp.max(b_last - br + li_r, axis=1, keepdims=True))
        decay = jnp.exp(b_last + m0 - m_new)
        kw = kf * jnp.exp(b_last - bc + li_c - m_new)
        c_scr[h] = decay * c_old + lax.dot_general(
            kw.astype(BF16), v_bf, (((0,), (0,)), ((), ())), preferred_element_type=F32)
        n_scr[h] = decay * n_old + jnp.sum(kw, axis=0, keepdims=True)
        m_scr[h] = jnp.broadcast_to(m_new, (SUBLANES, LANES))
        fill()

    staged = scores(0)
    fill()
    for h in range(N_HEADS):
        upcoming = None
        if h + 1 < N_HEADS:
            upcoming = scores(h + 1)
            fill()
        finish(h, staged)
        staged = upcoming
    while fillers:
        fill()


def _fused_prompt_kernel(x_ref, g1_ref, w_ref, wgate_ref, convw_ref, convg_ref, mlg_ref,
                         bi_ref, bf_ref, wg_ref, wu_ref, wout_ref,
                         y_ref, convs_ref, c_out_ref, n_out_ref, m_out_ref,
                         wgu_bf_ref, wout_bf_ref,
                         xn_scr, z_scr, ubuf, yc_scr, c_scr, n_scr, m_scr):
    chunk = pl.program_id(1)
    L = x_ref.shape[0]

    @pl.when(chunk == 0)
    def _():
        ubuf[0:SUBLANES, :] = jnp.zeros((SUBLANES, D_CONV), F32)
        c_scr[...] = jnp.zeros(c_scr.shape, F32)
        n_scr[...] = jnp.zeros(n_scr.shape, F32)
        m_scr[...] = jnp.zeros(m_scr.shape, F32)

    gain = g1_ref[...]
    for r in range(0, L, NORM_ROWS):
        xn_scr[r:r + NORM_ROWS, :] = _rmsnorm_piece(x_ref[r:r + NORM_ROWS, :], gain, BF16)
    xn = xn_scr[...]
    gc = _gate_dot(xn, wgate_ref)
    gr = gc.T[0:SUBLANES, :]

    def project(j):
        cols = slice(j * W_SLAB, (j + 1) * W_SLAB)
        z_scr[:, cols] = jnp.dot(xn_scr[...], w_ref[:, cols], preferred_element_type=F32)

    piece = lambda j: functools.partial(project, j)
    skip = lambda: None
    sumsq = []

    def then_conv(j, part):
        def emit():
            project(j)
            sumsq.append(_prompt_conv_part(z_scr, part, convw_ref, ubuf, yc_scr))
        return emit

    project(0)
    wgu_bf_ref[0, :, 0:CAST_PIECE] = wg_ref[...].astype(BF16)
    wout_bf_ref[...] = wout_ref[...].astype(BF16)
    project(1)
    wgu_bf_ref[0, :, CAST_PIECE:2 * CAST_PIECE] = wu_ref[...].astype(BF16)
    fillers = [piece(2), piece(3), piece(4), piece(6),
               piece(5), then_conv(7, 0), then_conv(8, 1),
               piece(9), then_conv(10, 2), then_conv(11, 3),
               skip, skip]
    _prompt_mlstm(z_scr, gc, gr, mlg_ref, bi_ref, bf_ref, y_ref,
                  c_scr, n_scr, m_scr, fillers)
    _prompt_conv_finish(sum(sumsq[1:], sumsq[0]), yc_scr, convg_ref, y_ref)

    @pl.when(chunk == pl.num_programs(1) - 1)
    def _():
        convs_ref[0] = ubuf[SUBLANES - 2:SUBLANES, :]
        c_out_ref[0] = c_scr[...]
        for h in range(N_HEADS):
            n_out_ref[0, h:h + 1, :] = n_scr[h]
            m_out_ref[0, h:h + 1, :] = m_scr[h, 0:1, :]


def _fused_prompt(x2d, g1, w_bf, w_t, batch, seq, conv_w, conv_g, ml_g, b_i, b_f,
                  wg, wu, w_out):
    L = PROMPT_CHUNK
    nc = seq // L
    assert nc * L == seq, seq
    const = lambda b, c: (0, 0)
    wout_rows = D_MODEL // (batch * nc)
    assert wout_rows * batch * nc == D_MODEL and wout_rows % (2 * SUBLANES) == 0
    wout_piece = pl.BlockSpec((wout_rows, D_MODEL), lambda b, c: (b * nc + c, 0))
    n_pieces = D_FF // CAST_PIECE
    assert batch * nc >= n_pieces, "not enough grid steps to cast the FFN weights"
    per_tile = FFN_TILE // CAST_PIECE
    piece_of = lambda b, c: jnp.minimum(b * nc + c, n_pieces - 1)
    tiled = lambda b, c: (piece_of(b, c) // per_tile, 0, piece_of(b, c) % per_tile)
    return pl.pallas_call(
        _fused_prompt_kernel,
        grid=(batch, nc),
        in_specs=[
            pl.BlockSpec((L, D_MODEL), lambda b, c: (b * nc + c, 0)),
            pl.BlockSpec((1, D_MODEL), const),
            pl.BlockSpec((D_MODEL, N_MAIN), const, pipeline_mode=pl.Buffered(1)),
            GATE_BLOCK,
            CONV_W_BLOCK,
            pl.BlockSpec((1, D_CONV), const),
            pl.BlockSpec((1, N_HEADS * D_V), const),
            pl.BlockSpec(memory_space=pltpu.SMEM),
            pl.BlockSpec(memory_space=pltpu.SMEM),
            pl.BlockSpec((D_MODEL, CAST_PIECE), lambda b, c: (0, piece_of(b, c))),
            pl.BlockSpec((D_MODEL, CAST_PIECE), lambda b, c: (0, piece_of(b, c))),
            wout_piece,
        ],
        out_specs=[
            pl.BlockSpec((L, D_MODEL), lambda b, c: (b * nc + c, 0)),
            pl.BlockSpec((1, CONV_W - 1, D_CONV), lambda b, c: (b, 0, 0)),
            pl.BlockSpec((1, N_HEADS, D_K, D_V), lambda b, c: (b, 0, 0, 0)),
            pl.BlockSpec((1, N_HEADS, D_K), lambda b, c: (b, 0, 0)),
            pl.BlockSpec((1, N_HEADS, LANES), lambda b, c: (b, 0, 0)),
            pl.BlockSpec((1, D_MODEL, 2 * CAST_PIECE), tiled),
            wout_piece,
        ],
        out_shape=[
            jax.ShapeDtypeStruct((batch * seq, D_MODEL), BF16),
            jax.ShapeDtypeStruct((batch, CONV_W - 1, D_CONV), F32),
            jax.ShapeDtypeStruct((batch, N_HEADS, D_K, D_V), F32),
            jax.ShapeDtypeStruct((batch, N_HEADS, D_K), F32),
            jax.ShapeDtypeStruct((batch, N_HEADS, LANES), F32),
            jax.ShapeDtypeStruct((D_FF // FFN_TILE, D_MODEL, 2 * FFN_TILE), BF16),
            jax.ShapeDtypeStruct((D_MODEL, D_MODEL), BF16),
        ],
        scratch_shapes=[
            pltpu.VMEM((L, D_MODEL), BF16),
            pltpu.VMEM((L, N_MAIN), F32),
            pltpu.VMEM((L + 2 * SUBLANES, D_CONV), F32),
            pltpu.VMEM((L, D_CONV), F32),
            pltpu.VMEM((N_HEADS, D_K, D_V), F32),
            pltpu.VMEM((N_HEADS, 1, D_K), F32),
            pltpu.VMEM((N_HEADS, SUBLANES, LANES), F32),
        ],
        compiler_params=_compiler_params(("arbitrary", "arbitrary")),
        name="fused_prompt",
    )(x2d, g1, w_bf, w_t, conv_w, conv_g, ml_g, b_i, b_f, wg, wu, w_out)


def _mix_sample_kernel(z_ref, gcol_ref, grow_ref, mcol_ref, mrow_ref,
                       convs_in_ref, c_in_ref, n_in_ref,
                       convw_ref, convg_ref, mlg_ref, bi_ref, bf_ref,
                       y_ref, convs_ref, c_out_ref, n_out_ref, m_out_ref,
                       pad_scr, conv_scr, *, seq):
    L = z_ref.shape[0]
    group = L // seq
    stride = seq + SUBLANES

    for g in range(group):
        rows = slice(g * seq, (g + 1) * seq)
        u = _conv_operand(z_ref, CG, rows) * _conv_operand(z_ref, XT, rows)
        base = g * stride
        pad_scr[base + SUBLANES - 2:base + SUBLANES, :] = convs_in_ref[g]
        pad_scr[base + SUBLANES:base + SUBLANES + seq, :] = u
        conv_scr[g * seq:(g + 1) * seq, :] = (
            pad_scr[base + SUBLANES - 2:base + SUBLANES - 2 + seq, :] * convw_ref[0]
            + pad_scr[base + SUBLANES - 1:base + SUBLANES - 1 + seq, :] * convw_ref[1]
            + u * convw_ref[2])
        convs_ref[g] = pad_scr[base + seq + SUBLANES - 2:base + seq + SUBLANES, :]
    y_ref[:, 0:D_CONV] = _conv_norm(_conv_operand(z_ref, BG), conv_scr[...],
                                    convg_ref[...]).astype(y_ref.dtype)

    bias_row, bias_col = _gate_bias(bi_ref, bf_ref)
    pre_c = gcol_ref[...] + bias_row
    pre_r = grow_ref[0:SUBLANES, :] + bias_col
    row_id = lax.broadcasted_iota(jnp.int32, (L, L), 0)
    col_id = lax.broadcasted_iota(jnp.int32, (L, L), 1)
    same = (row_id // seq) == (col_id // seq)
    causal = same & (col_id <= row_id)
    lf_c = _log_sigmoid(pre_c)
    lf_r = _log_sigmoid(pre_r)
    b_c, b_r = _masked_sums(causal, lf_c, lf_r)
    tot_c, tot_r = _masked_sums(same, lf_c, lf_r)
    lane_seq = lax.broadcasted_iota(jnp.int32, (1, L), 1) // seq

    for h in range(N_HEADS):
        li_r = pre_r[h:h + 1, :]
        li_c = pre_c[:, h:h + 1]
        bc = b_c[:, N_HEADS + h:N_HEADS + h + 1]
        br = b_r[N_HEADS + h:N_HEADS + h + 1, :]
        bl_c = tot_c[:, N_HEADS + h:N_HEADS + h + 1]
        bl_r = tot_r[N_HEADS + h:N_HEADS + h + 1, :]
        m0_c = mcol_ref[:, h:h + 1]
        m0_r = mrow_ref[h:h + 1, :]
        qf = z_ref[:, _cols_q(h)] * (D_K ** -0.5)
        q_bf = qf.astype(BF16)
        kf = z_ref[:, _cols_k(h)]
        v_bf = z_ref[:, _cols_v(h)].astype(BF16)
        s, a_inter, m_t = _intra(q_bf, kf.astype(BF16), bc, br, li_r, m0_c, causal)

        qc_rows, qn_rows = [], []
        for g in range(group):
            rows = slice(g * seq, (g + 1) * seq)
            qc_rows.append(jnp.dot(qf[rows], c_in_ref[g, h], preferred_element_type=F32))
            qn_rows.append(jnp.sum(qf[rows] * n_in_ref[g, h:h + 1, :], axis=1, keepdims=True))
        num = (a_inter * jnp.concatenate(qc_rows, axis=0)
               + jnp.dot(s.astype(BF16), v_bf, preferred_element_type=F32))
        den = (a_inter * jnp.concatenate(qn_rows, axis=0)
               + jnp.sum(s, axis=1, keepdims=True))
        y_ref[:, D_CONV + h * D_V:D_CONV + (h + 1) * D_V] = _head_output(
            num, den, m_t, z_ref[:, _cols_o(h)],
            mlg_ref[:, h * D_V:(h + 1) * D_V]).astype(y_ref.dtype)

        g_r = bl_r - br + li_r
        gmax_c = jnp.max(jnp.where(same, g_r, -jnp.inf), axis=1, keepdims=True)
        gmax_r = jnp.max(jnp.where(same, gmax_c, -jnp.inf), axis=0, keepdims=True)
        m_new_c = jnp.maximum(bl_c + m0_c, gmax_c)
        m_new_r = jnp.maximum(bl_r + m0_r, gmax_r)
        decay_r = jnp.exp(bl_r + m0_r - m_new_r)
        kw = kf * jnp.exp(bl_c - bc + li_c - m_new_c)
        kw_t = kw.T
        for g in range(group):
            decay = decay_r[:, g * seq:g * seq + 1]
            kw_g = jnp.where(lane_seq == g, kw_t, 0.0).astype(BF16)
            c_out_ref[g, h] = decay * c_in_ref[g, h] + jnp.dot(
                kw_g, v_bf, preferred_element_type=F32)
            n_out_ref[g, h:h + 1, :] = (decay * n_in_ref[g, h:h + 1, :]
                                        + jnp.sum(kw[g * seq:(g + 1) * seq], axis=0, keepdims=True))
        m_out_ref[h:h + 1, :] = m_new_r
    m_out_ref[N_HEADS:, :] = jnp.zeros((SUBLANES - N_HEADS, L), F32)


def _mix_sample(z, gcol, grow, mcol, mrow, convs, c_state, n_state, batch, seq,
                conv_w, conv_g, ml_g, b_i, b_f):
    group = SAMPLE_GROUP
    L = group * seq
    assert seq == SUBLANES and L == LANES and batch % group == 0, (batch, seq)
    const = lambda i: (0, 0)
    return pl.pallas_call(
        functools.partial(_mix_sample_kernel, seq=seq),
        grid=(batch // group,),
        in_specs=[
            pl.BlockSpec((L, N_MAIN), lambda i: (i, 0)),
            pl.BlockSpec((L, LANES), lambda i: (i, 0)),
            pl.BlockSpec((GATE_ROWS, L), lambda i: (0, i)),
            pl.BlockSpec((L, LANES), lambda i: (i, 0)),
            pl.BlockSpec((SUBLANES, L), lambda i: (0, i)),
            pl.BlockSpec((group, CONV_W - 1, D_CONV), lambda i: (i, 0, 0)),
            pl.BlockSpec((group, N_HEADS, D_K, D_V), lambda i: (i, 0, 0, 0)),
            pl.BlockSpec((group, N_HEADS, D_K), lambda i: (i, 0, 0)),
            CONV_W_BLOCK,
            pl.BlockSpec((1, D_CONV), const),
            pl.BlockSpec((1, N_HEADS * D_V), const),
            pl.BlockSpec(memory_space=pltpu.SMEM),
            pl.BlockSpec(memory_space=pltpu.SMEM),
        ],
        out_specs=[
            pl.BlockSpec((L, D_MODEL), lambda i: (i, 0)),
            pl.BlockSpec((group, CONV_W - 1, D_CONV), lambda i: (i, 0, 0)),
            pl.BlockSpec((group, N_HEADS, D_K, D_V), lambda i: (i, 0, 0, 0)),
            pl.BlockSpec((group, N_HEADS, D_K), lambda i: (i, 0, 0)),
            pl.BlockSpec((SUBLANES, L), lambda i: (0, i)),
        ],
        out_shape=[
            jax.ShapeDtypeStruct((batch * seq, D_MODEL), BF16),
            jax.ShapeDtypeStruct((batch, CONV_W - 1, D_CONV), F32),
            jax.ShapeDtypeStruct((batch, N_HEADS, D_K, D_V), F32),
            jax.ShapeDtypeStruct((batch, N_HEADS, D_K), F32),
            jax.ShapeDtypeStruct((SUBLANES, batch * seq), F32),
        ],
        scratch_shapes=[
            pltpu.VMEM((group * (seq + SUBLANES), D_CONV), F32),
            pltpu.VMEM((L, D_CONV), F32),
        ],
        compiler_params=_compiler_params(("parallel",)),
        name="mix_sample",
    )(z, gcol, grow, mcol, mrow, convs, c_state, n_state,
      conv_w, conv_g, ml_g, b_i, b_f)


def _outproj_kernel(ya_ref, yb_ref, w_ref, xa_ref, xb_ref, o_ref, *, tiles_a):
    i = pl.program_id(0)

    @pl.when(i < tiles_a)
    def _():
        o_ref[...] = xa_ref[...] + jnp.dot(ya_ref[...], w_ref[...], preferred_element_type=F32)

    @pl.when(i >= tiles_a)
    def _():
        o_ref[...] = xb_ref[...] + jnp.dot(yb_ref[...], w_ref[...], preferred_element_type=F32)


def _outproj(ya, yb, w_out_bf, xa, xb):
    tm = 512
    assert xa.shape[0] % tm == 0 and xb.shape[0] % tm == 0, (xa.shape, xb.shape)
    tiles_a, tiles_b = xa.shape[0] // tm, xb.shape[0] // tm
    rows_a = pl.BlockSpec((tm, D_MODEL), lambda i: (jnp.minimum(i, tiles_a - 1), 0))
    rows_b = pl.BlockSpec((tm, D_MODEL), lambda i: (jnp.maximum(i - tiles_a, 0), 0))
    return pl.pallas_call(
        functools.partial(_outproj_kernel, tiles_a=tiles_a),
        grid=(tiles_a + tiles_b,),
        in_specs=[
            rows_a, rows_b,
            pl.BlockSpec((D_MODEL, D_MODEL), lambda i: (0, 0), pipeline_mode=pl.Buffered(1)),
            rows_a, rows_b,
        ],
        out_specs=pl.BlockSpec((tm, D_MODEL), lambda i: (i, 0)),
        out_shape=jax.ShapeDtypeStruct(((tiles_a + tiles_b) * tm, D_MODEL), F32),
        compiler_params=_compiler_params(("parallel",)),
        name="outproj",
    )(ya, yb, w_out_bf, xa, xb)


def _ffn_kernel(x_ref, g2_ref, wgu_ref, wd_ref, gf_ref, outa_hbm, outb_hbm,
                acc_ref, hn_ref, sem, *, final_norm, tiles_a, tiles_b):
    i = pl.program_id(0)
    j = pl.program_id(1)
    tm = x_ref.shape[0]
    slot = i % 2
    o_ref = acc_ref.at[slot]

    def out_copy(tile, s, group_b):
        dst = outb_hbm if group_b else outa_hbm
        row = (tile - (tiles_a if group_b else 0)) * tm
        if not isinstance(row, int):
            row = pl.multiple_of(row, tm)
        return pltpu.make_async_copy(acc_ref.at[s], dst.at[pl.ds(row, tm), :], sem.at[s])

    def for_tile(tile, s, action):
        if isinstance(tile, int):
            action(out_copy(tile, s, tile >= tiles_a))
        else:
            pl.when(tile < tiles_a)(lambda: action(out_copy(tile, s, False)))
            pl.when(tile >= tiles_a)(lambda: action(out_copy(tile, s, True)))

    @pl.when(j == 0)
    def _():
        pl.when(i >= 2)(lambda: for_tile(i - 2, slot, lambda c: c.wait()))
        _rmsnorm_rows(x_ref, g2_ref[...], hn_ref, copy_ref=o_ref)

    hn = hn_ref[...]
    half = wd_ref.shape[0] // 2
    halves = (slice(0, half), slice(half, 2 * half))
    pre = []
    for a in range(2):
        gate_up = jnp.dot(hn, wgu_ref[0, :, 2 * a * half:2 * (a + 1) * half],
                          preferred_element_type=F32)
        pre.append((gate_up[:, 0:half], gate_up[:, half:2 * half]))
    down = None
    for (gate, up), rows in zip(pre, halves):
        act = ((gate * jax.nn.sigmoid(gate)) * up).astype(BF16)
        part = jnp.dot(act, wd_ref[rows, :].astype(BF16), preferred_element_type=F32)
        down = part if down is None else down + part
    o_ref[...] += down

    @pl.when(j == pl.num_programs(1) - 1)
    def _():
        if final_norm:
            _rmsnorm_rows_inplace(o_ref, gf_ref[...])
        for_tile(i, slot, lambda c: c.start())

        @pl.when(i == tiles_a + tiles_b - 1)
        def _():
            last = tiles_a + tiles_b - 1
            for tile in range(max(last - 1, 0), last + 1):
                for_tile(tile, tile % 2, lambda c: c.wait())


def _ffn(x1, rows_a, g2, wgu_bf, wd, gf, final_norm):
    tm = 1024
    tf = FFN_TILE
    rows_b = x1.shape[0] - rows_a
    assert rows_a % tm == 0 and rows_b % tm == 0 and rows_a > 0 and rows_b > 0, (rows_a, rows_b)
    assert tm % (NORM_ROWS * NORM_GROUP) == 0, tm
    tiles_a, tiles_b = rows_a // tm, rows_b // tm
    return pl.pallas_call(
        functools.partial(_ffn_kernel, final_norm=final_norm, tiles_a=tiles_a, tiles_b=tiles_b),
        grid=(tiles_a + tiles_b, D_FF // tf),
        in_specs=[
            pl.BlockSpec((tm, D_MODEL), lambda i, j: (i, 0)),
            pl.BlockSpec((1, D_MODEL), lambda i, j: (0, 0)),
            pl.BlockSpec((1, D_MODEL, 2 * tf), lambda i, j: (j, 0, 0)),
            pl.BlockSpec((tf, D_MODEL), lambda i, j: (j, 0)),
            pl.BlockSpec((1, D_MODEL), lambda i, j: (0, 0)),
        ],
        out_specs=[pl.BlockSpec(memory_space=pl.ANY), pl.BlockSpec(memory_space=pl.ANY)],
        out_shape=[jax.ShapeDtypeStruct((rows_a, D_MODEL), F32),
                   jax.ShapeDtypeStruct((rows_b, D_MODEL), F32)],
        scratch_shapes=[
            pltpu.VMEM((2, tm, D_MODEL), F32),
            pltpu.VMEM((tm, D_MODEL), BF16),
            pltpu.SemaphoreType.DMA((2,)),
        ],
        compiler_params=_compiler_params(("arbitrary", "arbitrary")),
        name="ffn",
    )(x1, g2, wgu_bf, wd, gf)


def _stack(states, k):
    if len(states) == 1:
        return states[0][k][None]
    return jnp.stack([st[k] for st in states])


def kernel(x_prompt, x_sample, state_conv, state_mlstm_C, state_mlstm_n, state_mlstm_m,
           norm1_g, w_in, b_igate, b_fgate, conv_w, conv_out_g, mlstm_out_g, w_out,
           norm2_g, w_ffn_gate, w_ffn_up, w_ffn_down, final_norm_g):
    depth = w_in.shape[0]
    bp, sp, _ = x_prompt.shape
    bs, ss, _ = x_sample.shape
    hp = x_prompt.reshape(bp * sp, D_MODEL)
    hs = x_sample.reshape(bs * ss, D_MODEL)
    p_states, s_states = [], []
    for l in range(depth):
        final_norm = l == depth - 1
        w_t = w_in[l].T
        gate_b = (b_igate[l], b_fgate[l])
        conv_taps = conv_w[l][:, None, :]
        g1 = norm1_g[l][None, :]
        w_bf, z, gcol, grow = _cast_and_project(w_t, hs, g1)
        conv_g = conv_out_g[l][None, :]
        ml_g = mlstm_out_g[l][None, :]

        y_p, cb, c1, n1, m1, wgu_bf, w_out_bf = _fused_prompt(
            hp, g1, w_bf, w_t, bp, sp, conv_taps, conv_g, ml_g, *gate_b,
            w_ffn_gate[l], w_ffn_up[l], w_out[l])
        p_states.append((cb, c1, n1, m1[:, :, 0]))

        m_tok = jnp.repeat(state_mlstm_m[l], ss, axis=0)
        mcol = jnp.pad(m_tok, ((0, 0), (0, LANES - N_HEADS)))
        mrow = jnp.pad(m_tok.T, ((0, SUBLANES - N_HEADS), (0, 0)))
        y_s, cb, c1, n1, m_row = _mix_sample(z, gcol, grow, mcol, mrow, state_conv[l],
                                             state_mlstm_C[l], state_mlstm_n[l], bs, ss,
                                             conv_taps, conv_g, ml_g, *gate_b)
        s_states.append((cb, c1, n1, m_row[:N_HEADS, ::ss].T))

        x1 = _outproj(y_p, y_s, w_out_bf, hp, hs)
        hp, hs = _ffn(x1, bp * sp, norm2_g[l][None, :], wgu_bf, w_ffn_down[l],
                      final_norm_g[None, :], final_norm)

    return (hp.reshape(bp, sp, D_MODEL), hs.reshape(bs, ss, D_MODEL),
            _stack(p_states, 0), _stack(p_states, 1), _stack(p_states, 2), _stack(p_states, 3),
            _stack(s_states, 0), _stack(s_states, 1), _stack(s_states, 2), _stack(s_states, 3))
```

```python
import functools

import jax
import jax.numpy as jnp
from jax import lax
from jax.experimental import pallas as pl
from jax.experimental.pallas import tpu as pltpu

F32 = jnp.float32
BF16 = jnp.bfloat16

D_MODEL = 2048
D_CONV = 1024
CONV_W = 3
N_HEADS = 4
D_K = 128
D_V = 256
D_FF = 5632
EPS = 1e-6

LANES = 128
SUBLANES = 8
GATE_ROWS = 16
MIB = 1024 * 1024

PROMPT_CHUNK = 256
SAMPLE_GROUP = 16
NORM_ROWS = 64
NORM_GROUP = 4
W_SLAB = 512
CAST_SLAB = 1024
FFN_TILE = 512
CAST_PIECE = 256

VMEM_LIMIT_MIB = 62

SRC_Q = 3 * D_CONV
SRC_K = SRC_Q + N_HEADS * D_K
SRC_V = SRC_K + N_HEADS * D_K
SRC_O = SRC_V + N_HEADS * D_V
N_MAIN = SRC_O + N_HEADS * D_V

HEAD_COLS = 2 * D_K + 2 * D_V
N_MLSTM = N_HEADS * HEAD_COLS
CONV_PARTS = 4
CONV_PART = D_CONV // CONV_PARTS
BG, CG, XT = 0, 1, 2


def _cols_conv(kind, part):
    start = N_MLSTM + (3 * part + kind) * CONV_PART
    return slice(start, start + CONV_PART)


def _conv_operand(z, kind, rows=slice(None)):
    return jnp.concatenate([z[rows, _cols_conv(kind, p)] for p in range(CONV_PARTS)], axis=1)


def _cols_q(h):
    return slice(h * HEAD_COLS, h * HEAD_COLS + D_K)


def _cols_k(h):
    return slice(h * HEAD_COLS + D_K, h * HEAD_COLS + 2 * D_K)


def _cols_v(h):
    return slice(h * HEAD_COLS + 2 * D_K, h * HEAD_COLS + 2 * D_K + D_V)


def _cols_o(h):
    return slice(h * HEAD_COLS + 2 * D_K + D_V, (h + 1) * HEAD_COLS)


def _source_lane_block(j):
    per_head = HEAD_COLS // LANES
    h, r = j // per_head, j % per_head
    v_blocks = D_V // LANES
    mlstm_src = jnp.where(
        r == 0, SRC_Q // LANES + h,
        jnp.where(r == 1, SRC_K // LANES + h,
                  jnp.where(r < 2 + v_blocks,
                            SRC_V // LANES + v_blocks * h + (r - 2),
                            SRC_O // LANES + v_blocks * h + (r - 2 - v_blocks))))
    per_kind = CONV_PART // LANES
    c = j - N_MLSTM // LANES
    part, r = c // (3 * per_kind), c % (3 * per_kind)
    conv_src = (r // per_kind) * (D_CONV // LANES) + part * per_kind + r % per_kind
    return jnp.where(j < N_MLSTM // LANES, mlstm_src, conv_src)


def _compiler_params(semantics):
    return pltpu.CompilerParams(dimension_semantics=semantics,
                                vmem_limit_bytes=VMEM_LIMIT_MIB * MIB)


def _rmsnorm_piece(x, gain, dtype):
    ms = jnp.mean(x * x, axis=-1, keepdims=True)
    return ((x * lax.rsqrt(ms + EPS)) * gain).astype(dtype)


def _rmsnorm_rows(src_ref, gain, dst_ref, copy_ref=None):
    rows = src_ref.shape[0]
    chunk = min(NORM_ROWS, rows)

    def body(i, carry):
        r = pl.multiple_of(i * chunk, chunk)
        x = src_ref[pl.ds(r, chunk), :].astype(F32)
        if copy_ref is not None:
            copy_ref[pl.ds(r, chunk), :] = x
        dst_ref[pl.ds(r, chunk), :] = _rmsnorm_piece(x, gain, dst_ref.dtype)
        return carry

    lax.fori_loop(0, rows // chunk, body, 0, unroll=2)


def _rmsnorm_rows_inplace(ref, gain):
    rows = ref.shape[0]
    chunk = min(NORM_ROWS, rows)
    group = min(NORM_GROUP, rows // chunk)

    def body(i, carry):
        starts = [pl.multiple_of((i * group + k) * chunk, chunk) for k in range(group)]
        scales = []
        for r in starts:
            x = ref[pl.ds(r, chunk), :]
            scales.append(lax.rsqrt(jnp.mean(x * x, axis=-1, keepdims=True) + EPS))
        for r, scale in zip(starts, scales):
            ref[pl.ds(r, chunk), :] = (ref[pl.ds(r, chunk), :] * scale) * gain
        return carry

    lax.fori_loop(0, rows // (chunk * group), body, 0)


GATE_BLOCK = pl.BlockSpec((2 * N_HEADS, D_MODEL), lambda *_: (N_MAIN // (2 * N_HEADS), 0))
CONV_W_BLOCK = pl.BlockSpec((CONV_W, 1, D_CONV), lambda *_: (0, 0, 0))


def _gate_bias(bi_ref, bf_ref):
    lane = lax.broadcasted_iota(jnp.int32, (1, LANES), 1)
    sub = lax.broadcasted_iota(jnp.int32, (2 * N_HEADS, 1), 0)
    row = jnp.zeros((1, LANES), F32)
    col = jnp.zeros((2 * N_HEADS, 1), F32)
    for k in range(N_HEADS):
        for pos, value in ((k, bi_ref[k]), (N_HEADS + k, bf_ref[k])):
            row = jnp.where(lane == pos, value, row)
            col = jnp.where(sub == pos, value, col)
    return row, col


def _gate_dot(xn, wgate_t_ref):
    rows = wgate_t_ref[...]
    padded = jnp.concatenate([rows, jnp.zeros((LANES - rows.shape[0], D_MODEL), F32)], axis=0)
    return lax.dot_general(xn, padded.astype(BF16), (((1,), (1,)), ((), ())),
                           preferred_element_type=F32)


def _cast_project_kernel(*refs, n_src):
    (w_hbm, x_ref, g_ref, wgate_ref, wbf_ref, z_ref, gcol_ref, grow_ref,
     ring_ref, sem, xn_ref) = refs
    step = pl.program_id(0)
    n_steps = pl.num_programs(0)
    n_slots = ring_ref.shape[0]

    def block_copy(at_step, r):
        src = pl.multiple_of(_source_lane_block(n_src * at_step + r) * LANES, LANES)
        slot = at_step % n_slots
        return pltpu.make_async_copy(w_hbm.at[pl.ds(src, LANES), :],
                                     ring_ref.at[slot, pl.ds(r * LANES, LANES), :],
                                     sem.at[slot, r])

    def start_step(at_step):
        for r in range(n_src):
            block_copy(at_step, r).start()

    @pl.when(step == 0)
    def _():
        start_step(step)
        start_step(step + 1)
        _rmsnorm_rows(x_ref, g_ref[...], xn_ref)
        gc = _gate_dot(xn_ref[...], wgate_ref)
        gcol_ref[...] = gc
        grow_ref[...] = gc.T[0:GATE_ROWS, :]

    pl.when(step + 2 < n_steps)(lambda: start_step(step + 2))

    slot = step % n_slots
    per = W_SLAB // LANES
    for s in range(n_src // per):
        for r in range(s * per, (s + 1) * per):
            block_copy(step, r).wait()
            block = ring_ref[slot, r * LANES:(r + 1) * LANES, :]
            wbf_ref[:, r * LANES:(r + 1) * LANES] = block.T.astype(wbf_ref.dtype)
        cols = slice(s * W_SLAB, (s + 1) * W_SLAB)
        z_ref[:, cols] = jnp.dot(xn_ref[...], wbf_ref[:, cols], preferred_element_type=F32)


def _cast_and_project(w_t, x2d, g1):
    tokens = x2d.shape[0]
    assert tokens <= 1024 and tokens % LANES == 0, tokens
    per_slab = CAST_SLAB // LANES
    const = lambda j: (0, 0)
    n_slots = 3
    assert N_MAIN // CAST_SLAB >= 2
    return pl.pallas_call(
        functools.partial(_cast_project_kernel, n_src=per_slab),
        grid=(N_MAIN // CAST_SLAB,),
        in_specs=[
            pl.BlockSpec(memory_space=pl.ANY),
            pl.BlockSpec((tokens, D_MODEL), const, pipeline_mode=pl.Buffered(1)),
            pl.BlockSpec((1, D_MODEL), const),
            GATE_BLOCK,
        ],
        out_specs=[
            pl.BlockSpec((D_MODEL, CAST_SLAB), lambda j: (0, j)),
            pl.BlockSpec((tokens, CAST_SLAB), lambda j: (0, j)),
            pl.BlockSpec((tokens, LANES), const),
            pl.BlockSpec((GATE_ROWS, tokens), const),
        ],
        out_shape=[
            jax.ShapeDtypeStruct((D_MODEL, N_MAIN), BF16),
            jax.ShapeDtypeStruct((tokens, N_MAIN), F32),
            jax.ShapeDtypeStruct((tokens, LANES), F32),
            jax.ShapeDtypeStruct((GATE_ROWS, tokens), F32),
        ],
        scratch_shapes=[
            pltpu.VMEM((n_slots, CAST_SLAB, D_MODEL), F32),
            pltpu.SemaphoreType.DMA((n_slots, per_slab)),
            pltpu.VMEM((tokens, D_MODEL), BF16),
        ],
        compiler_params=_compiler_params(("arbitrary",)),
        name="cast_project",
    )(w_t, x2d, g1, w_t)


def _log_sigmoid(x):
    return jnp.minimum(x, 0.0) - jnp.log1p(jnp.exp(-jnp.abs(x)))


def _split3(x):
    hi = x.astype(BF16).astype(F32)
    mid = (x - hi).astype(BF16).astype(F32)
    lo = ((x - hi) - mid).astype(BF16).astype(F32)
    return hi, mid, lo


def _masked_sums(mask, lf_col, lf_row):
    m_bf = mask.astype(BF16)
    n, k = lf_col.shape[1], lf_row.shape[0]
    col_terms = jnp.concatenate(_split3(lf_col), axis=1).astype(BF16)
    row_terms = jnp.concatenate(_split3(lf_row) + (jnp.zeros_like(lf_row),),
                                axis=0).astype(BF16)
    cols = jnp.dot(m_bf, col_terms, preferred_element_type=F32)
    rows = lax.dot_general(row_terms, m_bf, (((1,), (1,)), ((), ())),
                           preferred_element_type=F32)
    return (cols[:, 0:n] + cols[:, n:2 * n] + cols[:, 2 * n:3 * n],
            rows[0:k] + rows[k:2 * k] + rows[2 * k:3 * k])


def _intra(q_bf, k_bf, b_col, b_row, li_row, m0_col, mask):
    dmat = jnp.where(mask, b_col - b_row + li_row, -jnp.inf)
    inter = b_col + m0_col
    m_t = jnp.maximum(inter, jnp.max(dmat, axis=1, keepdims=True))
    w = jnp.exp(dmat - m_t)
    a_inter = jnp.exp(inter - m_t)
    qk = lax.dot_general(q_bf, k_bf, (((1,), (1,)), ((), ())), preferred_element_type=F32)
    return qk * w, a_inter, m_t


def _head_output(num, den, m_t, o_gate, gain):
    denom = jnp.maximum(jnp.abs(den), jnp.exp(-m_t))
    h = num * (1.0 / denom)
    hn = (h * lax.rsqrt(jnp.mean(h * h, axis=1, keepdims=True) + EPS)) * gain
    return jax.nn.sigmoid(o_gate) * hn


def _conv_norm(bg, conv, gain):
    yc = bg * conv
    return (yc * lax.rsqrt(jnp.mean(yc * yc, axis=1, keepdims=True) + EPS)) * gain


def _prompt_conv_part(z, part, convw_ref, ubuf, yc_scr):
    L = z.shape[0]
    ch = slice(part * CONV_PART, (part + 1) * CONV_PART)
    u = z[:, _cols_conv(CG, part)] * z[:, _cols_conv(XT, part)]
    ubuf[SUBLANES:SUBLANES + L, ch] = u
    conv = (ubuf[SUBLANES - 2:SUBLANES - 2 + L, ch] * convw_ref[0, :, ch]
            + ubuf[SUBLANES - 1:SUBLANES - 1 + L, ch] * convw_ref[1, :, ch]
            + u * convw_ref[2, :, ch])
    ubuf[0:SUBLANES, ch] = ubuf[L:L + SUBLANES, ch]
    yc = z[:, _cols_conv(BG, part)] * conv
    yc_scr[:, ch] = yc
    return jnp.sum(yc * yc, axis=1, keepdims=True)


def _prompt_conv_finish(sumsq, yc_scr, convg_ref, y_ref):
    scale = lax.rsqrt(sumsq * (1.0 / D_CONV) + EPS)
    y_ref[:, 0:D_CONV] = ((yc_scr[...] * scale) * convg_ref[...]).astype(y_ref.dtype)


def _prompt_mlstm(z, gc, gr, mlg_ref, bi_ref, bf_ref, y_ref, c_scr, n_scr, m_scr,
                  fillers):
    L = z.shape[0]
    fillers = list(fillers)

    def fill():
        if fillers:
            fillers.pop(0)()

    bias_row, bias_col = _gate_bias(bi_ref, bf_ref)
    pre_c = gc + bias_row
    pre_r = gr + bias_col
    row_id = lax.broadcasted_iota(jnp.int32, (L, L), 0)
    col_id = lax.broadcasted_iota(jnp.int32, (L, L), 1)
    causal = col_id <= row_id
    b_c, b_r = _masked_sums(causal, _log_sigmoid(pre_c), _log_sigmoid(pre_r))

    def scores(h):
        li_r = pre_r[h:h + 1, :]
        bc = b_c[:, N_HEADS + h:N_HEADS + h + 1]
        br = b_r[N_HEADS + h:N_HEADS + h + 1, :]
        m0 = m_scr[h, 0:1, 0:1]
        qf = z[:, _cols_q(h)] * (D_K ** -0.5)
        q_bf = qf.astype(BF16)
        kf = z[:, _cols_k(h)]
        v_bf = z[:, _cols_v(h)].astype(BF16)
        s, a_inter, m_t = _intra(q_bf, kf.astype(BF16), bc, br, li_r, m0, causal)
        return li_r, bc, br, m0, qf, q_bf, kf, v_bf, s, a_inter, m_t

    def finish(h, staged):
        li_r, bc, br, m0, qf, q_bf, kf, v_bf, s, a_inter, m_t = staged
        li_c = pre_c[:, h:h + 1]
        c_old = c_scr[h]
        n_old = n_scr[h]
        num = (a_inter * jnp.dot(q_bf, c_old.astype(BF16), preferred_element_type=F32)
               + jnp.dot(s.astype(BF16), v_bf, preferred_element_type=F32))
        den = (a_inter * jnp.sum(qf * n_old, axis=1, keepdims=True)
               + jnp.sum(s, axis=1, keepdims=True))
        fill()
        y_ref[:, D_CONV + h * D_V:D_CONV + (h + 1) * D_V] = _head_output(
            num, den, m_t, z[:, _cols_o(h)],
            mlg_ref[:, h * D_V:(h + 1) * D_V]).astype(y_ref.dtype)

        b_last = br[:, L - 1:L]
        m_new = jnp.maximum(b_last + m0,
                            jnp.max(b_last - br + li_r, axis=1, keepdims=True))
        decay = jnp.exp(b_last + m0 - m_new)
        kw = kf * jnp.exp(b_last - bc + li_c - m_new)
        c_scr[h] = decay * c_old + lax.dot_general(
            kw.astype(BF16), v_bf, (((0,), (0,)), ((), ())), preferred_element_type=F32)
        n_scr[h] = decay * n_old + jnp.sum(kw, axis=0, keepdims=True)
        m_scr[h] = jnp.broadcast_to(m_new, (SUBLANES, LANES))
        fill()

    staged = scores(0)
    fill()
    for h in range(N_HEADS):
        upcoming = None
        if h + 1 < N_HEADS:
            upcoming = scores(h + 1)
            fill()
        finish(h, staged)
        staged = upcoming
    while fillers:
        fill()


def _fused_prompt_kernel(x_ref, g1_ref, w_ref, wgate_ref, convw_ref, convg_ref, mlg_ref,
                         bi_ref, bf_ref, wg_ref, wu_ref, wout_ref,
                         y_ref, convs_ref, c_out_ref, n_out_ref, m_out_ref,
                         wgu_bf_ref, wout_bf_ref,
                         xn_scr, z_scr, ubuf, yc_scr, c_scr, n_scr, m_scr):
    chunk = pl.program_id(1)
    L = x_ref.shape[0]

    @pl.when(chunk == 0)
    def _():
        ubuf[0:SUBLANES, :] = jnp.zeros((SUBLANES, D_CONV), F32)
        c_scr[...] = jnp.zeros(c_scr.shape, F32)
        n_scr[...] = jnp.zeros(n_scr.shape, F32)
        m_scr[...] = jnp.zeros(m_scr.shape, F32)

    gain = g1_ref[...]
    for r in range(0, L, NORM_ROWS):
        xn_scr[r:r + NORM_ROWS, :] = _rmsnorm_piece(x_ref[r:r + NORM_ROWS, :], gain, BF16)
    xn = xn_scr[...]
    gc = _gate_dot(xn, wgate_ref)
    gr = gc.T[0:SUBLANES, :]

    def project(j):
        cols = slice(j * W_SLAB, (j + 1) * W_SLAB)
        z_scr[:, cols] = jnp.dot(xn_scr[...], w_ref[:, cols], preferred_element_type=F32)

    piece = lambda j: functools.partial(project, j)
    skip = lambda: None
    sumsq = []

    def then_conv(j, part):
        def emit():
            project(j)
            sumsq.append(_prompt_conv_part(z_scr, part, convw_ref, ubuf, yc_scr))
        return emit

    project(0)
    wgu_bf_ref[0, :, 0:CAST_PIECE] = wg_ref[...].astype(BF16)
    wout_bf_ref[...] = wout_ref[...].astype(BF16)
    project(1)
    wgu_bf_ref[0, :, CAST_PIECE:2 * CAST_PIECE] = wu_ref[...].astype(BF16)
    fillers = [piece(2), piece(3), piece(4), piece(6),
               piece(5), then_conv(7, 0), then_conv(8, 1),
               piece(9), then_conv(10, 2), then_conv(11, 3),
               skip, skip]
    _prompt_mlstm(z_scr, gc, gr, mlg_ref, bi_ref, bf_ref, y_ref,
                  c_scr, n_scr, m_scr, fillers)
    _prompt_conv_finish(sum(sumsq[1:], sumsq[0]), yc_scr, convg_ref, y_ref)

    @pl.when(chunk == pl.num_programs(1) - 1)
    def _():
        convs_ref[0] = ubuf[SUBLANES - 2:SUBLANES, :]
        c_out_ref[0] = c_scr[...]
        for h in range(N_HEADS):
            n_out_ref[0, h:h + 1, :] = n_scr[h]
            m_out_ref[0, h:h + 1, :] = m_scr[h, 0:1, :]


def _fused_prompt(x2d, g1, w_bf, w_t, batch, seq, conv_w, conv_g, ml_g, b_i, b_f,
                  wg, wu, w_out):
    L = PROMPT_CHUNK
    nc = seq // L
    assert nc * L == seq, seq
    const = lambda b, c: (0, 0)
    wout_rows = D_MODEL // (batch * nc)
    assert wout_rows * batch * nc == D_MODEL and wout_rows % (2 * SUBLANES) == 0
    wout_piece = pl.BlockSpec((wout_rows, D_MODEL), lambda b, c: (b * nc + c, 0))
    n_pieces = D_FF // CAST_PIECE
    assert batch * nc >= n_pieces, "not enough grid steps to cast the FFN weights"
    per_tile = FFN_TILE // CAST_PIECE
    piece_of = lambda b, c: jnp.minimum(b * nc + c, n_pieces - 1)
    tiled = lambda b, c: (piece_of(b, c) // per_tile, 0, piece_of(b, c) % per_tile)
    return pl.pallas_call(
        _fused_prompt_kernel,
        grid=(batch, nc),
        in_specs=[
            pl.BlockSpec((L, D_MODEL), lambda b, c: (b * nc + c, 0)),
            pl.BlockSpec((1, D_MODEL), const),
            pl.BlockSpec((D_MODEL, N_MAIN), const, pipeline_mode=pl.Buffered(1)),
            GATE_BLOCK,
            CONV_W_BLOCK,
            pl.BlockSpec((1, D_CONV), const),
            pl.BlockSpec((1, N_HEADS * D_V), const),
            pl.BlockSpec(memory_space=pltpu.SMEM),
            pl.BlockSpec(memory_space=pltpu.SMEM),
            pl.BlockSpec((D_MODEL, CAST_PIECE), lambda b, c: (0, piece_of(b, c))),
            pl.BlockSpec((D_MODEL, CAST_PIECE), lambda b, c: (0, piece_of(b, c))),
            wout_piece,
        ],
        out_specs=[
            pl.BlockSpec((L, D_MODEL), lambda b, c: (b * nc + c, 0)),
            pl.BlockSpec((1, CONV_W - 1, D_CONV), lambda b, c: (b, 0, 0)),
            pl.BlockSpec((1, N_HEADS, D_K, D_V), lambda b, c: (b, 0, 0, 0)),
            pl.BlockSpec((1, N_HEADS, D_K), lambda b, c: (b, 0, 0)),
            pl.BlockSpec((1, N_HEADS, LANES), lambda b, c: (b, 0, 0)),
            pl.BlockSpec((1, D_MODEL, 2 * CAST_PIECE), tiled),
            wout_piece,
        ],
        out_shape=[
            jax.ShapeDtypeStruct((batch * seq, D_MODEL), BF16),
            jax.ShapeDtypeStruct((batch, CONV_W - 1, D_CONV), F32),
            jax.ShapeDtypeStruct((batch, N_HEADS, D_K, D_V), F32),
            jax.ShapeDtypeStruct((batch, N_HEADS, D_K), F32),
            jax.ShapeDtypeStruct((batch, N_HEADS, LANES), F32),
            jax.ShapeDtypeStruct((D_FF // FFN_TILE, D_MODEL, 2 * FFN_TILE), BF16),
            jax.ShapeDtypeStruct((D_MODEL, D_MODEL), BF16),
        ],
        scratch_shapes=[
            pltpu.VMEM((L, D_MODEL), BF16),
            pltpu.VMEM((L, N_MAIN), F32),
            pltpu.VMEM((L + 2 * SUBLANES, D_CONV), F32),
            pltpu.VMEM((L, D_CONV), F32),
            pltpu.VMEM((N_HEADS, D_K, D_V), F32),
            pltpu.VMEM((N_HEADS, 1, D_K), F32),
            pltpu.VMEM((N_HEADS, SUBLANES, LANES), F32),
        ],
        compiler_params=_compiler_params(("arbitrary", "arbitrary")),
        name="fused_prompt",
    )(x2d, g1, w_bf, w_t, conv_w, conv_g, ml_g, b_i, b_f, wg, wu, w_out)


def _mix_sample_kernel(z_ref, gcol_ref, grow_ref, mcol_ref, mrow_ref,
                       convs_in_ref, c_in_ref, n_in_ref,
                       convw_ref, convg_ref, mlg_ref, bi_ref, bf_ref,
                       y_ref, convs_ref, c_out_ref, n_out_ref, m_out_ref,
                       pad_scr, conv_scr, *, seq):
    L = z_ref.shape[0]
    group = L // seq
    stride = seq + SUBLANES

    for g in range(group):
        rows = slice(g * seq, (g + 1) * seq)
        u = _conv_operand(z_ref, CG, rows) * _conv_operand(z_ref, XT, rows)
        base = g * stride
        pad_scr[base + SUBLANES - 2:base + SUBLANES, :] = convs_in_ref[g]
        pad_scr[base + SUBLANES:base + SUBLANES + seq, :] = u
        conv_scr[g * seq:(g + 1) * seq, :] = (
            pad_scr[base + SUBLANES - 2:base + SUBLANES - 2 + seq, :] * convw_ref[0]
            + pad_scr[base + SUBLANES - 1:base + SUBLANES - 1 + seq, :] * convw_ref[1]
            + u * convw_ref[2])
        convs_ref[g] = pad_scr[base + seq + SUBLANES - 2:base + seq + SUBLANES, :]
    y_ref[:, 0:D_CONV] = _conv_norm(_conv_operand(z_ref, BG), conv_scr[...],
                                    convg_ref[...]).astype(y_ref.dtype)

    bias_row, bias_col = _gate_bias(bi_ref, bf_ref)
    pre_c = gcol_ref[...] + bias_row
    pre_r = grow_ref[0:SUBLANES, :] + bias_col
    row_id = lax.broadcasted_iota(jnp.int32, (L, L), 0)
    col_id = lax.broadcasted_iota(jnp.int32, (L, L), 1)
    same = (row_id // seq) == (col_id // seq)
    causal = same & (col_id <= row_id)
    lf_c = _log_sigmoid(pre_c)
    lf_r = _log_sigmoid(pre_r)
    b_c, b_r = _masked_sums(causal, lf_c, lf_r)
    tot_c, tot_r = _masked_sums(same, lf_c, lf_r)
    lane_seq = lax.broadcasted_iota(jnp.int32, (1, L), 1) // seq

    for h in range(N_HEADS):
        li_r = pre_r[h:h + 1, :]
        li_c = pre_c[:, h:h + 1]
        bc = b_c[:, N_HEADS + h:N_HEADS + h + 1]
        br = b_r[N_HEADS + h:N_HEADS + h + 1, :]
        bl_c = tot_c[:, N_HEADS + h:N_HEADS + h + 1]
        bl_r = tot_r[N_HEADS + h:N_HEADS + h + 1, :]
        m0_c = mcol_ref[:, h:h + 1]
        m0_r = mrow_ref[h:h + 1, :]
        qf = z_ref[:, _cols_q(h)] * (D_K ** -0.5)
        q_bf = qf.astype(BF16)
        kf = z_ref[:, _cols_k(h)]
        v_bf = z_ref[:, _cols_v(h)].astype(BF16)
        s, a_inter, m_t = _intra(q_bf, kf.astype(BF16), bc, br, li_r, m0_c, causal)

        qc_rows, qn_rows = [], []
        for g in range(group):
            rows = slice(g * seq, (g + 1) * seq)
            qc_rows.append(jnp.dot(qf[rows], c_in_ref[g, h], preferred_element_type=F32))
            qn_rows.append(jnp.sum(qf[rows] * n_in_ref[g, h:h + 1, :], axis=1, keepdims=True))
        num = (a_inter * jnp.concatenate(qc_rows, axis=0)
               + jnp.dot(s.astype(BF16), v_bf, preferred_element_type=F32))
        den = (a_inter * jnp.concatenate(qn_rows, axis=0)
               + jnp.sum(s, axis=1, keepdims=True))
        y_ref[:, D_CONV + h * D_V:D_CONV + (h + 1) * D_V] = _head_output(
            num, den, m_t, z_ref[:, _cols_o(h)],
            mlg_ref[:, h * D_V:(h + 1) * D_V]).astype(y_ref.dtype)

        g_r = bl_r - br + li_r
        gmax_c = jnp.max(jnp.where(same, g_r, -jnp.inf), axis=1, keepdims=True)
        gmax_r = jnp.max(jnp.where(same, gmax_c, -jnp.inf), axis=0, keepdims=True)
        m_new_c = jnp.maximum(bl_c + m0_c, gmax_c)
        m_new_r = jnp.maximum(bl_r + m0_r, gmax_r)
        decay_r = jnp.exp(bl_r + m0_r - m_new_r)
        kw = kf * jnp.exp(bl_c - bc + li_c - m_new_c)
        kw_t = kw.T
        for g in range(group):
            decay = decay_r[:, g * seq:g * seq + 1]
            kw_g = jnp.where(lane_seq == g, kw_t, 0.0).astype(BF16)
            c_out_ref[g, h] = decay * c_in_ref[g, h] + jnp.dot(
                kw_g, v_bf, preferred_element_type=F32)
            n_out_ref[g, h:h + 1, :] = (decay * n_in_ref[g, h:h + 1, :]
                                        + jnp.sum(kw[g * seq:(g + 1) * seq], axis=0, keepdims=True))
        m_out_ref[h:h + 1, :] = m_new_r
    m_out_ref[N_HEADS:, :] = jnp.zeros((SUBLANES - N_HEADS, L), F32)


def _mix_sample(z, gcol, grow, mcol, mrow, convs, c_state, n_state, batch, seq,
                conv_w, conv_g, ml_g, b_i, b_f):
    group = SAMPLE_GROUP
    L = group * seq
    assert seq == SUBLANES and L == LANES and batch % group == 0, (batch, seq)
    const = lambda i: (0, 0)
    return pl.pallas_call(
        functools.partial(_mix_sample_kernel, seq=seq),
        grid=(batch // group,),
        in_specs=[
            pl.BlockSpec((L, N_MAIN), lambda i: (i, 0)),
            pl.BlockSpec((L, LANES), lambda i: (i, 0)),
            pl.BlockSpec((GATE_ROWS, L), lambda i: (0, i)),
            pl.BlockSpec((L, LANES), lambda i: (i, 0)),
            pl.BlockSpec((SUBLANES, L), lambda i: (0, i)),
            pl.BlockSpec((group, CONV_W - 1, D_CONV), lambda i: (i, 0, 0)),
            pl.BlockSpec((group, N_HEADS, D_K, D_V), lambda i: (i, 0, 0, 0)),
            pl.BlockSpec((group, N_HEADS, D_K), lambda i: (i, 0, 0)),
            CONV_W_BLOCK,
            pl.BlockSpec((1, D_CONV), const),
            pl.BlockSpec((1, N_HEADS * D_V), const),
            pl.BlockSpec(memory_space=pltpu.SMEM),
            pl.BlockSpec(memory_space=pltpu.SMEM),
        ],
        out_specs=[
            pl.BlockSpec((L, D_MODEL), lambda i: (i, 0)),
            pl.BlockSpec((group, CONV_W - 1, D_CONV), lambda i: (i, 0, 0)),
            pl.BlockSpec((group, N_HEADS, D_K, D_V), lambda i: (i, 0, 0, 0)),
            pl.BlockSpec((group, N_HEADS, D_K), lambda i: (i, 0, 0)),
            pl.BlockSpec((SUBLANES, L), lambda i: (0, i)),
        ],
        out_shape=[
            jax.ShapeDtypeStruct((batch * seq, D_MODEL), BF16),
            jax.ShapeDtypeStruct((batch, CONV_W - 1, D_CONV), F32),
            jax.ShapeDtypeStruct((batch, N_HEADS, D_K, D_V), F32),
            jax.ShapeDtypeStruct((batch, N_HEADS, D_K), F32),
            jax.ShapeDtypeStruct((SUBLANES, batch * seq), F32),
        ],
        scratch_shapes=[
            pltpu.VMEM((group * (seq + SUBLANES), D_CONV), F32),
            pltpu.VMEM((L, D_CONV), F32),
        ],
        compiler_params=_compiler_params(("parallel",)),
        name="mix_sample",
    )(z, gcol, grow, mcol, mrow, convs, c_state, n_state,
      conv_w, conv_g, ml_g, b_i, b_f)


def _outproj_kernel(ya_ref, yb_ref, w_ref, xa_ref, xb_ref, o_ref, *, tiles_a):
    i = pl.program_id(0)

    @pl.when(i < tiles_a)
    def _():
        o_ref[...] = xa_ref[...] + jnp.dot(ya_ref[...], w_ref[...], preferred_element_type=F32)

    @pl.when(i >= tiles_a)
    def _():
        o_ref[...] = xb_ref[...] + jnp.dot(yb_ref[...], w_ref[...], preferred_element_type=F32)


def _outproj(ya, yb, w_out_bf, xa, xb):
    tm = 512
    assert xa.shape[0] % tm == 0 and xb.shape[0] % tm == 0, (xa.shape, xb.shape)
    tiles_a, tiles_b = xa.shape[0] // tm, xb.shape[0] // tm
    rows_a = pl.BlockSpec((tm, D_MODEL), lambda i: (jnp.minimum(i, tiles_a - 1), 0))
    rows_b = pl.BlockSpec((tm, D_MODEL), lambda i: (jnp.maximum(i - tiles_a, 0), 0))
    return pl.pallas_call(
        functools.partial(_outproj_kernel, tiles_a=tiles_a),
        grid=(tiles_a + tiles_b,),
        in_specs=[
            rows_a, rows_b,
            pl.BlockSpec((D_MODEL, D_MODEL), lambda i: (0, 0), pipeline_mode=pl.Buffered(1)),
            rows_a, rows_b,
        ],
        out_specs=pl.BlockSpec((tm, D_MODEL), lambda i: (i, 0)),
        out_shape=jax.ShapeDtypeStruct(((tiles_a + tiles_b) * tm, D_MODEL), F32),
        compiler_params=_compiler_params(("parallel",)),
        name="outproj",
    )(ya, yb, w_out_bf, xa, xb)


def _ffn_kernel(x_ref, g2_ref, wgu_ref, wd_ref, gf_ref, outa_hbm, outb_hbm,
                acc_ref, hn_ref, sem, *, final_norm, tiles_a, tiles_b):
    i = pl.program_id(0)
    j = pl.program_id(1)
    tm = x_ref.shape[0]
    slot = i % 2
    o_ref = acc_ref.at[slot]

    def out_copy(tile, s, group_b):
        dst = outb_hbm if group_b else outa_hbm
        row = (tile - (tiles_a if group_b else 0)) * tm
        if not isinstance(row, int):
            row = pl.multiple_of(row, tm)
        return pltpu.make_async_copy(acc_ref.at[s], dst.at[pl.ds(row, tm), :], sem.at[s])

    def for_tile(tile, s, action):
        if isinstance(tile, int):
            action(out_copy(tile, s, tile >= tiles_a))
        else:
            pl.when(tile < tiles_a)(lambda: action(out_copy(tile, s, False)))
            pl.when(tile >= tiles_a)(lambda: action(out_copy(tile, s, True)))

    @pl.when(j == 0)
    def _():
        pl.when(i >= 2)(lambda: for_tile(i - 2, slot, lambda c: c.wait()))
        _rmsnorm_rows(x_ref, g2_ref[...], hn_ref, copy_ref=o_ref)

    hn = hn_ref[...]
    half = wd_ref.shape[0] // 2
    halves = (slice(0, half), slice(half, 2 * half))
    pre = []
    for a in range(2):
        gate_up = jnp.dot(hn, wgu_ref[0, :, 2 * a * half:2 * (a + 1) * half],
                          preferred_element_type=F32)
        pre.append((gate_up[:, 0:half], gate_up[:, half:2 * half]))
    down = None
    for (gate, up), rows in zip(pre, halves):
        act = ((gate * jax.nn.sigmoid(gate)) * up).astype(BF16)
        part = jnp.dot(act, wd_ref[rows, :].astype(BF16), preferred_element_type=F32)
        down = part if down is None else down + part
    o_ref[...] += down

    @pl.when(j == pl.num_programs(1) - 1)
    def _():
        if final_norm:
            _rmsnorm_rows_inplace(o_ref, gf_ref[...])
        for_tile(i, slot, lambda c: c.start())

        @pl.when(i == tiles_a + tiles_b - 1)
        def _():
            last = tiles_a + tiles_b - 1
            for tile in range(max(last - 1, 0), last + 1):
                for_tile(tile, tile % 2, lambda c: c.wait())


def _ffn(x1, rows_a, g2, wgu_bf, wd, gf, final_norm):
    tm = 1024
    tf = FFN_TILE
    rows_b = x1.shape[0] - rows_a
    assert rows_a % tm == 0 and rows_b % tm == 0 and rows_a > 0 and rows_b > 0, (rows_a, rows_b)
    assert tm % (NORM_ROWS * NORM_GROUP) == 0, tm
    tiles_a, tiles_b = rows_a // tm, rows_b // tm
    return pl.pallas_call(
        functools.partial(_ffn_kernel, final_norm=final_norm, tiles_a=tiles_a, tiles_b=tiles_b),
        grid=(tiles_a + tiles_b, D_FF // tf),
        in_specs=[
            pl.BlockSpec((tm, D_MODEL), lambda i, j: (i, 0)),
            pl.BlockSpec((1, D_MODEL), lambda i, j: (0, 0)),
            pl.BlockSpec((1, D_MODEL, 2 * tf), lambda i, j: (j, 0, 0)),
            pl.BlockSpec((tf, D_MODEL), lambda i, j: (j, 0)),
            pl.BlockSpec((1, D_MODEL), lambda i, j: (0, 0)),
        ],
        out_specs=[pl.BlockSpec(memory_space=pl.ANY), pl.BlockSpec(memory_space=pl.ANY)],
        out_shape=[jax.ShapeDtypeStruct((rows_a, D_MODEL), F32),
                   jax.ShapeDtypeStruct((rows_b, D_MODEL), F32)],
        scratch_shapes=[
            pltpu.VMEM((2, tm, D_MODEL), F32),
            pltpu.VMEM((tm, D_MODEL), BF16),
            pltpu.SemaphoreType.DMA((2,)),
        ],
        compiler_params=_compiler_params(("arbitrary", "arbitrary")),
        name="ffn",
    )(x1, g2, wgu_bf, wd, gf)


def _stack(states, k):
    if len(states) == 1:
        return states[0][k][None]
    return jnp.stack([st[k] for st in states])


def kernel(x_prompt, x_sample, state_conv, state_mlstm_C, state_mlstm_n, state_mlstm_m,
           norm1_g, w_in, b_igate, b_fgate, conv_w, conv_out_g, mlstm_out_g, w_out,
           norm2_g, w_ffn_gate, w_ffn_up, w_ffn_down, final_norm_g):
    depth = w_in.shape[0]
    bp, sp, _ = x_prompt.shape
    bs, ss, _ = x_sample.shape
    hp = x_prompt.reshape(bp * sp, D_MODEL)
    hs = x_sample.reshape(bs * ss, D_MODEL)
    p_states, s_states = [], []
    for l in range(depth):
        final_norm = l == depth - 1
        w_t = w_in[l].T
        gate_b = (b_igate[l], b_fgate[l])
        conv_taps = conv_w[l][:, None, :]
        g1 = norm1_g[l][None, :]
        w_bf, z, gcol, grow = _cast_and_project(w_t, hs, g1)
        conv_g = conv_out_g[l][None, :]
        ml_g = mlstm_out_g[l][None, :]

        y_p, cb, c1, n1, m1, wgu_bf, w_out_bf = _fused_prompt(
            hp, g1, w_bf, w_t, bp, sp, conv_taps, conv_g, ml_g, *gate_b,
            w_ffn_gate[l], w_ffn_up[l], w_out[l])
        p_states.append((cb, c1, n1, m1[:, :, 0]))

        m_tok = jnp.repeat(state_mlstm_m[l], ss, axis=0)
        mcol = jnp.pad(m_tok, ((0, 0), (0, LANES - N_HEADS)))
        mrow = jnp.pad(m_tok.T, ((0, SUBLANES - N_HEADS), (0, 0)))
        y_s, cb, c1, n1, m_row = _mix_sample(z, gcol, grow, mcol, mrow, state_conv[l],
                                             state_mlstm_C[l], state_mlstm_n[l], bs, ss,
                                             conv_taps, conv_g, ml_g, *gate_b)
        s_states.append((cb, c1, n1, m_row[:N_HEADS, ::ss].T))

        x1 = _outproj(y_p, y_s, w_out_bf, hp, hs)
        hp, hs = _ffn(x1, bp * sp, norm2_g[l][None, :], wgu_bf, w_ffn_down[l],
                      final_norm_g[None, :], final_norm)

    return (hp.reshape(bp, sp, D_MODEL), hs.reshape(bs, ss, D_MODEL),
            _stack(p_states, 0), _stack(p_states, 1), _stack(p_states, 2), _stack(p_states, 3),
            _stack(s_states, 0), _stack(s_states, 1), _stack(s_states, 2), _stack(s_states, 3))
```
